```python
import math
import functools
import jax
import jax.numpy as jnp
from jax import lax
import numpy as np

D_MODEL = 1024
BATCH = 2
SEQ = 8192
DEPTH = 4
DEC_BATCH = 32
DEC_SEQ = 1
PAST_LEN = 8192
PAGE_SIZE = 128

N_MIXERS = 3
HEAD_DIM = 64
MIX_WIDTH = D_MODEL
N_MEM = 256
MEM_HEADS = 4
MEM_WIDTH = MEM_HEADS * HEAD_DIM
SEQ_WIDTH = MIX_WIDTH - MEM_WIDTH
S5_GROUP = 16
S5_GROUPS = SEQ_WIDTH // S5_GROUP
S5_STATE = 64
DSA_HEADS = SEQ_WIDTH // HEAD_DIM
DSA_KV_HEADS = 4
DSA_GQA = DSA_HEADS // DSA_KV_HEADS
IDX_HEADS = 8
IDX_DIM = 64
TOPK_MAX = 256
QBLOCK = 128
N_BUCKETS = 32
MAX_DISTANCE = 128
GDN_HEADS = SEQ_WIDTH // HEAD_DIM
CONV_W = 4
CONV_CH = 3 * SEQ_WIDTH
GDN_CHUNK = 64
D_FF = 2816
EPS = 1e-6
N_LAYERS_A = (DEPTH + 2) // 3
N_LAYERS_B = (DEPTH + 1) // 3
N_LAYERS_C = DEPTH // 3
DSA_SPLITS = (DSA_HEADS * HEAD_DIM, DSA_KV_HEADS * HEAD_DIM, DSA_KV_HEADS * HEAD_DIM, IDX_HEADS * IDX_DIM, IDX_DIM, IDX_HEADS)
GDN_SPLITS = (CONV_CH, GDN_HEADS, GDN_HEADS, SEQ_WIDTH)
IN_A = MEM_WIDTH + SEQ_WIDTH
IN_B = MEM_WIDTH + sum(DSA_SPLITS)
IN_C = MEM_WIDTH + sum(GDN_SPLITS)

kernel_name = 'hybrid_s5_dsa_gdn_macaron_step'


def split_cols(z, sizes):
    return jnp.split(z, np.cumsum(sizes)[:-1].tolist(), axis=-1)


def rmsnorm(x, g):
    x32 = x.astype(jnp.float32)
    r = lax.rsqrt(jnp.mean(x32 * x32, axis=-1, keepdims=True) + EPS)
    return (x32 * r).astype(x.dtype) * g


def l2norm(x):
    return x * lax.rsqrt(jnp.sum(x * x, axis=-1, keepdims=True) + EPS)


def swiglu(x, w_gate, w_up, w_down):
    return (jax.nn.silu(x @ w_gate) * (x @ w_up)) @ w_down


def take_rows(rows, idx):
    return jax.vmap(lambda r, i: r[i])(rows, idx)


def mem_kv(mem, w_kv):
    k, v = jnp.split(mem @ w_kv, 2, axis=-1)
    shp = (mem.shape[0], mem.shape[1], MEM_HEADS, HEAD_DIM)
    return k.reshape(shp), v.reshape(shp)


def mem_attend(cq, mk, mv):
    bsz, lq = cq.shape[:2]
    q = cq.reshape(bsz, lq, MEM_HEADS, HEAD_DIM)
    logits = jnp.einsum('bqhd,bmhd->bhqm', q, mk).astype(jnp.float32) * HEAD_DIM ** -0.5
    p = jax.nn.softmax(logits, axis=-1).astype(mv.dtype)
    return jnp.einsum('bhqm,bmhd->bqhd', p, mv).reshape(bsz, lq, MEM_WIDTH)


def complex_affine_combine(x, y):
    ar1, ai1, br1, bi1 = x
    ar2, ai2, br2, bi2 = y
    return (ar1 * ar2 - ai1 * ai2, ar1 * ai2 + ai1 * ar2,
            ar2 * br1 - ai2 * bi1 + br2, ar2 * bi1 + ai2 * br1 + bi2)


def s5_mix(u, h0_re, h0_im, lam_re, lam_im, log_step, b_re, b_im, c_re, c_im, d_skip, w_glu, b_glu):
    f32 = jnp.float32
    bsz, L, _ = u.shape
    lr, li = lam_re.astype(f32), lam_im.astype(f32)
    step = jnp.exp(log_step.astype(f32))[:, None]
    mag = jnp.exp(lr * step)
    ab_re, ab_im = mag * jnp.cos(li * step), mag * jnp.sin(li * step)
    den = lr * lr + li * li
    nr, ni = ab_re - 1.0, ab_im
    f_re = (nr * lr + ni * li) / den
    f_im = (ni * lr - nr * li) / den
    br, bi = b_re.astype(f32), b_im.astype(f32)
    bb_re = f_re[..., None] * br - f_im[..., None] * bi
    bb_im = f_re[..., None] * bi + f_im[..., None] * br
    ug = u.astype(f32).reshape(bsz, L, S5_GROUPS, S5_GROUP)
    bu_re = jnp.einsum('blgc,gpc->blgp', ug, bb_re)
    bu_im = jnp.einsum('blgc,gpc->blgp', ug, bb_im)
    e_re = jnp.concatenate([h0_re.astype(f32)[:, None], bu_re], axis=1)
    e_im = jnp.concatenate([h0_im.astype(f32)[:, None], bu_im], axis=1)
    a_re = jnp.broadcast_to(ab_re, e_re.shape)
    a_im = jnp.broadcast_to(ab_im, e_im.shape)
    _, _, h_re, h_im = lax.associative_scan(complex_affine_combine, (a_re, a_im, e_re, e_im), axis=1)
    h_re, h_im = h_re[:, 1:], h_im[:, 1:]
    y = (jnp.einsum('gcp,blgp->blgc', c_re.astype(f32), h_re)
         - jnp.einsum('gcp,blgp->blgc', c_im.astype(f32), h_im))
    y = y.reshape(bsz, L, SEQ_WIDTH).astype(u.dtype) + d_skip * u
    y = jax.nn.gelu(y)
    y = y * jax.nn.sigmoid(y @ w_glu + b_glu)
    return y, h_re[:, -1].astype(h0_re.dtype), h_im[:, -1].astype(h0_im.dtype)


def t5_bucket(dist):
    max_exact = N_BUCKETS // 2
    n = jnp.maximum(dist, 0)
    nf = jnp.maximum(n, 1).astype(jnp.float32)
    large = max_exact + (jnp.log(nf / max_exact) / math.log(MAX_DISTANCE / max_exact)
                         * (N_BUCKETS - max_exact)).astype(jnp.int32)
    return jnp.where(n < max_exact, n, jnp.minimum(large, N_BUCKETS - 1))


def dsa_heads(z):
    bsz, L = z.shape[:2]
    q, k, v, qi, ki, wi = split_cols(z, DSA_SPLITS)
    return (q.reshape(bsz, L, DSA_KV_HEADS, DSA_GQA, HEAD_DIM),
            k.reshape(bsz, L, DSA_KV_HEADS, HEAD_DIM),
            v.reshape(bsz, L, DSA_KV_HEADS, HEAD_DIM),
            qi.reshape(bsz, L, IDX_HEADS, IDX_DIM), ki, wi * IDX_HEADS ** -0.5)


def dsa_block(q, qi, wi, qpos, ki_all, gather_kv, rel_bias, n_top):
    dots = jnp.einsum('bqhd,bsd->bqhs', qi, ki_all) * IDX_DIM ** -0.5
    score = jnp.einsum('bqh,bqhs->bqs', wi, jax.nn.relu(dots)).astype(jnp.float32)
    kpos = jnp.arange(ki_all.shape[1], dtype=jnp.int32)
    score = jnp.where(kpos[None, None, :] <= qpos[None, :, None], score, -jnp.inf)
    _, idx = lax.top_k(score, n_top)
    valid = idx <= qpos[None, :, None]
    k_sel, v_sel = gather_kv(idx)
    logits = jnp.einsum('bqkgd,bqnkd->bqkgn', q, k_sel).astype(jnp.float32) * HEAD_DIM ** -0.5
    bias = rel_bias[t5_bucket(qpos[None, :, None] - idx)].astype(jnp.float32)
    bias = bias.reshape(bias.shape[:3] + (DSA_KV_HEADS, DSA_GQA))
    logits = logits + jnp.moveaxis(bias, 2, -1)
    logits = jnp.where(valid[:, :, None, None, :], logits, -jnp.inf)
    p = jax.nn.softmax(logits, axis=-1).astype(v_sel.dtype)
    return jnp.einsum('bqkgn,bqnkd->bqkgd', p, v_sel)


def dsa_attend(q, qi, wi, qpos, ki_all, gather_kv, rel_bias):
    bsz, lq = q.shape[:2]
    n_top = min(TOPK_MAX, ki_all.shape[1] // 4)
    block = functools.partial(dsa_block, ki_all=ki_all, gather_kv=gather_kv, rel_bias=rel_bias, n_top=n_top)
    if lq <= QBLOCK or lq % QBLOCK != 0:
        o = block(q, qi, wi, qpos)
    else:
        nb = lq // QBLOCK
        def to_blocks(a):
            return jnp.moveaxis(a.reshape((bsz, nb, QBLOCK) + a.shape[2:]), 1, 0)
        o = lax.map(lambda a: block(*a), (to_blocks(q), to_blocks(qi), to_blocks(wi), qpos.reshape(nb, QBLOCK)))
        o = jnp.moveaxis(o, 0, 1)
    return o.reshape(bsz, lq, SEQ_WIDTH)


def local_gather(k, v):
    return lambda idx: (take_rows(k, idx), take_rows(v, idx))


def paged_gather(pool_k, pool_v, page_table, k_new, v_new):
    past = page_table.shape[1] * PAGE_SIZE
    lnew = k_new.shape[1]
    def gather(idx):
        ic = jnp.minimum(idx, past - 1)
        phys = take_rows(page_table, ic // PAGE_SIZE)
        off = ic % PAGE_SIZE
        inew = jnp.clip(idx - past, 0, lnew - 1)
        in_past = (idx < past)[..., None, None]
        return (jnp.where(in_past, pool_k[phys, off], take_rows(k_new, inew)),
                jnp.where(in_past, pool_v[phys, off], take_rows(v_new, inew)))
    return gather


def causal_conv(x, ctx, w):
    L = x.shape[1]
    xp = jnp.concatenate([ctx, x], axis=1)
    y = xp[:, 0:L] * w[0]
    for j in range(1, CONV_W):
        y = y + xp[:, j:j + L] * w[j]
    return jax.nn.silu(y), xp[:, L:]


def gated_delta(q, k, v, g, beta, s0):
    bsz, L, nh, dk = q.shape
    c = GDN_CHUNK if L % GDN_CHUNK == 0 else L
    n = L // c
    def chunks(a):
        a = a.reshape((bsz, n, c) + a.shape[2:])
        return jnp.moveaxis(jnp.moveaxis(a, 3, 2), 1, 0)
    q = chunks(q) * dk ** -0.5
    k, v, g, beta = chunks(k), chunks(v), chunks(g), chunks(beta)
    gc = jnp.cumsum(g, axis=-1)
    tril = jnp.tril(jnp.ones((c, c), dtype=bool))
    stril = jnp.tril(jnp.ones((c, c), dtype=bool), -1)
    gam = jnp.exp(jnp.where(tril, gc[..., :, None] - gc[..., None, :], -jnp.inf))
    a_mat = jnp.where(stril, beta[..., :, None] * jnp.einsum('nbhtd,nbhjd->nbhtj', k, k) * gam, 0.0)
    lhs = a_mat + jnp.eye(c, dtype=a_mat.dtype)
    solve = functools.partial(lax.linalg.triangular_solve, left_side=True, lower=True, unit_diagonal=True)
    uv = solve(lhs, beta[..., None] * v)
    wk = solve(lhs, (beta * jnp.exp(gc))[..., None] * k)
    qk = jnp.where(tril, jnp.einsum('nbhtd,nbhjd->nbhtj', q, k) * gam, 0.0)
    q_head = jnp.exp(gc)[..., None] * q
    g_last = gc[..., -1]
    k_tail = jnp.exp(g_last[..., None] - gc)[..., None] * k
    def step(s, xs):
        uv_c, wk_c, qk_c, qh_c, kt_c, gl_c = xs
        u = uv_c - jnp.einsum('bhck,bhvk->bhcv', wk_c, s)
        o = jnp.einsum('bhck,bhvk->bhcv', qh_c, s) + jnp.einsum('bhtj,bhjv->bhtv', qk_c, u)
        s = jnp.exp(gl_c)[..., None, None] * s + jnp.einsum('bhcv,bhck->bhvk', u, kt_c)
        return s, o
    s_fin, o = lax.scan(step, s0, (uv, wk, qk, q_head, k_tail, g_last))
    o = jnp.moveaxis(jnp.moveaxis(o, 0, 1), 2, 3).reshape(bsz, L, nh, v.shape[-1])
    return o, s_fin


def gdn_mix(z, conv_ctx, s0, conv_w, a_log, dt_bias, o_norm):
    f32 = jnp.float32
    bsz, L = z.shape[:2]
    qkv, a, b, gate = split_cols(z, GDN_SPLITS)
    qkv, conv_state = causal_conv(qkv, conv_ctx, conv_w)
    q, k, v = jnp.split(qkv.astype(f32), 3, axis=-1)
    hs = (bsz, L, GDN_HEADS, HEAD_DIM)
    q, k, v = l2norm(q.reshape(hs)), l2norm(k.reshape(hs)), v.reshape(hs)
    g = -jnp.exp(a_log.astype(f32)) * jax.nn.softplus((a + dt_bias).astype(f32))
    beta = jax.nn.sigmoid(b.astype(f32))
    o, s_fin = gated_delta(q, k, v, g, beta, s0.astype(f32))
    o = rmsnorm(o.astype(z.dtype), o_norm) * jax.nn.silu(gate.reshape(hs))
    return o.reshape(bsz, L, SEQ_WIDTH), conv_state, s_fin.astype(s0.dtype)


def pre_mix(x, g, ffn, w_in):
    x = x + 0.5 * swiglu(rmsnorm(x, g[0]), *ffn)
    z = rmsnorm(x, g[1]) @ w_in
    return x, z[..., :MEM_WIDTH], z[..., MEM_WIDTH:]


def post_mix(x, o_mem, o_mix, w_out, g, ffn):
    x = x + jnp.concatenate([o_mem, o_mix], axis=-1) @ w_out
    return x + 0.5 * swiglu(rmsnorm(x, g[2]), *ffn)


def setup_inputs(seed: int = 0) -> dict:
    key = jax.random.key(seed)
    keys = iter(jax.random.split(key, 64))
    f32 = jnp.float32
    def nrm(shape, scale=1.0):
        return jax.random.normal(next(keys), shape, f32) * scale
    def unif(shape, lo, hi):
        return jax.random.uniform(next(keys), shape, f32, lo, hi)
    n_pages = PAST_LEN // PAGE_SIZE
    n_phys = (5 * DEC_BATCH * n_pages) // 4
    page_table = jax.random.permutation(next(keys), n_phys)[: DEC_BATCH * n_pages]
    page_table = page_table.reshape(DEC_BATCH, n_pages).astype(jnp.int32)
    dt = jnp.exp(unif((N_LAYERS_C, GDN_HEADS), math.log(1e-3), math.log(1e-1)))
    lam_im = jnp.broadcast_to(jnp.pi * jnp.arange(S5_STATE, dtype=f32), (N_LAYERS_A, S5_GROUPS, S5_STATE))
    return {
        'x_prompt': nrm((BATCH, SEQ, D_MODEL)),
        'x_sample': nrm((DEC_BATCH, DEC_SEQ, D_MODEL)),
        'cache_mem_k': nrm((DEPTH, DEC_BATCH, N_MEM, MEM_HEADS, HEAD_DIM)),
        'cache_mem_v': nrm((DEPTH, DEC_BATCH, N_MEM, MEM_HEADS, HEAD_DIM)),
        'state_ssm_re': nrm((N_LAYERS_A, DEC_BATCH, S5_GROUPS, S5_STATE), 0.1),
        'state_ssm_im': nrm((N_LAYERS_A, DEC_BATCH, S5_GROUPS, S5_STATE), 0.1),
        'cache_k': nrm((N_LAYERS_B, n_phys, PAGE_SIZE, DSA_KV_HEADS, HEAD_DIM)),
        'cache_v': nrm((N_LAYERS_B, n_phys, PAGE_SIZE, DSA_KV_HEADS, HEAD_DIM)),
        'cache_kidx': nrm((N_LAYERS_B, n_phys, PAGE_SIZE, IDX_DIM)),
        'state_conv': nrm((N_LAYERS_C, DEC_BATCH, CONV_W - 1, CONV_CH)),
        'state_delta': nrm((N_LAYERS_C, DEC_BATCH, GDN_HEADS, HEAD_DIM, HEAD_DIM), 0.1),
        'page_table': page_table,
        'mem_prompt': nrm((BATCH, N_MEM, D_MODEL)),
        'norm_g': 1.0 + nrm((DEPTH, 3, D_MODEL), 0.02),
        'final_norm': 1.0 + nrm((D_MODEL,), 0.02),
        'w_in_a': nrm((N_LAYERS_A, D_MODEL, IN_A), D_MODEL ** -0.5),
        'w_in_b': nrm((N_LAYERS_B, D_MODEL, IN_B), D_MODEL ** -0.5),
        'w_in_c': nrm((N_LAYERS_C, D_MODEL, IN_C), D_MODEL ** -0.5),
        'w_out': nrm((DEPTH, MIX_WIDTH, D_MODEL), MIX_WIDTH ** -0.5),
        'w_mem_kv': nrm((DEPTH, D_MODEL, 2 * MEM_WIDTH), D_MODEL ** -0.5),
        'ffn1_gate': nrm((DEPTH, D_MODEL, D_FF), D_MODEL ** -0.5),
        'ffn1_up': nrm((DEPTH, D_MODEL, D_FF), D_MODEL ** -0.5),
        'ffn1_down': nrm((DEPTH, D_FF, D_MODEL), D_FF ** -0.5),
        'ffn2_gate': nrm((DEPTH, D_MODEL, D_FF), D_MODEL ** -0.5),
        'ffn2_up': nrm((DEPTH, D_MODEL, D_FF), D_MODEL ** -0.5),
        'ffn2_down': nrm((DEPTH, D_FF, D_MODEL), D_FF ** -0.5),
        's5_lam_re': -0.5 + nrm((N_LAYERS_A, S5_GROUPS, S5_STATE), 0.01),
        's5_lam_im': lam_im,
        's5_log_step': unif((N_LAYERS_A, S5_GROUPS), math.log(1e-3), math.log(1e-1)),
        's5_b_re': nrm((N_LAYERS_A, S5_GROUPS, S5_STATE, S5_GROUP), (2 * S5_GROUP) ** -0.5),
        's5_b_im': nrm((N_LAYERS_A, S5_GROUPS, S5_STATE, S5_GROUP), (2 * S5_GROUP) ** -0.5),
        's5_c_re': nrm((N_LAYERS_A, S5_GROUPS, S5_GROUP, S5_STATE), S5_STATE ** -0.5),
        's5_c_im': nrm((N_LAYERS_A, S5_GROUPS, S5_GROUP, S5_STATE), S5_STATE ** -0.5),
        's5_d': nrm((N_LAYERS_A, SEQ_WIDTH), 0.5),
        's5_w_glu': nrm((N_LAYERS_A, SEQ_WIDTH, SEQ_WIDTH), SEQ_WIDTH ** -0.5),
        's5_b_glu': nrm((N_LAYERS_A, SEQ_WIDTH), 0.01),
        'rel_bias': nrm((N_BUCKETS, DSA_HEADS), 0.5),
        'gdn_conv_w': nrm((N_LAYERS_C, CONV_W, CONV_CH), CONV_W ** -0.5),
        'gdn_a_log': jnp.log(unif((N_LAYERS_C, GDN_HEADS), 1.0, 16.0)),
        'gdn_dt_bias': dt + jnp.log(-jnp.expm1(-dt)),
        'gdn_o_norm': 1.0 + nrm((N_LAYERS_C, HEAD_DIM), 0.02),
    }


def reference(x_prompt, x_sample, cache_mem_k, cache_mem_v, state_ssm_re, state_ssm_im,
              cache_k, cache_v, cache_kidx, state_conv, state_delta, page_table, mem_prompt,
              norm_g, final_norm, w_in_a, w_in_b, w_in_c, w_out, w_mem_kv,
              ffn1_gate, ffn1_up, ffn1_down, ffn2_gate, ffn2_up, ffn2_down,
              s5_lam_re, s5_lam_im, s5_log_step, s5_b_re, s5_b_im, s5_c_re, s5_c_im,
              s5_d, s5_w_glu, s5_b_glu, rel_bias, gdn_conv_w, gdn_a_log, gdn_dt_bias, gdn_o_norm):
    bp, bs = x_prompt.shape[0], x_sample.shape[0]
    past = page_table.shape[1] * PAGE_SIZE
    hp, hs = x_prompt, x_sample
    mem_k_p, mem_v_p = [], []
    ssm_re_p, ssm_im_p, ssm_re_s, ssm_im_s = [], [], [], []
    k_p, v_p, ki_p, k_s, v_s, ki_s = [], [], [], [], [], []
    conv_p, delta_p, conv_s, delta_s = [], [], [], []
    w_in_by_kind = (w_in_a, w_in_b, w_in_c)
    for i in range(DEPTH):
        kind, j = i % N_MIXERS, i // N_MIXERS
        w_in = w_in_by_kind[kind][j]
        ffn1 = (ffn1_gate[i], ffn1_up[i], ffn1_down[i])
        ffn2 = (ffn2_gate[i], ffn2_up[i], ffn2_down[i])
        hp, cq_p, zp = pre_mix(hp, norm_g[i], ffn1, w_in)
        hs, cq_s, zs = pre_mix(hs, norm_g[i], ffn1, w_in)
        mk, mv = mem_kv(mem_prompt, w_mem_kv[i])
        mem_k_p.append(mk)
        mem_v_p.append(mv)
        om_p = mem_attend(cq_p, mk, mv)
        om_s = mem_attend(cq_s, cache_mem_k[i], cache_mem_v[i])
        if kind == 0:
            s5 = (s5_lam_re[j], s5_lam_im[j], s5_log_step[j], s5_b_re[j], s5_b_im[j],
                  s5_c_re[j], s5_c_im[j], s5_d[j], s5_w_glu[j], s5_b_glu[j])
            h0 = jnp.zeros((bp, S5_GROUPS, S5_STATE), state_ssm_re.dtype)
            op, hr, hi = s5_mix(zp, h0, h0, *s5)
            ssm_re_p.append(hr)
            ssm_im_p.append(hi)
            osm, hr, hi = s5_mix(zs, state_ssm_re[j], state_ssm_im[j], *s5)
            ssm_re_s.append(hr)
            ssm_im_s.append(hi)
        elif kind == 1:
            q, k, v, qi, ki, wi = dsa_heads(zp)
            qpos = jnp.arange(zp.shape[1], dtype=jnp.int32)
            op = dsa_attend(q, qi, wi, qpos, ki, local_gather(k, v), rel_bias)
            k_p.append(k)
            v_p.append(v)
            ki_p.append(ki)
            q, k, v, qi, ki, wi = dsa_heads(zs)
            ki_all = jnp.concatenate([cache_kidx[j][page_table].reshape(bs, past, IDX_DIM), ki], axis=1)
            qpos = past + jnp.arange(zs.shape[1], dtype=jnp.int32)
            gather = paged_gather(cache_k[j], cache_v[j], page_table, k, v)
            osm = dsa_attend(q, qi, wi, qpos, ki_all, gather, rel_bias)
            k_s.append(k)
            v_s.append(v)
            ki_s.append(ki)
        else:
            gdn = (gdn_conv_w[j], gdn_a_log[j], gdn_dt_bias[j], gdn_o_norm[j])
            ctx0 = jnp.zeros((bp, CONV_W - 1, CONV_CH), zp.dtype)
            s00 = jnp.zeros((bp, GDN_HEADS, HEAD_DIM, HEAD_DIM), state_delta.dtype)
            op, cst, sst = gdn_mix(zp, ctx0, s00, *gdn)
            conv_p.append(cst)
            delta_p.append(sst)
            osm, cst, sst = gdn_mix(zs, state_conv[j], state_delta[j], *gdn)
            conv_s.append(cst)
            delta_s.append(sst)
        hp = post_mix(hp, om_p, op, w_out[i], norm_g[i], ffn2)
        hs = post_mix(hs, om_s, osm, w_out[i], norm_g[i], ffn2)
    y_prompt = rmsnorm(hp, final_norm)
    y_sample = rmsnorm(hs, final_norm)
    new_mem_k_prompt = jnp.stack(mem_k_p)
    new_mem_v_prompt = jnp.stack(mem_v_p)
    new_ssm_re_prompt = jnp.stack(ssm_re_p)
    new_ssm_im_prompt = jnp.stack(ssm_im_p)
    new_ssm_re_sample = jnp.stack(ssm_re_s)
    new_ssm_im_sample = jnp.stack(ssm_im_s)
    new_k_prompt = jnp.stack(k_p)
    new_v_prompt = jnp.stack(v_p)
    new_kidx_prompt = jnp.stack(ki_p)
    new_k_sample = jnp.stack(k_s)
    new_v_sample = jnp.stack(v_s)
    new_kidx_sample = jnp.stack(ki_s)
    new_conv_prompt = jnp.stack(conv_p)
    new_delta_prompt = jnp.stack(delta_p)
    new_conv_sample = jnp.stack(conv_s)
    new_delta_sample = jnp.stack(delta_s)
    return (y_prompt, y_sample, new_mem_k_prompt, new_mem_v_prompt,
            new_ssm_re_prompt, new_ssm_im_prompt, new_ssm_re_sample, new_ssm_im_sample,
            new_k_prompt, new_v_prompt, new_kidx_prompt, new_k_sample, new_v_sample, new_kidx_sample,
            new_conv_prompt, new_delta_prompt, new_conv_sample, new_delta_sample)
```

```python
import math
import functools
import jax
import jax.numpy as jnp
from jax import lax
import numpy as np
from jax.experimental import pallas as pl
from jax.experimental.pallas import tpu as pltpu

D_MODEL = 1024
N_MIXERS = 3
HEAD_DIM = 64
MIX_WIDTH = D_MODEL
N_MEM = 256
MEM_HEADS = 4
MEM_WIDTH = MEM_HEADS * HEAD_DIM
SEQ_WIDTH = MIX_WIDTH - MEM_WIDTH
S5_GROUP = 16
S5_GROUPS = SEQ_WIDTH // S5_GROUP
S5_STATE = 64
DSA_HEADS = SEQ_WIDTH // HEAD_DIM
DSA_KV_HEADS = 4
DSA_GQA = DSA_HEADS // DSA_KV_HEADS
IDX_HEADS = 8
IDX_DIM = 64
TOPK_MAX = 256
QBLOCK = 128
N_BUCKETS = 32
MAX_DISTANCE = 128
GDN_HEADS = SEQ_WIDTH // HEAD_DIM
CONV_W = 4
CONV_CH = 3 * SEQ_WIDTH
GDN_CHUNK = 64
D_FF = 2816
EPS = 1e-6
PAGE_SIZE = 128
DSA_SPLITS = (DSA_HEADS * HEAD_DIM, DSA_KV_HEADS * HEAD_DIM, DSA_KV_HEADS * HEAD_DIM, IDX_HEADS * IDX_DIM, IDX_DIM, IDX_HEADS)
GDN_SPLITS = (CONV_CH, GDN_HEADS, GDN_HEADS, SEQ_WIDTH)

VMEM_LIMIT_BYTES = 48 * 1024 * 1024


def split_cols(z, sizes):
    return jnp.split(z, np.cumsum(sizes)[:-1].tolist(), axis=-1)


def rmsnorm(x, g):
    x32 = x.astype(jnp.float32)
    r = lax.rsqrt(jnp.mean(x32 * x32, axis=-1, keepdims=True) + EPS)
    return (x32 * r).astype(x.dtype) * g


def l2norm(x):
    return x * lax.rsqrt(jnp.sum(x * x, axis=-1, keepdims=True) + EPS)


def take_rows(rows, idx):
    return jax.vmap(lambda r, i: r[i])(rows, idx)


def mem_kv(mem, w_kv):
    k, v = jnp.split(mem @ w_kv, 2, axis=-1)
    shp = (mem.shape[0], mem.shape[1], MEM_HEADS, HEAD_DIM)
    return k.reshape(shp), v.reshape(shp)


def mem_attend(cq, mk, mv):
    bsz, lq = cq.shape[:2]
    q = cq.reshape(bsz, lq, MEM_HEADS, HEAD_DIM)
    logits = jnp.einsum('bqhd,bmhd->bhqm', q, mk).astype(jnp.float32) * HEAD_DIM ** -0.5
    p = jax.nn.softmax(logits, axis=-1).astype(mv.dtype)
    return jnp.einsum('bhqm,bmhd->bqhd', p, mv).reshape(bsz, lq, MEM_WIDTH)


def complex_affine_combine(x, y):
    ar1, ai1, br1, bi1 = x
    ar2, ai2, br2, bi2 = y
    return (ar1 * ar2 - ai1 * ai2, ar1 * ai2 + ai1 * ar2,
            ar2 * br1 - ai2 * bi1 + br2, ar2 * bi1 + ai2 * br1 + bi2)


def s5_mix(u, h0_re, h0_im, lam_re, lam_im, log_step, b_re, b_im, c_re, c_im, d_skip, w_glu, b_glu):
    f32 = jnp.float32
    bsz, L, _ = u.shape
    lr, li = lam_re.astype(f32), lam_im.astype(f32)
    step = jnp.exp(log_step.astype(f32))[:, None]
    mag = jnp.exp(lr * step)
    ab_re, ab_im = mag * jnp.cos(li * step), mag * jnp.sin(li * step)
    den = lr * lr + li * li
    nr, ni = ab_re - 1.0, ab_im
    f_re = (nr * lr + ni * li) / den
    f_im = (ni * lr - nr * li) / den
    br, bi = b_re.astype(f32), b_im.astype(f32)
    bb_re = f_re[..., None] * br - f_im[..., None] * bi
    bb_im = f_re[..., None] * bi + f_im[..., None] * br
    ug = u.astype(f32).reshape(bsz, L, S5_GROUPS, S5_GROUP)
    bu_re = jnp.einsum('blgc,gpc->blgp', ug, bb_re)
    bu_im = jnp.einsum('blgc,gpc->blgp', ug, bb_im)
    e_re = jnp.concatenate([h0_re.astype(f32)[:, None], bu_re], axis=1)
    e_im = jnp.concatenate([h0_im.astype(f32)[:, None], bu_im], axis=1)
    a_re = jnp.broadcast_to(ab_re, e_re.shape)
    a_im = jnp.broadcast_to(ab_im, e_im.shape)
    _, _, h_re, h_im = lax.associative_scan(complex_affine_combine, (a_re, a_im, e_re, e_im), axis=1)
    h_re, h_im = h_re[:, 1:], h_im[:, 1:]
    y = (jnp.einsum('gcp,blgp->blgc', c_re.astype(f32), h_re)
         - jnp.einsum('gcp,blgp->blgc', c_im.astype(f32), h_im))
    y = y.reshape(bsz, L, SEQ_WIDTH).astype(u.dtype) + d_skip * u
    y = jax.nn.gelu(y)
    y = y * jax.nn.sigmoid(y @ w_glu + b_glu)
    return y, h_re[:, -1].astype(h0_re.dtype), h_im[:, -1].astype(h0_im.dtype)


def t5_bucket(dist):
    max_exact = N_BUCKETS // 2
    n = jnp.maximum(dist, 0)
    nf = jnp.maximum(n, 1).astype(jnp.float32)
    large = max_exact + (jnp.log(nf / max_exact) / math.log(MAX_DISTANCE / max_exact)
                         * (N_BUCKETS - max_exact)).astype(jnp.int32)
    return jnp.where(n < max_exact, n, jnp.minimum(large, N_BUCKETS - 1))


def dsa_heads(z):
    bsz, L = z.shape[:2]
    q, k, v, qi, ki, wi = split_cols(z, DSA_SPLITS)
    return (q.reshape(bsz, L, DSA_KV_HEADS, DSA_GQA, HEAD_DIM),
            k.reshape(bsz, L, DSA_KV_HEADS, HEAD_DIM),
            v.reshape(bsz, L, DSA_KV_HEADS, HEAD_DIM),
            qi.reshape(bsz, L, IDX_HEADS, IDX_DIM), ki, wi * IDX_HEADS ** -0.5)


def dsa_block(q, qi, wi, qpos, ki_all, gather_kv, rel_bias, n_top):
    dots = jnp.einsum('bqhd,bsd->bqhs', qi, ki_all) * IDX_DIM ** -0.5
    score = jnp.einsum('bqh,bqhs->bqs', wi, jax.nn.relu(dots)).astype(jnp.float32)
    kpos = jnp.arange(ki_all.shape[1], dtype=jnp.int32)
    score = jnp.where(kpos[None, None, :] <= qpos[None, :, None], score, -jnp.inf)
    _, idx = lax.top_k(score, n_top)
    valid = idx <= qpos[None, :, None]
    k_sel, v_sel = gather_kv(idx)
    logits = jnp.einsum('bqkgd,bqnkd->bqkgn', q, k_sel).astype(jnp.float32) * HEAD_DIM ** -0.5
    bias = rel_bias[t5_bucket(qpos[None, :, None] - idx)].astype(jnp.float32)
    bias = bias.reshape(bias.shape[:3] + (DSA_KV_HEADS, DSA_GQA))
    logits = logits + jnp.moveaxis(bias, 2, -1)
    logits = jnp.where(valid[:, :, None, None, :], logits, -jnp.inf)
    p = jax.nn.softmax(logits, axis=-1).astype(v_sel.dtype)
    return jnp.einsum('bqkgn,bqnkd->bqkgd', p, v_sel)


def dsa_attend(q, qi, wi, qpos, ki_all, gather_kv, rel_bias):
    bsz, lq = q.shape[:2]
    n_top = min(TOPK_MAX, ki_all.shape[1] // 4)
    block = functools.partial(dsa_block, ki_all=ki_all, gather_kv=gather_kv, rel_bias=rel_bias, n_top=n_top)
    if lq <= QBLOCK or lq % QBLOCK != 0:
        o = block(q, qi, wi, qpos)
    else:
        nb = lq // QBLOCK
        def to_blocks(a):
            return jnp.moveaxis(a.reshape((bsz, nb, QBLOCK) + a.shape[2:]), 1, 0)
        o = lax.map(lambda a: block(*a), (to_blocks(q), to_blocks(qi), to_blocks(wi), qpos.reshape(nb, QBLOCK)))
        o = jnp.moveaxis(o, 0, 1)
    return o.reshape(bsz, lq, SEQ_WIDTH)


def local_gather(k, v):
    return lambda idx: (take_rows(k, idx), take_rows(v, idx))


def paged_gather(pool_k, pool_v, page_table, k_new, v_new):
    past = page_table.shape[1] * PAGE_SIZE
    lnew = k_new.shape[1]
    def gather(idx):
        ic = jnp.minimum(idx, past - 1)
        phys = take_rows(page_table, ic // PAGE_SIZE)
        off = ic % PAGE_SIZE
        inew = jnp.clip(idx - past, 0, lnew - 1)
        in_past = (idx < past)[..., None, None]
        return (jnp.where(in_past, pool_k[phys, off], take_rows(k_new, inew)),
                jnp.where(in_past, pool_v[phys, off], take_rows(v_new, inew)))
    return gather


def causal_conv(x, ctx, w):
    L = x.shape[1]
    xp = jnp.concatenate([ctx, x], axis=1)
    y = xp[:, 0:L] * w[0]
    for j in range(1, CONV_W):
        y = y + xp[:, j:j + L] * w[j]
    return jax.nn.silu(y), xp[:, L:]


def gated_delta(q, k, v, g, beta, s0):
    bsz, L, nh, dk = q.shape
    c = GDN_CHUNK if L % GDN_CHUNK == 0 else L
    n = L // c
    def chunks(a):
        a = a.reshape((bsz, n, c) + a.shape[2:])
        return jnp.moveaxis(jnp.moveaxis(a, 3, 2), 1, 0)
    q = chunks(q) * dk ** -0.5
    k, v, g, beta = chunks(k), chunks(v), chunks(g), chunks(beta)
    gc = jnp.cumsum(g, axis=-1)
    tril = jnp.tril(jnp.ones((c, c), dtype=bool))
    stril = jnp.tril(jnp.ones((c, c), dtype=bool), -1)
    gam = jnp.exp(jnp.where(tril, gc[..., :, None] - gc[..., None, :], -jnp.inf))
    a_mat = jnp.where(stril, beta[..., :, None] * jnp.einsum('nbhtd,nbhjd->nbhtj', k, k) * gam, 0.0)
    lhs = a_mat + jnp.eye(c, dtype=a_mat.dtype)
    solve = functools.partial(lax.linalg.triangular_solve, left_side=True, lower=True, unit_diagonal=True)
    uv = solve(lhs, beta[..., None] * v)
    wk = solve(lhs, (beta * jnp.exp(gc))[..., None] * k)
    qk = jnp.where(tril, jnp.einsum('nbhtd,nbhjd->nbhtj', q, k) * gam, 0.0)
    q_head = jnp.exp(gc)[..., None] * q
    g_last = gc[..., -1]
    k_tail = jnp.exp(g_last[..., None] - gc)[..., None] * k
    def step(s, xs):
        uv_c, wk_c, qk_c, qh_c, kt_c, gl_c = xs
        u = uv_c - jnp.einsum('bhck,bhvk->bhcv', wk_c, s)
        o = jnp.einsum('bhck,bhvk->bhcv', qh_c, s) + jnp.einsum('bhtj,bhjv->bhtv', qk_c, u)
        s = jnp.exp(gl_c)[..., None, None] * s + jnp.einsum('bhcv,bhck->bhvk', u, kt_c)
        return s, o
    s_fin, o = lax.scan(step, s0, (uv, wk, qk, q_head, k_tail, g_last))
    o = jnp.moveaxis(jnp.moveaxis(o, 0, 1), 2, 3).reshape(bsz, L, nh, v.shape[-1])
    return o, s_fin


def gdn_mix(z, conv_ctx, s0, conv_w, a_log, dt_bias, o_norm):
    f32 = jnp.float32
    bsz, L = z.shape[:2]
    qkv, a, b, gate = split_cols(z, GDN_SPLITS)
    qkv, conv_state = causal_conv(qkv, conv_ctx, conv_w)
    q, k, v = jnp.split(qkv.astype(f32), 3, axis=-1)
    hs = (bsz, L, GDN_HEADS, HEAD_DIM)
    q, k, v = l2norm(q.reshape(hs)), l2norm(k.reshape(hs)), v.reshape(hs)
    g = -jnp.exp(a_log.astype(f32)) * jax.nn.softplus((a + dt_bias).astype(f32))
    beta = jax.nn.sigmoid(b.astype(f32))
    o, s_fin = gated_delta(q, k, v, g, beta, s0.astype(f32))
    o = rmsnorm(o.astype(z.dtype), o_norm) * jax.nn.silu(gate.reshape(hs))
    return o.reshape(bsz, L, SEQ_WIDTH), conv_state, s_fin.astype(s0.dtype)


def _ffn_body(x_ref, g_ref, wg_ref, wu_ref, wd_ref, o_ref, n_ref, acc_ref):
    f = pl.program_id(1)

    @pl.when(f == 0)
    def _():
        x = x_ref[...]
        r = lax.rsqrt(jnp.mean(x * x, axis=-1, keepdims=True) + EPS)
        n_ref[...] = ((x * r) * g_ref[...]).astype(jnp.bfloat16)
        acc_ref[...] = jnp.zeros_like(acc_ref)

    n = n_ref[...]
    a = jnp.dot(n, wg_ref[...], preferred_element_type=jnp.float32)
    b = jnp.dot(n, wu_ref[...], preferred_element_type=jnp.float32)
    h = (a * (1.0 / (1.0 + jnp.exp(-a)))) * b
    acc_ref[...] += jnp.dot(h.astype(jnp.bfloat16), wd_ref[...], preferred_element_type=jnp.float32)

    @pl.when(f == pl.num_programs(1) - 1)
    def _():
        o_ref[...] = x_ref[...] + 0.5 * acc_ref[...]


def ffn_residual(x, g, wg, wu, wd, *, tm, tf):
    t, d = x.shape
    f = wg.shape[1]
    return pl.pallas_call(
        _ffn_body,
        grid=(t // tm, f // tf),
        in_specs=[
            pl.BlockSpec((tm, d), lambda i, j: (i, 0)),
            pl.BlockSpec((1, d), lambda i, j: (0, 0)),
            pl.BlockSpec((d, tf), lambda i, j: (0, j)),
            pl.BlockSpec((d, tf), lambda i, j: (0, j)),
            pl.BlockSpec((tf, d), lambda i, j: (j, 0)),
        ],
        out_specs=pl.BlockSpec((tm, d), lambda i, j: (i, 0)),
        out_shape=jax.ShapeDtypeStruct((t, d), jnp.float32),
        scratch_shapes=[pltpu.VMEM((tm, d), jnp.bfloat16), pltpu.VMEM((tm, d), jnp.float32)],
        compiler_params=pltpu.CompilerParams(
            dimension_semantics=("parallel", "arbitrary"), vmem_limit_bytes=VMEM_LIMIT_BYTES),
        name="ffn_residual",
    )(x, g.reshape(1, d), wg, wu, wd)


def _ffn_tokens(x, g, ffn):
    b, l, d = x.shape
    t = b * l
    tm = 512 if t % 512 == 0 else t
    return ffn_residual(x.reshape(t, d), g, *ffn, tm=tm, tf=256).reshape(b, l, d)


def pre_mix(x, g, ffn, w_in):
    x = _ffn_tokens(x, g[0], ffn)
    z = rmsnorm(x, g[1]) @ w_in
    return x, z[..., :MEM_WIDTH], z[..., MEM_WIDTH:]


def post_mix(x, o_mem, o_mix, w_out, g, ffn):
    x = x + jnp.concatenate([o_mem, o_mix], axis=-1) @ w_out
    return _ffn_tokens(x, g[2], ffn)


def kernel(x_prompt, x_sample, cache_mem_k, cache_mem_v, state_ssm_re, state_ssm_im, cache_k, cache_v, cache_kidx, state_conv, state_delta, page_table, mem_prompt, norm_g, final_norm, w_in_a, w_in_b, w_in_c, w_out, w_mem_kv, ffn1_gate, ffn1_up, ffn1_down, ffn2_gate, ffn2_up, ffn2_down, s5_lam_re, s5_lam_im, s5_log_step, s5_b_re, s5_b_im, s5_c_re, s5_c_im, s5_d, s5_w_glu, s5_b_glu, rel_bias, gdn_conv_w, gdn_a_log, gdn_dt_bias, gdn_o_norm):
    depth = norm_g.shape[0]
    bp, bs = x_prompt.shape[0], x_sample.shape[0]
    past = page_table.shape[1] * PAGE_SIZE
    bf16 = jnp.bfloat16
    hp, hs = x_prompt, x_sample
    mem_k_p, mem_v_p = [], []
    ssm_re_p, ssm_im_p, ssm_re_s, ssm_im_s = [], [], [], []
    k_p, v_p, ki_p, k_s, v_s, ki_s = [], [], [], [], [], []
    conv_p, delta_p, conv_s, delta_s = [], [], [], []
    w_in_by_kind = (w_in_a, w_in_b, w_in_c)
    for i in range(depth):
        kind, j = i % N_MIXERS, i // N_MIXERS
        w_in = w_in_by_kind[kind][j]
        ffn1 = (ffn1_gate[i].astype(bf16), ffn1_up[i].astype(bf16), ffn1_down[i].astype(bf16))
        ffn2 = (ffn2_gate[i].astype(bf16), ffn2_up[i].astype(bf16), ffn2_down[i].astype(bf16))
        hp, cq_p, zp = pre_mix(hp, norm_g[i], ffn1, w_in)
        hs, cq_s, zs = pre_mix(hs, norm_g[i], ffn1, w_in)
        mk, mv = mem_kv(mem_prompt, w_mem_kv[i])
        mem_k_p.append(mk)
        mem_v_p.append(mv)
        om_p = mem_attend(cq_p, mk, mv)
        om_s = mem_attend(cq_s, cache_mem_k[i], cache_mem_v[i])
        if kind == 0:
            s5 = (s5_lam_re[j], s5_lam_im[j], s5_log_step[j], s5_b_re[j], s5_b_im[j],
                  s5_c_re[j], s5_c_im[j], s5_d[j], s5_w_glu[j], s5_b_glu[j])
            h0 = jnp.zeros((bp, S5_GROUPS, S5_STATE), state_ssm_re.dtype)
            op, hr, hi = s5_mix(zp, h0, h0, *s5)
            ssm_re_p.append(hr)
            ssm_im_p.append(hi)
            osm, hr, hi = s5_mix(zs, state_ssm_re[j], state_ssm_im[j], *s5)
            ssm_re_s.append(hr)
            ssm_im_s.append(hi)
        elif kind == 1:
            q, k, v, qi, ki, wi = dsa_heads(zp)
            qpos = jnp.arange(zp.shape[1], dtype=jnp.int32)
            op = dsa_attend(q, qi, wi, qpos, ki, local_gather(k, v), rel_bias)
            k_p.append(k)
            v_p.append(v)
            ki_p.append(ki)
            q, k, v, qi, ki, wi = dsa_heads(zs)
            ki_all = jnp.concatenate([cache_kidx[j][page_table].reshape(bs, past, IDX_DIM), ki], axis=1)
            qpos = past + jnp.arange(zs.shape[1], dtype=jnp.int32)
            gather = paged_gather(cache_k[j], cache_v[j], page_table, k, v)
            osm = dsa_attend(q, qi, wi, qpos, ki_all, gather, rel_bias)
            k_s.append(k)
            v_s.append(v)
            ki_s.append(ki)
        else:
            gdn = (gdn_conv_w[j], gdn_a_log[j], gdn_dt_bias[j], gdn_o_norm[j])
            ctx0 = jnp.zeros((bp, CONV_W - 1, CONV_CH), zp.dtype)
            s00 = jnp.zeros((bp, GDN_HEADS, HEAD_DIM, HEAD_DIM), state_delta.dtype)
            op, cst, sst = gdn_mix(zp, ctx0, s00, *gdn)
            conv_p.append(cst)
            delta_p.append(sst)
            osm, cst, sst = gdn_mix(zs, state_conv[j], state_delta[j], *gdn)
            conv_s.append(cst)
            delta_s.append(sst)
        hp = post_mix(hp, om_p, op, w_out[i], norm_g[i], ffn2)
        hs = post_mix(hs, om_s, osm, w_out[i], norm_g[i], ffn2)
    y_prompt = rmsnorm(hp, final_norm)
    y_sample = rmsnorm(hs, final_norm)
    st = jnp.stack
    return (y_prompt, y_sample, st(mem_k_p), st(mem_v_p),
            st(ssm_re_p), st(ssm_im_p), st(ssm_re_s), st(ssm_im_s),
            st(k_p), st(v_p), st(ki_p), st(k_s), st(v_s), st(ki_s),
            st(conv_p), st(delta_p), st(conv_s), st(delta_s))
```

```python
import math
import functools
import jax
import jax.numpy as jnp
from jax import lax
import numpy as np
from jax.experimental import pallas as pl
from jax.experimental.pallas import tpu as pltpu

D_MODEL = 1024
N_MIXERS = 3
HEAD_DIM = 64
MIX_WIDTH = D_MODEL
N_MEM = 256
MEM_HEADS = 4
MEM_WIDTH = MEM_HEADS * HEAD_DIM
SEQ_WIDTH = MIX_WIDTH - MEM_WIDTH
S5_GROUP = 16
S5_GROUPS = SEQ_WIDTH // S5_GROUP
S5_STATE = 64
DSA_HEADS = SEQ_WIDTH // HEAD_DIM
DSA_KV_HEADS = 4
DSA_GQA = DSA_HEADS // DSA_KV_HEADS
IDX_HEADS = 8
IDX_DIM = 64
TOPK_MAX = 256
QBLOCK = 128
N_BUCKETS = 32
MAX_DISTANCE = 128
GDN_HEADS = SEQ_WIDTH // HEAD_DIM
CONV_W = 4
CONV_CH = 3 * SEQ_WIDTH
GDN_CHUNK = 64
D_FF = 2816
EPS = 1e-6
PAGE_SIZE = 128
DSA_SPLITS = (DSA_HEADS * HEAD_DIM, DSA_KV_HEADS * HEAD_DIM, DSA_KV_HEADS * HEAD_DIM, IDX_HEADS * IDX_DIM, IDX_DIM, IDX_HEADS)
GDN_SPLITS = (CONV_CH, GDN_HEADS, GDN_HEADS, SEQ_WIDTH)

VMEM_LIMIT_BYTES = 48 * 1024 * 1024


def split_cols(z, sizes):
    return jnp.split(z, np.cumsum(sizes)[:-1].tolist(), axis=-1)


def rmsnorm(x, g):
    x32 = x.astype(jnp.float32)
    r = lax.rsqrt(jnp.mean(x32 * x32, axis=-1, keepdims=True) + EPS)
    return (x32 * r).astype(x.dtype) * g


def l2norm(x):
    return x * lax.rsqrt(jnp.sum(x * x, axis=-1, keepdims=True) + EPS)


def take_rows(rows, idx):
    return jax.vmap(lambda r, i: r[i])(rows, idx)


def mem_kv(mem, w_kv):
    k, v = jnp.split(mem @ w_kv, 2, axis=-1)
    shp = (mem.shape[0], mem.shape[1], MEM_HEADS, HEAD_DIM)
    return k.reshape(shp), v.reshape(shp)


def mem_attend(cq, mk, mv):
    bsz, lq = cq.shape[:2]
    q = cq.reshape(bsz, lq, MEM_HEADS, HEAD_DIM)
    logits = jnp.einsum('bqhd,bmhd->bhqm', q, mk).astype(jnp.float32) * HEAD_DIM ** -0.5
    p = jax.nn.softmax(logits, axis=-1).astype(mv.dtype)
    return jnp.einsum('bhqm,bmhd->bqhd', p, mv).reshape(bsz, lq, MEM_WIDTH)


def complex_affine_combine(x, y):
    ar1, ai1, br1, bi1 = x
    ar2, ai2, br2, bi2 = y
    return (ar1 * ar2 - ai1 * ai2, ar1 * ai2 + ai1 * ar2,
            ar2 * br1 - ai2 * bi1 + br2, ar2 * bi1 + ai2 * br1 + bi2)


def s5_mix(u, h0_re, h0_im, lam_re, lam_im, log_step, b_re, b_im, c_re, c_im, d_skip, w_glu, b_glu):
    f32 = jnp.float32
    bsz, L, _ = u.shape
    lr, li = lam_re.astype(f32), lam_im.astype(f32)
    step = jnp.exp(log_step.astype(f32))[:, None]
    mag = jnp.exp(lr * step)
    ab_re, ab_im = mag * jnp.cos(li * step), mag * jnp.sin(li * step)
    den = lr * lr + li * li
    nr, ni = ab_re - 1.0, ab_im
    f_re = (nr * lr + ni * li) / den
    f_im = (ni * lr - nr * li) / den
    br, bi = b_re.astype(f32), b_im.astype(f32)
    bb_re = f_re[..., None] * br - f_im[..., None] * bi
    bb_im = f_re[..., None] * bi + f_im[..., None] * br
    ug = u.astype(f32).reshape(bsz, L, S5_GROUPS, S5_GROUP)
    bu_re = jnp.einsum('blgc,gpc->blgp', ug, bb_re)
    bu_im = jnp.einsum('blgc,gpc->blgp', ug, bb_im)
    e_re = jnp.concatenate([h0_re.astype(f32)[:, None], bu_re], axis=1)
    e_im = jnp.concatenate([h0_im.astype(f32)[:, None], bu_im], axis=1)
    a_re = jnp.broadcast_to(ab_re, e_re.shape)
    a_im = jnp.broadcast_to(ab_im, e_im.shape)
    _, _, h_re, h_im = lax.associative_scan(complex_affine_combine, (a_re, a_im, e_re, e_im), axis=1)
    h_re, h_im = h_re[:, 1:], h_im[:, 1:]
    y = (jnp.einsum('gcp,blgp->blgc', c_re.astype(f32), h_re)
         - jnp.einsum('gcp,blgp->blgc', c_im.astype(f32), h_im))
    y = y.reshape(bsz, L, SEQ_WIDTH).astype(u.dtype) + d_skip * u
    y = jax.nn.gelu(y)
    y = y * jax.nn.sigmoid(y @ w_glu + b_glu)
    return y, h_re[:, -1].astype(h0_re.dtype), h_im[:, -1].astype(h0_im.dtype)


def t5_bucket(dist):
    max_exact = N_BUCKETS // 2
    n = jnp.maximum(dist, 0)
    nf = jnp.maximum(n, 1).astype(jnp.float32)
    large = max_exact + (jnp.log(nf / max_exact) / math.log(MAX_DISTANCE / max_exact)
                         * (N_BUCKETS - max_exact)).astype(jnp.int32)
    return jnp.where(n < max_exact, n, jnp.minimum(large, N_BUCKETS - 1))


def dsa_heads(z):
    bsz, L = z.shape[:2]
    q, k, v, qi, ki, wi = split_cols(z, DSA_SPLITS)
    return (q.reshape(bsz, L, DSA_KV_HEADS, DSA_GQA, HEAD_DIM),
            k.reshape(bsz, L, DSA_KV_HEADS, HEAD_DIM),
            v.reshape(bsz, L, DSA_KV_HEADS, HEAD_DIM),
            qi.reshape(bsz, L, IDX_HEADS, IDX_DIM), ki, wi * IDX_HEADS ** -0.5)


def dsa_block(q, qi, wi, qpos, ki_all, gather_kv, rel_bias, n_top):
    dots = jnp.einsum('bqhd,bsd->bqhs', qi, ki_all) * IDX_DIM ** -0.5
    score = jnp.einsum('bqh,bqhs->bqs', wi, jax.nn.relu(dots)).astype(jnp.float32)
    kpos = jnp.arange(ki_all.shape[1], dtype=jnp.int32)
    score = jnp.where(kpos[None, None, :] <= qpos[None, :, None], score, -jnp.inf)
    _, idx = lax.top_k(score, n_top)
    valid = idx <= qpos[None, :, None]
    k_sel, v_sel = gather_kv(idx)
    logits = jnp.einsum('bqkgd,bqnkd->bqkgn', q, k_sel).astype(jnp.float32) * HEAD_DIM ** -0.5
    bias = rel_bias[t5_bucket(qpos[None, :, None] - idx)].astype(jnp.float32)
    bias = bias.reshape(bias.shape[:3] + (DSA_KV_HEADS, DSA_GQA))
    logits = logits + jnp.moveaxis(bias, 2, -1)
    logits = jnp.where(valid[:, :, None, None, :], logits, -jnp.inf)
    p = jax.nn.softmax(logits, axis=-1).astype(v_sel.dtype)
    return jnp.einsum('bqkgn,bqnkd->bqkgd', p, v_sel)


def dsa_attend(q, qi, wi, qpos, ki_all, gather_kv, rel_bias):
    bsz, lq = q.shape[:2]
    n_top = min(TOPK_MAX, ki_all.shape[1] // 4)
    block = functools.partial(dsa_block, ki_all=ki_all, gather_kv=gather_kv, rel_bias=rel_bias, n_top=n_top)
    if lq <= QBLOCK or lq % QBLOCK != 0:
        o = block(q, qi, wi, qpos)
    else:
        nb = lq // QBLOCK
        def to_blocks(a):
            return jnp.moveaxis(a.reshape((bsz, nb, QBLOCK) + a.shape[2:]), 1, 0)
        o = lax.map(lambda a: block(*a), (to_blocks(q), to_blocks(qi), to_blocks(wi), qpos.reshape(nb, QBLOCK)))
        o = jnp.moveaxis(o, 0, 1)
    return o.reshape(bsz, lq, SEQ_WIDTH)


def local_gather(k, v):
    return lambda idx: (take_rows(k, idx), take_rows(v, idx))


def paged_gather(pool_k, pool_v, page_table, k_new, v_new):
    past = page_table.shape[1] * PAGE_SIZE
    lnew = k_new.shape[1]
    def gather(idx):
        ic = jnp.minimum(idx, past - 1)
        phys = take_rows(page_table, ic // PAGE_SIZE)
        off = ic % PAGE_SIZE
        inew = jnp.clip(idx - past, 0, lnew - 1)
        in_past = (idx < past)[..., None, None]
        return (jnp.where(in_past, pool_k[phys, off], take_rows(k_new, inew)),
                jnp.where(in_past, pool_v[phys, off], take_rows(v_new, inew)))
    return gather


def causal_conv(x, ctx, w):
    L = x.shape[1]
    xp = jnp.concatenate([ctx, x], axis=1)
    y = xp[:, 0:L] * w[0]
    for j in range(1, CONV_W):
        y = y + xp[:, j:j + L] * w[j]
    return jax.nn.silu(y), xp[:, L:]


def gated_delta(q, k, v, g, beta, s0):
    bsz, L, nh, dk = q.shape
    c = GDN_CHUNK if L % GDN_CHUNK == 0 else L
    n = L // c
    def chunks(a):
        a = a.reshape((bsz, n, c) + a.shape[2:])
        return jnp.moveaxis(jnp.moveaxis(a, 3, 2), 1, 0)
    q = chunks(q) * dk ** -0.5
    k, v, g, beta = chunks(k), chunks(v), chunks(g), chunks(beta)
    gc = jnp.cumsum(g, axis=-1)
    tril = jnp.tril(jnp.ones((c, c), dtype=bool))
    stril = jnp.tril(jnp.ones((c, c), dtype=bool), -1)
    gam = jnp.exp(jnp.where(tril, gc[..., :, None] - gc[..., None, :], -jnp.inf))
    a_mat = jnp.where(stril, beta[..., :, None] * jnp.einsum('nbhtd,nbhjd->nbhtj', k, k) * gam, 0.0)
    lhs = a_mat + jnp.eye(c, dtype=a_mat.dtype)
    solve = functools.partial(lax.linalg.triangular_solve, left_side=True, lower=True, unit_diagonal=True)
    uv = solve(lhs, beta[..., None] * v)
    wk = solve(lhs, (beta * jnp.exp(gc))[..., None] * k)
    qk = jnp.where(tril, jnp.einsum('nbhtd,nbhjd->nbhtj', q, k) * gam, 0.0)
    q_head = jnp.exp(gc)[..., None] * q
    g_last = gc[..., -1]
    k_tail = jnp.exp(g_last[..., None] - gc)[..., None] * k
    def step(s, xs):
        uv_c, wk_c, qk_c, qh_c, kt_c, gl_c = xs
        u = uv_c - jnp.einsum('bhck,bhvk->bhcv', wk_c, s)
        o = jnp.einsum('bhck,bhvk->bhcv', qh_c, s) + jnp.einsum('bhtj,bhjv->bhtv', qk_c, u)
        s = jnp.exp(gl_c)[..., None, None] * s + jnp.einsum('bhcv,bhck->bhvk', u, kt_c)
        return s, o
    s_fin, o = lax.scan(step, s0, (uv, wk, qk, q_head, k_tail, g_last))
    o = jnp.moveaxis(jnp.moveaxis(o, 0, 1), 2, 3).reshape(bsz, L, nh, v.shape[-1])
    return o, s_fin


def gdn_mix(z, conv_ctx, s0, conv_w, a_log, dt_bias, o_norm):
    f32 = jnp.float32
    bsz, L = z.shape[:2]
    qkv, a, b, gate = split_cols(z, GDN_SPLITS)
    qkv, conv_state = causal_conv(qkv, conv_ctx, conv_w)
    q, k, v = jnp.split(qkv.astype(f32), 3, axis=-1)
    hs = (bsz, L, GDN_HEADS, HEAD_DIM)
    q, k, v = l2norm(q.reshape(hs)), l2norm(k.reshape(hs)), v.reshape(hs)
    g = -jnp.exp(a_log.astype(f32)) * jax.nn.softplus((a + dt_bias).astype(f32))
    beta = jax.nn.sigmoid(b.astype(f32))
    o, s_fin = gated_delta(q, k, v, g, beta, s0.astype(f32))
    o = rmsnorm(o.astype(z.dtype), o_norm) * jax.nn.silu(gate.reshape(hs))
    return o.reshape(bsz, L, SEQ_WIDTH), conv_state, s_fin.astype(s0.dtype)


def _ffn_body(x_ref, g_ref, wg_ref, wu_ref, wd_ref, o_ref, n_ref, acc_ref):
    f = pl.program_id(1)

    @pl.when(f == 0)
    def _():
        x = x_ref[...]
        r = lax.rsqrt(jnp.mean(x * x, axis=-1, keepdims=True) + EPS)
        n_ref[...] = ((x * r) * g_ref[...]).astype(jnp.bfloat16)
        acc_ref[...] = jnp.zeros_like(acc_ref)

    n = n_ref[...]
    a = jnp.dot(n, wg_ref[...], preferred_element_type=jnp.float32)
    b = jnp.dot(n, wu_ref[...], preferred_element_type=jnp.float32)
    h = (a * (1.0 / (1.0 + jnp.exp(-a)))) * b
    acc_ref[...] += jnp.dot(h.astype(jnp.bfloat16), wd_ref[...], preferred_element_type=jnp.float32)

    @pl.when(f == pl.num_programs(1) - 1)
    def _():
        o_ref[...] = x_ref[...] + 0.5 * acc_ref[...]


def ffn_residual(x, g, wg, wu, wd, *, tm, tf):
    t, d = x.shape
    f = wg.shape[1]
    return pl.pallas_call(
        _ffn_body,
        grid=(t // tm, f // tf),
        in_specs=[
            pl.BlockSpec((tm, d), lambda i, j: (i, 0)),
            pl.BlockSpec((1, d), lambda i, j: (0, 0)),
            pl.BlockSpec((d, tf), lambda i, j: (0, j)),
            pl.BlockSpec((d, tf), lambda i, j: (0, j)),
            pl.BlockSpec((tf, d), lambda i, j: (j, 0)),
        ],
        out_specs=pl.BlockSpec((tm, d), lambda i, j: (i, 0)),
        out_shape=jax.ShapeDtypeStruct((t, d), jnp.float32),
        scratch_shapes=[pltpu.VMEM((tm, d), jnp.bfloat16), pltpu.VMEM((tm, d), jnp.float32)],
        compiler_params=pltpu.CompilerParams(
            dimension_semantics=("parallel", "arbitrary"), vmem_limit_bytes=VMEM_LIMIT_BYTES),
        name="ffn_residual",
    )(x, g.reshape(1, d), wg, wu, wd)


LANES = 128
S5_LANES = S5_GROUPS * S5_STATE
S5_GROUPS_PER_BLOCK = LANES // S5_GROUP
S5_BLOCKS = SEQ_WIDTH // LANES
S5_BLOCK_STATES = S5_GROUPS_PER_BLOCK * S5_STATE


def s5_discretize(lam_re, lam_im, log_step, b_re, b_im, c_re, c_im):
    f32, bf16 = jnp.float32, jnp.bfloat16
    lr, li = lam_re.astype(f32), lam_im.astype(f32)
    step = jnp.exp(log_step.astype(f32))[:, None]
    mag = jnp.exp(lr * step)
    ab_re, ab_im = mag * jnp.cos(li * step), mag * jnp.sin(li * step)
    den = lr * lr + li * li
    nr, ni = ab_re - 1.0, ab_im
    f_re = (nr * lr + ni * li) / den
    f_im = (ni * lr - nr * li) / den
    br, bi = b_re.astype(f32), b_im.astype(f32)
    bb_re = f_re[..., None] * br - f_im[..., None] * bi
    bb_im = f_re[..., None] * bi + f_im[..., None] * br
    eye = jnp.eye(S5_GROUPS_PER_BLOCK, dtype=f32)
    nb, gb = S5_BLOCKS, S5_GROUPS_PER_BLOCK

    def in_blocks(bb):
        w = jnp.einsum('jgpc,gh->jgchp', bb.reshape(nb, gb, S5_STATE, S5_GROUP), eye)
        return w.reshape(nb, LANES, S5_BLOCK_STATES).astype(bf16)

    def out_blocks(c):
        w = jnp.einsum('jgop,gh->jgpho', c.astype(f32).reshape(nb, gb, S5_GROUP, S5_STATE), eye)
        return w.reshape(nb, S5_BLOCK_STATES, LANES).astype(bf16)

    return (ab_re.reshape(1, S5_LANES), ab_im.reshape(1, S5_LANES),
            in_blocks(bb_re), in_blocks(bb_im), out_blocks(c_re), out_blocks(-c_im))


def _s5_project_in(u, wbr_ref, wbi_ref, bur_ref, bui_ref):
    ub = u.astype(jnp.bfloat16)
    for j in range(S5_BLOCKS):
        uj = ub[:, j * LANES:(j + 1) * LANES]
        sl = slice(j * S5_BLOCK_STATES, (j + 1) * S5_BLOCK_STATES)
        bur_ref[:, sl] = jnp.dot(uj, wbr_ref[j], preferred_element_type=jnp.float32)
        bui_ref[:, sl] = jnp.dot(uj, wbi_ref[j], preferred_element_type=jnp.float32)


def _s5_project_out(hr_ref, hi_ref, wcr_ref, wci_ref):
    cols = []
    for j in range(S5_BLOCKS):
        sl = slice(j * S5_BLOCK_STATES, (j + 1) * S5_BLOCK_STATES)
        cols.append(jnp.dot(hr_ref[:, sl].astype(jnp.bfloat16), wcr_ref[j], preferred_element_type=jnp.float32)
                    + jnp.dot(hi_ref[:, sl].astype(jnp.bfloat16), wci_ref[j], preferred_element_type=jnp.float32))
    return jnp.concatenate(cols, axis=-1)


def _s5_gate(y_ssm, u, d_ref, wglu_ref, bglu_ref):
    y = y_ssm + d_ref[...] * u
    y = 0.5 * y * (1.0 + jnp.tanh(math.sqrt(2.0 / math.pi) * (y + 0.044715 * (y * y * y))))
    z = jnp.dot(y.astype(jnp.bfloat16), wglu_ref[...], preferred_element_type=jnp.float32) + bglu_ref[...]
    return y * (1.0 / (1.0 + jnp.exp(-z)))


def _s5_seq_body(u_ref, h0r_ref, h0i_ref, ar_ref, ai_ref, wbr_ref, wbi_ref, wcr_ref, wci_ref, d_ref, wglu_ref, bglu_ref,
                 y_ref, hfr_ref, hfi_ref, bur_ref, bui_ref, cr_ref, ci_ref):
    tt = u_ref.shape[0]

    @pl.when(pl.program_id(1) == 0)
    def _():
        cr_ref[...] = h0r_ref[...]
        ci_ref[...] = h0i_ref[...]

    u = u_ref[...]
    _s5_project_in(u, wbr_ref, wbi_ref, bur_ref, bui_ref)
    ar, ai = ar_ref[...], ai_ref[...]

    def step(t, carry):
        hr, hi = carry
        nhr = ar * hr - ai * hi + bur_ref[pl.ds(t, 1), :]
        nhi = ar * hi + ai * hr + bui_ref[pl.ds(t, 1), :]
        bur_ref[pl.ds(t, 1), :] = nhr
        bui_ref[pl.ds(t, 1), :] = nhi
        return nhr, nhi

    hr, hi = lax.fori_loop(0, tt, step, (cr_ref[...], ci_ref[...]))
    cr_ref[...] = hr
    ci_ref[...] = hi
    hfr_ref[...] = hr
    hfi_ref[...] = hi
    y_ref[...] = _s5_gate(_s5_project_out(bur_ref, bui_ref, wcr_ref, wci_ref), u, d_ref, wglu_ref, bglu_ref)


def _s5_step_body(u_ref, h0r_ref, h0i_ref, ar_ref, ai_ref, wbr_ref, wbi_ref, wcr_ref, wci_ref, d_ref, wglu_ref, bglu_ref,
                  y_ref, hfr_ref, hfi_ref, bur_ref, bui_ref):
    u = u_ref[...]
    _s5_project_in(u, wbr_ref, wbi_ref, bur_ref, bui_ref)
    ar, ai = ar_ref[...], ai_ref[...]
    hr, hi = h0r_ref[...], h0i_ref[...]
    nhr = ar * hr - ai * hi + bur_ref[...]
    nhi = ar * hi + ai * hr + bui_ref[...]
    bur_ref[...] = nhr
    bui_ref[...] = nhi
    hfr_ref[...] = nhr
    hfi_ref[...] = nhi
    y_ref[...] = _s5_gate(_s5_project_out(bur_ref, bui_ref, wcr_ref, wci_ref), u, d_ref, wglu_ref, bglu_ref)


def s5_mix_pallas(u, h0_re, h0_im, disc, d_skip, w_glu, b_glu, *, tt=512):
    bsz, L, w = u.shape
    f32 = jnp.float32
    ar, ai, wbr, wbi, wcr, wci = disc
    d2, bg2, wg = d_skip.reshape(1, w).astype(f32), b_glu.reshape(1, w).astype(f32), w_glu.astype(jnp.bfloat16)
    const2 = lambda *_: (0, 0)
    const3 = lambda *_: (0, 0, 0)
    w_specs = [pl.BlockSpec((1, S5_LANES), const2), pl.BlockSpec((1, S5_LANES), const2),
               pl.BlockSpec(wbr.shape, const3), pl.BlockSpec(wbi.shape, const3),
               pl.BlockSpec(wcr.shape, const3), pl.BlockSpec(wci.shape, const3),
               pl.BlockSpec((1, w), const2), pl.BlockSpec((w, w), const2), pl.BlockSpec((1, w), const2)]
    w_args = (ar, ai, wbr, wbi, wcr, wci, d2, wg, bg2)
    if L == 1:
        rows = bsz
        h0r, h0i = h0_re.reshape(rows, S5_LANES).astype(f32), h0_im.reshape(rows, S5_LANES).astype(f32)
        row_spec = lambda n: pl.BlockSpec((rows, n), const2)
        y, hr, hi = pl.pallas_call(
            _s5_step_body,
            grid=(1,),
            in_specs=[row_spec(w), row_spec(S5_LANES), row_spec(S5_LANES)] + w_specs,
            out_specs=[row_spec(w), row_spec(S5_LANES), row_spec(S5_LANES)],
            out_shape=[jax.ShapeDtypeStruct((rows, w), f32), jax.ShapeDtypeStruct((rows, S5_LANES), f32),
                       jax.ShapeDtypeStruct((rows, S5_LANES), f32)],
            scratch_shapes=[pltpu.VMEM((rows, S5_LANES), f32), pltpu.VMEM((rows, S5_LANES), f32)],
            compiler_params=pltpu.CompilerParams(vmem_limit_bytes=VMEM_LIMIT_BYTES),
            name="s5_step",
        )(u.reshape(rows, w), h0r, h0i, *w_args)
        y = y.reshape(bsz, 1, w)
    else:
        tt = min(tt, L)
        h0r, h0i = h0_re.reshape(bsz, 1, S5_LANES).astype(f32), h0_im.reshape(bsz, 1, S5_LANES).astype(f32)
        st_spec = pl.BlockSpec((None, 1, S5_LANES), lambda b, t: (b, 0, 0))
        y, hr, hi = pl.pallas_call(
            _s5_seq_body,
            grid=(bsz, L // tt),
            in_specs=[pl.BlockSpec((None, tt, w), lambda b, t: (b, t, 0)), st_spec, st_spec] + w_specs,
            out_specs=[pl.BlockSpec((None, tt, w), lambda b, t: (b, t, 0)), st_spec, st_spec],
            out_shape=[jax.ShapeDtypeStruct((bsz, L, w), f32), jax.ShapeDtypeStruct((bsz, 1, S5_LANES), f32),
                       jax.ShapeDtypeStruct((bsz, 1, S5_LANES), f32)],
            scratch_shapes=[pltpu.VMEM((tt, S5_LANES), f32), pltpu.VMEM((tt, S5_LANES), f32),
                            pltpu.VMEM((1, S5_LANES), f32), pltpu.VMEM((1, S5_LANES), f32)],
            compiler_params=pltpu.CompilerParams(
                dimension_semantics=("parallel", "arbitrary"), vmem_limit_bytes=VMEM_LIMIT_BYTES),
            name="s5_seq",
        )(u, h0r, h0i, *w_args)
    shp = (bsz, S5_GROUPS, S5_STATE)
    return y, hr.reshape(shp).astype(h0_re.dtype), hi.reshape(shp).astype(h0_im.dtype)


def _ffn_tokens(x, g, ffn):
    b, l, d = x.shape
    t = b * l
    tm = 512 if t % 512 == 0 else t
    return ffn_residual(x.reshape(t, d), g, *ffn, tm=tm, tf=256).reshape(b, l, d)


def pre_mix(x, g, ffn, w_in):
    x = _ffn_tokens(x, g[0], ffn)
    z = rmsnorm(x, g[1]) @ w_in
    return x, z[..., :MEM_WIDTH], z[..., MEM_WIDTH:]


def post_mix(x, o_mem, o_mix, w_out, g, ffn):
    x = x + jnp.concatenate([o_mem, o_mix], axis=-1) @ w_out
    return _ffn_tokens(x, g[2], ffn)


def kernel(x_prompt, x_sample, cache_mem_k, cache_mem_v, state_ssm_re, state_ssm_im, cache_k, cache_v, cache_kidx, state_conv, state_delta, page_table, mem_prompt, norm_g, final_norm, w_in_a, w_in_b, w_in_c, w_out, w_mem_kv, ffn1_gate, ffn1_up, ffn1_down, ffn2_gate, ffn2_up, ffn2_down, s5_lam_re, s5_lam_im, s5_log_step, s5_b_re, s5_b_im, s5_c_re, s5_c_im, s5_d, s5_w_glu, s5_b_glu, rel_bias, gdn_conv_w, gdn_a_log, gdn_dt_bias, gdn_o_norm):
    depth = norm_g.shape[0]
    bp, bs = x_prompt.shape[0], x_sample.shape[0]
    past = page_table.shape[1] * PAGE_SIZE
    bf16 = jnp.bfloat16
    hp, hs = x_prompt, x_sample
    mem_k_p, mem_v_p = [], []
    ssm_re_p, ssm_im_p, ssm_re_s, ssm_im_s = [], [], [], []
    k_p, v_p, ki_p, k_s, v_s, ki_s = [], [], [], [], [], []
    conv_p, delta_p, conv_s, delta_s = [], [], [], []
    w_in_by_kind = (w_in_a, w_in_b, w_in_c)
    for i in range(depth):
        kind, j = i % N_MIXERS, i // N_MIXERS
        w_in = w_in_by_kind[kind][j]
        ffn1 = (ffn1_gate[i].astype(bf16), ffn1_up[i].astype(bf16), ffn1_down[i].astype(bf16))
        ffn2 = (ffn2_gate[i].astype(bf16), ffn2_up[i].astype(bf16), ffn2_down[i].astype(bf16))
        hp, cq_p, zp = pre_mix(hp, norm_g[i], ffn1, w_in)
        hs, cq_s, zs = pre_mix(hs, norm_g[i], ffn1, w_in)
        mk, mv = mem_kv(mem_prompt, w_mem_kv[i])
        mem_k_p.append(mk)
        mem_v_p.append(mv)
        om_p = mem_attend(cq_p, mk, mv)
        om_s = mem_attend(cq_s, cache_mem_k[i], cache_mem_v[i])
        if kind == 0:
            disc = s5_discretize(s5_lam_re[j], s5_lam_im[j], s5_log_step[j], s5_b_re[j], s5_b_im[j],
                                 s5_c_re[j], s5_c_im[j])
            gate = (s5_d[j], s5_w_glu[j], s5_b_glu[j])
            h0 = jnp.zeros((bp, S5_GROUPS, S5_STATE), state_ssm_re.dtype)
            op, hr, hi = s5_mix_pallas(zp, h0, h0, disc, *gate)
            ssm_re_p.append(hr)
            ssm_im_p.append(hi)
            osm, hr, hi = s5_mix_pallas(zs, state_ssm_re[j], state_ssm_im[j], disc, *gate)
            ssm_re_s.append(hr)
            ssm_im_s.append(hi)
        elif kind == 1:
            q, k, v, qi, ki, wi = dsa_heads(zp)
            qpos = jnp.arange(zp.shape[1], dtype=jnp.int32)
            op = dsa_attend(q, qi, wi, qpos, ki, local_gather(k, v), rel_bias)
            k_p.append(k)
            v_p.append(v)
            ki_p.append(ki)
            q, k, v, qi, ki, wi = dsa_heads(zs)
            ki_all = jnp.concatenate([cache_kidx[j][page_table].reshape(bs, past, IDX_DIM), ki], axis=1)
            qpos = past + jnp.arange(zs.shape[1], dtype=jnp.int32)
            gather = paged_gather(cache_k[j], cache_v[j], page_table, k, v)
            osm = dsa_attend(q, qi, wi, qpos, ki_all, gather, rel_bias)
            k_s.append(k)
            v_s.append(v)
            ki_s.append(ki)
        else:
            gdn = (gdn_conv_w[j], gdn_a_log[j], gdn_dt_bias[j], gdn_o_norm[j])
            ctx0 = jnp.zeros((bp, CONV_W - 1, CONV_CH), zp.dtype)
            s00 = jnp.zeros((bp, GDN_HEADS, HEAD_DIM, HEAD_DIM), state_delta.dtype)
            op, cst, sst = gdn_mix(zp, ctx0, s00, *gdn)
            conv_p.append(cst)
            delta_p.append(sst)
            osm, cst, sst = gdn_mix(zs, state_conv[j], state_delta[j], *gdn)
            conv_s.append(cst)
            delta_s.append(sst)
        hp = post_mix(hp, om_p, op, w_out[i], norm_g[i], ffn2)
        hs = post_mix(hs, om_s, osm, w_out[i], norm_g[i], ffn2)
    y_prompt = rmsnorm(hp, final_norm)
    y_sample = rmsnorm(hs, final_norm)
    st = jnp.stack
    return (y_prompt, y_sample, st(mem_k_p), st(mem_v_p),
            st(ssm_re_p), st(ssm_im_p), st(ssm_re_s), st(ssm_im_s),
            st(k_p), st(v_p), st(ki_p), st(k_s), st(v_s), st(ki_s),
            st(conv_p), st(delta_p), st(conv_s), st(delta_s))
```

```python
import math
import functools
import jax
import jax.numpy as jnp
from jax import lax
import numpy as np
from jax.experimental import pallas as pl
from jax.experimental.pallas import tpu as pltpu

D_MODEL = 1024
N_MIXERS = 3
HEAD_DIM = 64
MIX_WIDTH = D_MODEL
N_MEM = 256
MEM_HEADS = 4
MEM_WIDTH = MEM_HEADS * HEAD_DIM
SEQ_WIDTH = MIX_WIDTH - MEM_WIDTH
S5_GROUP = 16
S5_GROUPS = SEQ_WIDTH // S5_GROUP
S5_STATE = 64
DSA_HEADS = SEQ_WIDTH // HEAD_DIM
DSA_KV_HEADS = 4
DSA_GQA = DSA_HEADS // DSA_KV_HEADS
IDX_HEADS = 8
IDX_DIM = 64
TOPK_MAX = 256
QBLOCK = 128
N_BUCKETS = 32
MAX_DISTANCE = 128
GDN_HEADS = SEQ_WIDTH // HEAD_DIM
CONV_W = 4
CONV_CH = 3 * SEQ_WIDTH
GDN_CHUNK = 64
D_FF = 2816
EPS = 1e-6
PAGE_SIZE = 128
DSA_SPLITS = (DSA_HEADS * HEAD_DIM, DSA_KV_HEADS * HEAD_DIM, DSA_KV_HEADS * HEAD_DIM, IDX_HEADS * IDX_DIM, IDX_DIM, IDX_HEADS)
GDN_SPLITS = (CONV_CH, GDN_HEADS, GDN_HEADS, SEQ_WIDTH)

VMEM_LIMIT_BYTES = 48 * 1024 * 1024


def split_cols(z, sizes):
    return jnp.split(z, np.cumsum(sizes)[:-1].tolist(), axis=-1)


def rmsnorm(x, g):
    x32 = x.astype(jnp.float32)
    r = lax.rsqrt(jnp.mean(x32 * x32, axis=-1, keepdims=True) + EPS)
    return (x32 * r).astype(x.dtype) * g


def l2norm(x):
    return x * lax.rsqrt(jnp.sum(x * x, axis=-1, keepdims=True) + EPS)


def take_rows(rows, idx):
    return jax.vmap(lambda r, i: r[i])(rows, idx)


def mem_kv(mem, w_kv):
    k, v = jnp.split(mem @ w_kv, 2, axis=-1)
    shp = (mem.shape[0], mem.shape[1], MEM_HEADS, HEAD_DIM)
    return k.reshape(shp), v.reshape(shp)


def mem_attend(cq, mk, mv):
    bsz, lq = cq.shape[:2]
    q = cq.reshape(bsz, lq, MEM_HEADS, HEAD_DIM)
    logits = jnp.einsum('bqhd,bmhd->bhqm', q, mk).astype(jnp.float32) * HEAD_DIM ** -0.5
    p = jax.nn.softmax(logits, axis=-1).astype(mv.dtype)
    return jnp.einsum('bhqm,bmhd->bqhd', p, mv).reshape(bsz, lq, MEM_WIDTH)


def complex_affine_combine(x, y):
    ar1, ai1, br1, bi1 = x
    ar2, ai2, br2, bi2 = y
    return (ar1 * ar2 - ai1 * ai2, ar1 * ai2 + ai1 * ar2,
            ar2 * br1 - ai2 * bi1 + br2, ar2 * bi1 + ai2 * br1 + bi2)


def s5_mix(u, h0_re, h0_im, lam_re, lam_im, log_step, b_re, b_im, c_re, c_im, d_skip, w_glu, b_glu):
    f32 = jnp.float32
    bsz, L, _ = u.shape
    lr, li = lam_re.astype(f32), lam_im.astype(f32)
    step = jnp.exp(log_step.astype(f32))[:, None]
    mag = jnp.exp(lr * step)
    ab_re, ab_im = mag * jnp.cos(li * step), mag * jnp.sin(li * step)
    den = lr * lr + li * li
    nr, ni = ab_re - 1.0, ab_im
    f_re = (nr * lr + ni * li) / den
    f_im = (ni * lr - nr * li) / den
    br, bi = b_re.astype(f32), b_im.astype(f32)
    bb_re = f_re[..., None] * br - f_im[..., None] * bi
    bb_im = f_re[..., None] * bi + f_im[..., None] * br
    ug = u.astype(f32).reshape(bsz, L, S5_GROUPS, S5_GROUP)
    bu_re = jnp.einsum('blgc,gpc->blgp', ug, bb_re)
    bu_im = jnp.einsum('blgc,gpc->blgp', ug, bb_im)
    e_re = jnp.concatenate([h0_re.astype(f32)[:, None], bu_re], axis=1)
    e_im = jnp.concatenate([h0_im.astype(f32)[:, None], bu_im], axis=1)
    a_re = jnp.broadcast_to(ab_re, e_re.shape)
    a_im = jnp.broadcast_to(ab_im, e_im.shape)
    _, _, h_re, h_im = lax.associative_scan(complex_affine_combine, (a_re, a_im, e_re, e_im), axis=1)
    h_re, h_im = h_re[:, 1:], h_im[:, 1:]
    y = (jnp.einsum('gcp,blgp->blgc', c_re.astype(f32), h_re)
         - jnp.einsum('gcp,blgp->blgc', c_im.astype(f32), h_im))
    y = y.reshape(bsz, L, SEQ_WIDTH).astype(u.dtype) + d_skip * u
    y = jax.nn.gelu(y)
    y = y * jax.nn.sigmoid(y @ w_glu + b_glu)
    return y, h_re[:, -1].astype(h0_re.dtype), h_im[:, -1].astype(h0_im.dtype)


def t5_bucket(dist):
    max_exact = N_BUCKETS // 2
    n = jnp.maximum(dist, 0)
    nf = jnp.maximum(n, 1).astype(jnp.float32)
    large = max_exact + (jnp.log(nf / max_exact) / math.log(MAX_DISTANCE / max_exact)
                         * (N_BUCKETS - max_exact)).astype(jnp.int32)
    return jnp.where(n < max_exact, n, jnp.minimum(large, N_BUCKETS - 1))


def dsa_heads(z):
    bsz, L = z.shape[:2]
    q, k, v, qi, ki, wi = split_cols(z, DSA_SPLITS)
    return (q.reshape(bsz, L, DSA_KV_HEADS, DSA_GQA, HEAD_DIM),
            k.reshape(bsz, L, DSA_KV_HEADS, HEAD_DIM),
            v.reshape(bsz, L, DSA_KV_HEADS, HEAD_DIM),
            qi.reshape(bsz, L, IDX_HEADS, IDX_DIM), ki, wi * IDX_HEADS ** -0.5)


def dsa_block(q, qi, wi, qpos, ki_all, gather_kv, rel_bias, n_top):
    dots = jnp.einsum('bqhd,bsd->bqhs', qi, ki_all) * IDX_DIM ** -0.5
    score = jnp.einsum('bqh,bqhs->bqs', wi, jax.nn.relu(dots)).astype(jnp.float32)
    kpos = jnp.arange(ki_all.shape[1], dtype=jnp.int32)
    score = jnp.where(kpos[None, None, :] <= qpos[None, :, None], score, -jnp.inf)
    _, idx = lax.top_k(score, n_top)
    valid = idx <= qpos[None, :, None]
    k_sel, v_sel = gather_kv(idx)
    logits = jnp.einsum('bqkgd,bqnkd->bqkgn', q, k_sel).astype(jnp.float32) * HEAD_DIM ** -0.5
    bias = rel_bias[t5_bucket(qpos[None, :, None] - idx)].astype(jnp.float32)
    bias = bias.reshape(bias.shape[:3] + (DSA_KV_HEADS, DSA_GQA))
    logits = logits + jnp.moveaxis(bias, 2, -1)
    logits = jnp.where(valid[:, :, None, None, :], logits, -jnp.inf)
    p = jax.nn.softmax(logits, axis=-1).astype(v_sel.dtype)
    return jnp.einsum('bqkgn,bqnkd->bqkgd', p, v_sel)


def dsa_attend(q, qi, wi, qpos, ki_all, gather_kv, rel_bias):
    bsz, lq = q.shape[:2]
    n_top = min(TOPK_MAX, ki_all.shape[1] // 4)
    block = functools.partial(dsa_block, ki_all=ki_all, gather_kv=gather_kv, rel_bias=rel_bias, n_top=n_top)
    if lq <= QBLOCK or lq % QBLOCK != 0:
        o = block(q, qi, wi, qpos)
    else:
        nb = lq // QBLOCK
        def to_blocks(a):
            return jnp.moveaxis(a.reshape((bsz, nb, QBLOCK) + a.shape[2:]), 1, 0)
        o = lax.map(lambda a: block(*a), (to_blocks(q), to_blocks(qi), to_blocks(wi), qpos.reshape(nb, QBLOCK)))
        o = jnp.moveaxis(o, 0, 1)
    return o.reshape(bsz, lq, SEQ_WIDTH)


def local_gather(k, v):
    return lambda idx: (take_rows(k, idx), take_rows(v, idx))


def paged_gather(pool_k, pool_v, page_table, k_new, v_new):
    past = page_table.shape[1] * PAGE_SIZE
    lnew = k_new.shape[1]
    def gather(idx):
        ic = jnp.minimum(idx, past - 1)
        phys = take_rows(page_table, ic // PAGE_SIZE)
        off = ic % PAGE_SIZE
        inew = jnp.clip(idx - past, 0, lnew - 1)
        in_past = (idx < past)[..., None, None]
        return (jnp.where(in_past, pool_k[phys, off], take_rows(k_new, inew)),
                jnp.where(in_past, pool_v[phys, off], take_rows(v_new, inew)))
    return gather


def causal_conv(x, ctx, w):
    L = x.shape[1]
    xp = jnp.concatenate([ctx, x], axis=1)
    y = xp[:, 0:L] * w[0]
    for j in range(1, CONV_W):
        y = y + xp[:, j:j + L] * w[j]
    return jax.nn.silu(y), xp[:, L:]


def gated_delta(q, k, v, g, beta, s0):
    bsz, L, nh, dk = q.shape
    c = GDN_CHUNK if L % GDN_CHUNK == 0 else L
    n = L // c
    def chunks(a):
        a = a.reshape((bsz, n, c) + a.shape[2:])
        return jnp.moveaxis(jnp.moveaxis(a, 3, 2), 1, 0)
    q = chunks(q) * dk ** -0.5
    k, v, g, beta = chunks(k), chunks(v), chunks(g), chunks(beta)
    gc = jnp.cumsum(g, axis=-1)
    tril = jnp.tril(jnp.ones((c, c), dtype=bool))
    stril = jnp.tril(jnp.ones((c, c), dtype=bool), -1)
    gam = jnp.exp(jnp.where(tril, gc[..., :, None] - gc[..., None, :], -jnp.inf))
    a_mat = jnp.where(stril, beta[..., :, None] * jnp.einsum('nbhtd,nbhjd->nbhtj', k, k) * gam, 0.0)
    lhs = a_mat + jnp.eye(c, dtype=a_mat.dtype)
    solve = functools.partial(lax.linalg.triangular_solve, left_side=True, lower=True, unit_diagonal=True)
    uv = solve(lhs, beta[..., None] * v)
    wk = solve(lhs, (beta * jnp.exp(gc))[..., None] * k)
    qk = jnp.where(tril, jnp.einsum('nbhtd,nbhjd->nbhtj', q, k) * gam, 0.0)
    q_head = jnp.exp(gc)[..., None] * q
    g_last = gc[..., -1]
    k_tail = jnp.exp(g_last[..., None] - gc)[..., None] * k
    def step(s, xs):
        uv_c, wk_c, qk_c, qh_c, kt_c, gl_c = xs
        u = uv_c - jnp.einsum('bhck,bhvk->bhcv', wk_c, s)
        o = jnp.einsum('bhck,bhvk->bhcv', qh_c, s) + jnp.einsum('bhtj,bhjv->bhtv', qk_c, u)
        s = jnp.exp(gl_c)[..., None, None] * s + jnp.einsum('bhcv,bhck->bhvk', u, kt_c)
        return s, o
    s_fin, o = lax.scan(step, s0, (uv, wk, qk, q_head, k_tail, g_last))
    o = jnp.moveaxis(jnp.moveaxis(o, 0, 1), 2, 3).reshape(bsz, L, nh, v.shape[-1])
    return o, s_fin


def gdn_mix(z, conv_ctx, s0, conv_w, a_log, dt_bias, o_norm):
    f32 = jnp.float32
    bsz, L = z.shape[:2]
    qkv, a, b, gate = split_cols(z, GDN_SPLITS)
    qkv, conv_state = causal_conv(qkv, conv_ctx, conv_w)
    q, k, v = jnp.split(qkv.astype(f32), 3, axis=-1)
    hs = (bsz, L, GDN_HEADS, HEAD_DIM)
    q, k, v = l2norm(q.reshape(hs)), l2norm(k.reshape(hs)), v.reshape(hs)
    g = -jnp.exp(a_log.astype(f32)) * jax.nn.softplus((a + dt_bias).astype(f32))
    beta = jax.nn.sigmoid(b.astype(f32))
    o, s_fin = gated_delta(q, k, v, g, beta, s0.astype(f32))
    o = rmsnorm(o.astype(z.dtype), o_norm) * jax.nn.silu(gate.reshape(hs))
    return o.reshape(bsz, L, SEQ_WIDTH), conv_state, s_fin.astype(s0.dtype)


def _ffn_body(x_ref, g_ref, wg_ref, wu_ref, wd_ref, o_ref, n_ref, acc_ref):
    f = pl.program_id(1)

    @pl.when(f == 0)
    def _():
        x = x_ref[...]
        r = lax.rsqrt(jnp.mean(x * x, axis=-1, keepdims=True) + EPS)
        n_ref[...] = ((x * r) * g_ref[...]).astype(jnp.bfloat16)
        acc_ref[...] = jnp.zeros_like(acc_ref)

    n = n_ref[...]
    a = jnp.dot(n, wg_ref[...], preferred_element_type=jnp.float32)
    b = jnp.dot(n, wu_ref[...], preferred_element_type=jnp.float32)
    h = (a * (1.0 / (1.0 + jnp.exp(-a)))) * b
    acc_ref[...] += jnp.dot(h.astype(jnp.bfloat16), wd_ref[...], preferred_element_type=jnp.float32)

    @pl.when(f == pl.num_programs(1) - 1)
    def _():
        o_ref[...] = x_ref[...] + 0.5 * acc_ref[...]


def ffn_residual(x, g, wg, wu, wd, *, tm, tf):
    t, d = x.shape
    f = wg.shape[1]
    return pl.pallas_call(
        _ffn_body,
        grid=(t // tm, f // tf),
        in_specs=[
            pl.BlockSpec((tm, d), lambda i, j: (i, 0)),
            pl.BlockSpec((1, d), lambda i, j: (0, 0)),
            pl.BlockSpec((d, tf), lambda i, j: (0, j)),
            pl.BlockSpec((d, tf), lambda i, j: (0, j)),
            pl.BlockSpec((tf, d), lambda i, j: (j, 0)),
        ],
        out_specs=pl.BlockSpec((tm, d), lambda i, j: (i, 0)),
        out_shape=jax.ShapeDtypeStruct((t, d), jnp.float32),
        scratch_shapes=[pltpu.VMEM((tm, d), jnp.bfloat16), pltpu.VMEM((tm, d), jnp.float32)],
        compiler_params=pltpu.CompilerParams(
            dimension_semantics=("parallel", "arbitrary"), vmem_limit_bytes=VMEM_LIMIT_BYTES),
        name="ffn_residual",
    )(x, g.reshape(1, d), wg, wu, wd)


LANES = 128
S5_LANES = S5_GROUPS * S5_STATE
S5_GROUPS_PER_BLOCK = LANES // S5_GROUP
S5_BLOCKS = SEQ_WIDTH // LANES
S5_BLOCK_STATES = S5_GROUPS_PER_BLOCK * S5_STATE


def s5_discretize(lam_re, lam_im, log_step, b_re, b_im, c_re, c_im):
    f32, bf16 = jnp.float32, jnp.bfloat16
    lr, li = lam_re.astype(f32), lam_im.astype(f32)
    step = jnp.exp(log_step.astype(f32))[:, None]
    mag = jnp.exp(lr * step)
    ab_re, ab_im = mag * jnp.cos(li * step), mag * jnp.sin(li * step)
    den = lr * lr + li * li
    nr, ni = ab_re - 1.0, ab_im
    f_re = (nr * lr + ni * li) / den
    f_im = (ni * lr - nr * li) / den
    br, bi = b_re.astype(f32), b_im.astype(f32)
    bb_re = f_re[..., None] * br - f_im[..., None] * bi
    bb_im = f_re[..., None] * bi + f_im[..., None] * br
    eye = jnp.eye(S5_GROUPS_PER_BLOCK, dtype=f32)
    nb, gb = S5_BLOCKS, S5_GROUPS_PER_BLOCK

    def in_blocks(bb):
        w = jnp.einsum('jgpc,gh->jgchp', bb.reshape(nb, gb, S5_STATE, S5_GROUP), eye)
        return w.reshape(nb, LANES, S5_BLOCK_STATES).astype(bf16)

    def out_blocks(c):
        w = jnp.einsum('jgop,gh->jgpho', c.astype(f32).reshape(nb, gb, S5_GROUP, S5_STATE), eye)
        return w.reshape(nb, S5_BLOCK_STATES, LANES).astype(bf16)

    return (ab_re.reshape(1, S5_LANES), ab_im.reshape(1, S5_LANES),
            in_blocks(bb_re), in_blocks(bb_im), out_blocks(c_re), out_blocks(-c_im))


def _s5_project_in(u, wbr_ref, wbi_ref, bur_ref, bui_ref):
    ub = u.astype(jnp.bfloat16)
    for j in range(S5_BLOCKS):
        uj = ub[:, j * LANES:(j + 1) * LANES]
        sl = slice(j * S5_BLOCK_STATES, (j + 1) * S5_BLOCK_STATES)
        bur_ref[:, sl] = jnp.dot(uj, wbr_ref[j], preferred_element_type=jnp.float32)
        bui_ref[:, sl] = jnp.dot(uj, wbi_ref[j], preferred_element_type=jnp.float32)


def _s5_project_out(hr_ref, hi_ref, wcr_ref, wci_ref):
    cols = []
    for j in range(S5_BLOCKS):
        sl = slice(j * S5_BLOCK_STATES, (j + 1) * S5_BLOCK_STATES)
        cols.append(jnp.dot(hr_ref[:, sl].astype(jnp.bfloat16), wcr_ref[j], preferred_element_type=jnp.float32)
                    + jnp.dot(hi_ref[:, sl].astype(jnp.bfloat16), wci_ref[j], preferred_element_type=jnp.float32))
    return jnp.concatenate(cols, axis=-1)


def _s5_gate(y_ssm, u, d_ref, wglu_ref, bglu_ref):
    y = y_ssm + d_ref[...] * u
    y = 0.5 * y * (1.0 + jnp.tanh(math.sqrt(2.0 / math.pi) * (y + 0.044715 * (y * y * y))))
    z = jnp.dot(y.astype(jnp.bfloat16), wglu_ref[...], preferred_element_type=jnp.float32) + bglu_ref[...]
    return y * (1.0 / (1.0 + jnp.exp(-z)))


def _s5_seq_body(u_ref, h0r_ref, h0i_ref, ar_ref, ai_ref, wbr_ref, wbi_ref, wcr_ref, wci_ref, d_ref, wglu_ref, bglu_ref,
                 y_ref, hfr_ref, hfi_ref, bur_ref, bui_ref, cr_ref, ci_ref):
    tt = u_ref.shape[0]

    @pl.when(pl.program_id(1) == 0)
    def _():
        cr_ref[...] = h0r_ref[...]
        ci_ref[...] = h0i_ref[...]

    u = u_ref[...]
    _s5_project_in(u, wbr_ref, wbi_ref, bur_ref, bui_ref)
    ar, ai = ar_ref[...], ai_ref[...]

    def step(t, carry):
        hr, hi = carry
        nhr = ar * hr - ai * hi + bur_ref[pl.ds(t, 1), :]
        nhi = ar * hi + ai * hr + bui_ref[pl.ds(t, 1), :]
        bur_ref[pl.ds(t, 1), :] = nhr
        bui_ref[pl.ds(t, 1), :] = nhi
        return nhr, nhi

    hr, hi = lax.fori_loop(0, tt, step, (cr_ref[...], ci_ref[...]))
    cr_ref[...] = hr
    ci_ref[...] = hi
    hfr_ref[...] = hr
    hfi_ref[...] = hi
    y_ref[...] = _s5_gate(_s5_project_out(bur_ref, bui_ref, wcr_ref, wci_ref), u, d_ref, wglu_ref, bglu_ref)


def _s5_step_body(u_ref, h0r_ref, h0i_ref, ar_ref, ai_ref, wbr_ref, wbi_ref, wcr_ref, wci_ref, d_ref, wglu_ref, bglu_ref,
                  y_ref, hfr_ref, hfi_ref, bur_ref, bui_ref):
    u = u_ref[...]
    _s5_project_in(u, wbr_ref, wbi_ref, bur_ref, bui_ref)
    ar, ai = ar_ref[...], ai_ref[...]
    hr, hi = h0r_ref[...], h0i_ref[...]
    nhr = ar * hr - ai * hi + bur_ref[...]
    nhi = ar * hi + ai * hr + bui_ref[...]
    bur_ref[...] = nhr
    bui_ref[...] = nhi
    hfr_ref[...] = nhr
    hfi_ref[...] = nhi
    y_ref[...] = _s5_gate(_s5_project_out(bur_ref, bui_ref, wcr_ref, wci_ref), u, d_ref, wglu_ref, bglu_ref)


def s5_mix_pallas(u, h0_re, h0_im, disc, d_skip, w_glu, b_glu, *, tt=512):
    bsz, L, w = u.shape
    f32 = jnp.float32
    ar, ai, wbr, wbi, wcr, wci = disc
    d2, bg2, wg = d_skip.reshape(1, w).astype(f32), b_glu.reshape(1, w).astype(f32), w_glu.astype(jnp.bfloat16)
    const2 = lambda *_: (0, 0)
    const3 = lambda *_: (0, 0, 0)
    w_specs = [pl.BlockSpec((1, S5_LANES), const2), pl.BlockSpec((1, S5_LANES), const2),
               pl.BlockSpec(wbr.shape, const3), pl.BlockSpec(wbi.shape, const3),
               pl.BlockSpec(wcr.shape, const3), pl.BlockSpec(wci.shape, const3),
               pl.BlockSpec((1, w), const2), pl.BlockSpec((w, w), const2), pl.BlockSpec((1, w), const2)]
    w_args = (ar, ai, wbr, wbi, wcr, wci, d2, wg, bg2)
    if L == 1:
        rows = bsz
        h0r, h0i = h0_re.reshape(rows, S5_LANES).astype(f32), h0_im.reshape(rows, S5_LANES).astype(f32)
        row_spec = lambda n: pl.BlockSpec((rows, n), const2)
        y, hr, hi = pl.pallas_call(
            _s5_step_body,
            grid=(1,),
            in_specs=[row_spec(w), row_spec(S5_LANES), row_spec(S5_LANES)] + w_specs,
            out_specs=[row_spec(w), row_spec(S5_LANES), row_spec(S5_LANES)],
            out_shape=[jax.ShapeDtypeStruct((rows, w), f32), jax.ShapeDtypeStruct((rows, S5_LANES), f32),
                       jax.ShapeDtypeStruct((rows, S5_LANES), f32)],
            scratch_shapes=[pltpu.VMEM((rows, S5_LANES), f32), pltpu.VMEM((rows, S5_LANES), f32)],
            compiler_params=pltpu.CompilerParams(vmem_limit_bytes=VMEM_LIMIT_BYTES),
            name="s5_step",
        )(u.reshape(rows, w), h0r, h0i, *w_args)
        y = y.reshape(bsz, 1, w)
    else:
        tt = min(tt, L)
        h0r, h0i = h0_re.reshape(bsz, 1, S5_LANES).astype(f32), h0_im.reshape(bsz, 1, S5_LANES).astype(f32)
        st_spec = pl.BlockSpec((None, 1, S5_LANES), lambda b, t: (b, 0, 0))
        y, hr, hi = pl.pallas_call(
            _s5_seq_body,
            grid=(bsz, L // tt),
            in_specs=[pl.BlockSpec((None, tt, w), lambda b, t: (b, t, 0)), st_spec, st_spec] + w_specs,
            out_specs=[pl.BlockSpec((None, tt, w), lambda b, t: (b, t, 0)), st_spec, st_spec],
            out_shape=[jax.ShapeDtypeStruct((bsz, L, w), f32), jax.ShapeDtypeStruct((bsz, 1, S5_LANES), f32),
                       jax.ShapeDtypeStruct((bsz, 1, S5_LANES), f32)],
            scratch_shapes=[pltpu.VMEM((tt, S5_LANES), f32), pltpu.VMEM((tt, S5_LANES), f32),
                            pltpu.VMEM((1, S5_LANES), f32), pltpu.VMEM((1, S5_LANES), f32)],
            compiler_params=pltpu.CompilerParams(
                dimension_semantics=("parallel", "arbitrary"), vmem_limit_bytes=VMEM_LIMIT_BYTES),
            name="s5_seq",
        )(u, h0r, h0i, *w_args)
    shp = (bsz, S5_GROUPS, S5_STATE)
    return y, hr.reshape(shp).astype(h0_re.dtype), hi.reshape(shp).astype(h0_im.dtype)


DSA_KEY_CHUNK = 512
DSA_NEAR = 2 * QBLOCK
INT32_MIN = -2 ** 31
NEG_BIG = -1e30


def _sortable_key(s):
    bits = lax.bitcast_convert_type(s, jnp.int32)
    return jnp.where(bits < 0, bits ^ jnp.int32(0x7FFFFFFF), bits)


def _dsa_prompt_body(n_top, rb_ref, q_ref, qi_ref, wi_ref, kt_ref, v_ref, kit_ref, o_ref, keys_ref, bias_ref):
    i = pl.program_id(1)
    f32, bf16 = jnp.float32, jnp.bfloat16
    kc = DSA_KEY_CHUNK
    q_start = i * QBLOCK
    n_all = (q_start + QBLOCK + kc - 1) // kc
    near_start = jnp.maximum(q_start - QBLOCK, 0)
    n_far = (near_start + kc - 1) // kc
    row = lax.broadcasted_iota(jnp.int32, (QBLOCK, 1), 0)
    qpos = q_start + row

    @pl.when(i == 0)
    def _():
        r = lax.broadcasted_iota(jnp.int32, (QBLOCK, DSA_NEAR), 0)
        c = lax.broadcasted_iota(jnp.int32, (QBLOCK, DSA_NEAR), 1)
        for tile in range(2):
            bucket = t5_bucket(r + tile * QBLOCK - c)
            for h in range(DSA_HEADS):
                b = jnp.zeros((QBLOCK, DSA_NEAR), f32)
                for bk in range(N_BUCKETS):
                    b = jnp.where(bucket == bk, rb_ref[bk, h] - rb_ref[N_BUCKETS - 1, h], b)
                kv, g = divmod(h, DSA_GQA)
                bias_ref[tile, kv, g * QBLOCK:(g + 1) * QBLOCK, :] = b

    qi = (qi_ref[...].astype(f32) * IDX_DIM ** -0.5).astype(bf16)
    wi = wi_ref[...]

    def score_chunk(c, _):
        off = pl.multiple_of(c * kc, kc)
        d = jnp.dot(qi, kit_ref[:, pl.ds(off, kc)], preferred_element_type=f32)
        s = jnp.zeros((QBLOCK, kc), f32)
        for h in range(IDX_HEADS):
            s = s + wi[:, h:h + 1] * jnp.maximum(d[h * QBLOCK:(h + 1) * QBLOCK], 0.0)
        kpos = off + lax.broadcasted_iota(jnp.int32, (QBLOCK, kc), 1)
        s = jnp.where(s == 0.0, 0.0, s)
        s = jnp.where(kpos <= qpos, s, -jnp.inf)
        keys_ref[:, pl.ds(off, kc)] = _sortable_key(s)
        return 0

    lax.fori_loop(0, n_all, score_chunk, 0)

    def count_where(pred_fn):
        def body(c, acc):
            off = pl.multiple_of(c * kc, kc)
            hit = pred_fn(keys_ref[:, pl.ds(off, kc)], off)
            part = jnp.where(hit, 1.0, 0.0)
            for j in range(kc // 128):
                acc = acc + part[:, j * 128:(j + 1) * 128]
            return acc
        acc = lax.fori_loop(0, n_all, body, jnp.zeros((QBLOCK, 128), f32))
        return jnp.sum(acc, axis=-1, keepdims=True)

    def thr_bit(it, t):
        cand = t + lax.shift_left(jnp.int32(1), 31 - it)
        cnt = count_where(lambda k, off: k >= cand)
        return jnp.where(cnt >= n_top, cand, t)

    thr = lax.fori_loop(0, 32, thr_bit, jnp.full((QBLOCK, 1), INT32_MIN, jnp.int32))

    def is_valid(off, width):
        return (off + lax.broadcasted_iota(jnp.int32, (QBLOCK, width), 1)) <= qpos

    n_gt = count_where(lambda k, off: (k > thr) & is_valid(off, kc))
    n_eq = count_where(lambda k, off: (k == thr) & is_valid(off, kc))
    need = n_top - n_gt
    has_extra_ties = jnp.max(jnp.where(n_eq > need, 1.0, 0.0)) > 0.0

    def tie_search():
        def idx_bit(it, j):
            cand = j + lax.shift_left(jnp.int32(1), 13 - it)
            cnt = count_where(lambda k, off: (k == thr) & is_valid(off, kc)
                              & ((off + lax.broadcasted_iota(jnp.int32, (QBLOCK, kc), 1)) < cand))
            return jnp.where(cnt <= need, cand, j)
        return lax.fori_loop(0, 14, idx_bit, jnp.zeros((QBLOCK, 1), jnp.int32))

    tie_end = lax.cond(has_extra_ties, tie_search, lambda: jnp.full((QBLOCK, 1), 2 ** 14, jnp.int32))

    def selected(keys, off, width):
        kpos = off + lax.broadcasted_iota(jnp.int32, (QBLOCK, width), 1)
        return (kpos <= qpos) & ((keys > thr) | ((keys == thr) & (kpos < tie_end))), kpos

    rows = DSA_GQA * QBLOCK
    tile = jnp.minimum(i, 1)
    qks = [(q_ref[kv].astype(f32) * HEAD_DIM ** -0.5).astype(bf16) for kv in range(DSA_KV_HEADS)]

    def attend(carry, qk, kt, vv, mask_bias):
        m, l, acc = carry
        logits = jnp.dot(qk, kt, preferred_element_type=f32) + mask_bias
        m_new = jnp.maximum(m, jnp.max(logits, axis=-1, keepdims=True))
        alpha = jnp.exp(m - m_new)
        p = jnp.exp(logits - m_new)
        l = alpha * l + jnp.sum(p, axis=-1, keepdims=True)
        acc = alpha * acc + jnp.dot(p.astype(bf16), vv, preferred_element_type=f32)
        return m_new, l, acc

    def far_chunk(c, carry):
        off = pl.multiple_of(c * kc, kc)
        sel, kpos = selected(keys_ref[:, pl.ds(off, kc)], off, kc)
        mb = jnp.where(sel & (kpos < near_start), 0.0, NEG_BIG)
        mb3 = jnp.concatenate([mb] * DSA_GQA, axis=0)
        return tuple(attend(carry[kv], qks[kv], kt_ref[kv * HEAD_DIM:(kv + 1) * HEAD_DIM, pl.ds(off, kc)],
                            v_ref[kv, pl.ds(off, kc), :], mb3) for kv in range(DSA_KV_HEADS))

    init = tuple((jnp.full((rows, 1), NEG_BIG, f32), jnp.zeros((rows, 1), f32), jnp.zeros((rows, HEAD_DIM), f32))
                 for _ in range(DSA_KV_HEADS))
    carry = lax.fori_loop(0, n_far, far_chunk, init)
    off = pl.multiple_of(near_start, QBLOCK)
    sel, _ = selected(keys_ref[:, pl.ds(off, DSA_NEAR)], off, DSA_NEAR)
    sel3 = jnp.concatenate([sel] * DSA_GQA, axis=0)
    for kv in range(DSA_KV_HEADS):
        mb3 = jnp.where(sel3, bias_ref[tile, kv], NEG_BIG)
        m, l, acc = attend(carry[kv], qks[kv], kt_ref[kv * HEAD_DIM:(kv + 1) * HEAD_DIM, pl.ds(off, DSA_NEAR)],
                           v_ref[kv, pl.ds(off, DSA_NEAR), :], mb3)
        o_ref[kv] = acc / l


def dsa_prompt_pallas(q, k, v, qi, ki, wi, rel_bias):
    bsz, L = q.shape[:2]
    nq = L // QBLOCK
    n_top = min(TOPK_MAX, L // 4)
    f32, bf16 = jnp.float32, jnp.bfloat16
    q_blk = q.astype(bf16).reshape(bsz, nq, QBLOCK, DSA_KV_HEADS, DSA_GQA, HEAD_DIM)
    q_blk = q_blk.transpose(0, 1, 3, 4, 2, 5).reshape(bsz, nq, DSA_KV_HEADS, DSA_GQA * QBLOCK, HEAD_DIM)
    qi_blk = qi.astype(bf16).reshape(bsz, nq, QBLOCK, IDX_HEADS, IDX_DIM)
    qi_blk = qi_blk.transpose(0, 1, 3, 2, 4).reshape(bsz, nq, IDX_HEADS * QBLOCK, IDX_DIM)
    k_t = k.astype(bf16).reshape(bsz, L, DSA_KV_HEADS * HEAD_DIM).swapaxes(1, 2)
    v_h = v.astype(bf16).swapaxes(1, 2)
    ki_t = ki.astype(bf16).swapaxes(1, 2)
    lk = max(L, DSA_KEY_CHUNK)
    if lk != L:
        k_t = jnp.pad(k_t, ((0, 0), (0, 0), (0, lk - L)))
        v_h = jnp.pad(v_h, ((0, 0), (0, 0), (0, lk - L), (0, 0)))
        ki_t = jnp.pad(ki_t, ((0, 0), (0, 0), (0, lk - L)))
    o = pl.pallas_call(
        functools.partial(_dsa_prompt_body, n_top),
        grid=(bsz, nq),
        in_specs=[
            pl.BlockSpec(memory_space=pltpu.SMEM),
            pl.BlockSpec((None, None, DSA_KV_HEADS, DSA_GQA * QBLOCK, HEAD_DIM), lambda b, i: (b, i, 0, 0, 0)),
            pl.BlockSpec((None, None, IDX_HEADS * QBLOCK, IDX_DIM), lambda b, i: (b, i, 0, 0)),
            pl.BlockSpec((None, QBLOCK, IDX_HEADS), lambda b, i: (b, i, 0)),
            pl.BlockSpec((None, DSA_KV_HEADS * HEAD_DIM, lk), lambda b, i: (b, 0, 0)),
            pl.BlockSpec((None, DSA_KV_HEADS, lk, HEAD_DIM), lambda b, i: (b, 0, 0, 0)),
            pl.BlockSpec((None, IDX_DIM, lk), lambda b, i: (b, 0, 0)),
        ],
        out_specs=pl.BlockSpec((None, None, DSA_KV_HEADS, DSA_GQA * QBLOCK, HEAD_DIM), lambda b, i: (b, i, 0, 0, 0)),
        out_shape=jax.ShapeDtypeStruct((bsz, nq, DSA_KV_HEADS, DSA_GQA * QBLOCK, HEAD_DIM), f32),
        scratch_shapes=[pltpu.VMEM((QBLOCK, lk), jnp.int32),
                        pltpu.VMEM((2, DSA_KV_HEADS, DSA_GQA * QBLOCK, DSA_NEAR), f32)],
        compiler_params=pltpu.CompilerParams(
            dimension_semantics=("parallel", "arbitrary"), vmem_limit_bytes=VMEM_LIMIT_BYTES),
        name="dsa_prompt",
    )(rel_bias.astype(f32), q_blk, qi_blk, wi.astype(f32), k_t, v_h, ki_t)
    o = o.reshape(bsz, nq, DSA_KV_HEADS, DSA_GQA, QBLOCK, HEAD_DIM).transpose(0, 1, 4, 2, 3, 5)
    return o.reshape(bsz, L, SEQ_WIDTH)


def _ffn_tokens(x, g, ffn):
    b, l, d = x.shape
    t = b * l
    tm = 512 if t % 512 == 0 else t
    return ffn_residual(x.reshape(t, d), g, *ffn, tm=tm, tf=256).reshape(b, l, d)


def pre_mix(x, g, ffn, w_in):
    x = _ffn_tokens(x, g[0], ffn)
    z = rmsnorm(x, g[1]) @ w_in
    return x, z[..., :MEM_WIDTH], z[..., MEM_WIDTH:]


def post_mix(x, o_mem, o_mix, w_out, g, ffn):
    x = x + jnp.concatenate([o_mem, o_mix], axis=-1) @ w_out
    return _ffn_tokens(x, g[2], ffn)


def kernel(x_prompt, x_sample, cache_mem_k, cache_mem_v, state_ssm_re, state_ssm_im, cache_k, cache_v, cache_kidx, state_conv, state_delta, page_table, mem_prompt, norm_g, final_norm, w_in_a, w_in_b, w_in_c, w_out, w_mem_kv, ffn1_gate, ffn1_up, ffn1_down, ffn2_gate, ffn2_up, ffn2_down, s5_lam_re, s5_lam_im, s5_log_step, s5_b_re, s5_b_im, s5_c_re, s5_c_im, s5_d, s5_w_glu, s5_b_glu, rel_bias, gdn_conv_w, gdn_a_log, gdn_dt_bias, gdn_o_norm):
    depth = norm_g.shape[0]
    bp, bs = x_prompt.shape[0], x_sample.shape[0]
    past = page_table.shape[1] * PAGE_SIZE
    bf16 = jnp.bfloat16
    hp, hs = x_prompt, x_sample
    mem_k_p, mem_v_p = [], []
    ssm_re_p, ssm_im_p, ssm_re_s, ssm_im_s = [], [], [], []
    k_p, v_p, ki_p, k_s, v_s, ki_s = [], [], [], [], [], []
    conv_p, delta_p, conv_s, delta_s = [], [], [], []
    w_in_by_kind = (w_in_a, w_in_b, w_in_c)
    for i in range(depth):
        kind, j = i % N_MIXERS, i // N_MIXERS
        w_in = w_in_by_kind[kind][j]
        ffn1 = (ffn1_gate[i].astype(bf16), ffn1_up[i].astype(bf16), ffn1_down[i].astype(bf16))
        ffn2 = (ffn2_gate[i].astype(bf16), ffn2_up[i].astype(bf16), ffn2_down[i].astype(bf16))
        hp, cq_p, zp = pre_mix(hp, norm_g[i], ffn1, w_in)
        hs, cq_s, zs = pre_mix(hs, norm_g[i], ffn1, w_in)
        mk, mv = mem_kv(mem_prompt, w_mem_kv[i])
        mem_k_p.append(mk)
        mem_v_p.append(mv)
        om_p = mem_attend(cq_p, mk, mv)
        om_s = mem_attend(cq_s, cache_mem_k[i], cache_mem_v[i])
        if kind == 0:
            disc = s5_discretize(s5_lam_re[j], s5_lam_im[j], s5_log_step[j], s5_b_re[j], s5_b_im[j],
                                 s5_c_re[j], s5_c_im[j])
            gate = (s5_d[j], s5_w_glu[j], s5_b_glu[j])
            h0 = jnp.zeros((bp, S5_GROUPS, S5_STATE), state_ssm_re.dtype)
            op, hr, hi = s5_mix_pallas(zp, h0, h0, disc, *gate)
            ssm_re_p.append(hr)
            ssm_im_p.append(hi)
            osm, hr, hi = s5_mix_pallas(zs, state_ssm_re[j], state_ssm_im[j], disc, *gate)
            ssm_re_s.append(hr)
            ssm_im_s.append(hi)
        elif kind == 1:
            q, k, v, qi, ki, wi = dsa_heads(zp)
            op = dsa_prompt_pallas(q, k, v, qi, ki, wi, rel_bias)
            k_p.append(k)
            v_p.append(v)
            ki_p.append(ki)
            q, k, v, qi, ki, wi = dsa_heads(zs)
            ki_all = jnp.concatenate([cache_kidx[j][page_table].reshape(bs, past, IDX_DIM), ki], axis=1)
            qpos = past + jnp.arange(zs.shape[1], dtype=jnp.int32)
            gather = paged_gather(cache_k[j], cache_v[j], page_table, k, v)
            osm = dsa_attend(q, qi, wi, qpos, ki_all, gather, rel_bias)
            k_s.append(k)
            v_s.append(v)
            ki_s.append(ki)
        else:
            gdn = (gdn_conv_w[j], gdn_a_log[j], gdn_dt_bias[j], gdn_o_norm[j])
            ctx0 = jnp.zeros((bp, CONV_W - 1, CONV_CH), zp.dtype)
            s00 = jnp.zeros((bp, GDN_HEADS, HEAD_DIM, HEAD_DIM), state_delta.dtype)
            op, cst, sst = gdn_mix(zp, ctx0, s00, *gdn)
            conv_p.append(cst)
            delta_p.append(sst)
            osm, cst, sst = gdn_mix(zs, state_conv[j], state_delta[j], *gdn)
            conv_s.append(cst)
            delta_s.append(sst)
        hp = post_mix(hp, om_p, op, w_out[i], norm_g[i], ffn2)
        hs = post_mix(hs, om_s, osm, w_out[i], norm_g[i], ffn2)
    y_prompt = rmsnorm(hp, final_norm)
    y_sample = rmsnorm(hs, final_norm)
    st = jnp.stack
    return (y_prompt, y_sample, st(mem_k_p), st(mem_v_p),
            st(ssm_re_p), st(ssm_im_p), st(ssm_re_s), st(ssm_im_s),
            st(k_p), st(v_p), st(ki_p), st(k_s), st(v_s), st(ki_s),
            st(conv_p), st(delta_p), st(conv_s), st(delta_s))
```

```python
import math
import functools
import jax
import jax.numpy as jnp
from jax import lax
import numpy as np
from jax.experimental import pallas as pl
from jax.experimental.pallas import tpu as pltpu

D_MODEL = 1024
N_MIXERS = 3
HEAD_DIM = 64
MIX_WIDTH = D_MODEL
N_MEM = 256
MEM_HEADS = 4
MEM_WIDTH = MEM_HEADS * HEAD_DIM
SEQ_WIDTH = MIX_WIDTH - MEM_WIDTH
S5_GROUP = 16
S5_GROUPS = SEQ_WIDTH // S5_GROUP
S5_STATE = 64
DSA_HEADS = SEQ_WIDTH // HEAD_DIM
DSA_KV_HEADS = 4
DSA_GQA = DSA_HEADS // DSA_KV_HEADS
IDX_HEADS = 8
IDX_DIM = 64
TOPK_MAX = 256
QBLOCK = 128
N_BUCKETS = 32
MAX_DISTANCE = 128
GDN_HEADS = SEQ_WIDTH // HEAD_DIM
CONV_W = 4
CONV_CH = 3 * SEQ_WIDTH
GDN_CHUNK = 64
D_FF = 2816
EPS = 1e-6
PAGE_SIZE = 128
DSA_SPLITS = (DSA_HEADS * HEAD_DIM, DSA_KV_HEADS * HEAD_DIM, DSA_KV_HEADS * HEAD_DIM, IDX_HEADS * IDX_DIM, IDX_DIM, IDX_HEADS)
GDN_SPLITS = (CONV_CH, GDN_HEADS, GDN_HEADS, SEQ_WIDTH)

VMEM_LIMIT_BYTES = 48 * 1024 * 1024


def split_cols(z, sizes):
    return jnp.split(z, np.cumsum(sizes)[:-1].tolist(), axis=-1)


def rmsnorm(x, g):
    x32 = x.astype(jnp.float32)
    r = lax.rsqrt(jnp.mean(x32 * x32, axis=-1, keepdims=True) + EPS)
    return (x32 * r).astype(x.dtype) * g


def l2norm(x):
    return x * lax.rsqrt(jnp.sum(x * x, axis=-1, keepdims=True) + EPS)


def take_rows(rows, idx):
    return jax.vmap(lambda r, i: r[i])(rows, idx)


def mem_kv(mem, w_kv):
    k, v = jnp.split(mem @ w_kv, 2, axis=-1)
    shp = (mem.shape[0], mem.shape[1], MEM_HEADS, HEAD_DIM)
    return k.reshape(shp), v.reshape(shp)


def mem_attend(cq, mk, mv):
    bsz, lq = cq.shape[:2]
    q = cq.reshape(bsz, lq, MEM_HEADS, HEAD_DIM)
    logits = jnp.einsum('bqhd,bmhd->bhqm', q, mk).astype(jnp.float32) * HEAD_DIM ** -0.5
    p = jax.nn.softmax(logits, axis=-1).astype(mv.dtype)
    return jnp.einsum('bhqm,bmhd->bqhd', p, mv).reshape(bsz, lq, MEM_WIDTH)


def complex_affine_combine(x, y):
    ar1, ai1, br1, bi1 = x
    ar2, ai2, br2, bi2 = y
    return (ar1 * ar2 - ai1 * ai2, ar1 * ai2 + ai1 * ar2,
            ar2 * br1 - ai2 * bi1 + br2, ar2 * bi1 + ai2 * br1 + bi2)


def s5_mix(u, h0_re, h0_im, lam_re, lam_im, log_step, b_re, b_im, c_re, c_im, d_skip, w_glu, b_glu):
    f32 = jnp.float32
    bsz, L, _ = u.shape
    lr, li = lam_re.astype(f32), lam_im.astype(f32)
    step = jnp.exp(log_step.astype(f32))[:, None]
    mag = jnp.exp(lr * step)
    ab_re, ab_im = mag * jnp.cos(li * step), mag * jnp.sin(li * step)
    den = lr * lr + li * li
    nr, ni = ab_re - 1.0, ab_im
    f_re = (nr * lr + ni * li) / den
    f_im = (ni * lr - nr * li) / den
    br, bi = b_re.astype(f32), b_im.astype(f32)
    bb_re = f_re[..., None] * br - f_im[..., None] * bi
    bb_im = f_re[..., None] * bi + f_im[..., None] * br
    ug = u.astype(f32).reshape(bsz, L, S5_GROUPS, S5_GROUP)
    bu_re = jnp.einsum('blgc,gpc->blgp', ug, bb_re)
    bu_im = jnp.einsum('blgc,gpc->blgp', ug, bb_im)
    e_re = jnp.concatenate([h0_re.astype(f32)[:, None], bu_re], axis=1)
    e_im = jnp.concatenate([h0_im.astype(f32)[:, None], bu_im], axis=1)
    a_re = jnp.broadcast_to(ab_re, e_re.shape)
    a_im = jnp.broadcast_to(ab_im, e_im.shape)
    _, _, h_re, h_im = lax.associative_scan(complex_affine_combine, (a_re, a_im, e_re, e_im), axis=1)
    h_re, h_im = h_re[:, 1:], h_im[:, 1:]
    y = (jnp.einsum('gcp,blgp->blgc', c_re.astype(f32), h_re)
         - jnp.einsum('gcp,blgp->blgc', c_im.astype(f32), h_im))
    y = y.reshape(bsz, L, SEQ_WIDTH).astype(u.dtype) + d_skip * u
    y = jax.nn.gelu(y)
    y = y * jax.nn.sigmoid(y @ w_glu + b_glu)
    return y, h_re[:, -1].astype(h0_re.dtype), h_im[:, -1].astype(h0_im.dtype)


def t5_bucket(dist):
    max_exact = N_BUCKETS // 2
    n = jnp.maximum(dist, 0)
    nf = jnp.maximum(n, 1).astype(jnp.float32)
    large = max_exact + (jnp.log(nf / max_exact) / math.log(MAX_DISTANCE / max_exact)
                         * (N_BUCKETS - max_exact)).astype(jnp.int32)
    return jnp.where(n < max_exact, n, jnp.minimum(large, N_BUCKETS - 1))


def dsa_heads(z):
    bsz, L = z.shape[:2]
    q, k, v, qi, ki, wi = split_cols(z, DSA_SPLITS)
    return (q.reshape(bsz, L, DSA_KV_HEADS, DSA_GQA, HEAD_DIM),
            k.reshape(bsz, L, DSA_KV_HEADS, HEAD_DIM),
            v.reshape(bsz, L, DSA_KV_HEADS, HEAD_DIM),
            qi.reshape(bsz, L, IDX_HEADS, IDX_DIM), ki, wi * IDX_HEADS ** -0.5)


def dsa_block(q, qi, wi, qpos, ki_all, gather_kv, rel_bias, n_top):
    dots = jnp.einsum('bqhd,bsd->bqhs', qi, ki_all) * IDX_DIM ** -0.5
    score = jnp.einsum('bqh,bqhs->bqs', wi, jax.nn.relu(dots)).astype(jnp.float32)
    kpos = jnp.arange(ki_all.shape[1], dtype=jnp.int32)
    score = jnp.where(kpos[None, None, :] <= qpos[None, :, None], score, -jnp.inf)
    _, idx = lax.top_k(score, n_top)
    valid = idx <= qpos[None, :, None]
    k_sel, v_sel = gather_kv(idx)
    logits = jnp.einsum('bqkgd,bqnkd->bqkgn', q, k_sel).astype(jnp.float32) * HEAD_DIM ** -0.5
    bias = rel_bias[t5_bucket(qpos[None, :, None] - idx)].astype(jnp.float32)
    bias = bias.reshape(bias.shape[:3] + (DSA_KV_HEADS, DSA_GQA))
    logits = logits + jnp.moveaxis(bias, 2, -1)
    logits = jnp.where(valid[:, :, None, None, :], logits, -jnp.inf)
    p = jax.nn.softmax(logits, axis=-1).astype(v_sel.dtype)
    return jnp.einsum('bqkgn,bqnkd->bqkgd', p, v_sel)


def dsa_attend(q, qi, wi, qpos, ki_all, gather_kv, rel_bias):
    bsz, lq = q.shape[:2]
    n_top = min(TOPK_MAX, ki_all.shape[1] // 4)
    block = functools.partial(dsa_block, ki_all=ki_all, gather_kv=gather_kv, rel_bias=rel_bias, n_top=n_top)
    if lq <= QBLOCK or lq % QBLOCK != 0:
        o = block(q, qi, wi, qpos)
    else:
        nb = lq // QBLOCK
        def to_blocks(a):
            return jnp.moveaxis(a.reshape((bsz, nb, QBLOCK) + a.shape[2:]), 1, 0)
        o = lax.map(lambda a: block(*a), (to_blocks(q), to_blocks(qi), to_blocks(wi), qpos.reshape(nb, QBLOCK)))
        o = jnp.moveaxis(o, 0, 1)
    return o.reshape(bsz, lq, SEQ_WIDTH)


def local_gather(k, v):
    return lambda idx: (take_rows(k, idx), take_rows(v, idx))


def paged_gather(pool_k, pool_v, page_table, k_new, v_new):
    past = page_table.shape[1] * PAGE_SIZE
    lnew = k_new.shape[1]
    def gather(idx):
        ic = jnp.minimum(idx, past - 1)
        phys = take_rows(page_table, ic // PAGE_SIZE)
        off = ic % PAGE_SIZE
        inew = jnp.clip(idx - past, 0, lnew - 1)
        in_past = (idx < past)[..., None, None]
        return (jnp.where(in_past, pool_k[phys, off], take_rows(k_new, inew)),
                jnp.where(in_past, pool_v[phys, off], take_rows(v_new, inew)))
    return gather


def causal_conv(x, ctx, w):
    L = x.shape[1]
    xp = jnp.concatenate([ctx, x], axis=1)
    y = xp[:, 0:L] * w[0]
    for j in range(1, CONV_W):
        y = y + xp[:, j:j + L] * w[j]
    return jax.nn.silu(y), xp[:, L:]


def gated_delta(q, k, v, g, beta, s0):
    bsz, L, nh, dk = q.shape
    c = GDN_CHUNK if L % GDN_CHUNK == 0 else L
    n = L // c
    def chunks(a):
        a = a.reshape((bsz, n, c) + a.shape[2:])
        return jnp.moveaxis(jnp.moveaxis(a, 3, 2), 1, 0)
    q = chunks(q) * dk ** -0.5
    k, v, g, beta = chunks(k), chunks(v), chunks(g), chunks(beta)
    gc = jnp.cumsum(g, axis=-1)
    tril = jnp.tril(jnp.ones((c, c), dtype=bool))
    stril = jnp.tril(jnp.ones((c, c), dtype=bool), -1)
    gam = jnp.exp(jnp.where(tril, gc[..., :, None] - gc[..., None, :], -jnp.inf))
    a_mat = jnp.where(stril, beta[..., :, None] * jnp.einsum('nbhtd,nbhjd->nbhtj', k, k) * gam, 0.0)
    lhs = a_mat + jnp.eye(c, dtype=a_mat.dtype)
    solve = functools.partial(lax.linalg.triangular_solve, left_side=True, lower=True, unit_diagonal=True)
    uv = solve(lhs, beta[..., None] * v)
    wk = solve(lhs, (beta * jnp.exp(gc))[..., None] * k)
    qk = jnp.where(tril, jnp.einsum('nbhtd,nbhjd->nbhtj', q, k) * gam, 0.0)
    q_head = jnp.exp(gc)[..., None] * q
    g_last = gc[..., -1]
    k_tail = jnp.exp(g_last[..., None] - gc)[..., None] * k
    def step(s, xs):
        uv_c, wk_c, qk_c, qh_c, kt_c, gl_c = xs
        u = uv_c - jnp.einsum('bhck,bhvk->bhcv', wk_c, s)
        o = jnp.einsum('bhck,bhvk->bhcv', qh_c, s) + jnp.einsum('bhtj,bhjv->bhtv', qk_c, u)
        s = jnp.exp(gl_c)[..., None, None] * s + jnp.einsum('bhcv,bhck->bhvk', u, kt_c)
        return s, o
    s_fin, o = lax.scan(step, s0, (uv, wk, qk, q_head, k_tail, g_last))
    o = jnp.moveaxis(jnp.moveaxis(o, 0, 1), 2, 3).reshape(bsz, L, nh, v.shape[-1])
    return o, s_fin


def gdn_mix(z, conv_ctx, s0, conv_w, a_log, dt_bias, o_norm):
    f32 = jnp.float32
    bsz, L = z.shape[:2]
    qkv, a, b, gate = split_cols(z, GDN_SPLITS)
    qkv, conv_state = causal_conv(qkv, conv_ctx, conv_w)
    q, k, v = jnp.split(qkv.astype(f32), 3, axis=-1)
    hs = (bsz, L, GDN_HEADS, HEAD_DIM)
    q, k, v = l2norm(q.reshape(hs)), l2norm(k.reshape(hs)), v.reshape(hs)
    g = -jnp.exp(a_log.astype(f32)) * jax.nn.softplus((a + dt_bias).astype(f32))
    beta = jax.nn.sigmoid(b.astype(f32))
    o, s_fin = gated_delta(q, k, v, g, beta, s0.astype(f32))
    o = rmsnorm(o.astype(z.dtype), o_norm) * jax.nn.silu(gate.reshape(hs))
    return o.reshape(bsz, L, SEQ_WIDTH), conv_state, s_fin.astype(s0.dtype)


def _ffn_body(x_ref, g_ref, wg_ref, wu_ref, wd_ref, o_ref, n_ref, acc_ref):
    f = pl.program_id(1)

    @pl.when(f == 0)
    def _():
        x = x_ref[...]
        r = lax.rsqrt(jnp.mean(x * x, axis=-1, keepdims=True) + EPS)
        n_ref[...] = ((x * r) * g_ref[...]).astype(jnp.bfloat16)
        acc_ref[...] = jnp.zeros_like(acc_ref)

    n = n_ref[...]
    a = jnp.dot(n, wg_ref[...], preferred_element_type=jnp.float32)
    b = jnp.dot(n, wu_ref[...], preferred_element_type=jnp.float32)
    h = (a * (1.0 / (1.0 + jnp.exp(-a)))) * b
    acc_ref[...] += jnp.dot(h.astype(jnp.bfloat16), wd_ref[...], preferred_element_type=jnp.float32)

    @pl.when(f == pl.num_programs(1) - 1)
    def _():
        o_ref[...] = x_ref[...] + 0.5 * acc_ref[...]


def ffn_residual(x, g, wg, wu, wd, *, tm, tf):
    t, d = x.shape
    f = wg.shape[1]
    return pl.pallas_call(
        _ffn_body,
        grid=(t // tm, f // tf),
        in_specs=[
            pl.BlockSpec((tm, d), lambda i, j: (i, 0)),
            pl.BlockSpec((1, d), lambda i, j: (0, 0)),
            pl.BlockSpec((d, tf), lambda i, j: (0, j)),
            pl.BlockSpec((d, tf), lambda i, j: (0, j)),
            pl.BlockSpec((tf, d), lambda i, j: (j, 0)),
        ],
        out_specs=pl.BlockSpec((tm, d), lambda i, j: (i, 0)),
        out_shape=jax.ShapeDtypeStruct((t, d), jnp.float32),
        scratch_shapes=[pltpu.VMEM((tm, d), jnp.bfloat16), pltpu.VMEM((tm, d), jnp.float32)],
        compiler_params=pltpu.CompilerParams(
            dimension_semantics=("parallel", "arbitrary"), vmem_limit_bytes=VMEM_LIMIT_BYTES),
        name="ffn_residual",
    )(x, g.reshape(1, d), wg, wu, wd)


LANES = 128
S5_LANES = S5_GROUPS * S5_STATE
S5_GROUPS_PER_BLOCK = LANES // S5_GROUP
S5_BLOCKS = SEQ_WIDTH // LANES
S5_BLOCK_STATES = S5_GROUPS_PER_BLOCK * S5_STATE


def s5_discretize(lam_re, lam_im, log_step, b_re, b_im, c_re, c_im):
    f32, bf16 = jnp.float32, jnp.bfloat16
    lr, li = lam_re.astype(f32), lam_im.astype(f32)
    step = jnp.exp(log_step.astype(f32))[:, None]
    mag = jnp.exp(lr * step)
    ab_re, ab_im = mag * jnp.cos(li * step), mag * jnp.sin(li * step)
    den = lr * lr + li * li
    nr, ni = ab_re - 1.0, ab_im
    f_re = (nr * lr + ni * li) / den
    f_im = (ni * lr - nr * li) / den
    br, bi = b_re.astype(f32), b_im.astype(f32)
    bb_re = f_re[..., None] * br - f_im[..., None] * bi
    bb_im = f_re[..., None] * bi + f_im[..., None] * br
    eye = jnp.eye(S5_GROUPS_PER_BLOCK, dtype=f32)
    nb, gb = S5_BLOCKS, S5_GROUPS_PER_BLOCK

    def in_blocks(bb):
        w = jnp.einsum('jgpc,gh->jgchp', bb.reshape(nb, gb, S5_STATE, S5_GROUP), eye)
        return w.reshape(nb, LANES, S5_BLOCK_STATES).astype(bf16)

    def out_blocks(c):
        w = jnp.einsum('jgop,gh->jgpho', c.astype(f32).reshape(nb, gb, S5_GROUP, S5_STATE), eye)
        return w.reshape(nb, S5_BLOCK_STATES, LANES).astype(bf16)

    return (ab_re.reshape(1, S5_LANES), ab_im.reshape(1, S5_LANES),
            in_blocks(bb_re), in_blocks(bb_im), out_blocks(c_re), out_blocks(-c_im))


def _s5_project_in(u, wbr_ref, wbi_ref, bur_ref, bui_ref):
    ub = u.astype(jnp.bfloat16)
    for j in range(S5_BLOCKS):
        uj = ub[:, j * LANES:(j + 1) * LANES]
        sl = slice(j * S5_BLOCK_STATES, (j + 1) * S5_BLOCK_STATES)
        bur_ref[:, sl] = jnp.dot(uj, wbr_ref[j], preferred_element_type=jnp.float32)
        bui_ref[:, sl] = jnp.dot(uj, wbi_ref[j], preferred_element_type=jnp.float32)


def _s5_project_out(hr_ref, hi_ref, wcr_ref, wci_ref):
    cols = []
    for j in range(S5_BLOCKS):
        sl = slice(j * S5_BLOCK_STATES, (j + 1) * S5_BLOCK_STATES)
        cols.append(jnp.dot(hr_ref[:, sl].astype(jnp.bfloat16), wcr_ref[j], preferred_element_type=jnp.float32)
                    + jnp.dot(hi_ref[:, sl].astype(jnp.bfloat16), wci_ref[j], preferred_element_type=jnp.float32))
    return jnp.concatenate(cols, axis=-1)


def _s5_gate(y_ssm, u, d_ref, wglu_ref, bglu_ref):
    y = y_ssm + d_ref[...] * u
    y = 0.5 * y * (1.0 + jnp.tanh(math.sqrt(2.0 / math.pi) * (y + 0.044715 * (y * y * y))))
    z = jnp.dot(y.astype(jnp.bfloat16), wglu_ref[...], preferred_element_type=jnp.float32) + bglu_ref[...]
    return y * (1.0 / (1.0 + jnp.exp(-z)))


def _s5_seq_body(u_ref, h0r_ref, h0i_ref, ar_ref, ai_ref, wbr_ref, wbi_ref, wcr_ref, wci_ref, d_ref, wglu_ref, bglu_ref,
                 y_ref, hfr_ref, hfi_ref, bur_ref, bui_ref, cr_ref, ci_ref):
    tt = u_ref.shape[0]

    @pl.when(pl.program_id(1) == 0)
    def _():
        cr_ref[...] = h0r_ref[...]
        ci_ref[...] = h0i_ref[...]

    u = u_ref[...]
    _s5_project_in(u, wbr_ref, wbi_ref, bur_ref, bui_ref)
    ar, ai = ar_ref[...], ai_ref[...]

    def step(t, carry):
        hr, hi = carry
        nhr = ar * hr - ai * hi + bur_ref[pl.ds(t, 1), :]
        nhi = ar * hi + ai * hr + bui_ref[pl.ds(t, 1), :]
        bur_ref[pl.ds(t, 1), :] = nhr
        bui_ref[pl.ds(t, 1), :] = nhi
        return nhr, nhi

    hr, hi = lax.fori_loop(0, tt, step, (cr_ref[...], ci_ref[...]))
    cr_ref[...] = hr
    ci_ref[...] = hi
    hfr_ref[...] = hr
    hfi_ref[...] = hi
    y_ref[...] = _s5_gate(_s5_project_out(bur_ref, bui_ref, wcr_ref, wci_ref), u, d_ref, wglu_ref, bglu_ref)


def _s5_step_body(u_ref, h0r_ref, h0i_ref, ar_ref, ai_ref, wbr_ref, wbi_ref, wcr_ref, wci_ref, d_ref, wglu_ref, bglu_ref,
                  y_ref, hfr_ref, hfi_ref, bur_ref, bui_ref):
    u = u_ref[...]
    _s5_project_in(u, wbr_ref, wbi_ref, bur_ref, bui_ref)
    ar, ai = ar_ref[...], ai_ref[...]
    hr, hi = h0r_ref[...], h0i_ref[...]
    nhr = ar * hr - ai * hi + bur_ref[...]
    nhi = ar * hi + ai * hr + bui_ref[...]
    bur_ref[...] = nhr
    bui_ref[...] = nhi
    hfr_ref[...] = nhr
    hfi_ref[...] = nhi
    y_ref[...] = _s5_gate(_s5_project_out(bur_ref, bui_ref, wcr_ref, wci_ref), u, d_ref, wglu_ref, bglu_ref)


def s5_mix_pallas(u, h0_re, h0_im, disc, d_skip, w_glu, b_glu, *, tt=512):
    bsz, L, w = u.shape
    f32 = jnp.float32
    ar, ai, wbr, wbi, wcr, wci = disc
    d2, bg2, wg = d_skip.reshape(1, w).astype(f32), b_glu.reshape(1, w).astype(f32), w_glu.astype(jnp.bfloat16)
    const2 = lambda *_: (0, 0)
    const3 = lambda *_: (0, 0, 0)
    w_specs = [pl.BlockSpec((1, S5_LANES), const2), pl.BlockSpec((1, S5_LANES), const2),
               pl.BlockSpec(wbr.shape, const3), pl.BlockSpec(wbi.shape, const3),
               pl.BlockSpec(wcr.shape, const3), pl.BlockSpec(wci.shape, const3),
               pl.BlockSpec((1, w), const2), pl.BlockSpec((w, w), const2), pl.BlockSpec((1, w), const2)]
    w_args = (ar, ai, wbr, wbi, wcr, wci, d2, wg, bg2)
    if L == 1:
        rows = bsz
        h0r, h0i = h0_re.reshape(rows, S5_LANES).astype(f32), h0_im.reshape(rows, S5_LANES).astype(f32)
        row_spec = lambda n: pl.BlockSpec((rows, n), const2)
        y, hr, hi = pl.pallas_call(
            _s5_step_body,
            grid=(1,),
            in_specs=[row_spec(w), row_spec(S5_LANES), row_spec(S5_LANES)] + w_specs,
            out_specs=[row_spec(w), row_spec(S5_LANES), row_spec(S5_LANES)],
            out_shape=[jax.ShapeDtypeStruct((rows, w), f32), jax.ShapeDtypeStruct((rows, S5_LANES), f32),
                       jax.ShapeDtypeStruct((rows, S5_LANES), f32)],
            scratch_shapes=[pltpu.VMEM((rows, S5_LANES), f32), pltpu.VMEM((rows, S5_LANES), f32)],
            compiler_params=pltpu.CompilerParams(vmem_limit_bytes=VMEM_LIMIT_BYTES),
            name="s5_step",
        )(u.reshape(rows, w), h0r, h0i, *w_args)
        y = y.reshape(bsz, 1, w)
    else:
        tt = min(tt, L)
        h0r, h0i = h0_re.reshape(bsz, 1, S5_LANES).astype(f32), h0_im.reshape(bsz, 1, S5_LANES).astype(f32)
        st_spec = pl.BlockSpec((None, 1, S5_LANES), lambda b, t: (b, 0, 0))
        y, hr, hi = pl.pallas_call(
            _s5_seq_body,
            grid=(bsz, L // tt),
            in_specs=[pl.BlockSpec((None, tt, w), lambda b, t: (b, t, 0)), st_spec, st_spec] + w_specs,
            out_specs=[pl.BlockSpec((None, tt, w), lambda b, t: (b, t, 0)), st_spec, st_spec],
            out_shape=[jax.ShapeDtypeStruct((bsz, L, w), f32), jax.ShapeDtypeStruct((bsz, 1, S5_LANES), f32),
                       jax.ShapeDtypeStruct((bsz, 1, S5_LANES), f32)],
            scratch_shapes=[pltpu.VMEM((tt, S5_LANES), f32), pltpu.VMEM((tt, S5_LANES), f32),
                            pltpu.VMEM((1, S5_LANES), f32), pltpu.VMEM((1, S5_LANES), f32)],
            compiler_params=pltpu.CompilerParams(
                dimension_semantics=("parallel", "arbitrary"), vmem_limit_bytes=VMEM_LIMIT_BYTES),
            name="s5_seq",
        )(u, h0r, h0i, *w_args)
    shp = (bsz, S5_GROUPS, S5_STATE)
    return y, hr.reshape(shp).astype(h0_re.dtype), hi.reshape(shp).astype(h0_im.dtype)


DSA_KEY_CHUNK = 512
DSA_NEAR = 2 * QBLOCK
INT32_MIN = -2 ** 31
NEG_BIG = -1e30


def _sortable_key(s):
    bits = lax.bitcast_convert_type(s, jnp.int32)
    return jnp.where(bits < 0, bits ^ jnp.int32(0x7FFFFFFF), bits)


def _dsa_prompt_body(n_top, rb_ref, q_ref, qi_ref, wi_ref, kt_ref, v_ref, kit_ref, o_ref, keys_ref, bias_ref):
    i = pl.program_id(1)
    f32, bf16 = jnp.float32, jnp.bfloat16
    kc = DSA_KEY_CHUNK
    q_start = i * QBLOCK
    n_all = (q_start + QBLOCK + kc - 1) // kc
    near_start = jnp.maximum(q_start - QBLOCK, 0)
    n_far = (near_start + kc - 1) // kc
    row = lax.broadcasted_iota(jnp.int32, (QBLOCK, 1), 0)
    qpos = q_start + row

    @pl.when(i == 0)
    def _():
        r = lax.broadcasted_iota(jnp.int32, (QBLOCK, DSA_NEAR), 0)
        c = lax.broadcasted_iota(jnp.int32, (QBLOCK, DSA_NEAR), 1)
        for tile in range(2):
            bucket = t5_bucket(r + tile * QBLOCK - c)
            for h in range(DSA_HEADS):
                b = jnp.zeros((QBLOCK, DSA_NEAR), f32)
                for bk in range(N_BUCKETS):
                    b = jnp.where(bucket == bk, rb_ref[bk, h] - rb_ref[N_BUCKETS - 1, h], b)
                kv, g = divmod(h, DSA_GQA)
                bias_ref[tile, kv, g * QBLOCK:(g + 1) * QBLOCK, :] = b

    qi = (qi_ref[...].astype(f32) * IDX_DIM ** -0.5).astype(bf16)
    wi = wi_ref[...]

    def score_chunk(c, _):
        off = pl.multiple_of(c * kc, kc)
        d = jnp.dot(qi, kit_ref[:, pl.ds(off, kc)], preferred_element_type=f32)
        s = jnp.zeros((QBLOCK, kc), f32)
        for h in range(IDX_HEADS):
            s = s + wi[:, h:h + 1] * jnp.maximum(d[h * QBLOCK:(h + 1) * QBLOCK], 0.0)
        kpos = off + lax.broadcasted_iota(jnp.int32, (QBLOCK, kc), 1)
        s = jnp.where(s == 0.0, 0.0, s)
        s = jnp.where(kpos <= qpos, s, -jnp.inf)
        keys_ref[:, pl.ds(off, kc)] = _sortable_key(s)
        return 0

    lax.fori_loop(0, n_all, score_chunk, 0)

    def count_where(pred_fn):
        def body(c, acc):
            off = pl.multiple_of(c * kc, kc)
            hit = pred_fn(keys_ref[:, pl.ds(off, kc)], off)
            part = jnp.where(hit, 1.0, 0.0)
            for j in range(kc // 128):
                acc = acc + part[:, j * 128:(j + 1) * 128]
            return acc
        acc = lax.fori_loop(0, n_all, body, jnp.zeros((QBLOCK, 128), f32))
        return jnp.sum(acc, axis=-1, keepdims=True)

    def thr_bit(it, t):
        cand = t + lax.shift_left(jnp.int32(1), 31 - it)
        cnt = count_where(lambda k, off: k >= cand)
        return jnp.where(cnt >= n_top, cand, t)

    thr = lax.fori_loop(0, 32, thr_bit, jnp.full((QBLOCK, 1), INT32_MIN, jnp.int32))

    def is_valid(off, width):
        return (off + lax.broadcasted_iota(jnp.int32, (QBLOCK, width), 1)) <= qpos

    n_gt = count_where(lambda k, off: (k > thr) & is_valid(off, kc))
    n_eq = count_where(lambda k, off: (k == thr) & is_valid(off, kc))
    need = n_top - n_gt
    has_extra_ties = jnp.max(jnp.where(n_eq > need, 1.0, 0.0)) > 0.0

    def tie_search():
        def idx_bit(it, j):
            cand = j + lax.shift_left(jnp.int32(1), 13 - it)
            cnt = count_where(lambda k, off: (k == thr) & is_valid(off, kc)
                              & ((off + lax.broadcasted_iota(jnp.int32, (QBLOCK, kc), 1)) < cand))
            return jnp.where(cnt <= need, cand, j)
        return lax.fori_loop(0, 14, idx_bit, jnp.zeros((QBLOCK, 1), jnp.int32))

    tie_end = lax.cond(has_extra_ties, tie_search, lambda: jnp.full((QBLOCK, 1), 2 ** 14, jnp.int32))

    def selected(keys, off, width):
        kpos = off + lax.broadcasted_iota(jnp.int32, (QBLOCK, width), 1)
        return (kpos <= qpos) & ((keys > thr) | ((keys == thr) & (kpos < tie_end))), kpos

    rows = DSA_GQA * QBLOCK
    tile = jnp.minimum(i, 1)
    qks = [(q_ref[kv].astype(f32) * HEAD_DIM ** -0.5).astype(bf16) for kv in range(DSA_KV_HEADS)]

    def attend(carry, qk, kt, vv, mask_bias):
        m, l, acc = carry
        logits = jnp.dot(qk, kt, preferred_element_type=f32) + mask_bias
        m_new = jnp.maximum(m, jnp.max(logits, axis=-1, keepdims=True))
        alpha = jnp.exp(m - m_new)
        p = jnp.exp(logits - m_new)
        l = alpha * l + jnp.sum(p, axis=-1, keepdims=True)
        acc = alpha * acc + jnp.dot(p.astype(bf16), vv, preferred_element_type=f32)
        return m_new, l, acc

    def far_chunk(c, carry):
        off = pl.multiple_of(c * kc, kc)
        sel, kpos = selected(keys_ref[:, pl.ds(off, kc)], off, kc)
        mb = jnp.where(sel & (kpos < near_start), 0.0, NEG_BIG)
        mb3 = jnp.concatenate([mb] * DSA_GQA, axis=0)
        return tuple(attend(carry[kv], qks[kv], kt_ref[kv * HEAD_DIM:(kv + 1) * HEAD_DIM, pl.ds(off, kc)],
                            v_ref[kv, pl.ds(off, kc), :], mb3) for kv in range(DSA_KV_HEADS))

    init = tuple((jnp.full((rows, 1), NEG_BIG, f32), jnp.zeros((rows, 1), f32), jnp.zeros((rows, HEAD_DIM), f32))
                 for _ in range(DSA_KV_HEADS))
    carry = lax.fori_loop(0, n_far, far_chunk, init)
    off = pl.multiple_of(near_start, QBLOCK)
    sel, _ = selected(keys_ref[:, pl.ds(off, DSA_NEAR)], off, DSA_NEAR)
    sel3 = jnp.concatenate([sel] * DSA_GQA, axis=0)
    for kv in range(DSA_KV_HEADS):
        mb3 = jnp.where(sel3, bias_ref[tile, kv], NEG_BIG)
        m, l, acc = attend(carry[kv], qks[kv], kt_ref[kv * HEAD_DIM:(kv + 1) * HEAD_DIM, pl.ds(off, DSA_NEAR)],
                           v_ref[kv, pl.ds(off, DSA_NEAR), :], mb3)
        o_ref[kv] = acc / l


def dsa_prompt_pallas(q, k, v, qi, ki, wi, rel_bias):
    bsz, L = q.shape[:2]
    nq = L // QBLOCK
    n_top = min(TOPK_MAX, L // 4)
    f32, bf16 = jnp.float32, jnp.bfloat16
    q_blk = q.astype(bf16).reshape(bsz, nq, QBLOCK, DSA_KV_HEADS, DSA_GQA, HEAD_DIM)
    q_blk = q_blk.transpose(0, 1, 3, 4, 2, 5).reshape(bsz, nq, DSA_KV_HEADS, DSA_GQA * QBLOCK, HEAD_DIM)
    qi_blk = qi.astype(bf16).reshape(bsz, nq, QBLOCK, IDX_HEADS, IDX_DIM)
    qi_blk = qi_blk.transpose(0, 1, 3, 2, 4).reshape(bsz, nq, IDX_HEADS * QBLOCK, IDX_DIM)
    k_t = k.astype(bf16).reshape(bsz, L, DSA_KV_HEADS * HEAD_DIM).swapaxes(1, 2)
    v_h = v.astype(bf16).swapaxes(1, 2)
    ki_t = ki.astype(bf16).swapaxes(1, 2)
    lk = max(L, DSA_KEY_CHUNK)
    if lk != L:
        k_t = jnp.pad(k_t, ((0, 0), (0, 0), (0, lk - L)))
        v_h = jnp.pad(v_h, ((0, 0), (0, 0), (0, lk - L), (0, 0)))
        ki_t = jnp.pad(ki_t, ((0, 0), (0, 0), (0, lk - L)))
    o = pl.pallas_call(
        functools.partial(_dsa_prompt_body, n_top),
        grid=(bsz, nq),
        in_specs=[
            pl.BlockSpec(memory_space=pltpu.SMEM),
            pl.BlockSpec((None, None, DSA_KV_HEADS, DSA_GQA * QBLOCK, HEAD_DIM), lambda b, i: (b, i, 0, 0, 0)),
            pl.BlockSpec((None, None, IDX_HEADS * QBLOCK, IDX_DIM), lambda b, i: (b, i, 0, 0)),
            pl.BlockSpec((None, QBLOCK, IDX_HEADS), lambda b, i: (b, i, 0)),
            pl.BlockSpec((None, DSA_KV_HEADS * HEAD_DIM, lk), lambda b, i: (b, 0, 0)),
            pl.BlockSpec((None, DSA_KV_HEADS, lk, HEAD_DIM), lambda b, i: (b, 0, 0, 0)),
            pl.BlockSpec((None, IDX_DIM, lk), lambda b, i: (b, 0, 0)),
        ],
        out_specs=pl.BlockSpec((None, None, DSA_KV_HEADS, DSA_GQA * QBLOCK, HEAD_DIM), lambda b, i: (b, i, 0, 0, 0)),
        out_shape=jax.ShapeDtypeStruct((bsz, nq, DSA_KV_HEADS, DSA_GQA * QBLOCK, HEAD_DIM), f32),
        scratch_shapes=[pltpu.VMEM((QBLOCK, lk), jnp.int32),
                        pltpu.VMEM((2, DSA_KV_HEADS, DSA_GQA * QBLOCK, DSA_NEAR), f32)],
        compiler_params=pltpu.CompilerParams(
            dimension_semantics=("parallel", "arbitrary"), vmem_limit_bytes=VMEM_LIMIT_BYTES),
        name="dsa_prompt",
    )(rel_bias.astype(f32), q_blk, qi_blk, wi.astype(f32), k_t, v_h, ki_t)
    o = o.reshape(bsz, nq, DSA_KV_HEADS, DSA_GQA, QBLOCK, HEAD_DIM).transpose(0, 1, 4, 2, 3, 5)
    return o.reshape(bsz, L, SEQ_WIDTH)


CONV_HALO = 8


def _sigmoid(x):
    return 1.0 / (1.0 + jnp.exp(-x))


def _gdn_prep_body(x_ref, ab_ref, ctx_ref, w_ref, alog_ref, dtb_ref, q_ref, k_ref, v_ref, gb_ref, cs_ref, xp_ref):
    tt = x_ref.shape[0]
    halo = CONV_W - 1

    @pl.when(pl.program_id(1) == 0)
    def _():
        xp_ref[CONV_HALO - halo:CONV_HALO, :] = ctx_ref[...]

    x = x_ref[...]
    xp_ref[CONV_HALO:CONV_HALO + tt, :] = x
    w = w_ref[...]
    y = xp_ref[CONV_HALO - halo:CONV_HALO - halo + tt, :] * w[0:1]
    for j in range(1, CONV_W):
        y = y + xp_ref[CONV_HALO - halo + j:CONV_HALO - halo + j + tt, :] * w[j:j + 1]
    last = x[tt - halo:, :]
    xp_ref[CONV_HALO - halo:CONV_HALO, :] = last
    cs_ref[...] = last
    y = y * _sigmoid(y)

    def l2n(a, scale):
        cols = []
        for h in range(GDN_HEADS):
            s = a[:, h * HEAD_DIM:(h + 1) * HEAD_DIM]
            cols.append(s * (lax.rsqrt(jnp.sum(s * s, axis=-1, keepdims=True) + EPS) * scale))
        return jnp.concatenate(cols, axis=-1)

    q_ref[...] = l2n(y[:, :SEQ_WIDTH], HEAD_DIM ** -0.5)
    k_ref[...] = l2n(y[:, SEQ_WIDTH:2 * SEQ_WIDTH], 1.0)
    v_ref[...] = y[:, 2 * SEQ_WIDTH:]
    ab = ab_ref[...]
    xa = ab + dtb_ref[...]
    softplus = jnp.maximum(xa, 0.0) + jnp.log(1.0 + jnp.exp(-jnp.abs(xa)))
    g = -jnp.exp(alog_ref[...]) * softplus
    lane = lax.broadcasted_iota(jnp.int32, ab.shape, 1)
    gb_ref[...] = jnp.where(lane < GDN_HEADS, g, _sigmoid(ab))


def gdn_prep_pallas(qkv, ab, ctx, conv_w, a_log, dt_bias, *, tt=256):
    bsz, L, ch = qkv.shape
    tt = min(tt, L)
    f32 = jnp.float32
    pad = lambda r: jnp.pad(r.astype(f32).reshape(1, -1), ((0, 0), (0, LANES - r.shape[-1])))
    tok = lambda n: pl.BlockSpec((None, tt, n), lambda b, t: (b, t, 0))
    const2 = lambda b, t: (0, 0)
    return pl.pallas_call(
        _gdn_prep_body,
        grid=(bsz, L // tt),
        in_specs=[tok(ch), tok(LANES), pl.BlockSpec((None, CONV_W - 1, ch), lambda b, t: (b, 0, 0)),
                  pl.BlockSpec((CONV_W, ch), const2), pl.BlockSpec((1, LANES), const2), pl.BlockSpec((1, LANES), const2)],
        out_specs=[tok(SEQ_WIDTH), tok(SEQ_WIDTH), tok(SEQ_WIDTH), tok(LANES),
                   pl.BlockSpec((None, CONV_W - 1, ch), lambda b, t: (b, 0, 0))],
        out_shape=[jax.ShapeDtypeStruct((bsz, L, SEQ_WIDTH), f32)] * 3
        + [jax.ShapeDtypeStruct((bsz, L, LANES), f32), jax.ShapeDtypeStruct((bsz, CONV_W - 1, ch), f32)],
        scratch_shapes=[pltpu.VMEM((CONV_HALO + tt, ch), f32)],
        compiler_params=pltpu.CompilerParams(
            dimension_semantics=("parallel", "arbitrary"), vmem_limit_bytes=VMEM_LIMIT_BYTES),
        name="gdn_prep",
    )(qkv, ab, ctx.astype(f32), conv_w.astype(f32), pad(a_log), pad(dt_bias))


def _split3(a):
    bf16, f32 = jnp.bfloat16, jnp.float32
    h = a.astype(bf16)
    r = a - h.astype(f32)
    m = r.astype(bf16)
    return h, m, (r - m.astype(f32)).astype(bf16)


def _mm_hi(a, b):
    f32 = jnp.float32
    ah, am, _ = _split3(a)
    bh, bm, _ = _split3(b)
    d = lambda x, y: jnp.dot(x, y, preferred_element_type=f32)
    return d(ah, bh) + (d(ah, bm) + d(am, bh))


def _mm_sel(sel, b):
    f32 = jnp.float32
    s = sel.astype(jnp.bfloat16)
    bh, bm, bl = _split3(b)
    d = lambda y: jnp.dot(s, y, preferred_element_type=f32)
    return d(bh) + (d(bm) + d(bl))


def _dot_nt(a, b):
    return lax.dot_general(a, b, (((1,), (1,)), ((), ())), preferred_element_type=jnp.float32)


def _dot_tn(a, b):
    return lax.dot_general(a, b, (((0,), (0,)), ((), ())), preferred_element_type=jnp.float32)


def _gdn_local_body(q_ref, k_ref, v_ref, gb_ref, uv_ref, wk_ref, qh_ref, kt_ref, qk_ref):
    f32, bf16 = jnp.float32, jnp.bfloat16
    c = GDN_CHUNK
    r_i = lax.broadcasted_iota(jnp.int32, (c, c), 0)
    c_i = lax.broadcasted_iota(jnp.int32, (c, c), 1)
    tril = r_i >= c_i
    stril = r_i > c_i
    triu = r_i <= c_i
    eye = jnp.where(r_i == c_i, 1.0, 0.0)
    ones = jnp.ones((c, c), jnp.bool_)
    lane = lax.broadcasted_iota(jnp.int32, (c, LANES), 1)
    gb = gb_ref[...]
    heads = range(GDN_HEADS)
    sl = [slice(h * HEAD_DIM, (h + 1) * HEAD_DIM) for h in heads]
    pick = lambda idx: jnp.sum(jnp.where(lane == idx, gb, 0.0), axis=-1, keepdims=True)
    g_col = [pick(h) for h in heads]
    beta = [pick(GDN_HEADS + h) for h in heads]
    q = [q_ref[:, sl[h]] for h in heads]
    k = [k_ref[:, sl[h]] for h in heads]
    v = [v_ref[:, sl[h]] for h in heads]
    g_mat = [jnp.broadcast_to(g_col[h], (c, c)) for h in heads]
    cum_col = [_mm_sel(tril, g_mat[h]) for h in heads]
    cum_row = [_mm_sel(ones, jnp.where(triu, g_mat[h], 0.0)) for h in heads]
    gam = [jnp.where(tril, jnp.exp(jnp.where(tril, cum_col[h] - cum_row[h], 0.0)), 0.0) for h in heads]
    kb = [k[h].astype(bf16) for h in heads]
    a_mat = [jnp.where(stril, beta[h] * _dot_nt(kb[h], kb[h]) * gam[h], 0.0) for h in heads]
    t_inv = [eye - a_mat[h] for h in heads]
    pw = a_mat
    for _ in range(5):
        pw = [_mm_hi(pw[h], pw[h]) for h in heads]
        t_inv = [t_inv[h] + _mm_hi(t_inv[h], pw[h]) for h in heads]
    gc = [cum_col[h][:, :1] for h in heads]
    egc = [jnp.exp(gc[h]) for h in heads]
    tw = [_mm_hi(t_inv[h], jnp.concatenate([beta[h] * v[h], (beta[h] * egc[h]) * k[h]], axis=-1)) for h in heads]
    qk = [jnp.where(tril, _dot_nt(q[h].astype(bf16), kb[h]) * gam[h], 0.0) for h in heads]
    cat = lambda xs: jnp.concatenate(xs, axis=-1)
    uv_ref[...] = cat([tw[h][:, :HEAD_DIM] for h in heads])
    wk_ref[...] = cat([tw[h][:, HEAD_DIM:] for h in heads]).astype(bf16)
    qh_ref[...] = cat([egc[h] * q[h] for h in heads]).astype(bf16)
    eye_b = eye.astype(bf16)
    kt = [(jnp.exp(gc[h][c - 1:c, :] - gc[h]) * k[h]).astype(bf16) for h in heads]
    kt_ref[...] = cat([_dot_tn(kt[h], eye_b) for h in heads]).astype(bf16)
    qk_ref[...] = cat(qk).astype(bf16)


def gdn_local_pallas(qn, kn, vv, gb):
    bsz, L, w = qn.shape
    c = GDN_CHUNK
    f32, bf16 = jnp.float32, jnp.bfloat16
    tok = lambda n: pl.BlockSpec((None, c, n), lambda b, t: (b, t, 0))
    return pl.pallas_call(
        _gdn_local_body,
        grid=(bsz, L // c),
        in_specs=[tok(w), tok(w), tok(w), tok(LANES)],
        out_specs=[tok(w)] * 5,
        out_shape=[jax.ShapeDtypeStruct((bsz, L, w), f32)] + [jax.ShapeDtypeStruct((bsz, L, w), bf16)] * 4,
        compiler_params=pltpu.CompilerParams(
            dimension_semantics=("parallel", "parallel"), vmem_limit_bytes=VMEM_LIMIT_BYTES),
        name="gdn_local",
    )(qn, kn, vv, gb)


def _gdn_scan_body(uv_ref, wk_ref, qh_ref, ktt_ref, qk_ref, gb_ref, gate_ref, s0_ref, on_ref, o_ref, sf_ref, s_ref):
    f32, bf16 = jnp.float32, jnp.bfloat16
    c = GDN_CHUNK
    tt = uv_ref.shape[0]
    heads = range(GDN_HEADS)
    sl = [slice(h * HEAD_DIM, (h + 1) * HEAD_DIM) for h in heads]

    @pl.when(pl.program_id(1) == 0)
    def _():
        s_ref[...] = s0_ref[...]

    lane = lax.broadcasted_iota(jnp.int32, (1, LANES), 1)
    dot = lambda a, b: jnp.dot(a, b, preferred_element_type=f32)

    def chunk(ci, _):
        rows = pl.ds(pl.multiple_of(ci * c, c), c)
        eg_last = jnp.exp(jnp.sum(gb_ref[rows, :], axis=0, keepdims=True))
        eg = [jnp.sum(jnp.where(lane == h, eg_last, 0.0), axis=-1, keepdims=True) for h in heads]
        st = [s_ref[h] for h in heads]
        sb = [st[h].astype(bf16) for h in heads]
        u = [uv_ref[rows, sl[h]] - dot(wk_ref[rows, sl[h]], sb[h]) for h in heads]
        ub = [u[h].astype(bf16) for h in heads]
        new = [eg[h] * st[h] + dot(ktt_ref[rows, sl[h]], ub[h]) for h in heads]
        for h in heads:
            s_ref[h] = new[h]
        o = [dot(qh_ref[rows, sl[h]], sb[h]) + dot(qk_ref[rows, sl[h]], ub[h]) for h in heads]
        o = [o[h] * lax.rsqrt(jnp.mean(o[h] * o[h], axis=-1, keepdims=True) + EPS) for h in heads]
        gt = gate_ref[rows, :]
        o_ref[rows, :] = (jnp.concatenate(o, axis=-1) * on_ref[...]) * (gt * _sigmoid(gt))
        return 0

    lax.fori_loop(0, tt // c, chunk, 0)
    sf_ref[...] = s_ref[...]


def gdn_scan_pallas(uv, wk, qh, kt, qk, gb, gate, s0, o_norm, *, tt=256):
    bsz, L, w = uv.shape
    tt = min(tt, L)
    f32 = jnp.float32
    tok = lambda n: pl.BlockSpec((None, tt, n), lambda b, t: (b, t, 0))
    st = pl.BlockSpec((None, GDN_HEADS, HEAD_DIM, HEAD_DIM), lambda b, t: (b, 0, 0, 0))
    on = jnp.tile(o_norm.astype(f32), GDN_HEADS).reshape(1, w)
    return pl.pallas_call(
        _gdn_scan_body,
        grid=(bsz, L // tt),
        in_specs=[tok(w)] * 5 + [tok(LANES), tok(w), st, pl.BlockSpec((1, w), lambda b, t: (0, 0))],
        out_specs=[tok(w), st],
        out_shape=[jax.ShapeDtypeStruct((bsz, L, w), f32), jax.ShapeDtypeStruct(s0.shape, f32)],
        scratch_shapes=[pltpu.VMEM((GDN_HEADS, HEAD_DIM, HEAD_DIM), f32)],
        compiler_params=pltpu.CompilerParams(
            dimension_semantics=("parallel", "arbitrary"), vmem_limit_bytes=VMEM_LIMIT_BYTES),
        name="gdn_scan",
    )(uv, wk, qh, kt, qk, gb, gate, s0.astype(f32).swapaxes(2, 3), on)


def gdn_mix_pallas(z, conv_ctx, s0, conv_w, a_log, dt_bias, o_norm):
    qkv = z[..., :CONV_CH]
    ab = jnp.pad(z[..., CONV_CH:CONV_CH + 2 * GDN_HEADS], ((0, 0), (0, 0), (0, LANES - 2 * GDN_HEADS)))
    gate = z[..., CONV_CH + 2 * GDN_HEADS:]
    qn, kn, vv, gb, conv_state = gdn_prep_pallas(qkv, ab, conv_ctx, conv_w, a_log, dt_bias)
    uv, wk, qh, kt, qk = gdn_local_pallas(qn, kn, vv, gb)
    o, s_fin_t = gdn_scan_pallas(uv, wk, qh, kt, qk, gb, gate, s0, o_norm)
    return o, conv_state.astype(z.dtype), s_fin_t.swapaxes(2, 3).astype(s0.dtype)


def _ffn_tokens(x, g, ffn):
    b, l, d = x.shape
    t = b * l
    tm = 512 if t % 512 == 0 else t
    return ffn_residual(x.reshape(t, d), g, *ffn, tm=tm, tf=256).reshape(b, l, d)


def pre_mix(x, g, ffn, w_in):
    x = _ffn_tokens(x, g[0], ffn)
    z = rmsnorm(x, g[1]) @ w_in
    return x, z[..., :MEM_WIDTH], z[..., MEM_WIDTH:]


def post_mix(x, o_mem, o_mix, w_out, g, ffn):
    x = x + jnp.concatenate([o_mem, o_mix], axis=-1) @ w_out
    return _ffn_tokens(x, g[2], ffn)


def kernel(x_prompt, x_sample, cache_mem_k, cache_mem_v, state_ssm_re, state_ssm_im, cache_k, cache_v, cache_kidx, state_conv, state_delta, page_table, mem_prompt, norm_g, final_norm, w_in_a, w_in_b, w_in_c, w_out, w_mem_kv, ffn1_gate, ffn1_up, ffn1_down, ffn2_gate, ffn2_up, ffn2_down, s5_lam_re, s5_lam_im, s5_log_step, s5_b_re, s5_b_im, s5_c_re, s5_c_im, s5_d, s5_w_glu, s5_b_glu, rel_bias, gdn_conv_w, gdn_a_log, gdn_dt_bias, gdn_o_norm):
    depth = norm_g.shape[0]
    bp, bs = x_prompt.shape[0], x_sample.shape[0]
    past = page_table.shape[1] * PAGE_SIZE
    bf16 = jnp.bfloat16
    hp, hs = x_prompt, x_sample
    mem_k_p, mem_v_p = [], []
    ssm_re_p, ssm_im_p, ssm_re_s, ssm_im_s = [], [], [], []
    k_p, v_p, ki_p, k_s, v_s, ki_s = [], [], [], [], [], []
    conv_p, delta_p, conv_s, delta_s = [], [], [], []
    w_in_by_kind = (w_in_a, w_in_b, w_in_c)
    for i in range(depth):
        kind, j = i % N_MIXERS, i // N_MIXERS
        w_in = w_in_by_kind[kind][j]
        ffn1 = (ffn1_gate[i].astype(bf16), ffn1_up[i].astype(bf16), ffn1_down[i].astype(bf16))
        ffn2 = (ffn2_gate[i].astype(bf16), ffn2_up[i].astype(bf16), ffn2_down[i].astype(bf16))
        hp, cq_p, zp = pre_mix(hp, norm_g[i], ffn1, w_in)
        hs, cq_s, zs = pre_mix(hs, norm_g[i], ffn1, w_in)
        mk, mv = mem_kv(mem_prompt, w_mem_kv[i])
        mem_k_p.append(mk)
        mem_v_p.append(mv)
        om_p = mem_attend(cq_p, mk, mv)
        om_s = mem_attend(cq_s, cache_mem_k[i], cache_mem_v[i])
        if kind == 0:
            disc = s5_discretize(s5_lam_re[j], s5_lam_im[j], s5_log_step[j], s5_b_re[j], s5_b_im[j],
                                 s5_c_re[j], s5_c_im[j])
            gate = (s5_d[j], s5_w_glu[j], s5_b_glu[j])
            h0 = jnp.zeros((bp, S5_GROUPS, S5_STATE), state_ssm_re.dtype)
            op, hr, hi = s5_mix_pallas(zp, h0, h0, disc, *gate)
            ssm_re_p.append(hr)
            ssm_im_p.append(hi)
            osm, hr, hi = s5_mix_pallas(zs, state_ssm_re[j], state_ssm_im[j], disc, *gate)
            ssm_re_s.append(hr)
            ssm_im_s.append(hi)
        elif kind == 1:
            q, k, v, qi, ki, wi = dsa_heads(zp)
            op = dsa_prompt_pallas(q, k, v, qi, ki, wi, rel_bias)
            k_p.append(k)
            v_p.append(v)
            ki_p.append(ki)
            q, k, v, qi, ki, wi = dsa_heads(zs)
            ki_all = jnp.concatenate([cache_kidx[j][page_table].reshape(bs, past, IDX_DIM), ki], axis=1)
            qpos = past + jnp.arange(zs.shape[1], dtype=jnp.int32)
            gather = paged_gather(cache_k[j], cache_v[j], page_table, k, v)
            osm = dsa_attend(q, qi, wi, qpos, ki_all, gather, rel_bias)
            k_s.append(k)
            v_s.append(v)
            ki_s.append(ki)
        else:
            gdn = (gdn_conv_w[j], gdn_a_log[j], gdn_dt_bias[j], gdn_o_norm[j])
            ctx0 = jnp.zeros((bp, CONV_W - 1, CONV_CH), zp.dtype)
            s00 = jnp.zeros((bp, GDN_HEADS, HEAD_DIM, HEAD_DIM), state_delta.dtype)
            op, cst, sst = gdn_mix_pallas(zp, ctx0, s00, *gdn)
            conv_p.append(cst)
            delta_p.append(sst)
            osm, cst, sst = gdn_mix(zs, state_conv[j], state_delta[j], *gdn)
            conv_s.append(cst)
            delta_s.append(sst)
        hp = post_mix(hp, om_p, op, w_out[i], norm_g[i], ffn2)
        hs = post_mix(hs, om_s, osm, w_out[i], norm_g[i], ffn2)
    y_prompt = rmsnorm(hp, final_norm)
    y_sample = rmsnorm(hs, final_norm)
    st = jnp.stack
    return (y_prompt, y_sample, st(mem_k_p), st(mem_v_p),
            st(ssm_re_p), st(ssm_im_p), st(ssm_re_s), st(ssm_im_s),
            st(k_p), st(v_p), st(ki_p), st(k_s), st(v_s), st(ki_s),
            st(conv_p), st(delta_p), st(conv_s), st(delta_s))
```

```python
import math
import functools
import jax
import jax.numpy as jnp
from jax import lax
import numpy as np
from jax.experimental import pallas as pl
from jax.experimental.pallas import tpu as pltpu

D_MODEL = 1024
N_MIXERS = 3
HEAD_DIM = 64
MIX_WIDTH = D_MODEL
N_MEM = 256
MEM_HEADS = 4
MEM_WIDTH = MEM_HEADS * HEAD_DIM
SEQ_WIDTH = MIX_WIDTH - MEM_WIDTH
S5_GROUP = 16
S5_GROUPS = SEQ_WIDTH // S5_GROUP
S5_STATE = 64
DSA_HEADS = SEQ_WIDTH // HEAD_DIM
DSA_KV_HEADS = 4
DSA_GQA = DSA_HEADS // DSA_KV_HEADS
IDX_HEADS = 8
IDX_DIM = 64
TOPK_MAX = 256
QBLOCK = 128
N_BUCKETS = 32
MAX_DISTANCE = 128
GDN_HEADS = SEQ_WIDTH // HEAD_DIM
CONV_W = 4
CONV_CH = 3 * SEQ_WIDTH
GDN_CHUNK = 64
D_FF = 2816
EPS = 1e-6
PAGE_SIZE = 128
DSA_SPLITS = (DSA_HEADS * HEAD_DIM, DSA_KV_HEADS * HEAD_DIM, DSA_KV_HEADS * HEAD_DIM, IDX_HEADS * IDX_DIM, IDX_DIM, IDX_HEADS)
GDN_SPLITS = (CONV_CH, GDN_HEADS, GDN_HEADS, SEQ_WIDTH)

LANES = 128
VMEM_LIMIT_BYTES = 48 * 1024 * 1024


def split_cols(z, sizes):
    return jnp.split(z, np.cumsum(sizes)[:-1].tolist(), axis=-1)


def rmsnorm(x, g):
    x32 = x.astype(jnp.float32)
    r = lax.rsqrt(jnp.mean(x32 * x32, axis=-1, keepdims=True) + EPS)
    return (x32 * r).astype(x.dtype) * g


def l2norm(x):
    return x * lax.rsqrt(jnp.sum(x * x, axis=-1, keepdims=True) + EPS)


def take_rows(rows, idx):
    return jax.vmap(lambda r, i: r[i])(rows, idx)


def _sigmoid(x):
    return 1.0 / (1.0 + jnp.exp(-x))


def t5_bucket(dist):
    max_exact = N_BUCKETS // 2
    n = jnp.maximum(dist, 0)
    nf = jnp.maximum(n, 1).astype(jnp.float32)
    large = max_exact + (jnp.log(nf / max_exact) / math.log(MAX_DISTANCE / max_exact)
                         * (N_BUCKETS - max_exact)).astype(jnp.int32)
    return jnp.where(n < max_exact, n, jnp.minimum(large, N_BUCKETS - 1))


def dsa_block(q, qi, wi, qpos, ki_all, gather_kv, rel_bias, n_top):
    dots = jnp.einsum('bqhd,bsd->bqhs', qi, ki_all) * IDX_DIM ** -0.5
    score = jnp.einsum('bqh,bqhs->bqs', wi, jax.nn.relu(dots)).astype(jnp.float32)
    kpos = jnp.arange(ki_all.shape[1], dtype=jnp.int32)
    score = jnp.where(kpos[None, None, :] <= qpos[None, :, None], score, -jnp.inf)
    _, idx = lax.top_k(score, n_top)
    valid = idx <= qpos[None, :, None]
    k_sel, v_sel = gather_kv(idx)
    logits = jnp.einsum('bqkgd,bqnkd->bqkgn', q, k_sel).astype(jnp.float32) * HEAD_DIM ** -0.5
    bias = rel_bias[t5_bucket(qpos[None, :, None] - idx)].astype(jnp.float32)
    bias = bias.reshape(bias.shape[:3] + (DSA_KV_HEADS, DSA_GQA))
    logits = logits + jnp.moveaxis(bias, 2, -1)
    logits = jnp.where(valid[:, :, None, None, :], logits, -jnp.inf)
    p = jax.nn.softmax(logits, axis=-1).astype(v_sel.dtype)
    return jnp.einsum('bqkgn,bqnkd->bqkgd', p, v_sel)


def dsa_attend(q, qi, wi, qpos, ki_all, gather_kv, rel_bias):
    bsz, lq = q.shape[:2]
    n_top = min(TOPK_MAX, ki_all.shape[1] // 4)
    block = functools.partial(dsa_block, ki_all=ki_all, gather_kv=gather_kv, rel_bias=rel_bias, n_top=n_top)
    if lq <= QBLOCK or lq % QBLOCK != 0:
        o = block(q, qi, wi, qpos)
    else:
        nb = lq // QBLOCK
        def to_blocks(a):
            return jnp.moveaxis(a.reshape((bsz, nb, QBLOCK) + a.shape[2:]), 1, 0)
        o = lax.map(lambda a: block(*a), (to_blocks(q), to_blocks(qi), to_blocks(wi), qpos.reshape(nb, QBLOCK)))
        o = jnp.moveaxis(o, 0, 1)
    return o.reshape(bsz, lq, SEQ_WIDTH)


def paged_gather(pool_k, pool_v, page_table, k_new, v_new):
    past = page_table.shape[1] * PAGE_SIZE
    lnew = k_new.shape[1]
    def gather(idx):
        ic = jnp.minimum(idx, past - 1)
        phys = take_rows(page_table, ic // PAGE_SIZE)
        off = ic % PAGE_SIZE
        inew = jnp.clip(idx - past, 0, lnew - 1)
        in_past = (idx < past)[..., None, None]
        return (jnp.where(in_past, pool_k[phys, off], take_rows(k_new, inew)),
                jnp.where(in_past, pool_v[phys, off], take_rows(v_new, inew)))
    return gather


def causal_conv(x, ctx, w):
    L = x.shape[1]
    xp = jnp.concatenate([ctx, x], axis=1)
    y = xp[:, 0:L] * w[0]
    for j in range(1, CONV_W):
        y = y + xp[:, j:j + L] * w[j]
    return jax.nn.silu(y), xp[:, L:]


def gated_delta(q, k, v, g, beta, s0):
    bsz, L, nh, dk = q.shape
    c = GDN_CHUNK if L % GDN_CHUNK == 0 else L
    n = L // c
    def chunks(a):
        a = a.reshape((bsz, n, c) + a.shape[2:])
        return jnp.moveaxis(jnp.moveaxis(a, 3, 2), 1, 0)
    q = chunks(q) * dk ** -0.5
    k, v, g, beta = chunks(k), chunks(v), chunks(g), chunks(beta)
    gc = jnp.cumsum(g, axis=-1)
    tril = jnp.tril(jnp.ones((c, c), dtype=bool))
    stril = jnp.tril(jnp.ones((c, c), dtype=bool), -1)
    gam = jnp.exp(jnp.where(tril, gc[..., :, None] - gc[..., None, :], -jnp.inf))
    a_mat = jnp.where(stril, beta[..., :, None] * jnp.einsum('nbhtd,nbhjd->nbhtj', k, k) * gam, 0.0)
    lhs = a_mat + jnp.eye(c, dtype=a_mat.dtype)
    solve = functools.partial(lax.linalg.triangular_solve, left_side=True, lower=True, unit_diagonal=True)
    uv = solve(lhs, beta[..., None] * v)
    wk = solve(lhs, (beta * jnp.exp(gc))[..., None] * k)
    qk = jnp.where(tril, jnp.einsum('nbhtd,nbhjd->nbhtj', q, k) * gam, 0.0)
    q_head = jnp.exp(gc)[..., None] * q
    g_last = gc[..., -1]
    k_tail = jnp.exp(g_last[..., None] - gc)[..., None] * k
    def step(s, xs):
        uv_c, wk_c, qk_c, qh_c, kt_c, gl_c = xs
        u = uv_c - jnp.einsum('bhck,bhvk->bhcv', wk_c, s)
        o = jnp.einsum('bhck,bhvk->bhcv', qh_c, s) + jnp.einsum('bhtj,bhjv->bhtv', qk_c, u)
        s = jnp.exp(gl_c)[..., None, None] * s + jnp.einsum('bhcv,bhck->bhvk', u, kt_c)
        return s, o
    s_fin, o = lax.scan(step, s0, (uv, wk, qk, q_head, k_tail, g_last))
    o = jnp.moveaxis(jnp.moveaxis(o, 0, 1), 2, 3).reshape(bsz, L, nh, v.shape[-1])
    return o, s_fin


def gdn_mix(z, conv_ctx, s0, conv_w, a_log, dt_bias, o_norm):
    f32 = jnp.float32
    bsz, L = z.shape[:2]
    qkv, a, b, gate = split_cols(z, GDN_SPLITS)
    qkv, conv_state = causal_conv(qkv, conv_ctx, conv_w)
    q, k, v = jnp.split(qkv.astype(f32), 3, axis=-1)
    hs = (bsz, L, GDN_HEADS, HEAD_DIM)
    q, k, v = l2norm(q.reshape(hs)), l2norm(k.reshape(hs)), v.reshape(hs)
    g = -jnp.exp(a_log.astype(f32)) * jax.nn.softplus((a + dt_bias).astype(f32))
    beta = jax.nn.sigmoid(b.astype(f32))
    o, s_fin = gated_delta(q, k, v, g, beta, s0.astype(f32))
    o = rmsnorm(o.astype(z.dtype), o_norm) * jax.nn.silu(gate.reshape(hs))
    return o.reshape(bsz, L, SEQ_WIDTH), conv_state, s_fin.astype(s0.dtype)


def _ffn_body(final_norm, x_ref, g_ref, wg_ref, wu_ref, wd_ref, gf_ref, o_ref, n_ref, acc_ref):
    f = pl.program_id(1)

    @pl.when(f == 0)
    def _():
        x = x_ref[...]
        r = lax.rsqrt(jnp.mean(x * x, axis=-1, keepdims=True) + EPS)
        n_ref[...] = ((x * r) * g_ref[...]).astype(jnp.bfloat16)
        acc_ref[...] = jnp.zeros_like(acc_ref)

    n = n_ref[...]
    a = jnp.dot(n, wg_ref[...], preferred_element_type=jnp.float32)
    b = jnp.dot(n, wu_ref[...], preferred_element_type=jnp.float32)
    h = (a * _sigmoid(a)) * b
    acc_ref[...] += jnp.dot(h.astype(jnp.bfloat16), wd_ref[...], preferred_element_type=jnp.float32)

    @pl.when(f == pl.num_programs(1) - 1)
    def _():
        y = x_ref[...] + 0.5 * acc_ref[...]
        if final_norm:
            y = (y * lax.rsqrt(jnp.mean(y * y, axis=-1, keepdims=True) + EPS)) * gf_ref[...]
        o_ref[...] = y


def ffn_residual(x, g, wg, wu, wd, final_g=None, *, tm=512, tf=256):
    t, d = x.shape
    f = wg.shape[1]
    tm = min(tm, t)
    gf = jnp.ones((d,), jnp.float32) if final_g is None else final_g
    return pl.pallas_call(
        functools.partial(_ffn_body, final_g is not None),
        grid=(t // tm, f // tf),
        in_specs=[
            pl.BlockSpec((tm, d), lambda i, j: (i, 0)),
            pl.BlockSpec((1, d), lambda i, j: (0, 0)),
            pl.BlockSpec((d, tf), lambda i, j: (0, j)),
            pl.BlockSpec((d, tf), lambda i, j: (0, j)),
            pl.BlockSpec((tf, d), lambda i, j: (j, 0)),
            pl.BlockSpec((1, d), lambda i, j: (0, 0)),
        ],
        out_specs=pl.BlockSpec((tm, d), lambda i, j: (i, 0)),
        out_shape=jax.ShapeDtypeStruct((t, d), jnp.float32),
        scratch_shapes=[pltpu.VMEM((tm, d), jnp.bfloat16), pltpu.VMEM((tm, d), jnp.float32)],
        compiler_params=pltpu.CompilerParams(
            dimension_semantics=("parallel", "arbitrary"), vmem_limit_bytes=VMEM_LIMIT_BYTES),
        name="ffn_residual",
    )(x, g.reshape(1, d).astype(jnp.float32), wg, wu, wd, gf.reshape(1, d).astype(jnp.float32))


def _proj_in_body(segments, normalize, x_ref, g_ref, w_ref, *o_refs):
    x = x_ref[...]
    if normalize:
        r = lax.rsqrt(jnp.mean(x * x, axis=-1, keepdims=True) + EPS)
        x = (x * r) * g_ref[...]
    n = x.astype(jnp.bfloat16)
    for (start, width), o_ref in zip(segments, o_refs):
        o_ref[...] = jnp.dot(n, w_ref[:, start:start + width], preferred_element_type=jnp.float32).astype(o_ref.dtype)


def proj_in(x, g, w, segments, dtypes, *, normalize=True, tm=256):
    t, d = x.shape
    tm = min(tm, t)
    return pl.pallas_call(
        functools.partial(_proj_in_body, tuple(segments), normalize),
        grid=(t // tm,),
        in_specs=[pl.BlockSpec((tm, d), lambda i: (i, 0)), pl.BlockSpec((1, d), lambda i: (0, 0)),
                  pl.BlockSpec(w.shape, lambda i: (0, 0))],
        out_specs=[pl.BlockSpec((tm, width), lambda i: (i, 0)) for _, width in segments],
        out_shape=[jax.ShapeDtypeStruct((t, width), dt) for (_, width), dt in zip(segments, dtypes)],
        compiler_params=pltpu.CompilerParams(dimension_semantics=("parallel",), vmem_limit_bytes=VMEM_LIMIT_BYTES),
        name="proj_in",
    )(x, g.reshape(1, d).astype(jnp.float32), w)


def _mix_out_body(x_ref, cq_ref, om_ref, mkt_ref, mv_ref, w_ref, o_ref):
    f32, bf16 = jnp.float32, jnp.bfloat16
    cq = cq_ref[...]
    heads = []
    for h in range(MEM_HEADS):
        sl = slice(h * HEAD_DIM, (h + 1) * HEAD_DIM)
        logits = jnp.dot(cq[:, sl], mkt_ref[sl, :], preferred_element_type=f32) * HEAD_DIM ** -0.5
        p = jnp.exp(logits - jnp.max(logits, axis=-1, keepdims=True))
        p = p / jnp.sum(p, axis=-1, keepdims=True)
        heads.append(jnp.dot(p.astype(bf16), mv_ref[:, sl], preferred_element_type=f32))
    o_mem = jnp.concatenate(heads, axis=-1).astype(bf16)
    y = (jnp.dot(o_mem, w_ref[:MEM_WIDTH, :], preferred_element_type=f32)
         + jnp.dot(om_ref[...].astype(bf16), w_ref[MEM_WIDTH:, :], preferred_element_type=f32))
    o_ref[...] = x_ref[...] + y


def mix_out(x, cq, o_mix, mk, mv, w_out, *, tm=512):
    bsz, L, d = x.shape
    tm = min(tm, L)
    bf16 = jnp.bfloat16
    mkt = mk.astype(bf16).reshape(bsz, N_MEM, MEM_WIDTH).swapaxes(1, 2)
    mvf = mv.astype(bf16).reshape(bsz, N_MEM, MEM_WIDTH)
    tok = lambda n: pl.BlockSpec((None, tm, n), lambda b, t: (b, t, 0))
    per_b = lambda r, c: pl.BlockSpec((None, r, c), lambda b, t: (b, 0, 0))
    return pl.pallas_call(
        _mix_out_body,
        grid=(bsz, L // tm),
        in_specs=[tok(d), tok(MEM_WIDTH), tok(SEQ_WIDTH), per_b(MEM_WIDTH, N_MEM), per_b(N_MEM, MEM_WIDTH),
                  pl.BlockSpec((d, d), lambda b, t: (0, 0))],
        out_specs=tok(d),
        out_shape=jax.ShapeDtypeStruct((bsz, L, d), jnp.float32),
        compiler_params=pltpu.CompilerParams(
            dimension_semantics=("parallel", "parallel"), vmem_limit_bytes=VMEM_LIMIT_BYTES),
        name="mix_out",
    )(x, cq, o_mix, mkt, mvf, w_out)


S5_LANES = S5_GROUPS * S5_STATE
S5_GROUPS_PER_BLOCK = LANES // S5_GROUP
S5_BLOCKS = SEQ_WIDTH // LANES
S5_BLOCK_STATES = S5_GROUPS_PER_BLOCK * S5_STATE


def s5_discretize(lam_re, lam_im, log_step, b_re, b_im, c_re, c_im):
    f32, bf16 = jnp.float32, jnp.bfloat16
    lr, li = lam_re.astype(f32), lam_im.astype(f32)
    step = jnp.exp(log_step.astype(f32))[:, None]
    mag = jnp.exp(lr * step)
    ab_re, ab_im = mag * jnp.cos(li * step), mag * jnp.sin(li * step)
    den = lr * lr + li * li
    nr, ni = ab_re - 1.0, ab_im
    f_re = (nr * lr + ni * li) / den
    f_im = (ni * lr - nr * li) / den
    br, bi = b_re.astype(f32), b_im.astype(f32)
    bb_re = f_re[..., None] * br - f_im[..., None] * bi
    bb_im = f_re[..., None] * bi + f_im[..., None] * br
    eye = jnp.eye(S5_GROUPS_PER_BLOCK, dtype=f32)
    nb, gb = S5_BLOCKS, S5_GROUPS_PER_BLOCK

    def in_blocks(bb):
        w = jnp.einsum('jgpc,gh->jgchp', bb.reshape(nb, gb, S5_STATE, S5_GROUP), eye)
        return w.reshape(nb, LANES, S5_BLOCK_STATES).astype(bf16)

    def out_blocks(c):
        w = jnp.einsum('jgop,gh->jgpho', c.astype(f32).reshape(nb, gb, S5_GROUP, S5_STATE), eye)
        return w.reshape(nb, S5_BLOCK_STATES, LANES).astype(bf16)

    return (ab_re.reshape(1, S5_LANES), ab_im.reshape(1, S5_LANES),
            in_blocks(bb_re), in_blocks(bb_im), out_blocks(c_re), out_blocks(-c_im))


def _s5_project_in(u, wbr_ref, wbi_ref, bur_ref, bui_ref):
    ub = u.astype(jnp.bfloat16)
    for j in range(S5_BLOCKS):
        uj = ub[:, j * LANES:(j + 1) * LANES]
        sl = slice(j * S5_BLOCK_STATES, (j + 1) * S5_BLOCK_STATES)
        bur_ref[:, sl] = jnp.dot(uj, wbr_ref[j], preferred_element_type=jnp.float32)
        bui_ref[:, sl] = jnp.dot(uj, wbi_ref[j], preferred_element_type=jnp.float32)


def _s5_project_out(hr_ref, hi_ref, wcr_ref, wci_ref):
    cols = []
    for j in range(S5_BLOCKS):
        sl = slice(j * S5_BLOCK_STATES, (j + 1) * S5_BLOCK_STATES)
        cols.append(jnp.dot(hr_ref[:, sl].astype(jnp.bfloat16), wcr_ref[j], preferred_element_type=jnp.float32)
                    + jnp.dot(hi_ref[:, sl].astype(jnp.bfloat16), wci_ref[j], preferred_element_type=jnp.float32))
    return jnp.concatenate(cols, axis=-1)


def _s5_gate(y_ssm, u, d_ref, wglu_ref, bglu_ref):
    y = y_ssm + d_ref[...] * u
    y = 0.5 * y * (1.0 + jnp.tanh(math.sqrt(2.0 / math.pi) * (y + 0.044715 * (y * y * y))))
    z = jnp.dot(y.astype(jnp.bfloat16), wglu_ref[...], preferred_element_type=jnp.float32) + bglu_ref[...]
    return y * (1.0 / (1.0 + jnp.exp(-z)))


def _s5_seq_body(u_ref, h0r_ref, h0i_ref, ar_ref, ai_ref, wbr_ref, wbi_ref, wcr_ref, wci_ref, d_ref, wglu_ref, bglu_ref,
                 y_ref, hfr_ref, hfi_ref, bur_ref, bui_ref, cr_ref, ci_ref):
    tt = u_ref.shape[0]

    @pl.when(pl.program_id(1) == 0)
    def _():
        cr_ref[...] = h0r_ref[...]
        ci_ref[...] = h0i_ref[...]

    u = u_ref[...]
    _s5_project_in(u, wbr_ref, wbi_ref, bur_ref, bui_ref)
    ar, ai = ar_ref[...], ai_ref[...]

    def step(t, carry):
        hr, hi = carry
        nhr = ar * hr - ai * hi + bur_ref[pl.ds(t, 1), :]
        nhi = ar * hi + ai * hr + bui_ref[pl.ds(t, 1), :]
        bur_ref[pl.ds(t, 1), :] = nhr
        bui_ref[pl.ds(t, 1), :] = nhi
        return nhr, nhi

    hr, hi = lax.fori_loop(0, tt, step, (cr_ref[...], ci_ref[...]))
    cr_ref[...] = hr
    ci_ref[...] = hi
    hfr_ref[...] = hr
    hfi_ref[...] = hi
    y_ref[...] = _s5_gate(_s5_project_out(bur_ref, bui_ref, wcr_ref, wci_ref), u, d_ref, wglu_ref, bglu_ref)


def _s5_step_body(u_ref, h0r_ref, h0i_ref, ar_ref, ai_ref, wbr_ref, wbi_ref, wcr_ref, wci_ref, d_ref, wglu_ref, bglu_ref,
                  y_ref, hfr_ref, hfi_ref, bur_ref, bui_ref):
    u = u_ref[...]
    _s5_project_in(u, wbr_ref, wbi_ref, bur_ref, bui_ref)
    ar, ai = ar_ref[...], ai_ref[...]
    hr, hi = h0r_ref[...], h0i_ref[...]
    nhr = ar * hr - ai * hi + bur_ref[...]
    nhi = ar * hi + ai * hr + bui_ref[...]
    bur_ref[...] = nhr
    bui_ref[...] = nhi
    hfr_ref[...] = nhr
    hfi_ref[...] = nhi
    y_ref[...] = _s5_gate(_s5_project_out(bur_ref, bui_ref, wcr_ref, wci_ref), u, d_ref, wglu_ref, bglu_ref)


def s5_mix_pallas(u, h0_re, h0_im, disc, d_skip, w_glu, b_glu, *, tt=512):
    bsz, L, w = u.shape
    f32 = jnp.float32
    ar, ai, wbr, wbi, wcr, wci = disc
    d2, bg2, wg = d_skip.reshape(1, w).astype(f32), b_glu.reshape(1, w).astype(f32), w_glu.astype(jnp.bfloat16)
    const2 = lambda *_: (0, 0)
    const3 = lambda *_: (0, 0, 0)
    w_specs = [pl.BlockSpec((1, S5_LANES), const2), pl.BlockSpec((1, S5_LANES), const2),
               pl.BlockSpec(wbr.shape, const3), pl.BlockSpec(wbi.shape, const3),
               pl.BlockSpec(wcr.shape, const3), pl.BlockSpec(wci.shape, const3),
               pl.BlockSpec((1, w), const2), pl.BlockSpec((w, w), const2), pl.BlockSpec((1, w), const2)]
    w_args = (ar, ai, wbr, wbi, wcr, wci, d2, wg, bg2)
    if L == 1:
        rows = bsz
        h0r, h0i = h0_re.reshape(rows, S5_LANES).astype(f32), h0_im.reshape(rows, S5_LANES).astype(f32)
        row_spec = lambda n: pl.BlockSpec((rows, n), const2)
        y, hr, hi = pl.pallas_call(
            _s5_step_body,
            grid=(1,),
            in_specs=[row_spec(w), row_spec(S5_LANES), row_spec(S5_LANES)] + w_specs,
            out_specs=[row_spec(w), row_spec(S5_LANES), row_spec(S5_LANES)],
            out_shape=[jax.ShapeDtypeStruct((rows, w), f32), jax.ShapeDtypeStruct((rows, S5_LANES), f32),
                       jax.ShapeDtypeStruct((rows, S5_LANES), f32)],
            scratch_shapes=[pltpu.VMEM((rows, S5_LANES), f32), pltpu.VMEM((rows, S5_LANES), f32)],
            compiler_params=pltpu.CompilerParams(vmem_limit_bytes=VMEM_LIMIT_BYTES),
            name="s5_step",
        )(u.reshape(rows, w), h0r, h0i, *w_args)
        y = y.reshape(bsz, 1, w)
    else:
        tt = min(tt, L)
        h0r, h0i = h0_re.reshape(bsz, 1, S5_LANES).astype(f32), h0_im.reshape(bsz, 1, S5_LANES).astype(f32)
        st_spec = pl.BlockSpec((None, 1, S5_LANES), lambda b, t: (b, 0, 0))
        y, hr, hi = pl.pallas_call(
            _s5_seq_body,
            grid=(bsz, L // tt),
            in_specs=[pl.BlockSpec((None, tt, w), lambda b, t: (b, t, 0)), st_spec, st_spec] + w_specs,
            out_specs=[pl.BlockSpec((None, tt, w), lambda b, t: (b, t, 0)), st_spec, st_spec],
            out_shape=[jax.ShapeDtypeStruct((bsz, L, w), f32), jax.ShapeDtypeStruct((bsz, 1, S5_LANES), f32),
                       jax.ShapeDtypeStruct((bsz, 1, S5_LANES), f32)],
            scratch_shapes=[pltpu.VMEM((tt, S5_LANES), f32), pltpu.VMEM((tt, S5_LANES), f32),
                            pltpu.VMEM((1, S5_LANES), f32), pltpu.VMEM((1, S5_LANES), f32)],
            compiler_params=pltpu.CompilerParams(
                dimension_semantics=("parallel", "arbitrary"), vmem_limit_bytes=VMEM_LIMIT_BYTES),
            name="s5_seq",
        )(u, h0r, h0i, *w_args)
    shp = (bsz, S5_GROUPS, S5_STATE)
    return y, hr.reshape(shp).astype(h0_re.dtype), hi.reshape(shp).astype(h0_im.dtype)


DSA_KEY_CHUNK = 512
DSA_NEAR = 2 * QBLOCK
INT32_MIN = -2 ** 31
NEG_BIG = -1e30


def _sortable_key(s):
    bits = lax.bitcast_convert_type(s, jnp.int32)
    return jnp.where(bits < 0, bits ^ jnp.int32(0x7FFFFFFF), bits)


def _dsa_prompt_body(n_top, rb_ref, q_ref, qi_ref, kiwi_ref, kt_ref, v_ref, kit_ref, o_ref, keys_ref, bias_ref):
    i = pl.program_id(1)
    f32, bf16 = jnp.float32, jnp.bfloat16
    kc = DSA_KEY_CHUNK
    q_start = i * QBLOCK
    n_all = (q_start + QBLOCK + kc - 1) // kc
    near_start = jnp.maximum(q_start - QBLOCK, 0)
    n_far = (near_start + kc - 1) // kc
    row = lax.broadcasted_iota(jnp.int32, (QBLOCK, 1), 0)
    qpos = q_start + row

    @pl.when(i == 0)
    def _():
        r = lax.broadcasted_iota(jnp.int32, (QBLOCK, DSA_NEAR), 0)
        c = lax.broadcasted_iota(jnp.int32, (QBLOCK, DSA_NEAR), 1)
        for tile in range(2):
            bucket = t5_bucket(r + tile * QBLOCK - c)
            for h in range(DSA_HEADS):
                b = jnp.zeros((QBLOCK, DSA_NEAR), f32)
                for bk in range(N_BUCKETS):
                    b = jnp.where(bucket == bk, rb_ref[bk, h] - rb_ref[N_BUCKETS - 1, h], b)
                kv, g = divmod(h, DSA_GQA)
                bias_ref[tile, kv, g * QBLOCK:(g + 1) * QBLOCK, :] = b

    qi_all = (qi_ref[...].astype(f32) * IDX_DIM ** -0.5).astype(bf16)
    qi = jnp.concatenate([qi_all[:, h * IDX_DIM:(h + 1) * IDX_DIM] for h in range(IDX_HEADS)], axis=0)
    wi = kiwi_ref[:, IDX_DIM:IDX_DIM + IDX_HEADS] * IDX_HEADS ** -0.5

    def score_chunk(c, _):
        off = pl.multiple_of(c * kc, kc)
        d = jnp.dot(qi, kit_ref[:, pl.ds(off, kc)], preferred_element_type=f32)
        s = jnp.zeros((QBLOCK, kc), f32)
        for h in range(IDX_HEADS):
            s = s + wi[:, h:h + 1] * jnp.maximum(d[h * QBLOCK:(h + 1) * QBLOCK], 0.0)
        kpos = off + lax.broadcasted_iota(jnp.int32, (QBLOCK, kc), 1)
        s = jnp.where(s == 0.0, 0.0, s)
        s = jnp.where(kpos <= qpos, s, -jnp.inf)
        keys_ref[:, pl.ds(off, kc)] = _sortable_key(s)
        return 0

    lax.fori_loop(0, n_all, score_chunk, 0)

    def count_where(pred_fn):
        def body(c, acc):
            off = pl.multiple_of(c * kc, kc)
            hit = pred_fn(keys_ref[:, pl.ds(off, kc)], off)
            part = jnp.where(hit, 1.0, 0.0)
            for j in range(kc // 128):
                acc = acc + part[:, j * 128:(j + 1) * 128]
            return acc
        acc = lax.fori_loop(0, n_all, body, jnp.zeros((QBLOCK, 128), f32))
        return jnp.sum(acc, axis=-1, keepdims=True)

    def thr_bit(it, t):
        cand = t + lax.shift_left(jnp.int32(1), 31 - it)
        cnt = count_where(lambda k, off: k >= cand)
        return jnp.where(cnt >= n_top, cand, t)

    thr = lax.fori_loop(0, 32, thr_bit, jnp.full((QBLOCK, 1), INT32_MIN, jnp.int32))

    def is_valid(off, width):
        return (off + lax.broadcasted_iota(jnp.int32, (QBLOCK, width), 1)) <= qpos

    n_gt = count_where(lambda k, off: (k > thr) & is_valid(off, kc))
    n_eq = count_where(lambda k, off: (k == thr) & is_valid(off, kc))
    need = n_top - n_gt
    has_extra_ties = jnp.max(jnp.where(n_eq > need, 1.0, 0.0)) > 0.0

    def tie_search():
        def idx_bit(it, j):
            cand = j + lax.shift_left(jnp.int32(1), 13 - it)
            cnt = count_where(lambda k, off: (k == thr) & is_valid(off, kc)
                              & ((off + lax.broadcasted_iota(jnp.int32, (QBLOCK, kc), 1)) < cand))
            return jnp.where(cnt <= need, cand, j)
        return lax.fori_loop(0, 14, idx_bit, jnp.zeros((QBLOCK, 1), jnp.int32))

    tie_end = lax.cond(has_extra_ties, tie_search, lambda: jnp.full((QBLOCK, 1), 2 ** 14, jnp.int32))

    def selected(keys, off, width):
        kpos = off + lax.broadcasted_iota(jnp.int32, (QBLOCK, width), 1)
        return (kpos <= qpos) & ((keys > thr) | ((keys == thr) & (kpos < tie_end))), kpos

    rows = DSA_GQA * QBLOCK
    tile = jnp.minimum(i, 1)
    q_all = (q_ref[...].astype(f32) * HEAD_DIM ** -0.5).astype(bf16)
    qks = [jnp.concatenate([q_all[:, (kv * DSA_GQA + g) * HEAD_DIM:(kv * DSA_GQA + g + 1) * HEAD_DIM]
                            for g in range(DSA_GQA)], axis=0) for kv in range(DSA_KV_HEADS)]

    def attend(carry, qk, kt, vv, mask_bias):
        m, l, acc = carry
        logits = jnp.dot(qk, kt, preferred_element_type=f32) + mask_bias
        m_new = jnp.maximum(m, jnp.max(logits, axis=-1, keepdims=True))
        alpha = jnp.exp(m - m_new)
        p = jnp.exp(logits - m_new)
        l = alpha * l + jnp.sum(p, axis=-1, keepdims=True)
        acc = alpha * acc + jnp.dot(p.astype(bf16), vv, preferred_element_type=f32)
        return m_new, l, acc

    def far_chunk(c, carry):
        off = pl.multiple_of(c * kc, kc)
        sel, kpos = selected(keys_ref[:, pl.ds(off, kc)], off, kc)
        mb = jnp.where(sel & (kpos < near_start), 0.0, NEG_BIG)
        mb3 = jnp.concatenate([mb] * DSA_GQA, axis=0)
        return tuple(attend(carry[kv], qks[kv], kt_ref[kv * HEAD_DIM:(kv + 1) * HEAD_DIM, pl.ds(off, kc)],
                            v_ref[kv, pl.ds(off, kc), :], mb3) for kv in range(DSA_KV_HEADS))

    init = tuple((jnp.full((rows, 1), NEG_BIG, f32), jnp.zeros((rows, 1), f32), jnp.zeros((rows, HEAD_DIM), f32))
                 for _ in range(DSA_KV_HEADS))
    carry = lax.fori_loop(0, n_far, far_chunk, init)
    off = pl.multiple_of(near_start, QBLOCK)
    sel, _ = selected(keys_ref[:, pl.ds(off, DSA_NEAR)], off, DSA_NEAR)
    sel3 = jnp.concatenate([sel] * DSA_GQA, axis=0)
    outs = []
    for kv in range(DSA_KV_HEADS):
        mb3 = jnp.where(sel3, bias_ref[tile, kv], NEG_BIG)
        m, l, acc = attend(carry[kv], qks[kv], kt_ref[kv * HEAD_DIM:(kv + 1) * HEAD_DIM, pl.ds(off, DSA_NEAR)],
                           v_ref[kv, pl.ds(off, DSA_NEAR), :], mb3)
        o = acc / l
        outs += [o[g * QBLOCK:(g + 1) * QBLOCK] for g in range(DSA_GQA)]
    o_ref[...] = jnp.concatenate(outs, axis=-1)


def dsa_prompt_pallas(q, qi, kiwi, k, v, rel_bias):
    bsz, L = q.shape[:2]
    nq = L // QBLOCK
    n_top = min(TOPK_MAX, L // 4)
    f32, bf16 = jnp.float32, jnp.bfloat16
    k_t = k.astype(bf16).swapaxes(1, 2)
    v_h = v.astype(bf16).reshape(bsz, L, DSA_KV_HEADS, HEAD_DIM).swapaxes(1, 2)
    ki_t = kiwi[..., :IDX_DIM].astype(bf16).swapaxes(1, 2)
    lk = max(L, DSA_KEY_CHUNK)
    if lk != L:
        k_t = jnp.pad(k_t, ((0, 0), (0, 0), (0, lk - L)))
        v_h = jnp.pad(v_h, ((0, 0), (0, 0), (0, lk - L), (0, 0)))
        ki_t = jnp.pad(ki_t, ((0, 0), (0, 0), (0, lk - L)))
    tok = lambda n: pl.BlockSpec((None, QBLOCK, n), lambda b, i: (b, i, 0))
    return pl.pallas_call(
        functools.partial(_dsa_prompt_body, n_top),
        grid=(bsz, nq),
        in_specs=[
            pl.BlockSpec(memory_space=pltpu.SMEM),
            tok(SEQ_WIDTH), tok(IDX_HEADS * IDX_DIM), tok(LANES),
            pl.BlockSpec((None, DSA_KV_HEADS * HEAD_DIM, lk), lambda b, i: (b, 0, 0)),
            pl.BlockSpec((None, DSA_KV_HEADS, lk, HEAD_DIM), lambda b, i: (b, 0, 0, 0)),
            pl.BlockSpec((None, IDX_DIM, lk), lambda b, i: (b, 0, 0)),
        ],
        out_specs=tok(SEQ_WIDTH),
        out_shape=jax.ShapeDtypeStruct((bsz, L, SEQ_WIDTH), f32),
        scratch_shapes=[pltpu.VMEM((QBLOCK, lk), jnp.int32),
                        pltpu.VMEM((2, DSA_KV_HEADS, DSA_GQA * QBLOCK, DSA_NEAR), f32)],
        compiler_params=pltpu.CompilerParams(
            dimension_semantics=("parallel", "arbitrary"), vmem_limit_bytes=VMEM_LIMIT_BYTES),
        name="dsa_prompt",
    )(rel_bias.astype(f32), q, qi, kiwi, k_t, v_h, ki_t)


CONV_HALO = 8


def _gdn_prep_body(x_ref, ab_ref, ctx_ref, w_ref, alog_ref, dtb_ref, q_ref, k_ref, v_ref, gb_ref, cs_ref, xp_ref):
    tt = x_ref.shape[0]
    halo = CONV_W - 1

    @pl.when(pl.program_id(1) == 0)
    def _():
        xp_ref[CONV_HALO - halo:CONV_HALO, :] = ctx_ref[...]

    x = x_ref[...]
    xp_ref[CONV_HALO:CONV_HALO + tt, :] = x
    w = w_ref[...]
    y = xp_ref[CONV_HALO - halo:CONV_HALO - halo + tt, :] * w[0:1]
    for j in range(1, CONV_W):
        y = y + xp_ref[CONV_HALO - halo + j:CONV_HALO - halo + j + tt, :] * w[j:j + 1]
    last = x[tt - halo:, :]
    xp_ref[CONV_HALO - halo:CONV_HALO, :] = last
    cs_ref[...] = last
    y = y * _sigmoid(y)

    def l2n(a, scale):
        cols = []
        for h in range(GDN_HEADS):
            s = a[:, h * HEAD_DIM:(h + 1) * HEAD_DIM]
            cols.append(s * (lax.rsqrt(jnp.sum(s * s, axis=-1, keepdims=True) + EPS) * scale))
        return jnp.concatenate(cols, axis=-1)

    q_ref[...] = l2n(y[:, :SEQ_WIDTH], HEAD_DIM ** -0.5)
    k_ref[...] = l2n(y[:, SEQ_WIDTH:2 * SEQ_WIDTH], 1.0)
    v_ref[...] = y[:, 2 * SEQ_WIDTH:]
    ab = ab_ref[...]
    xa = ab + dtb_ref[...]
    softplus = jnp.maximum(xa, 0.0) + jnp.log(1.0 + jnp.exp(-jnp.abs(xa)))
    g = -jnp.exp(alog_ref[...]) * softplus
    lane = lax.broadcasted_iota(jnp.int32, ab.shape, 1)
    gb_ref[...] = jnp.where(lane < GDN_HEADS, g, _sigmoid(ab))


def gdn_prep_pallas(qkv, ab, ctx, conv_w, a_log, dt_bias, *, tt=256):
    bsz, L, ch = qkv.shape
    tt = min(tt, L)
    f32 = jnp.float32
    pad = lambda r: jnp.pad(r.astype(f32).reshape(1, -1), ((0, 0), (0, LANES - r.shape[-1])))
    tok = lambda n: pl.BlockSpec((None, tt, n), lambda b, t: (b, t, 0))
    const2 = lambda b, t: (0, 0)
    return pl.pallas_call(
        _gdn_prep_body,
        grid=(bsz, L // tt),
        in_specs=[tok(ch), tok(LANES), pl.BlockSpec((None, CONV_W - 1, ch), lambda b, t: (b, 0, 0)),
                  pl.BlockSpec((CONV_W, ch), const2), pl.BlockSpec((1, LANES), const2), pl.BlockSpec((1, LANES), const2)],
        out_specs=[tok(SEQ_WIDTH), tok(SEQ_WIDTH), tok(SEQ_WIDTH), tok(LANES),
                   pl.BlockSpec((None, CONV_W - 1, ch), lambda b, t: (b, 0, 0))],
        out_shape=[jax.ShapeDtypeStruct((bsz, L, SEQ_WIDTH), f32)] * 3
        + [jax.ShapeDtypeStruct((bsz, L, LANES), f32), jax.ShapeDtypeStruct((bsz, CONV_W - 1, ch), f32)],
        scratch_shapes=[pltpu.VMEM((CONV_HALO + tt, ch), f32)],
        compiler_params=pltpu.CompilerParams(
            dimension_semantics=("parallel", "arbitrary"), vmem_limit_bytes=VMEM_LIMIT_BYTES),
        name="gdn_prep",
    )(qkv, ab, ctx.astype(f32), conv_w.astype(f32), pad(a_log), pad(dt_bias))


def _split3(a):
    bf16, f32 = jnp.bfloat16, jnp.float32
    h = a.astype(bf16)
    r = a - h.astype(f32)
    m = r.astype(bf16)
    return h, m, (r - m.astype(f32)).astype(bf16)


def _mm_hi(a, b):
    f32 = jnp.float32
    ah, am, _ = _split3(a)
    bh, bm, _ = _split3(b)
    d = lambda x, y: jnp.dot(x, y, preferred_element_type=f32)
    return d(ah, bh) + (d(ah, bm) + d(am, bh))


def _mm_sel(sel, b):
    f32 = jnp.float32
    s = sel.astype(jnp.bfloat16)
    bh, bm, bl = _split3(b)
    d = lambda y: jnp.dot(s, y, preferred_element_type=f32)
    return d(bh) + (d(bm) + d(bl))


def _dot_nt(a, b):
    return lax.dot_general(a, b, (((1,), (1,)), ((), ())), preferred_element_type=jnp.float32)


def _dot_tn(a, b):
    return lax.dot_general(a, b, (((0,), (0,)), ((), ())), preferred_element_type=jnp.float32)


def _gdn_local_body(q_ref, k_ref, v_ref, gb_ref, uv_ref, wk_ref, qh_ref, kt_ref, qk_ref):
    f32, bf16 = jnp.float32, jnp.bfloat16
    c = GDN_CHUNK
    r_i = lax.broadcasted_iota(jnp.int32, (c, c), 0)
    c_i = lax.broadcasted_iota(jnp.int32, (c, c), 1)
    tril = r_i >= c_i
    stril = r_i > c_i
    triu = r_i <= c_i
    eye = jnp.where(r_i == c_i, 1.0, 0.0)
    ones = jnp.ones((c, c), jnp.bool_)
    lane = lax.broadcasted_iota(jnp.int32, (c, LANES), 1)
    gb = gb_ref[...]
    heads = range(GDN_HEADS)
    sl = [slice(h * HEAD_DIM, (h + 1) * HEAD_DIM) for h in heads]
    pick = lambda idx: jnp.sum(jnp.where(lane == idx, gb, 0.0), axis=-1, keepdims=True)
    g_col = [pick(h) for h in heads]
    beta = [pick(GDN_HEADS + h) for h in heads]
    q = [q_ref[:, sl[h]] for h in heads]
    k = [k_ref[:, sl[h]] for h in heads]
    v = [v_ref[:, sl[h]] for h in heads]
    g_mat = [jnp.broadcast_to(g_col[h], (c, c)) for h in heads]
    cum_col = [_mm_sel(tril, g_mat[h]) for h in heads]
    cum_row = [_mm_sel(ones, jnp.where(triu, g_mat[h], 0.0)) for h in heads]
    gam = [jnp.where(tril, jnp.exp(jnp.where(tril, cum_col[h] - cum_row[h], 0.0)), 0.0) for h in heads]
    kb = [k[h].astype(bf16) for h in heads]
    a_mat = [jnp.where(stril, beta[h] * _dot_nt(kb[h], kb[h]) * gam[h], 0.0) for h in heads]
    t_inv = [eye - a_mat[h] for h in heads]
    pw = a_mat
    for _ in range(5):
        pw = [_mm_hi(pw[h], pw[h]) for h in heads]
        t_inv = [t_inv[h] + _mm_hi(t_inv[h], pw[h]) for h in heads]
    gc = [cum_col[h][:, :1] for h in heads]
    egc = [jnp.exp(gc[h]) for h in heads]
    tw = [_mm_hi(t_inv[h], jnp.concatenate([beta[h] * v[h], (beta[h] * egc[h]) * k[h]], axis=-1)) for h in heads]
    qk = [jnp.where(tril, _dot_nt(q[h].astype(bf16), kb[h]) * gam[h], 0.0) for h in heads]
    cat = lambda xs: jnp.concatenate(xs, axis=-1)
    uv_ref[...] = cat([tw[h][:, :HEAD_DIM] for h in heads])
    wk_ref[...] = cat([tw[h][:, HEAD_DIM:] for h in heads]).astype(bf16)
    qh_ref[...] = cat([egc[h] * q[h] for h in heads]).astype(bf16)
    eye_b = eye.astype(bf16)
    kt = [(jnp.exp(gc[h][c - 1:c, :] - gc[h]) * k[h]).astype(bf16) for h in heads]
    kt_ref[...] = cat([_dot_tn(kt[h], eye_b) for h in heads]).astype(bf16)
    qk_ref[...] = cat(qk).astype(bf16)


def gdn_local_pallas(qn, kn, vv, gb):
    bsz, L, w = qn.shape
    c = GDN_CHUNK
    f32, bf16 = jnp.float32, jnp.bfloat16
    tok = lambda n: pl.BlockSpec((None, c, n), lambda b, t: (b, t, 0))
    return pl.pallas_call(
        _gdn_local_body,
        grid=(bsz, L // c),
        in_specs=[tok(w), tok(w), tok(w), tok(LANES)],
        out_specs=[tok(w)] * 5,
        out_shape=[jax.ShapeDtypeStruct((bsz, L, w), f32)] + [jax.ShapeDtypeStruct((bsz, L, w), bf16)] * 4,
        compiler_params=pltpu.CompilerParams(
            dimension_semantics=("parallel", "parallel"), vmem_limit_bytes=VMEM_LIMIT_BYTES),
        name="gdn_local",
    )(qn, kn, vv, gb)


def _gdn_scan_body(uv_ref, wk_ref, qh_ref, ktt_ref, qk_ref, gb_ref, gate_ref, s0_ref, on_ref, o_ref, sf_ref, s_ref):
    f32, bf16 = jnp.float32, jnp.bfloat16
    c = GDN_CHUNK
    tt = uv_ref.shape[0]
    heads = range(GDN_HEADS)
    sl = [slice(h * HEAD_DIM, (h + 1) * HEAD_DIM) for h in heads]

    @pl.when(pl.program_id(1) == 0)
    def _():
        s_ref[...] = s0_ref[...]

    lane = lax.broadcasted_iota(jnp.int32, (1, LANES), 1)
    dot = lambda a, b: jnp.dot(a, b, preferred_element_type=f32)

    def chunk(ci, _):
        rows = pl.ds(pl.multiple_of(ci * c, c), c)
        eg_last = jnp.exp(jnp.sum(gb_ref[rows, :], axis=0, keepdims=True))
        eg = [jnp.sum(jnp.where(lane == h, eg_last, 0.0), axis=-1, keepdims=True) for h in heads]
        st = [s_ref[h] for h in heads]
        sb = [st[h].astype(bf16) for h in heads]
        u = [uv_ref[rows, sl[h]] - dot(wk_ref[rows, sl[h]], sb[h]) for h in heads]
        ub = [u[h].astype(bf16) for h in heads]
        new = [eg[h] * st[h] + dot(ktt_ref[rows, sl[h]], ub[h]) for h in heads]
        for h in heads:
            s_ref[h] = new[h]
        o = [dot(qh_ref[rows, sl[h]], sb[h]) + dot(qk_ref[rows, sl[h]], ub[h]) for h in heads]
        o = [o[h] * lax.rsqrt(jnp.mean(o[h] * o[h], axis=-1, keepdims=True) + EPS) for h in heads]
        gt = gate_ref[rows, :]
        o_ref[rows, :] = (jnp.concatenate(o, axis=-1) * on_ref[...]) * (gt * _sigmoid(gt))
        return 0

    lax.fori_loop(0, tt // c, chunk, 0)
    sf_ref[...] = s_ref[...]


def gdn_scan_pallas(uv, wk, qh, kt, qk, gb, gate, s0, o_norm, *, tt=256):
    bsz, L, w = uv.shape
    tt = min(tt, L)
    f32 = jnp.float32
    tok = lambda n: pl.BlockSpec((None, tt, n), lambda b, t: (b, t, 0))
    st = pl.BlockSpec((None, GDN_HEADS, HEAD_DIM, HEAD_DIM), lambda b, t: (b, 0, 0, 0))
    on = jnp.tile(o_norm.astype(f32), GDN_HEADS).reshape(1, w)
    return pl.pallas_call(
        _gdn_scan_body,
        grid=(bsz, L // tt),
        in_specs=[tok(w)] * 5 + [tok(LANES), tok(w), st, pl.BlockSpec((1, w), lambda b, t: (0, 0))],
        out_specs=[tok(w), st],
        out_shape=[jax.ShapeDtypeStruct((bsz, L, w), f32), jax.ShapeDtypeStruct(s0.shape, f32)],
        scratch_shapes=[pltpu.VMEM((GDN_HEADS, HEAD_DIM, HEAD_DIM), f32)],
        compiler_params=pltpu.CompilerParams(
            dimension_semantics=("parallel", "arbitrary"), vmem_limit_bytes=VMEM_LIMIT_BYTES),
        name="gdn_scan",
    )(uv, wk, qh, kt, qk, gb, gate, s0.astype(f32).swapaxes(2, 3), on)


def gdn_mix_pallas(qkv, ab, gate, conv_ctx, s0, conv_w, a_log, dt_bias, o_norm):
    qn, kn, vv, gb, conv_state = gdn_prep_pallas(qkv, ab, conv_ctx, conv_w, a_log, dt_bias)
    uv, wk, qh, kt, qk = gdn_local_pallas(qn, kn, vv, gb)
    o, s_fin_t = gdn_scan_pallas(uv, wk, qh, kt, qk, gb, gate, s0, o_norm)
    return o, conv_state.astype(qkv.dtype), s_fin_t.swapaxes(2, 3).astype(s0.dtype)


def _pad_cols(w, n):
    return jnp.pad(w, ((0, 0), (0, n - w.shape[1])))


def _in_proj_layout(kind, w_in):
    f32, bf16 = jnp.float32, jnp.bfloat16
    if kind == 0:
        return w_in, ((0, MEM_WIDTH), (MEM_WIDTH, SEQ_WIDTH)), (bf16, f32)
    if kind == 1:
        w = _pad_cols(w_in, MEM_WIDTH + sum(DSA_SPLITS[:4]) + LANES)
        widths = (MEM_WIDTH,) + DSA_SPLITS[:4] + (LANES,)
        starts = np.cumsum((0,) + widths[:-1]).tolist()
        return w, tuple(zip(starts, widths)), (bf16, bf16, f32, f32, bf16, f32)
    c0, c1 = MEM_WIDTH + CONV_CH, MEM_WIDTH + CONV_CH + 2 * GDN_HEADS
    w = jnp.concatenate([w_in[:, :c0], w_in[:, c1:], _pad_cols(w_in[:, c0:c1], LANES)], axis=1)
    widths = (MEM_WIDTH, CONV_CH, SEQ_WIDTH, LANES)
    starts = np.cumsum((0,) + widths[:-1]).tolist()
    return w, tuple(zip(starts, widths)), (bf16, f32, f32, f32)


def kernel(x_prompt, x_sample, cache_mem_k, cache_mem_v, state_ssm_re, state_ssm_im, cache_k, cache_v, cache_kidx, state_conv, state_delta, page_table, mem_prompt, norm_g, final_norm, w_in_a, w_in_b, w_in_c, w_out, w_mem_kv, ffn1_gate, ffn1_up, ffn1_down, ffn2_gate, ffn2_up, ffn2_down, s5_lam_re, s5_lam_im, s5_log_step, s5_b_re, s5_b_im, s5_c_re, s5_c_im, s5_d, s5_w_glu, s5_b_glu, rel_bias, gdn_conv_w, gdn_a_log, gdn_dt_bias, gdn_o_norm):
    depth = norm_g.shape[0]
    bp, lp, d = x_prompt.shape
    bs, ls, _ = x_sample.shape
    past = page_table.shape[1] * PAGE_SIZE
    f32, bf16 = jnp.float32, jnp.bfloat16
    hp, hs = x_prompt.reshape(bp * lp, d), x_sample.reshape(bs * ls, d)
    mem_rows = mem_prompt.reshape(bp * N_MEM, d)
    mem_k_p, mem_v_p = [], []
    ssm_re_p, ssm_im_p, ssm_re_s, ssm_im_s = [], [], [], []
    k_p, v_p, ki_p, k_s, v_s, ki_s = [], [], [], [], [], []
    conv_p, delta_p, conv_s, delta_s = [], [], [], []
    w_in_by_kind = (w_in_a, w_in_b, w_in_c)
    for i in range(depth):
        kind, j = i % N_MIXERS, i // N_MIXERS
        last = i == depth - 1
        ffn1 = (ffn1_gate[i].astype(bf16), ffn1_up[i].astype(bf16), ffn1_down[i].astype(bf16))
        ffn2 = (ffn2_gate[i].astype(bf16), ffn2_up[i].astype(bf16), ffn2_down[i].astype(bf16))
        w_in, segments, dtypes = _in_proj_layout(kind, w_in_by_kind[kind][j])
        w_in = w_in.astype(bf16)
        hp = ffn_residual(hp, norm_g[i, 0], *ffn1)
        hs = ffn_residual(hs, norm_g[i, 0], *ffn1)
        zp = [a.reshape(bp, lp, -1) for a in proj_in(hp, norm_g[i, 1], w_in, segments, dtypes)]
        zs = [a.reshape(bs, ls, -1) for a in proj_in(hs, norm_g[i, 1], w_in, segments, dtypes)]
        mk, mv = proj_in(mem_rows, norm_g[i, 1], w_mem_kv[i].astype(bf16), ((0, MEM_WIDTH), (MEM_WIDTH, MEM_WIDTH)),
                         (f32, f32), normalize=False)
        mem_shape = (bp, N_MEM, MEM_HEADS, HEAD_DIM)
        mk, mv = mk.reshape(mem_shape), mv.reshape(mem_shape)
        mem_k_p.append(mk)
        mem_v_p.append(mv)
        if kind == 0:
            disc = s5_discretize(s5_lam_re[j], s5_lam_im[j], s5_log_step[j], s5_b_re[j], s5_b_im[j],
                                 s5_c_re[j], s5_c_im[j])
            gate = (s5_d[j], s5_w_glu[j], s5_b_glu[j])
            h0 = jnp.zeros((bp, S5_GROUPS, S5_STATE), state_ssm_re.dtype)
            op, hr, hi = s5_mix_pallas(zp[1], h0, h0, disc, *gate)
            ssm_re_p.append(hr)
            ssm_im_p.append(hi)
            osm, hr, hi = s5_mix_pallas(zs[1], state_ssm_re[j], state_ssm_im[j], disc, *gate)
            ssm_re_s.append(hr)
            ssm_im_s.append(hi)
        elif kind == 1:
            _, q, k, v, qi, kiwi = zp
            op = dsa_prompt_pallas(q, qi, kiwi, k, v, rel_bias)
            kv_shape = (DSA_KV_HEADS, HEAD_DIM)
            k_p.append(k.reshape((bp, lp) + kv_shape))
            v_p.append(v.reshape((bp, lp) + kv_shape))
            ki_p.append(kiwi[..., :IDX_DIM])
            _, q, k, v, qi, kiwi = zs
            q = q.astype(f32).reshape(bs, ls, DSA_KV_HEADS, DSA_GQA, HEAD_DIM)
            k, v = k.reshape((bs, ls) + kv_shape), v.reshape((bs, ls) + kv_shape)
            qi = qi.astype(f32).reshape(bs, ls, IDX_HEADS, IDX_DIM)
            ki, wi = kiwi[..., :IDX_DIM], kiwi[..., IDX_DIM:IDX_DIM + IDX_HEADS] * IDX_HEADS ** -0.5
            ki_all = jnp.concatenate([cache_kidx[j][page_table].reshape(bs, past, IDX_DIM), ki], axis=1)
            qpos = past + jnp.arange(ls, dtype=jnp.int32)
            gather = paged_gather(cache_k[j], cache_v[j], page_table, k, v)
            osm = dsa_attend(q, qi, wi, qpos, ki_all, gather, rel_bias)
            k_s.append(k)
            v_s.append(v)
            ki_s.append(ki)
        else:
            gdn = (gdn_conv_w[j], gdn_a_log[j], gdn_dt_bias[j], gdn_o_norm[j])
            ctx0 = jnp.zeros((bp, CONV_W - 1, CONV_CH), f32)
            s00 = jnp.zeros((bp, GDN_HEADS, HEAD_DIM, HEAD_DIM), state_delta.dtype)
            _, qkv, gate, ab = zp
            op, cst, sst = gdn_mix_pallas(qkv, ab, gate, ctx0, s00, *gdn)
            conv_p.append(cst)
            delta_p.append(sst)
            _, qkv, gate, ab = zs
            z_s = jnp.concatenate([qkv, ab[..., :2 * GDN_HEADS], gate], axis=-1)
            osm, cst, sst = gdn_mix(z_s, state_conv[j], state_delta[j], *gdn)
            conv_s.append(cst)
            delta_s.append(sst)
        w_o = w_out[i].astype(bf16)
        hp = mix_out(hp.reshape(bp, lp, d), zp[0], op, mk, mv, w_o).reshape(bp * lp, d)
        hs = mix_out(hs.reshape(bs, ls, d), zs[0], osm, cache_mem_k[i], cache_mem_v[i], w_o).reshape(bs * ls, d)
        fg = final_norm if last else None
        hp = ffn_residual(hp, norm_g[i, 2], *ffn2, fg)
        hs = ffn_residual(hs, norm_g[i, 2], *ffn2, fg)
    st = jnp.stack
    return (hp.reshape(bp, lp, d), hs.reshape(bs, ls, d), st(mem_k_p), st(mem_v_p),
            st(ssm_re_p), st(ssm_im_p), st(ssm_re_s), st(ssm_im_s),
            st(k_p), st(v_p), st(ki_p), st(k_s), st(v_s), st(ki_s),
            st(conv_p), st(delta_p), st(conv_s), st(delta_s))
```

```python
import math
import functools
import jax
import jax.numpy as jnp
from jax import lax
import numpy as np
from jax.experimental import pallas as pl
from jax.experimental.pallas import tpu as pltpu

D_MODEL = 1024
N_MIXERS = 3
HEAD_DIM = 64
MIX_WIDTH = D_MODEL
N_MEM = 256
MEM_HEADS = 4
MEM_WIDTH = MEM_HEADS * HEAD_DIM
SEQ_WIDTH = MIX_WIDTH - MEM_WIDTH
S5_GROUP = 16
S5_GROUPS = SEQ_WIDTH // S5_GROUP
S5_STATE = 64
DSA_HEADS = SEQ_WIDTH // HEAD_DIM
DSA_KV_HEADS = 4
DSA_GQA = DSA_HEADS // DSA_KV_HEADS
IDX_HEADS = 8
IDX_DIM = 64
TOPK_MAX = 256
QBLOCK = 128
N_BUCKETS = 32
MAX_DISTANCE = 128
GDN_HEADS = SEQ_WIDTH // HEAD_DIM
CONV_W = 4
CONV_CH = 3 * SEQ_WIDTH
GDN_CHUNK = 64
D_FF = 2816
EPS = 1e-6
PAGE_SIZE = 128
DSA_SPLITS = (DSA_HEADS * HEAD_DIM, DSA_KV_HEADS * HEAD_DIM, DSA_KV_HEADS * HEAD_DIM, IDX_HEADS * IDX_DIM, IDX_DIM, IDX_HEADS)
GDN_SPLITS = (CONV_CH, GDN_HEADS, GDN_HEADS, SEQ_WIDTH)

LANES = 128
VMEM_LIMIT_BYTES = 56 * 1024 * 1024


def split_cols(z, sizes):
    return jnp.split(z, np.cumsum(sizes)[:-1].tolist(), axis=-1)


def rmsnorm(x, g):
    x32 = x.astype(jnp.float32)
    r = lax.rsqrt(jnp.mean(x32 * x32, axis=-1, keepdims=True) + EPS)
    return (x32 * r).astype(x.dtype) * g


def l2norm(x):
    return x * lax.rsqrt(jnp.sum(x * x, axis=-1, keepdims=True) + EPS)


def _sigmoid(x):
    return 1.0 / (1.0 + jnp.exp(-x))


def t5_bucket(dist):
    max_exact = N_BUCKETS // 2
    n = jnp.maximum(dist, 0)
    nf = jnp.maximum(n, 1).astype(jnp.float32)
    large = max_exact + (jnp.log(nf / max_exact) / math.log(MAX_DISTANCE / max_exact)
                         * (N_BUCKETS - max_exact)).astype(jnp.int32)
    return jnp.where(n < max_exact, n, jnp.minimum(large, N_BUCKETS - 1))


def causal_conv(x, ctx, w):
    L = x.shape[1]
    xp = jnp.concatenate([ctx, x], axis=1)
    y = xp[:, 0:L] * w[0]
    for j in range(1, CONV_W):
        y = y + xp[:, j:j + L] * w[j]
    return jax.nn.silu(y), xp[:, L:]


def gated_delta(q, k, v, g, beta, s0):
    bsz, L, nh, dk = q.shape
    c = GDN_CHUNK if L % GDN_CHUNK == 0 else L
    n = L // c
    def chunks(a):
        a = a.reshape((bsz, n, c) + a.shape[2:])
        return jnp.moveaxis(jnp.moveaxis(a, 3, 2), 1, 0)
    q = chunks(q) * dk ** -0.5
    k, v, g, beta = chunks(k), chunks(v), chunks(g), chunks(beta)
    gc = jnp.cumsum(g, axis=-1)
    tril = jnp.tril(jnp.ones((c, c), dtype=bool))
    stril = jnp.tril(jnp.ones((c, c), dtype=bool), -1)
    gam = jnp.exp(jnp.where(tril, gc[..., :, None] - gc[..., None, :], -jnp.inf))
    a_mat = jnp.where(stril, beta[..., :, None] * jnp.einsum('nbhtd,nbhjd->nbhtj', k, k) * gam, 0.0)
    lhs = a_mat + jnp.eye(c, dtype=a_mat.dtype)
    solve = functools.partial(lax.linalg.triangular_solve, left_side=True, lower=True, unit_diagonal=True)
    uv = solve(lhs, beta[..., None] * v)
    wk = solve(lhs, (beta * jnp.exp(gc))[..., None] * k)
    qk = jnp.where(tril, jnp.einsum('nbhtd,nbhjd->nbhtj', q, k) * gam, 0.0)
    q_head = jnp.exp(gc)[..., None] * q
    g_last = gc[..., -1]
    k_tail = jnp.exp(g_last[..., None] - gc)[..., None] * k
    def step(s, xs):
        uv_c, wk_c, qk_c, qh_c, kt_c, gl_c = xs
        u = uv_c - jnp.einsum('bhck,bhvk->bhcv', wk_c, s)
        o = jnp.einsum('bhck,bhvk->bhcv', qh_c, s) + jnp.einsum('bhtj,bhjv->bhtv', qk_c, u)
        s = jnp.exp(gl_c)[..., None, None] * s + jnp.einsum('bhcv,bhck->bhvk', u, kt_c)
        return s, o
    s_fin, o = lax.scan(step, s0, (uv, wk, qk, q_head, k_tail, g_last))
    o = jnp.moveaxis(jnp.moveaxis(o, 0, 1), 2, 3).reshape(bsz, L, nh, v.shape[-1])
    return o, s_fin


def gdn_mix(z, conv_ctx, s0, conv_w, a_log, dt_bias, o_norm):
    f32 = jnp.float32
    bsz, L = z.shape[:2]
    qkv, a, b, gate = split_cols(z, GDN_SPLITS)
    qkv, conv_state = causal_conv(qkv, conv_ctx, conv_w)
    q, k, v = jnp.split(qkv.astype(f32), 3, axis=-1)
    hs = (bsz, L, GDN_HEADS, HEAD_DIM)
    q, k, v = l2norm(q.reshape(hs)), l2norm(k.reshape(hs)), v.reshape(hs)
    g = -jnp.exp(a_log.astype(f32)) * jax.nn.softplus((a + dt_bias).astype(f32))
    beta = jax.nn.sigmoid(b.astype(f32))
    o, s_fin = gated_delta(q, k, v, g, beta, s0.astype(f32))
    o = rmsnorm(o.astype(z.dtype), o_norm) * jax.nn.silu(gate.reshape(hs))
    return o.reshape(bsz, L, SEQ_WIDTH), conv_state, s_fin.astype(s0.dtype)


FFN_CHUNK = 256


def _ffn_body(final_norm, x_ref, g_ref, wg_ref, wu_ref, wd_ref, gf_ref, o_ref):
    f32, bf16 = jnp.float32, jnp.bfloat16
    x = x_ref[...]
    n = ((x * lax.rsqrt(jnp.mean(x * x, axis=-1, keepdims=True) + EPS)) * g_ref[...]).astype(bf16)
    acc = jnp.zeros(x.shape, f32)
    for c in range(wg_ref.shape[1] // FFN_CHUNK):
        sl = slice(c * FFN_CHUNK, (c + 1) * FFN_CHUNK)
        a = jnp.dot(n, wg_ref[:, sl], preferred_element_type=f32)
        b = jnp.dot(n, wu_ref[:, sl], preferred_element_type=f32)
        h = (a * _sigmoid(a)) * b
        acc = acc + jnp.dot(h.astype(bf16), wd_ref[sl, :], preferred_element_type=f32)
    y = x + 0.5 * acc
    if final_norm:
        y = (y * lax.rsqrt(jnp.mean(y * y, axis=-1, keepdims=True) + EPS)) * gf_ref[...]
    o_ref[...] = y


def ffn_residual(x, g, wg, wu, wd, final_g=None, *, tm=512):
    t, d = x.shape
    f = wg.shape[1]
    tm = min(tm, t)
    gf = jnp.ones((d,), jnp.float32) if final_g is None else final_g
    resident = lambda shape: pl.BlockSpec(shape, lambda i: (0, 0), pipeline_mode=pl.Buffered(1))
    return pl.pallas_call(
        functools.partial(_ffn_body, final_g is not None),
        grid=(t // tm,),
        in_specs=[pl.BlockSpec((tm, d), lambda i: (i, 0)), resident((1, d)),
                  resident((d, f)), resident((d, f)), resident((f, d)), resident((1, d))],
        out_specs=pl.BlockSpec((tm, d), lambda i: (i, 0)),
        out_shape=jax.ShapeDtypeStruct((t, d), jnp.float32),
        compiler_params=pltpu.CompilerParams(dimension_semantics=("parallel",), vmem_limit_bytes=VMEM_LIMIT_BYTES),
        name="ffn_residual",
    )(x, g.reshape(1, d).astype(jnp.float32), wg, wu, wd, gf.reshape(1, d).astype(jnp.float32))


def _proj_in_body(segments, normalize, x_ref, g_ref, w_ref, *o_refs):
    x = x_ref[...]
    if normalize:
        r = lax.rsqrt(jnp.mean(x * x, axis=-1, keepdims=True) + EPS)
        x = (x * r) * g_ref[...]
    n = x.astype(jnp.bfloat16)
    for (start, width), o_ref in zip(segments, o_refs):
        o_ref[...] = jnp.dot(n, w_ref[:, start:start + width], preferred_element_type=jnp.float32).astype(o_ref.dtype)


def proj_in(x, g, w, segments, dtypes, *, normalize=True, tm=256):
    t, d = x.shape
    tm = min(tm, t)
    return pl.pallas_call(
        functools.partial(_proj_in_body, tuple(segments), normalize),
        grid=(t // tm,),
        in_specs=[pl.BlockSpec((tm, d), lambda i: (i, 0)), pl.BlockSpec((1, d), lambda i: (0, 0)),
                  pl.BlockSpec(w.shape, lambda i: (0, 0))],
        out_specs=[pl.BlockSpec((tm, width), lambda i: (i, 0)) for _, width in segments],
        out_shape=[jax.ShapeDtypeStruct((t, width), dt) for (_, width), dt in zip(segments, dtypes)],
        compiler_params=pltpu.CompilerParams(dimension_semantics=("parallel",), vmem_limit_bytes=VMEM_LIMIT_BYTES),
        name="proj_in",
    )(x, g.reshape(1, d).astype(jnp.float32), w)


def _mix_out_body(x_ref, cq_ref, om_ref, mkt_ref, mv_ref, w_ref, o_ref):
    f32, bf16 = jnp.float32, jnp.bfloat16
    cq = cq_ref[...]
    heads = []
    for h in range(MEM_HEADS):
        sl = slice(h * HEAD_DIM, (h + 1) * HEAD_DIM)
        logits = jnp.dot(cq[:, sl], mkt_ref[sl, :], preferred_element_type=f32) * HEAD_DIM ** -0.5
        p = jnp.exp(logits - jnp.max(logits, axis=-1, keepdims=True))
        p = p / jnp.sum(p, axis=-1, keepdims=True)
        heads.append(jnp.dot(p.astype(bf16), mv_ref[:, sl], preferred_element_type=f32))
    o_mem = jnp.concatenate(heads, axis=-1).astype(bf16)
    y = (jnp.dot(o_mem, w_ref[:MEM_WIDTH, :], preferred_element_type=f32)
         + jnp.dot(om_ref[...].astype(bf16), w_ref[MEM_WIDTH:, :], preferred_element_type=f32))
    o_ref[...] = x_ref[...] + y


def mix_out(x, cq, o_mix, mk, mv, w_out, *, tm=512):
    bsz, L, d = x.shape
    tm = min(tm, L)
    bf16 = jnp.bfloat16
    mkt = mk.astype(bf16).reshape(bsz, N_MEM, MEM_WIDTH).swapaxes(1, 2)
    mvf = mv.astype(bf16).reshape(bsz, N_MEM, MEM_WIDTH)
    tok = lambda n: pl.BlockSpec((None, tm, n), lambda b, t: (b, t, 0))
    per_b = lambda r, c: pl.BlockSpec((None, r, c), lambda b, t: (b, 0, 0))
    return pl.pallas_call(
        _mix_out_body,
        grid=(bsz, L // tm),
        in_specs=[tok(d), tok(MEM_WIDTH), tok(SEQ_WIDTH), per_b(MEM_WIDTH, N_MEM), per_b(N_MEM, MEM_WIDTH),
                  pl.BlockSpec((d, d), lambda b, t: (0, 0))],
        out_specs=tok(d),
        out_shape=jax.ShapeDtypeStruct((bsz, L, d), jnp.float32),
        compiler_params=pltpu.CompilerParams(
            dimension_semantics=("parallel", "parallel"), vmem_limit_bytes=VMEM_LIMIT_BYTES),
        name="mix_out",
    )(x, cq, o_mix, mkt, mvf, w_out)


S5_LANES = S5_GROUPS * S5_STATE
S5_GROUPS_PER_BLOCK = LANES // S5_GROUP
S5_BLOCKS = SEQ_WIDTH // LANES
S5_BLOCK_STATES = S5_GROUPS_PER_BLOCK * S5_STATE


def s5_discretize(lam_re, lam_im, log_step, b_re, b_im, c_re, c_im):
    f32, bf16 = jnp.float32, jnp.bfloat16
    lr, li = lam_re.astype(f32), lam_im.astype(f32)
    step = jnp.exp(log_step.astype(f32))[:, None]
    mag = jnp.exp(lr * step)
    ab_re, ab_im = mag * jnp.cos(li * step), mag * jnp.sin(li * step)
    den = lr * lr + li * li
    nr, ni = ab_re - 1.0, ab_im
    f_re = (nr * lr + ni * li) / den
    f_im = (ni * lr - nr * li) / den
    br, bi = b_re.astype(f32), b_im.astype(f32)
    bb_re = f_re[..., None] * br - f_im[..., None] * bi
    bb_im = f_re[..., None] * bi + f_im[..., None] * br
    eye = jnp.eye(S5_GROUPS_PER_BLOCK, dtype=f32)
    nb, gb = S5_BLOCKS, S5_GROUPS_PER_BLOCK

    def in_blocks(bb):
        w = jnp.einsum('jgpc,gh->jgchp', bb.reshape(nb, gb, S5_STATE, S5_GROUP), eye)
        return w.reshape(nb, LANES, S5_BLOCK_STATES).astype(bf16)

    def out_blocks(c):
        w = jnp.einsum('jgop,gh->jgpho', c.astype(f32).reshape(nb, gb, S5_GROUP, S5_STATE), eye)
        return w.reshape(nb, S5_BLOCK_STATES, LANES).astype(bf16)

    return (ab_re.reshape(1, S5_LANES), ab_im.reshape(1, S5_LANES),
            in_blocks(bb_re), in_blocks(bb_im), out_blocks(c_re), out_blocks(-c_im))


def _s5_project_in(u, wbr_ref, wbi_ref, bur_ref, bui_ref):
    ub = u.astype(jnp.bfloat16)
    for j in range(S5_BLOCKS):
        uj = ub[:, j * LANES:(j + 1) * LANES]
        sl = slice(j * S5_BLOCK_STATES, (j + 1) * S5_BLOCK_STATES)
        bur_ref[:, sl] = jnp.dot(uj, wbr_ref[j], preferred_element_type=jnp.float32)
        bui_ref[:, sl] = jnp.dot(uj, wbi_ref[j], preferred_element_type=jnp.float32)


def _s5_project_out(hr_ref, hi_ref, wcr_ref, wci_ref):
    cols = []
    for j in range(S5_BLOCKS):
        sl = slice(j * S5_BLOCK_STATES, (j + 1) * S5_BLOCK_STATES)
        cols.append(jnp.dot(hr_ref[:, sl].astype(jnp.bfloat16), wcr_ref[j], preferred_element_type=jnp.float32)
                    + jnp.dot(hi_ref[:, sl].astype(jnp.bfloat16), wci_ref[j], preferred_element_type=jnp.float32))
    return jnp.concatenate(cols, axis=-1)


def _s5_gate(y_ssm, u, d_ref, wglu_ref, bglu_ref):
    y = y_ssm + d_ref[...] * u
    y = 0.5 * y * (1.0 + jnp.tanh(math.sqrt(2.0 / math.pi) * (y + 0.044715 * (y * y * y))))
    z = jnp.dot(y.astype(jnp.bfloat16), wglu_ref[...], preferred_element_type=jnp.float32) + bglu_ref[...]
    return y * (1.0 / (1.0 + jnp.exp(-z)))


def _s5_seq_body(u_ref, h0r_ref, h0i_ref, ar_ref, ai_ref, wbr_ref, wbi_ref, wcr_ref, wci_ref, d_ref, wglu_ref, bglu_ref,
                 y_ref, hfr_ref, hfi_ref, bur_ref, bui_ref, cr_ref, ci_ref):
    tt = u_ref.shape[0]

    @pl.when(pl.program_id(1) == 0)
    def _():
        cr_ref[...] = h0r_ref[...]
        ci_ref[...] = h0i_ref[...]

    u = u_ref[...]
    _s5_project_in(u, wbr_ref, wbi_ref, bur_ref, bui_ref)
    ar, ai = ar_ref[...], ai_ref[...]

    def step(t, carry):
        hr, hi = carry
        nhr = ar * hr - ai * hi + bur_ref[pl.ds(t, 1), :]
        nhi = ar * hi + ai * hr + bui_ref[pl.ds(t, 1), :]
        bur_ref[pl.ds(t, 1), :] = nhr
        bui_ref[pl.ds(t, 1), :] = nhi
        return nhr, nhi

    hr, hi = lax.fori_loop(0, tt, step, (cr_ref[...], ci_ref[...]))
    cr_ref[...] = hr
    ci_ref[...] = hi
    hfr_ref[...] = hr
    hfi_ref[...] = hi
    y_ref[...] = _s5_gate(_s5_project_out(bur_ref, bui_ref, wcr_ref, wci_ref), u, d_ref, wglu_ref, bglu_ref)


def _s5_step_body(u_ref, h0r_ref, h0i_ref, ar_ref, ai_ref, wbr_ref, wbi_ref, wcr_ref, wci_ref, d_ref, wglu_ref, bglu_ref,
                  y_ref, hfr_ref, hfi_ref, bur_ref, bui_ref):
    u = u_ref[...]
    _s5_project_in(u, wbr_ref, wbi_ref, bur_ref, bui_ref)
    ar, ai = ar_ref[...], ai_ref[...]
    hr, hi = h0r_ref[...], h0i_ref[...]
    nhr = ar * hr - ai * hi + bur_ref[...]
    nhi = ar * hi + ai * hr + bui_ref[...]
    bur_ref[...] = nhr
    bui_ref[...] = nhi
    hfr_ref[...] = nhr
    hfi_ref[...] = nhi
    y_ref[...] = _s5_gate(_s5_project_out(bur_ref, bui_ref, wcr_ref, wci_ref), u, d_ref, wglu_ref, bglu_ref)


def s5_mix_pallas(u, h0_re, h0_im, disc, d_skip, w_glu, b_glu, *, tt=512):
    bsz, L, w = u.shape
    f32 = jnp.float32
    ar, ai, wbr, wbi, wcr, wci = disc
    d2, bg2, wg = d_skip.reshape(1, w).astype(f32), b_glu.reshape(1, w).astype(f32), w_glu.astype(jnp.bfloat16)
    const2 = lambda *_: (0, 0)
    const3 = lambda *_: (0, 0, 0)
    w_specs = [pl.BlockSpec((1, S5_LANES), const2), pl.BlockSpec((1, S5_LANES), const2),
               pl.BlockSpec(wbr.shape, const3), pl.BlockSpec(wbi.shape, const3),
               pl.BlockSpec(wcr.shape, const3), pl.BlockSpec(wci.shape, const3),
               pl.BlockSpec((1, w), const2), pl.BlockSpec((w, w), const2), pl.BlockSpec((1, w), const2)]
    w_args = (ar, ai, wbr, wbi, wcr, wci, d2, wg, bg2)
    if L == 1:
        rows = bsz
        h0r, h0i = h0_re.reshape(rows, S5_LANES).astype(f32), h0_im.reshape(rows, S5_LANES).astype(f32)
        row_spec = lambda n: pl.BlockSpec((rows, n), const2)
        y, hr, hi = pl.pallas_call(
            _s5_step_body,
            grid=(1,),
            in_specs=[row_spec(w), row_spec(S5_LANES), row_spec(S5_LANES)] + w_specs,
            out_specs=[row_spec(w), row_spec(S5_LANES), row_spec(S5_LANES)],
            out_shape=[jax.ShapeDtypeStruct((rows, w), f32), jax.ShapeDtypeStruct((rows, S5_LANES), f32),
                       jax.ShapeDtypeStruct((rows, S5_LANES), f32)],
            scratch_shapes=[pltpu.VMEM((rows, S5_LANES), f32), pltpu.VMEM((rows, S5_LANES), f32)],
            compiler_params=pltpu.CompilerParams(vmem_limit_bytes=VMEM_LIMIT_BYTES),
            name="s5_step",
        )(u.reshape(rows, w), h0r, h0i, *w_args)
        y = y.reshape(bsz, 1, w)
    else:
        tt = min(tt, L)
        h0r, h0i = h0_re.reshape(bsz, 1, S5_LANES).astype(f32), h0_im.reshape(bsz, 1, S5_LANES).astype(f32)
        st_spec = pl.BlockSpec((None, 1, S5_LANES), lambda b, t: (b, 0, 0))
        y, hr, hi = pl.pallas_call(
            _s5_seq_body,
            grid=(bsz, L // tt),
            in_specs=[pl.BlockSpec((None, tt, w), lambda b, t: (b, t, 0)), st_spec, st_spec] + w_specs,
            out_specs=[pl.BlockSpec((None, tt, w), lambda b, t: (b, t, 0)), st_spec, st_spec],
            out_shape=[jax.ShapeDtypeStruct((bsz, L, w), f32), jax.ShapeDtypeStruct((bsz, 1, S5_LANES), f32),
                       jax.ShapeDtypeStruct((bsz, 1, S5_LANES), f32)],
            scratch_shapes=[pltpu.VMEM((tt, S5_LANES), f32), pltpu.VMEM((tt, S5_LANES), f32),
                            pltpu.VMEM((1, S5_LANES), f32), pltpu.VMEM((1, S5_LANES), f32)],
            compiler_params=pltpu.CompilerParams(
                dimension_semantics=("parallel", "arbitrary"), vmem_limit_bytes=VMEM_LIMIT_BYTES),
            name="s5_seq",
        )(u, h0r, h0i, *w_args)
    shp = (bsz, S5_GROUPS, S5_STATE)
    return y, hr.reshape(shp).astype(h0_re.dtype), hi.reshape(shp).astype(h0_im.dtype)


DSA_KEY_CHUNK = 512
DSA_NEAR = 2 * QBLOCK
INT32_MIN = -2 ** 31
NEG_BIG = -1e30


def _sortable_key(s):
    bits = lax.bitcast_convert_type(s, jnp.int32)
    return jnp.where(bits < 0, bits ^ jnp.int32(0x7FFFFFFF), bits)


def _dsa_prompt_body(n_top, rb_ref, q_ref, qi_ref, kiwi_ref, kt_ref, v_ref, kit_ref, o_ref, keys_ref, bias_ref):
    i = pl.program_id(1)
    f32, bf16 = jnp.float32, jnp.bfloat16
    kc = DSA_KEY_CHUNK
    q_start = i * QBLOCK
    n_all = (q_start + QBLOCK + kc - 1) // kc
    near_start = jnp.maximum(q_start - QBLOCK, 0)
    n_far = (near_start + kc - 1) // kc
    row = lax.broadcasted_iota(jnp.int32, (QBLOCK, 1), 0)
    qpos = q_start + row

    @pl.when(i == 0)
    def _():
        r = lax.broadcasted_iota(jnp.int32, (QBLOCK, DSA_NEAR), 0)
        c = lax.broadcasted_iota(jnp.int32, (QBLOCK, DSA_NEAR), 1)
        for tile in range(2):
            bucket = t5_bucket(r + tile * QBLOCK - c)
            for h in range(DSA_HEADS):
                b = jnp.zeros((QBLOCK, DSA_NEAR), f32)
                for bk in range(N_BUCKETS):
                    b = jnp.where(bucket == bk, rb_ref[bk, h] - rb_ref[N_BUCKETS - 1, h], b)
                kv, g = divmod(h, DSA_GQA)
                bias_ref[tile, kv, g * QBLOCK:(g + 1) * QBLOCK, :] = b

    qi_all = (qi_ref[...].astype(f32) * IDX_DIM ** -0.5).astype(bf16)
    qi = jnp.concatenate([qi_all[:, h * IDX_DIM:(h + 1) * IDX_DIM] for h in range(IDX_HEADS)], axis=0)
    wi = kiwi_ref[:, IDX_DIM:IDX_DIM + IDX_HEADS] * IDX_HEADS ** -0.5

    def score_chunk(c, _):
        off = pl.multiple_of(c * kc, kc)
        d = jnp.dot(qi, kit_ref[:, pl.ds(off, kc)], preferred_element_type=f32)
        s = jnp.zeros((QBLOCK, kc), f32)
        for h in range(IDX_HEADS):
            s = s + wi[:, h:h + 1] * jnp.maximum(d[h * QBLOCK:(h + 1) * QBLOCK], 0.0)
        kpos = off + lax.broadcasted_iota(jnp.int32, (QBLOCK, kc), 1)
        s = jnp.where(s == 0.0, 0.0, s)
        s = jnp.where(kpos <= qpos, s, -jnp.inf)
        keys_ref[:, pl.ds(off, kc)] = _sortable_key(s)
        return 0

    lax.fori_loop(0, n_all, score_chunk, 0)

    def count_where(pred_fn):
        def body(c, acc):
            off = pl.multiple_of(c * kc, kc)
            hit = pred_fn(keys_ref[:, pl.ds(off, kc)], off)
            part = jnp.where(hit, 1.0, 0.0)
            for j in range(kc // 128):
                acc = acc + part[:, j * 128:(j + 1) * 128]
            return acc
        acc = lax.fori_loop(0, n_all, body, jnp.zeros((QBLOCK, 128), f32))
        return jnp.sum(acc, axis=-1, keepdims=True)

    def thr_bit(it, t):
        cand = t + lax.shift_left(jnp.int32(1), 31 - it)
        cnt = count_where(lambda k, off: k >= cand)
        return jnp.where(cnt >= n_top, cand, t)

    thr = lax.fori_loop(0, 32, thr_bit, jnp.full((QBLOCK, 1), INT32_MIN, jnp.int32))

    def is_valid(off, width):
        return (off + lax.broadcasted_iota(jnp.int32, (QBLOCK, width), 1)) <= qpos

    n_gt = count_where(lambda k, off: (k > thr) & is_valid(off, kc))
    n_eq = count_where(lambda k, off: (k == thr) & is_valid(off, kc))
    need = n_top - n_gt
    has_extra_ties = jnp.max(jnp.where(n_eq > need, 1.0, 0.0)) > 0.0

    def tie_search():
        def idx_bit(it, j):
            cand = j + lax.shift_left(jnp.int32(1), 13 - it)
            cnt = count_where(lambda k, off: (k == thr) & is_valid(off, kc)
                              & ((off + lax.broadcasted_iota(jnp.int32, (QBLOCK, kc), 1)) < cand))
            return jnp.where(cnt <= need, cand, j)
        return lax.fori_loop(0, 14, idx_bit, jnp.zeros((QBLOCK, 1), jnp.int32))

    tie_end = lax.cond(has_extra_ties, tie_search, lambda: jnp.full((QBLOCK, 1), 2 ** 14, jnp.int32))

    def selected(keys, off, width):
        kpos = off + lax.broadcasted_iota(jnp.int32, (QBLOCK, width), 1)
        return (kpos <= qpos) & ((keys > thr) | ((keys == thr) & (kpos < tie_end))), kpos

    rows = DSA_GQA * QBLOCK
    tile = jnp.minimum(i, 1)
    q_all = (q_ref[...].astype(f32) * HEAD_DIM ** -0.5).astype(bf16)
    qks = [jnp.concatenate([q_all[:, (kv * DSA_GQA + g) * HEAD_DIM:(kv * DSA_GQA + g + 1) * HEAD_DIM]
                            for g in range(DSA_GQA)], axis=0) for kv in range(DSA_KV_HEADS)]

    def attend(carry, qk, kt, vv, mask_bias):
        m, l, acc = carry
        logits = jnp.dot(qk, kt, preferred_element_type=f32) + mask_bias
        m_new = jnp.maximum(m, jnp.max(logits, axis=-1, keepdims=True))
        alpha = jnp.exp(m - m_new)
        p = jnp.exp(logits - m_new)
        l = alpha * l + jnp.sum(p, axis=-1, keepdims=True)
        acc = alpha * acc + jnp.dot(p.astype(bf16), vv, preferred_element_type=f32)
        return m_new, l, acc

    def far_chunk(c, carry):
        off = pl.multiple_of(c * kc, kc)
        sel, kpos = selected(keys_ref[:, pl.ds(off, kc)], off, kc)
        mb = jnp.where(sel & (kpos < near_start), 0.0, NEG_BIG)
        mb3 = jnp.concatenate([mb] * DSA_GQA, axis=0)
        return tuple(attend(carry[kv], qks[kv], kt_ref[kv * HEAD_DIM:(kv + 1) * HEAD_DIM, pl.ds(off, kc)],
                            v_ref[kv, pl.ds(off, kc), :], mb3) for kv in range(DSA_KV_HEADS))

    init = tuple((jnp.full((rows, 1), NEG_BIG, f32), jnp.zeros((rows, 1), f32), jnp.zeros((rows, HEAD_DIM), f32))
                 for _ in range(DSA_KV_HEADS))
    carry = lax.fori_loop(0, n_far, far_chunk, init)
    off = pl.multiple_of(near_start, QBLOCK)
    sel, _ = selected(keys_ref[:, pl.ds(off, DSA_NEAR)], off, DSA_NEAR)
    sel3 = jnp.concatenate([sel] * DSA_GQA, axis=0)
    outs = []
    for kv in range(DSA_KV_HEADS):
        mb3 = jnp.where(sel3, bias_ref[tile, kv], NEG_BIG)
        m, l, acc = attend(carry[kv], qks[kv], kt_ref[kv * HEAD_DIM:(kv + 1) * HEAD_DIM, pl.ds(off, DSA_NEAR)],
                           v_ref[kv, pl.ds(off, DSA_NEAR), :], mb3)
        o = acc / l
        outs += [o[g * QBLOCK:(g + 1) * QBLOCK] for g in range(DSA_GQA)]
    o_ref[...] = jnp.concatenate(outs, axis=-1)


def dsa_prompt_pallas(q, qi, kiwi, k, v, rel_bias):
    bsz, L = q.shape[:2]
    nq = L // QBLOCK
    n_top = min(TOPK_MAX, L // 4)
    f32, bf16 = jnp.float32, jnp.bfloat16
    k_t = k.astype(bf16).swapaxes(1, 2)
    v_h = v.astype(bf16).reshape(bsz, L, DSA_KV_HEADS, HEAD_DIM).swapaxes(1, 2)
    ki_t = kiwi[..., :IDX_DIM].astype(bf16).swapaxes(1, 2)
    lk = max(L, DSA_KEY_CHUNK)
    if lk != L:
        k_t = jnp.pad(k_t, ((0, 0), (0, 0), (0, lk - L)))
        v_h = jnp.pad(v_h, ((0, 0), (0, 0), (0, lk - L), (0, 0)))
        ki_t = jnp.pad(ki_t, ((0, 0), (0, 0), (0, lk - L)))
    tok = lambda n: pl.BlockSpec((None, QBLOCK, n), lambda b, i: (b, i, 0))
    return pl.pallas_call(
        functools.partial(_dsa_prompt_body, n_top),
        grid=(bsz, nq),
        in_specs=[
            pl.BlockSpec(memory_space=pltpu.SMEM),
            tok(SEQ_WIDTH), tok(IDX_HEADS * IDX_DIM), tok(LANES),
            pl.BlockSpec((None, DSA_KV_HEADS * HEAD_DIM, lk), lambda b, i: (b, 0, 0)),
            pl.BlockSpec((None, DSA_KV_HEADS, lk, HEAD_DIM), lambda b, i: (b, 0, 0, 0)),
            pl.BlockSpec((None, IDX_DIM, lk), lambda b, i: (b, 0, 0)),
        ],
        out_specs=tok(SEQ_WIDTH),
        out_shape=jax.ShapeDtypeStruct((bsz, L, SEQ_WIDTH), f32),
        scratch_shapes=[pltpu.VMEM((QBLOCK, lk), jnp.int32),
                        pltpu.VMEM((2, DSA_KV_HEADS, DSA_GQA * QBLOCK, DSA_NEAR), f32)],
        compiler_params=pltpu.CompilerParams(
            dimension_semantics=("parallel", "arbitrary"), vmem_limit_bytes=VMEM_LIMIT_BYTES),
        name="dsa_prompt",
    )(rel_bias.astype(f32), q, qi, kiwi, k_t, v_h, ki_t)


DEC_PAGES_PER_STEP = 16
KV_WIDTH = DSA_KV_HEADS * HEAD_DIM


def _index_score(qi, wi, ki):
    d = _dot_nt(qi, ki)
    s = jnp.sum(wi * jnp.maximum(d, 0.0), axis=0, keepdims=True)
    return jnp.where(s == 0.0, 0.0, s)


def _dsa_dec_score_body(pt_ref, qi_ref, wi_ref, *refs):
    pages, s_ref = refs[:-1], refs[-1]
    qi = (qi_ref[...].astype(jnp.float32) * IDX_DIM ** -0.5).astype(jnp.bfloat16)
    wi = wi_ref[...]
    s_ref[...] = jnp.concatenate([_index_score(qi, wi, p[...].astype(jnp.bfloat16)) for p in pages], axis=0)


def _dsa_dec_attend_body(n_top, n_pages, pt_ref, rb_ref, q_ref, qi_ref, wi_ref, knew_ref, vnew_ref, kinew_ref, s_ref,
                         *refs):
    pp = DEC_PAGES_PER_STEP if n_pages >= DEC_PAGES_PER_STEP else n_pages
    k_pages, v_pages = refs[:pp], refs[pp:2 * pp]
    o_ref, thr_ref, tie_ref, snew_ref, m_ref, l_ref, acc_ref = refs[2 * pp:]
    f32, bf16 = jnp.float32, jnp.bfloat16
    t = pl.program_id(1)
    nt = pl.num_programs(1)
    past = n_pages * PAGE_SIZE
    page_i = lax.broadcasted_iota(jnp.int32, (n_pages, LANES), 0)
    lane_i = lax.broadcasted_iota(jnp.int32, (n_pages, LANES), 1)
    kpos_all = page_i * PAGE_SIZE + lane_i
    total = lambda x: jnp.sum(jnp.sum(x, axis=1, keepdims=True), axis=0, keepdims=True)

    @pl.when(t == 0)
    def _():
        qi = (qi_ref[...].astype(f32) * IDX_DIM ** -0.5).astype(bf16)
        d_new = jnp.sum(qi.astype(f32) * kinew_ref[...].astype(bf16).astype(f32), axis=-1, keepdims=True)
        s_new = jnp.sum(wi_ref[...] * jnp.maximum(d_new, 0.0), axis=0, keepdims=True)
        s_new = jnp.where(s_new == 0.0, 0.0, s_new)
        key_new = _sortable_key(s_new)
        keys = _sortable_key(s_ref[...])

        def count(pred_past, pred_new):
            return total(jnp.where(pred_past, 1.0, 0.0)) + jnp.where(pred_new, 1.0, 0.0)

        def thr_bit(it, thr):
            cand = thr + lax.shift_left(jnp.int32(1), 31 - it)
            return jnp.where(count(keys >= cand, key_new >= cand) >= n_top, cand, thr)

        thr = lax.fori_loop(0, 32, thr_bit, jnp.full((1, 1), INT32_MIN, jnp.int32))
        need = n_top - count(keys > thr, key_new > thr)

        def idx_bit(it, j):
            cand = j + lax.shift_left(jnp.int32(1), 14 - it)
            cnt = count((keys == thr) & (kpos_all < cand), (key_new == thr) & (past < cand))
            return jnp.where(cnt <= need, cand, j)

        n_eq = count(keys == thr, key_new == thr)
        tie_end = lax.cond(jnp.max(jnp.where(n_eq > need, 1.0, 0.0)) > 0.0,
                           lambda: lax.fori_loop(0, 15, idx_bit, jnp.zeros((1, 1), jnp.int32)),
                           lambda: jnp.full((1, 1), 2 ** 15, jnp.int32))
        thr_ref[...] = thr
        tie_ref[...] = tie_end
        snew_ref[...] = s_new
        m_ref[...] = jnp.full(m_ref.shape, NEG_BIG, f32)
        l_ref[...] = jnp.zeros(l_ref.shape, f32)
        acc_ref[...] = jnp.zeros(acc_ref.shape, f32)

    thr, tie_end = thr_ref[...], tie_ref[...]
    q = (q_ref[...].astype(f32) * HEAD_DIM ** -0.5).astype(bf16)
    lane12 = lax.broadcasted_iota(jnp.int32, (DSA_HEADS, LANES), 1)

    def head_bias(dist):
        bucket = t5_bucket(dist)
        b = jnp.zeros(dist.shape, f32)
        for h in range(DSA_HEADS):
            row = jnp.zeros(dist.shape, f32)
            for bk in range(N_BUCKETS):
                row = jnp.where(bucket == bk, rb_ref[bk, h] - rb_ref[N_BUCKETS - 1, h], row)
            b = jnp.where(lax.broadcasted_iota(jnp.int32, dist.shape, 0) == h, row, b)
        return b

    keys_step = _sortable_key(s_ref[pl.ds(pl.multiple_of(t * pp, pp), pp), :])
    tiles = []
    for j in range(pp):
        kpos = (t * pp + j) * PAGE_SIZE + lax.broadcasted_iota(jnp.int32, (1, LANES), 1)
        keys = keys_step[j:j + 1]
        sel = (keys > thr) | ((keys == thr) & (kpos < tie_end))
        logits = _dot_nt(q, k_pages[j][...].astype(bf16))
        if j == pp - 1:
            near = head_bias(past - (t * pp + j) * PAGE_SIZE - lane12)
            logits = logits + jnp.where(t == nt - 1, near, 0.0)
        tiles.append(jnp.where(sel, logits, NEG_BIG))
    m = m_ref[...]
    tile_max = tiles[0]
    for x in tiles[1:]:
        tile_max = jnp.maximum(tile_max, x)
    m_new = jnp.maximum(m, jnp.max(tile_max, axis=-1, keepdims=True))
    alpha = jnp.exp(m - m_new)
    ps = [jnp.exp(x - m_new) for x in tiles]
    p_sum = ps[0]
    for x in ps[1:]:
        p_sum = p_sum + x
    pv = jnp.dot(ps[0].astype(bf16), v_pages[0][...].astype(bf16), preferred_element_type=f32)
    for j in range(1, pp):
        pv = pv + jnp.dot(ps[j].astype(bf16), v_pages[j][...].astype(bf16), preferred_element_type=f32)
    l_ref[...] = alpha * l_ref[...] + jnp.sum(p_sum, axis=-1, keepdims=True)
    acc_ref[...] = alpha * acc_ref[...] + pv
    m_ref[...] = m_new

    @pl.when(t == nt - 1)
    def _():
        s_new = snew_ref[...]
        key_new = _sortable_key(s_new)
        sel_new = (key_new > thr) | ((key_new == thr) & (past < tie_end))
        k_new = knew_ref[...].astype(bf16).astype(f32)
        logit_new = jnp.sum(q.astype(f32) * k_new, axis=-1, keepdims=True)
        logit_new = logit_new + head_bias(jnp.zeros((DSA_HEADS, LANES), jnp.int32))[:, :1]
        logit_new = jnp.where(sel_new, logit_new, NEG_BIG)
        m = m_ref[...]
        m_new = jnp.maximum(m, logit_new)
        alpha = jnp.exp(m - m_new)
        p_new = jnp.exp(logit_new - m_new)
        l = alpha * l_ref[...] + p_new
        v_new = vnew_ref[...].astype(bf16).astype(f32)
        acc = alpha * acc_ref[...] + p_new.astype(bf16).astype(f32) * v_new
        o = acc / l
        o_ref[...] = jnp.concatenate(
            [o[h:h + 1, (h // DSA_GQA) * HEAD_DIM:(h // DSA_GQA + 1) * HEAD_DIM] for h in range(DSA_HEADS)], axis=-1)


def dsa_decode_pallas(q, qi, kiwi, k_new, v_new, pool_k, pool_v, pool_ki, page_table, rel_bias):
    bsz = q.shape[0]
    n_pages = page_table.shape[1]
    n_phys = pool_k.shape[0]
    pp = DEC_PAGES_PER_STEP if n_pages >= DEC_PAGES_PER_STEP else n_pages
    nt = n_pages // pp
    n_top = min(TOPK_MAX, (n_pages * PAGE_SIZE + 1) // 4)
    f32, bf16 = jnp.float32, jnp.bfloat16
    eye = jnp.eye(DSA_KV_HEADS, dtype=q.dtype)
    q_bd = jnp.einsum('bkgd,kj->bkgjd', q.reshape(bsz, DSA_KV_HEADS, DSA_GQA, HEAD_DIM), eye)
    q_bd = q_bd.reshape(bsz, DSA_HEADS, KV_WIDTH)
    qi3 = qi.reshape(bsz, IDX_HEADS, IDX_DIM)
    wi3 = (kiwi[:, IDX_DIM:IDX_DIM + IDX_HEADS] * IDX_HEADS ** -0.5).reshape(bsz, IDX_HEADS, 1)
    ki_new = kiwi[:, :IDX_DIM].reshape(bsz, 1, IDX_DIM)
    pk = pool_k.reshape(n_phys, PAGE_SIZE, KV_WIDTH)
    pv = pool_v.reshape(n_phys, PAGE_SIZE, KV_WIDTH)
    per_b = lambda *shape: pl.BlockSpec((None,) + shape, lambda b, t, pt: (b,) + (0,) * len(shape))
    page = lambda width, j: pl.BlockSpec((None, PAGE_SIZE, width), lambda b, t, pt: (pt[b, t * pp + j], 0, 0))
    scores = pl.pallas_call(
        _dsa_dec_score_body,
        grid_spec=pltpu.PrefetchScalarGridSpec(
            num_scalar_prefetch=1, grid=(bsz, nt),
            in_specs=[per_b(IDX_HEADS, IDX_DIM), per_b(IDX_HEADS, 1)] + [page(IDX_DIM, j) for j in range(pp)],
            out_specs=pl.BlockSpec((None, pp, LANES), lambda b, t, pt: (b, t, 0))),
        out_shape=jax.ShapeDtypeStruct((bsz, n_pages, LANES), f32),
        compiler_params=pltpu.CompilerParams(
            dimension_semantics=("parallel", "parallel"), vmem_limit_bytes=VMEM_LIMIT_BYTES),
        name="dsa_dec_score",
    )(page_table, qi3, wi3, *([pool_ki] * pp))
    one = lambda: pltpu.VMEM((1, 1), jnp.int32)
    o = pl.pallas_call(
        functools.partial(_dsa_dec_attend_body, n_top, n_pages),
        grid_spec=pltpu.PrefetchScalarGridSpec(
            num_scalar_prefetch=1, grid=(bsz, nt),
            in_specs=[pl.BlockSpec(memory_space=pltpu.SMEM), per_b(DSA_HEADS, KV_WIDTH), per_b(IDX_HEADS, IDX_DIM),
                      per_b(IDX_HEADS, 1), per_b(1, KV_WIDTH), per_b(1, KV_WIDTH), per_b(1, IDX_DIM),
                      per_b(n_pages, LANES)]
            + [page(KV_WIDTH, j) for j in range(pp)] * 2,
            out_specs=per_b(1, SEQ_WIDTH),
            scratch_shapes=[one(), one(), pltpu.VMEM((1, 1), f32), pltpu.VMEM((DSA_HEADS, 1), f32),
                            pltpu.VMEM((DSA_HEADS, 1), f32), pltpu.VMEM((DSA_HEADS, KV_WIDTH), f32)]),
        out_shape=jax.ShapeDtypeStruct((bsz, 1, SEQ_WIDTH), f32),
        compiler_params=pltpu.CompilerParams(
            dimension_semantics=("parallel", "arbitrary"), vmem_limit_bytes=VMEM_LIMIT_BYTES),
        name="dsa_dec_attend",
    )(page_table, rel_bias.astype(f32), q_bd, qi3, wi3, k_new.reshape(bsz, 1, KV_WIDTH),
      v_new.reshape(bsz, 1, KV_WIDTH), ki_new, scores, *([pk] * pp), *([pv] * pp))
    return o


CONV_HALO = 8


def _gdn_prep_body(x_ref, ab_ref, ctx_ref, w_ref, alog_ref, dtb_ref, q_ref, k_ref, v_ref, gb_ref, cs_ref, xp_ref):
    tt = x_ref.shape[0]
    halo = CONV_W - 1

    @pl.when(pl.program_id(1) == 0)
    def _():
        xp_ref[CONV_HALO - halo:CONV_HALO, :] = ctx_ref[...]

    x = x_ref[...]
    xp_ref[CONV_HALO:CONV_HALO + tt, :] = x
    w = w_ref[...]
    y = xp_ref[CONV_HALO - halo:CONV_HALO - halo + tt, :] * w[0:1]
    for j in range(1, CONV_W):
        y = y + xp_ref[CONV_HALO - halo + j:CONV_HALO - halo + j + tt, :] * w[j:j + 1]
    last = x[tt - halo:, :]
    xp_ref[CONV_HALO - halo:CONV_HALO, :] = last
    cs_ref[...] = last
    y = y * _sigmoid(y)

    def l2n(a, scale):
        cols = []
        for h in range(GDN_HEADS):
            s = a[:, h * HEAD_DIM:(h + 1) * HEAD_DIM]
            cols.append(s * (lax.rsqrt(jnp.sum(s * s, axis=-1, keepdims=True) + EPS) * scale))
        return jnp.concatenate(cols, axis=-1)

    q_ref[...] = l2n(y[:, :SEQ_WIDTH], HEAD_DIM ** -0.5)
    k_ref[...] = l2n(y[:, SEQ_WIDTH:2 * SEQ_WIDTH], 1.0)
    v_ref[...] = y[:, 2 * SEQ_WIDTH:]
    ab = ab_ref[...]
    xa = ab + dtb_ref[...]
    softplus = jnp.maximum(xa, 0.0) + jnp.log(1.0 + jnp.exp(-jnp.abs(xa)))
    g = -jnp.exp(alog_ref[...]) * softplus
    lane = lax.broadcasted_iota(jnp.int32, ab.shape, 1)
    gb_ref[...] = jnp.where(lane < GDN_HEADS, g, _sigmoid(ab))


def gdn_prep_pallas(qkv, ab, ctx, conv_w, a_log, dt_bias, *, tt=256):
    bsz, L, ch = qkv.shape
    tt = min(tt, L)
    f32 = jnp.float32
    pad = lambda r: jnp.pad(r.astype(f32).reshape(1, -1), ((0, 0), (0, LANES - r.shape[-1])))
    tok = lambda n: pl.BlockSpec((None, tt, n), lambda b, t: (b, t, 0))
    const2 = lambda b, t: (0, 0)
    return pl.pallas_call(
        _gdn_prep_body,
        grid=(bsz, L // tt),
        in_specs=[tok(ch), tok(LANES), pl.BlockSpec((None, CONV_W - 1, ch), lambda b, t: (b, 0, 0)),
                  pl.BlockSpec((CONV_W, ch), const2), pl.BlockSpec((1, LANES), const2), pl.BlockSpec((1, LANES), const2)],
        out_specs=[tok(SEQ_WIDTH), tok(SEQ_WIDTH), tok(SEQ_WIDTH), tok(LANES),
                   pl.BlockSpec((None, CONV_W - 1, ch), lambda b, t: (b, 0, 0))],
        out_shape=[jax.ShapeDtypeStruct((bsz, L, SEQ_WIDTH), f32)] * 3
        + [jax.ShapeDtypeStruct((bsz, L, LANES), f32), jax.ShapeDtypeStruct((bsz, CONV_W - 1, ch), f32)],
        scratch_shapes=[pltpu.VMEM((CONV_HALO + tt, ch), f32)],
        compiler_params=pltpu.CompilerParams(
            dimension_semantics=("parallel", "arbitrary"), vmem_limit_bytes=VMEM_LIMIT_BYTES),
        name="gdn_prep",
    )(qkv, ab, ctx.astype(f32), conv_w.astype(f32), pad(a_log), pad(dt_bias))


def _split3(a):
    bf16, f32 = jnp.bfloat16, jnp.float32
    h = a.astype(bf16)
    r = a - h.astype(f32)
    m = r.astype(bf16)
    return h, m, (r - m.astype(f32)).astype(bf16)


def _mm_hi(a, b):
    f32 = jnp.float32
    ah, am, _ = _split3(a)
    bh, bm, _ = _split3(b)
    d = lambda x, y: jnp.dot(x, y, preferred_element_type=f32)
    return d(ah, bh) + (d(ah, bm) + d(am, bh))


def _mm_sel(sel, b):
    f32 = jnp.float32
    s = sel.astype(jnp.bfloat16)
    bh, bm, bl = _split3(b)
    d = lambda y: jnp.dot(s, y, preferred_element_type=f32)
    return d(bh) + (d(bm) + d(bl))


def _dot_nt(a, b):
    return lax.dot_general(a, b, (((1,), (1,)), ((), ())), preferred_element_type=jnp.float32)


def _dot_tn(a, b):
    return lax.dot_general(a, b, (((0,), (0,)), ((), ())), preferred_element_type=jnp.float32)


def _gdn_local_body(q_ref, k_ref, v_ref, gb_ref, uv_ref, wk_ref, qh_ref, kt_ref, qk_ref):
    f32, bf16 = jnp.float32, jnp.bfloat16
    c = GDN_CHUNK
    r_i = lax.broadcasted_iota(jnp.int32, (c, c), 0)
    c_i = lax.broadcasted_iota(jnp.int32, (c, c), 1)
    tril = r_i >= c_i
    stril = r_i > c_i
    triu = r_i <= c_i
    eye = jnp.where(r_i == c_i, 1.0, 0.0)
    ones = jnp.ones((c, c), jnp.bool_)
    lane = lax.broadcasted_iota(jnp.int32, (c, LANES), 1)
    gb = gb_ref[...]
    heads = range(GDN_HEADS)
    sl = [slice(h * HEAD_DIM, (h + 1) * HEAD_DIM) for h in heads]
    pick = lambda idx: jnp.sum(jnp.where(lane == idx, gb, 0.0), axis=-1, keepdims=True)
    g_col = [pick(h) for h in heads]
    beta = [pick(GDN_HEADS + h) for h in heads]
    q = [q_ref[:, sl[h]] for h in heads]
    k = [k_ref[:, sl[h]] for h in heads]
    v = [v_ref[:, sl[h]] for h in heads]
    g_mat = [jnp.broadcast_to(g_col[h], (c, c)) for h in heads]
    cum_col = [_mm_sel(tril, g_mat[h]) for h in heads]
    cum_row = [_mm_sel(ones, jnp.where(triu, g_mat[h], 0.0)) for h in heads]
    gam = [jnp.where(tril, jnp.exp(jnp.where(tril, cum_col[h] - cum_row[h], 0.0)), 0.0) for h in heads]
    kb = [k[h].astype(bf16) for h in heads]
    a_mat = [jnp.where(stril, beta[h] * _dot_nt(kb[h], kb[h]) * gam[h], 0.0) for h in heads]
    t_inv = [eye - a_mat[h] for h in heads]
    pw = a_mat
    for _ in range(5):
        pw = [_mm_hi(pw[h], pw[h]) for h in heads]
        t_inv = [t_inv[h] + _mm_hi(t_inv[h], pw[h]) for h in heads]
    gc = [cum_col[h][:, :1] for h in heads]
    egc = [jnp.exp(gc[h]) for h in heads]
    tw = [_mm_hi(t_inv[h], jnp.concatenate([beta[h] * v[h], (beta[h] * egc[h]) * k[h]], axis=-1)) for h in heads]
    qk = [jnp.where(tril, _dot_nt(q[h].astype(bf16), kb[h]) * gam[h], 0.0) for h in heads]
    cat = lambda xs: jnp.concatenate(xs, axis=-1)
    uv_ref[...] = cat([tw[h][:, :HEAD_DIM] for h in heads])
    wk_ref[...] = cat([tw[h][:, HEAD_DIM:] for h in heads]).astype(bf16)
    qh_ref[...] = cat([egc[h] * q[h] for h in heads]).astype(bf16)
    eye_b = eye.astype(bf16)
    kt = [(jnp.exp(gc[h][c - 1:c, :] - gc[h]) * k[h]).astype(bf16) for h in heads]
    kt_ref[...] = cat([_dot_tn(kt[h], eye_b) for h in heads]).astype(bf16)
    qk_ref[...] = cat(qk).astype(bf16)


def gdn_local_pallas(qn, kn, vv, gb):
    bsz, L, w = qn.shape
    c = GDN_CHUNK
    f32, bf16 = jnp.float32, jnp.bfloat16
    tok = lambda n: pl.BlockSpec((None, c, n), lambda b, t: (b, t, 0))
    return pl.pallas_call(
        _gdn_local_body,
        grid=(bsz, L // c),
        in_specs=[tok(w), tok(w), tok(w), tok(LANES)],
        out_specs=[tok(w)] * 5,
        out_shape=[jax.ShapeDtypeStruct((bsz, L, w), f32)] + [jax.ShapeDtypeStruct((bsz, L, w), bf16)] * 4,
        compiler_params=pltpu.CompilerParams(
            dimension_semantics=("parallel", "parallel"), vmem_limit_bytes=VMEM_LIMIT_BYTES),
        name="gdn_local",
    )(qn, kn, vv, gb)


def _gdn_scan_body(uv_ref, wk_ref, qh_ref, ktt_ref, qk_ref, gb_ref, gate_ref, s0_ref, on_ref, o_ref, sf_ref, s_ref):
    f32, bf16 = jnp.float32, jnp.bfloat16
    c = GDN_CHUNK
    tt = uv_ref.shape[0]
    heads = range(GDN_HEADS)
    sl = [slice(h * HEAD_DIM, (h + 1) * HEAD_DIM) for h in heads]

    @pl.when(pl.program_id(1) == 0)
    def _():
        s_ref[...] = s0_ref[...]

    lane = lax.broadcasted_iota(jnp.int32, (1, LANES), 1)
    dot = lambda a, b: jnp.dot(a, b, preferred_element_type=f32)

    def chunk(ci, _):
        rows = pl.ds(pl.multiple_of(ci * c, c), c)
        eg_last = jnp.exp(jnp.sum(gb_ref[rows, :], axis=0, keepdims=True))
        eg = [jnp.sum(jnp.where(lane == h, eg_last, 0.0), axis=-1, keepdims=True) for h in heads]
        st = [s_ref[h] for h in heads]
        sb = [st[h].astype(bf16) for h in heads]
        u = [uv_ref[rows, sl[h]] - dot(wk_ref[rows, sl[h]], sb[h]) for h in heads]
        ub = [u[h].astype(bf16) for h in heads]
        new = [eg[h] * st[h] + dot(ktt_ref[rows, sl[h]], ub[h]) for h in heads]
        for h in heads:
            s_ref[h] = new[h]
        o = [dot(qh_ref[rows, sl[h]], sb[h]) + dot(qk_ref[rows, sl[h]], ub[h]) for h in heads]
        o = [o[h] * lax.rsqrt(jnp.mean(o[h] * o[h], axis=-1, keepdims=True) + EPS) for h in heads]
        gt = gate_ref[rows, :]
        o_ref[rows, :] = (jnp.concatenate(o, axis=-1) * on_ref[...]) * (gt * _sigmoid(gt))
        return 0

    lax.fori_loop(0, tt // c, chunk, 0)
    sf_ref[...] = s_ref[...]


def gdn_scan_pallas(uv, wk, qh, kt, qk, gb, gate, s0, o_norm, *, tt=256):
    bsz, L, w = uv.shape
    tt = min(tt, L)
    f32 = jnp.float32
    tok = lambda n: pl.BlockSpec((None, tt, n), lambda b, t: (b, t, 0))
    st = pl.BlockSpec((None, GDN_HEADS, HEAD_DIM, HEAD_DIM), lambda b, t: (b, 0, 0, 0))
    on = jnp.tile(o_norm.astype(f32), GDN_HEADS).reshape(1, w)
    return pl.pallas_call(
        _gdn_scan_body,
        grid=(bsz, L // tt),
        in_specs=[tok(w)] * 5 + [tok(LANES), tok(w), st, pl.BlockSpec((1, w), lambda b, t: (0, 0))],
        out_specs=[tok(w), st],
        out_shape=[jax.ShapeDtypeStruct((bsz, L, w), f32), jax.ShapeDtypeStruct(s0.shape, f32)],
        scratch_shapes=[pltpu.VMEM((GDN_HEADS, HEAD_DIM, HEAD_DIM), f32)],
        compiler_params=pltpu.CompilerParams(
            dimension_semantics=("parallel", "arbitrary"), vmem_limit_bytes=VMEM_LIMIT_BYTES),
        name="gdn_scan",
    )(uv, wk, qh, kt, qk, gb, gate, s0.astype(f32).swapaxes(2, 3), on)


def gdn_mix_pallas(qkv, ab, gate, conv_ctx, s0, conv_w, a_log, dt_bias, o_norm):
    qn, kn, vv, gb, conv_state = gdn_prep_pallas(qkv, ab, conv_ctx, conv_w, a_log, dt_bias)
    uv, wk, qh, kt, qk = gdn_local_pallas(qn, kn, vv, gb)
    o, s_fin_t = gdn_scan_pallas(uv, wk, qh, kt, qk, gb, gate, s0, o_norm)
    return o, conv_state.astype(qkv.dtype), s_fin_t.swapaxes(2, 3).astype(s0.dtype)


def _pad_cols(w, n):
    return jnp.pad(w, ((0, 0), (0, n - w.shape[1])))


def _in_proj_layout(kind, w_in):
    f32, bf16 = jnp.float32, jnp.bfloat16
    if kind == 0:
        return w_in, ((0, MEM_WIDTH), (MEM_WIDTH, SEQ_WIDTH)), (bf16, f32)
    if kind == 1:
        w = _pad_cols(w_in, MEM_WIDTH + sum(DSA_SPLITS[:4]) + LANES)
        widths = (MEM_WIDTH,) + DSA_SPLITS[:4] + (LANES,)
        starts = np.cumsum((0,) + widths[:-1]).tolist()
        return w, tuple(zip(starts, widths)), (bf16, bf16, f32, f32, bf16, f32)
    c0, c1 = MEM_WIDTH + CONV_CH, MEM_WIDTH + CONV_CH + 2 * GDN_HEADS
    w = jnp.concatenate([w_in[:, :c0], w_in[:, c1:], _pad_cols(w_in[:, c0:c1], LANES)], axis=1)
    widths = (MEM_WIDTH, CONV_CH, SEQ_WIDTH, LANES)
    starts = np.cumsum((0,) + widths[:-1]).tolist()
    return w, tuple(zip(starts, widths)), (bf16, f32, f32, f32)


def kernel(x_prompt, x_sample, cache_mem_k, cache_mem_v, state_ssm_re, state_ssm_im, cache_k, cache_v, cache_kidx, state_conv, state_delta, page_table, mem_prompt, norm_g, final_norm, w_in_a, w_in_b, w_in_c, w_out, w_mem_kv, ffn1_gate, ffn1_up, ffn1_down, ffn2_gate, ffn2_up, ffn2_down, s5_lam_re, s5_lam_im, s5_log_step, s5_b_re, s5_b_im, s5_c_re, s5_c_im, s5_d, s5_w_glu, s5_b_glu, rel_bias, gdn_conv_w, gdn_a_log, gdn_dt_bias, gdn_o_norm):
    depth = norm_g.shape[0]
    bp, lp, d = x_prompt.shape
    bs, ls, _ = x_sample.shape
    assert ls == 1, "the decode-step kernels handle one new token per sample"
    f32, bf16 = jnp.float32, jnp.bfloat16
    hp, hs = x_prompt.reshape(bp * lp, d), x_sample.reshape(bs * ls, d)
    mem_rows = mem_prompt.reshape(bp * N_MEM, d)
    mem_k_p, mem_v_p = [], []
    ssm_re_p, ssm_im_p, ssm_re_s, ssm_im_s = [], [], [], []
    k_p, v_p, ki_p, k_s, v_s, ki_s = [], [], [], [], [], []
    conv_p, delta_p, conv_s, delta_s = [], [], [], []
    w_in_by_kind = (w_in_a, w_in_b, w_in_c)
    for i in range(depth):
        kind, j = i % N_MIXERS, i // N_MIXERS
        last = i == depth - 1
        ffn1 = (ffn1_gate[i].astype(bf16), ffn1_up[i].astype(bf16), ffn1_down[i].astype(bf16))
        ffn2 = (ffn2_gate[i].astype(bf16), ffn2_up[i].astype(bf16), ffn2_down[i].astype(bf16))
        w_in, segments, dtypes = _in_proj_layout(kind, w_in_by_kind[kind][j])
        w_in = w_in.astype(bf16)
        hp = ffn_residual(hp, norm_g[i, 0], *ffn1)
        hs = ffn_residual(hs, norm_g[i, 0], *ffn1)
        zp = [a.reshape(bp, lp, -1) for a in proj_in(hp, norm_g[i, 1], w_in, segments, dtypes)]
        zs = [a.reshape(bs, ls, -1) for a in proj_in(hs, norm_g[i, 1], w_in, segments, dtypes)]
        mk, mv = proj_in(mem_rows, norm_g[i, 1], w_mem_kv[i].astype(bf16), ((0, MEM_WIDTH), (MEM_WIDTH, MEM_WIDTH)),
                         (f32, f32), normalize=False)
        mem_shape = (bp, N_MEM, MEM_HEADS, HEAD_DIM)
        mk, mv = mk.reshape(mem_shape), mv.reshape(mem_shape)
        mem_k_p.append(mk)
        mem_v_p.append(mv)
        if kind == 0:
            disc = s5_discretize(s5_lam_re[j], s5_lam_im[j], s5_log_step[j], s5_b_re[j], s5_b_im[j],
                                 s5_c_re[j], s5_c_im[j])
            gate = (s5_d[j], s5_w_glu[j], s5_b_glu[j])
            h0 = jnp.zeros((bp, S5_GROUPS, S5_STATE), state_ssm_re.dtype)
            op, hr, hi = s5_mix_pallas(zp[1], h0, h0, disc, *gate)
            ssm_re_p.append(hr)
            ssm_im_p.append(hi)
            osm, hr, hi = s5_mix_pallas(zs[1], state_ssm_re[j], state_ssm_im[j], disc, *gate)
            ssm_re_s.append(hr)
            ssm_im_s.append(hi)
        elif kind == 1:
            _, q, k, v, qi, kiwi = zp
            op = dsa_prompt_pallas(q, qi, kiwi, k, v, rel_bias)
            kv_shape = (DSA_KV_HEADS, HEAD_DIM)
            k_p.append(k.reshape((bp, lp) + kv_shape))
            v_p.append(v.reshape((bp, lp) + kv_shape))
            ki_p.append(kiwi[..., :IDX_DIM])
            _, q, k, v, qi, kiwi = zs
            osm = dsa_decode_pallas(q[:, 0], qi[:, 0], kiwi[:, 0], k[:, 0], v[:, 0], cache_k[j], cache_v[j],
                                    cache_kidx[j], page_table, rel_bias)
            k_s.append(k.reshape((bs, ls) + kv_shape))
            v_s.append(v.reshape((bs, ls) + kv_shape))
            ki_s.append(kiwi[..., :IDX_DIM])
        else:
            gdn = (gdn_conv_w[j], gdn_a_log[j], gdn_dt_bias[j], gdn_o_norm[j])
            ctx0 = jnp.zeros((bp, CONV_W - 1, CONV_CH), f32)
            s00 = jnp.zeros((bp, GDN_HEADS, HEAD_DIM, HEAD_DIM), state_delta.dtype)
            _, qkv, gate, ab = zp
            op, cst, sst = gdn_mix_pallas(qkv, ab, gate, ctx0, s00, *gdn)
            conv_p.append(cst)
            delta_p.append(sst)
            _, qkv, gate, ab = zs
            z_s = jnp.concatenate([qkv, ab[..., :2 * GDN_HEADS], gate], axis=-1)
            osm, cst, sst = gdn_mix(z_s, state_conv[j], state_delta[j], *gdn)
            conv_s.append(cst)
            delta_s.append(sst)
        w_o = w_out[i].astype(bf16)
        hp = mix_out(hp.reshape(bp, lp, d), zp[0], op, mk, mv, w_o).reshape(bp * lp, d)
        hs = mix_out(hs.reshape(bs, ls, d), zs[0], osm, cache_mem_k[i], cache_mem_v[i], w_o).reshape(bs * ls, d)
        fg = final_norm if last else None
        hp = ffn_residual(hp, norm_g[i, 2], *ffn2, fg)
        hs = ffn_residual(hs, norm_g[i, 2], *ffn2, fg)
    st = jnp.stack
    return (hp.reshape(bp, lp, d), hs.reshape(bs, ls, d), st(mem_k_p), st(mem_v_p),
            st(ssm_re_p), st(ssm_im_p), st(ssm_re_s), st(ssm_im_s),
            st(k_p), st(v_p), st(ki_p), st(k_s), st(v_s), st(ki_s),
            st(conv_p), st(delta_p), st(conv_s), st(delta_s))
```

```python
import math
import functools
import jax
import jax.numpy as jnp
from jax import lax
import numpy as np
from jax.experimental import pallas as pl
from jax.experimental.pallas import tpu as pltpu

D_MODEL = 1024
N_MIXERS = 3
HEAD_DIM = 64
MIX_WIDTH = D_MODEL
N_MEM = 256
MEM_HEADS = 4
MEM_WIDTH = MEM_HEADS * HEAD_DIM
SEQ_WIDTH = MIX_WIDTH - MEM_WIDTH
S5_GROUP = 16
S5_GROUPS = SEQ_WIDTH // S5_GROUP
S5_STATE = 64
DSA_HEADS = SEQ_WIDTH // HEAD_DIM
DSA_KV_HEADS = 4
DSA_GQA = DSA_HEADS // DSA_KV_HEADS
IDX_HEADS = 8
IDX_DIM = 64
TOPK_MAX = 256
QBLOCK = 128
N_BUCKETS = 32
MAX_DISTANCE = 128
GDN_HEADS = SEQ_WIDTH // HEAD_DIM
CONV_W = 4
CONV_CH = 3 * SEQ_WIDTH
GDN_CHUNK = 64
D_FF = 2816
EPS = 1e-6
PAGE_SIZE = 128
DSA_SPLITS = (DSA_HEADS * HEAD_DIM, DSA_KV_HEADS * HEAD_DIM, DSA_KV_HEADS * HEAD_DIM, IDX_HEADS * IDX_DIM, IDX_DIM, IDX_HEADS)
GDN_SPLITS = (CONV_CH, GDN_HEADS, GDN_HEADS, SEQ_WIDTH)

LANES = 128
VMEM_LIMIT_BYTES = 56 * 1024 * 1024


def split_cols(z, sizes):
    return jnp.split(z, np.cumsum(sizes)[:-1].tolist(), axis=-1)


def rmsnorm(x, g):
    x32 = x.astype(jnp.float32)
    r = lax.rsqrt(jnp.mean(x32 * x32, axis=-1, keepdims=True) + EPS)
    return (x32 * r).astype(x.dtype) * g


def l2norm(x):
    return x * lax.rsqrt(jnp.sum(x * x, axis=-1, keepdims=True) + EPS)


def _sigmoid(x):
    return 1.0 / (1.0 + jnp.exp(-x))


def t5_bucket(dist):
    max_exact = N_BUCKETS // 2
    n = jnp.maximum(dist, 0)
    nf = jnp.maximum(n, 1).astype(jnp.float32)
    large = max_exact + (jnp.log(nf / max_exact) / math.log(MAX_DISTANCE / max_exact)
                         * (N_BUCKETS - max_exact)).astype(jnp.int32)
    return jnp.where(n < max_exact, n, jnp.minimum(large, N_BUCKETS - 1))


def causal_conv(x, ctx, w):
    L = x.shape[1]
    xp = jnp.concatenate([ctx, x], axis=1)
    y = xp[:, 0:L] * w[0]
    for j in range(1, CONV_W):
        y = y + xp[:, j:j + L] * w[j]
    return jax.nn.silu(y), xp[:, L:]


def gated_delta(q, k, v, g, beta, s0):
    bsz, L, nh, dk = q.shape
    c = GDN_CHUNK if L % GDN_CHUNK == 0 else L
    n = L // c
    def chunks(a):
        a = a.reshape((bsz, n, c) + a.shape[2:])
        return jnp.moveaxis(jnp.moveaxis(a, 3, 2), 1, 0)
    q = chunks(q) * dk ** -0.5
    k, v, g, beta = chunks(k), chunks(v), chunks(g), chunks(beta)
    gc = jnp.cumsum(g, axis=-1)
    tril = jnp.tril(jnp.ones((c, c), dtype=bool))
    stril = jnp.tril(jnp.ones((c, c), dtype=bool), -1)
    gam = jnp.exp(jnp.where(tril, gc[..., :, None] - gc[..., None, :], -jnp.inf))
    a_mat = jnp.where(stril, beta[..., :, None] * jnp.einsum('nbhtd,nbhjd->nbhtj', k, k) * gam, 0.0)
    lhs = a_mat + jnp.eye(c, dtype=a_mat.dtype)
    solve = functools.partial(lax.linalg.triangular_solve, left_side=True, lower=True, unit_diagonal=True)
    uv = solve(lhs, beta[..., None] * v)
    wk = solve(lhs, (beta * jnp.exp(gc))[..., None] * k)
    qk = jnp.where(tril, jnp.einsum('nbhtd,nbhjd->nbhtj', q, k) * gam, 0.0)
    q_head = jnp.exp(gc)[..., None] * q
    g_last = gc[..., -1]
    k_tail = jnp.exp(g_last[..., None] - gc)[..., None] * k
    def step(s, xs):
        uv_c, wk_c, qk_c, qh_c, kt_c, gl_c = xs
        u = uv_c - jnp.einsum('bhck,bhvk->bhcv', wk_c, s)
        o = jnp.einsum('bhck,bhvk->bhcv', qh_c, s) + jnp.einsum('bhtj,bhjv->bhtv', qk_c, u)
        s = jnp.exp(gl_c)[..., None, None] * s + jnp.einsum('bhcv,bhck->bhvk', u, kt_c)
        return s, o
    s_fin, o = lax.scan(step, s0, (uv, wk, qk, q_head, k_tail, g_last))
    o = jnp.moveaxis(jnp.moveaxis(o, 0, 1), 2, 3).reshape(bsz, L, nh, v.shape[-1])
    return o, s_fin


def gdn_mix(z, conv_ctx, s0, conv_w, a_log, dt_bias, o_norm):
    f32 = jnp.float32
    bsz, L = z.shape[:2]
    qkv, a, b, gate = split_cols(z, GDN_SPLITS)
    qkv, conv_state = causal_conv(qkv, conv_ctx, conv_w)
    q, k, v = jnp.split(qkv.astype(f32), 3, axis=-1)
    hs = (bsz, L, GDN_HEADS, HEAD_DIM)
    q, k, v = l2norm(q.reshape(hs)), l2norm(k.reshape(hs)), v.reshape(hs)
    g = -jnp.exp(a_log.astype(f32)) * jax.nn.softplus((a + dt_bias).astype(f32))
    beta = jax.nn.sigmoid(b.astype(f32))
    o, s_fin = gated_delta(q, k, v, g, beta, s0.astype(f32))
    o = rmsnorm(o.astype(z.dtype), o_norm) * jax.nn.silu(gate.reshape(hs))
    return o.reshape(bsz, L, SEQ_WIDTH), conv_state, s_fin.astype(s0.dtype)


FFN_CHUNK = 256


def _ffn_body(final_norm, x_ref, g_ref, wg_ref, wu_ref, wd_ref, gf_ref, o_ref):
    f32, bf16 = jnp.float32, jnp.bfloat16
    x = x_ref[...]
    n = ((x * lax.rsqrt(jnp.mean(x * x, axis=-1, keepdims=True) + EPS)) * g_ref[...]).astype(bf16)
    acc = jnp.zeros(x.shape, f32)
    for c in range(wg_ref.shape[1] // FFN_CHUNK):
        sl = slice(c * FFN_CHUNK, (c + 1) * FFN_CHUNK)
        a = jnp.dot(n, wg_ref[:, sl], preferred_element_type=f32)
        b = jnp.dot(n, wu_ref[:, sl], preferred_element_type=f32)
        h = (a * _sigmoid(a)) * b
        acc = acc + jnp.dot(h.astype(bf16), wd_ref[sl, :], preferred_element_type=f32)
    y = x + 0.5 * acc
    if final_norm:
        y = (y * lax.rsqrt(jnp.mean(y * y, axis=-1, keepdims=True) + EPS)) * gf_ref[...]
    o_ref[...] = y


def ffn_residual(x, g, wg, wu, wd, final_g=None, *, tm=512):
    t, d = x.shape
    f = wg.shape[1]
    tm = min(tm, t)
    gf = jnp.ones((d,), jnp.float32) if final_g is None else final_g
    resident = lambda shape: pl.BlockSpec(shape, lambda i: (0, 0), pipeline_mode=pl.Buffered(1))
    return pl.pallas_call(
        functools.partial(_ffn_body, final_g is not None),
        grid=(t // tm,),
        in_specs=[pl.BlockSpec((tm, d), lambda i: (i, 0)), resident((1, d)),
                  resident((d, f)), resident((d, f)), resident((f, d)), resident((1, d))],
        out_specs=pl.BlockSpec((tm, d), lambda i: (i, 0)),
        out_shape=jax.ShapeDtypeStruct((t, d), jnp.float32),
        compiler_params=pltpu.CompilerParams(dimension_semantics=("parallel",), vmem_limit_bytes=VMEM_LIMIT_BYTES),
        name="ffn_residual",
    )(x, g.reshape(1, d).astype(jnp.float32), wg, wu, wd, gf.reshape(1, d).astype(jnp.float32))


def _proj_in_body(segments, normalize, x_ref, g_ref, w_ref, *o_refs):
    x = x_ref[...]
    if normalize:
        r = lax.rsqrt(jnp.mean(x * x, axis=-1, keepdims=True) + EPS)
        x = (x * r) * g_ref[...]
    n = x.astype(jnp.bfloat16)
    for (start, width), o_ref in zip(segments, o_refs):
        o_ref[...] = jnp.dot(n, w_ref[:, start:start + width], preferred_element_type=jnp.float32).astype(o_ref.dtype)


def proj_in(x, g, w, segments, dtypes, *, normalize=True, tm=256):
    t, d = x.shape
    tm = min(tm, t)
    return pl.pallas_call(
        functools.partial(_proj_in_body, tuple(segments), normalize),
        grid=(t // tm,),
        in_specs=[pl.BlockSpec((tm, d), lambda i: (i, 0)), pl.BlockSpec((1, d), lambda i: (0, 0)),
                  pl.BlockSpec(w.shape, lambda i: (0, 0))],
        out_specs=[pl.BlockSpec((tm, width), lambda i: (i, 0)) for _, width in segments],
        out_shape=[jax.ShapeDtypeStruct((t, width), dt) for (_, width), dt in zip(segments, dtypes)],
        compiler_params=pltpu.CompilerParams(dimension_semantics=("parallel",), vmem_limit_bytes=VMEM_LIMIT_BYTES),
        name="proj_in",
    )(x, g.reshape(1, d).astype(jnp.float32), w)


def _mix_out_body(x_ref, cq_ref, om_ref, mkt_ref, mv_ref, w_ref, o_ref):
    f32, bf16 = jnp.float32, jnp.bfloat16
    cq = cq_ref[...]
    heads = []
    for h in range(MEM_HEADS):
        sl = slice(h * HEAD_DIM, (h + 1) * HEAD_DIM)
        logits = jnp.dot(cq[:, sl], mkt_ref[sl, :], preferred_element_type=f32) * HEAD_DIM ** -0.5
        p = jnp.exp(logits - jnp.max(logits, axis=-1, keepdims=True))
        p = p / jnp.sum(p, axis=-1, keepdims=True)
        heads.append(jnp.dot(p.astype(bf16), mv_ref[:, sl], preferred_element_type=f32))
    o_mem = jnp.concatenate(heads, axis=-1).astype(bf16)
    y = (jnp.dot(o_mem, w_ref[:MEM_WIDTH, :], preferred_element_type=f32)
         + jnp.dot(om_ref[...].astype(bf16), w_ref[MEM_WIDTH:, :], preferred_element_type=f32))
    o_ref[...] = x_ref[...] + y


def mix_out(x, cq, o_mix, mk, mv, w_out, *, tm=512):
    bsz, L, d = x.shape
    tm = min(tm, L)
    bf16 = jnp.bfloat16
    mkt = mk.astype(bf16).reshape(bsz, N_MEM, MEM_WIDTH).swapaxes(1, 2)
    mvf = mv.astype(bf16).reshape(bsz, N_MEM, MEM_WIDTH)
    tok = lambda n: pl.BlockSpec((None, tm, n), lambda b, t: (b, t, 0))
    per_b = lambda r, c: pl.BlockSpec((None, r, c), lambda b, t: (b, 0, 0))
    return pl.pallas_call(
        _mix_out_body,
        grid=(bsz, L // tm),
        in_specs=[tok(d), tok(MEM_WIDTH), tok(SEQ_WIDTH), per_b(MEM_WIDTH, N_MEM), per_b(N_MEM, MEM_WIDTH),
                  pl.BlockSpec((d, d), lambda b, t: (0, 0))],
        out_specs=tok(d),
        out_shape=jax.ShapeDtypeStruct((bsz, L, d), jnp.float32),
        compiler_params=pltpu.CompilerParams(
            dimension_semantics=("parallel", "parallel"), vmem_limit_bytes=VMEM_LIMIT_BYTES),
        name="mix_out",
    )(x, cq, o_mix, mkt, mvf, w_out)


S5_LANES = S5_GROUPS * S5_STATE
S5_GROUPS_PER_BLOCK = LANES // S5_GROUP
S5_BLOCKS = SEQ_WIDTH // LANES
S5_BLOCK_STATES = S5_GROUPS_PER_BLOCK * S5_STATE


def s5_discretize(lam_re, lam_im, log_step, b_re, b_im, c_re, c_im):
    f32, bf16 = jnp.float32, jnp.bfloat16
    lr, li = lam_re.astype(f32), lam_im.astype(f32)
    step = jnp.exp(log_step.astype(f32))[:, None]
    mag = jnp.exp(lr * step)
    ab_re, ab_im = mag * jnp.cos(li * step), mag * jnp.sin(li * step)
    den = lr * lr + li * li
    nr, ni = ab_re - 1.0, ab_im
    f_re = (nr * lr + ni * li) / den
    f_im = (ni * lr - nr * li) / den
    br, bi = b_re.astype(f32), b_im.astype(f32)
    bb_re = f_re[..., None] * br - f_im[..., None] * bi
    bb_im = f_re[..., None] * bi + f_im[..., None] * br
    eye = jnp.eye(S5_GROUPS_PER_BLOCK, dtype=f32)
    nb, gb = S5_BLOCKS, S5_GROUPS_PER_BLOCK

    def in_blocks(bb):
        w = jnp.einsum('jgpc,gh->jgchp', bb.reshape(nb, gb, S5_STATE, S5_GROUP), eye)
        return w.reshape(nb, LANES, S5_BLOCK_STATES).astype(bf16)

    def out_blocks(c):
        w = jnp.einsum('jgop,gh->jgpho', c.astype(f32).reshape(nb, gb, S5_GROUP, S5_STATE), eye)
        return w.reshape(nb, S5_BLOCK_STATES, LANES).astype(bf16)

    return (ab_re.reshape(1, S5_LANES), ab_im.reshape(1, S5_LANES),
            in_blocks(bb_re), in_blocks(bb_im), out_blocks(c_re), out_blocks(-c_im))


def _s5_project_in(u, wbr_ref, wbi_ref, bur_ref, bui_ref):
    ub = u.astype(jnp.bfloat16)
    for j in range(S5_BLOCKS):
        uj = ub[:, j * LANES:(j + 1) * LANES]
        sl = slice(j * S5_BLOCK_STATES, (j + 1) * S5_BLOCK_STATES)
        bur_ref[:, sl] = jnp.dot(uj, wbr_ref[j], preferred_element_type=jnp.float32)
        bui_ref[:, sl] = jnp.dot(uj, wbi_ref[j], preferred_element_type=jnp.float32)


def _s5_project_out(hr_ref, hi_ref, wcr_ref, wci_ref):
    cols = []
    for j in range(S5_BLOCKS):
        sl = slice(j * S5_BLOCK_STATES, (j + 1) * S5_BLOCK_STATES)
        cols.append(jnp.dot(hr_ref[:, sl].astype(jnp.bfloat16), wcr_ref[j], preferred_element_type=jnp.float32)
                    + jnp.dot(hi_ref[:, sl].astype(jnp.bfloat16), wci_ref[j], preferred_element_type=jnp.float32))
    return jnp.concatenate(cols, axis=-1)


def _s5_gate(y_ssm, u, d_ref, wglu_ref, bglu_ref):
    y = y_ssm + d_ref[...] * u
    y = 0.5 * y * (1.0 + jnp.tanh(math.sqrt(2.0 / math.pi) * (y + 0.044715 * (y * y * y))))
    z = jnp.dot(y.astype(jnp.bfloat16), wglu_ref[...], preferred_element_type=jnp.float32) + bglu_ref[...]
    return y * (1.0 / (1.0 + jnp.exp(-z)))


def _s5_seq_body(u_ref, h0r_ref, h0i_ref, ar_ref, ai_ref, wbr_ref, wbi_ref, wcr_ref, wci_ref, d_ref, wglu_ref, bglu_ref,
                 y_ref, hfr_ref, hfi_ref, bur_ref, bui_ref, cr_ref, ci_ref):
    tt = u_ref.shape[0]

    @pl.when(pl.program_id(1) == 0)
    def _():
        cr_ref[...] = h0r_ref[...]
        ci_ref[...] = h0i_ref[...]

    u = u_ref[...]
    _s5_project_in(u, wbr_ref, wbi_ref, bur_ref, bui_ref)
    ar, ai = ar_ref[...], ai_ref[...]

    def step(t, carry):
        hr, hi = carry
        nhr = ar * hr - ai * hi + bur_ref[pl.ds(t, 1), :]
        nhi = ar * hi + ai * hr + bui_ref[pl.ds(t, 1), :]
        bur_ref[pl.ds(t, 1), :] = nhr
        bui_ref[pl.ds(t, 1), :] = nhi
        return nhr, nhi

    hr, hi = lax.fori_loop(0, tt, step, (cr_ref[...], ci_ref[...]))
    cr_ref[...] = hr
    ci_ref[...] = hi
    hfr_ref[...] = hr
    hfi_ref[...] = hi
    y_ref[...] = _s5_gate(_s5_project_out(bur_ref, bui_ref, wcr_ref, wci_ref), u, d_ref, wglu_ref, bglu_ref)


def _s5_step_body(u_ref, h0r_ref, h0i_ref, ar_ref, ai_ref, wbr_ref, wbi_ref, wcr_ref, wci_ref, d_ref, wglu_ref, bglu_ref,
                  y_ref, hfr_ref, hfi_ref, bur_ref, bui_ref):
    u = u_ref[...]
    _s5_project_in(u, wbr_ref, wbi_ref, bur_ref, bui_ref)
    ar, ai = ar_ref[...], ai_ref[...]
    hr, hi = h0r_ref[...], h0i_ref[...]
    nhr = ar * hr - ai * hi + bur_ref[...]
    nhi = ar * hi + ai * hr + bui_ref[...]
    bur_ref[...] = nhr
    bui_ref[...] = nhi
    hfr_ref[...] = nhr
    hfi_ref[...] = nhi
    y_ref[...] = _s5_gate(_s5_project_out(bur_ref, bui_ref, wcr_ref, wci_ref), u, d_ref, wglu_ref, bglu_ref)


def s5_mix_pallas(u, h0_re, h0_im, disc, d_skip, w_glu, b_glu, *, tt=512):
    bsz, L, w = u.shape
    f32 = jnp.float32
    ar, ai, wbr, wbi, wcr, wci = disc
    d2, bg2, wg = d_skip.reshape(1, w).astype(f32), b_glu.reshape(1, w).astype(f32), w_glu.astype(jnp.bfloat16)
    const2 = lambda *_: (0, 0)
    const3 = lambda *_: (0, 0, 0)
    w_specs = [pl.BlockSpec((1, S5_LANES), const2), pl.BlockSpec((1, S5_LANES), const2),
               pl.BlockSpec(wbr.shape, const3), pl.BlockSpec(wbi.shape, const3),
               pl.BlockSpec(wcr.shape, const3), pl.BlockSpec(wci.shape, const3),
               pl.BlockSpec((1, w), const2), pl.BlockSpec((w, w), const2), pl.BlockSpec((1, w), const2)]
    w_args = (ar, ai, wbr, wbi, wcr, wci, d2, wg, bg2)
    if L == 1:
        rows = bsz
        h0r, h0i = h0_re.reshape(rows, S5_LANES).astype(f32), h0_im.reshape(rows, S5_LANES).astype(f32)
        row_spec = lambda n: pl.BlockSpec((rows, n), const2)
        y, hr, hi = pl.pallas_call(
            _s5_step_body,
            grid=(1,),
            in_specs=[row_spec(w), row_spec(S5_LANES), row_spec(S5_LANES)] + w_specs,
            out_specs=[row_spec(w), row_spec(S5_LANES), row_spec(S5_LANES)],
            out_shape=[jax.ShapeDtypeStruct((rows, w), f32), jax.ShapeDtypeStruct((rows, S5_LANES), f32),
                       jax.ShapeDtypeStruct((rows, S5_LANES), f32)],
            scratch_shapes=[pltpu.VMEM((rows, S5_LANES), f32), pltpu.VMEM((rows, S5_LANES), f32)],
            compiler_params=pltpu.CompilerParams(vmem_limit_bytes=VMEM_LIMIT_BYTES),
            name="s5_step",
        )(u.reshape(rows, w), h0r, h0i, *w_args)
        y = y.reshape(bsz, 1, w)
    else:
        tt = min(tt, L)
        h0r, h0i = h0_re.reshape(bsz, 1, S5_LANES).astype(f32), h0_im.reshape(bsz, 1, S5_LANES).astype(f32)
        st_spec = pl.BlockSpec((None, 1, S5_LANES), lambda b, t: (b, 0, 0))
        y, hr, hi = pl.pallas_call(
            _s5_seq_body,
            grid=(bsz, L // tt),
            in_specs=[pl.BlockSpec((None, tt, w), lambda b, t: (b, t, 0)), st_spec, st_spec] + w_specs,
            out_specs=[pl.BlockSpec((None, tt, w), lambda b, t: (b, t, 0)), st_spec, st_spec],
            out_shape=[jax.ShapeDtypeStruct((bsz, L, w), f32), jax.ShapeDtypeStruct((bsz, 1, S5_LANES), f32),
                       jax.ShapeDtypeStruct((bsz, 1, S5_LANES), f32)],
            scratch_shapes=[pltpu.VMEM((tt, S5_LANES), f32), pltpu.VMEM((tt, S5_LANES), f32),
                            pltpu.VMEM((1, S5_LANES), f32), pltpu.VMEM((1, S5_LANES), f32)],
            compiler_params=pltpu.CompilerParams(
                dimension_semantics=("parallel", "arbitrary"), vmem_limit_bytes=VMEM_LIMIT_BYTES),
            name="s5_seq",
        )(u, h0r, h0i, *w_args)
    shp = (bsz, S5_GROUPS, S5_STATE)
    return y, hr.reshape(shp).astype(h0_re.dtype), hi.reshape(shp).astype(h0_im.dtype)


DSA_KEY_CHUNK = 512
DSA_NEAR = 2 * QBLOCK
INT32_MIN = -2 ** 31
NEG_BIG = -1e30
KT_ROWS = 2 * HEAD_DIM


def _sortable_key(s):
    bits = lax.bitcast_convert_type(s, jnp.int32)
    return jnp.where(bits < 0, bits ^ jnp.int32(0x7FFFFFFF), bits)


def _dsa_prompt_body(n_top, rb_ref, q_ref, qi_ref, kiwi_ref, kt_ref, v_ref, kit_ref, o_ref, keys_ref, bias_ref, bound_ref):
    i = pl.program_id(1)
    f32, bf16 = jnp.float32, jnp.bfloat16
    kc = DSA_KEY_CHUNK
    q_start = i * QBLOCK
    n_all = (q_start + QBLOCK + kc - 1) // kc
    near_start = jnp.maximum(q_start - QBLOCK, 0)
    n_far = (near_start + kc - 1) // kc
    row = lax.broadcasted_iota(jnp.int32, (QBLOCK, 1), 0)
    qpos = q_start + row

    @pl.when(i == 0)
    def _():
        r = lax.broadcasted_iota(jnp.int32, (QBLOCK, DSA_NEAR), 0)
        c = lax.broadcasted_iota(jnp.int32, (QBLOCK, DSA_NEAR), 1)
        for tile in range(2):
            bucket = t5_bucket(r + tile * QBLOCK - c)
            for h in range(DSA_HEADS):
                b = jnp.zeros((QBLOCK, DSA_NEAR), f32)
                for bk in range(N_BUCKETS):
                    b = jnp.where(bucket == bk, rb_ref[bk, h] - rb_ref[N_BUCKETS - 1, h], b)
                kv, g = divmod(h, DSA_GQA)
                bias_ref[tile, kv, g * QBLOCK:(g + 1) * QBLOCK, :] = b
        lane = lax.broadcasted_iota(jnp.int32, (1, LANES), 1)
        bound = jnp.zeros((1, LANES), f32)
        for kv in range(DSA_KV_HEADS):
            kk = kt_ref[kv * KT_ROWS:kv * KT_ROWS + HEAD_DIM, :].astype(f32)
            kmax = jnp.sqrt(jnp.max(jnp.sum(kk * kk, axis=0, keepdims=True), axis=-1, keepdims=True))
            bound = jnp.where(lane == kv, kmax, bound)
        for h in range(DSA_HEADS):
            kv, g = divmod(h, DSA_GQA)
            bmax = jnp.maximum(jnp.max(bias_ref[:, kv, g * QBLOCK:(g + 1) * QBLOCK, :]), 0.0)
            bound = jnp.where(lane == DSA_KV_HEADS + h, bmax, bound)
        bound_ref[...] = bound

    qi_all = (qi_ref[...].astype(f32) * IDX_DIM ** -0.5).astype(bf16)
    qi = jnp.concatenate([qi_all[:, h * IDX_DIM:(h + 1) * IDX_DIM] for h in range(IDX_HEADS)], axis=0)
    wi = kiwi_ref[:, IDX_DIM:IDX_DIM + IDX_HEADS] * IDX_HEADS ** -0.5

    def score_chunk(c, _):
        off = pl.multiple_of(c * kc, kc)
        d = jnp.dot(qi, kit_ref[:, pl.ds(off, kc)], preferred_element_type=f32)
        s = jnp.zeros((QBLOCK, kc), f32)
        for h in range(IDX_HEADS):
            s = s + wi[:, h:h + 1] * jnp.maximum(d[h * QBLOCK:(h + 1) * QBLOCK], 0.0)
        kpos = off + lax.broadcasted_iota(jnp.int32, (QBLOCK, kc), 1)
        s = jnp.where(s == 0.0, 0.0, s)
        s = jnp.where(kpos <= qpos, s, -jnp.inf)
        keys_ref[:, pl.ds(off, kc)] = _sortable_key(s)
        return 0

    lax.fori_loop(0, n_all, score_chunk, 0)

    def count_where(pred_fn):
        def body(c, acc):
            off = pl.multiple_of(c * kc, kc)
            hit = pred_fn(keys_ref[:, pl.ds(off, kc)], off)
            part = jnp.where(hit, 1.0, 0.0)
            for j in range(kc // 128):
                acc = acc + part[:, j * 128:(j + 1) * 128]
            return acc
        acc = lax.fori_loop(0, n_all, body, jnp.zeros((QBLOCK, 128), f32))
        return jnp.sum(acc, axis=-1, keepdims=True)

    def thr_bit(it, t):
        cand = t + lax.shift_left(jnp.int32(1), 31 - it)
        cnt = count_where(lambda k, off: k >= cand)
        return jnp.where(cnt >= n_top, cand, t)

    thr = lax.fori_loop(0, 32, thr_bit, jnp.full((QBLOCK, 1), INT32_MIN, jnp.int32))

    def is_valid(off, width):
        return (off + lax.broadcasted_iota(jnp.int32, (QBLOCK, width), 1)) <= qpos

    n_gt = count_where(lambda k, off: (k > thr) & is_valid(off, kc))
    n_eq = count_where(lambda k, off: (k == thr) & is_valid(off, kc))
    need = n_top - n_gt
    has_extra_ties = jnp.max(jnp.where(n_eq > need, 1.0, 0.0)) > 0.0

    def tie_search():
        def idx_bit(it, j):
            cand = j + lax.shift_left(jnp.int32(1), 13 - it)
            cnt = count_where(lambda k, off: (k == thr) & is_valid(off, kc)
                              & ((off + lax.broadcasted_iota(jnp.int32, (QBLOCK, kc), 1)) < cand))
            return jnp.where(cnt <= need, cand, j)
        return lax.fori_loop(0, 14, idx_bit, jnp.zeros((QBLOCK, 1), jnp.int32))

    tie_end = lax.cond(has_extra_ties, tie_search, lambda: jnp.full((QBLOCK, 1), 2 ** 14, jnp.int32))

    def selected(keys, off, width):
        kpos = off + lax.broadcasted_iota(jnp.int32, (QBLOCK, width), 1)
        return (kpos <= qpos) & ((keys > thr) | ((keys == thr) & (kpos < tie_end))), kpos

    tile = jnp.minimum(i, 1)
    q_all = (q_ref[...].astype(f32) * HEAD_DIM ** -0.5).astype(bf16)
    qh = [q_all[:, h * HEAD_DIM:(h + 1) * HEAD_DIM] for h in range(DSA_HEADS)]
    off_near = pl.multiple_of(near_start, QBLOCK)
    sel_near, _ = selected(keys_ref[:, pl.ds(off_near, DSA_NEAR)], off_near, DSA_NEAR)

    def near_bias(h):
        kv, g = divmod(h, DSA_GQA)
        return jnp.where(sel_near, bias_ref[tile, kv, g * QBLOCK:(g + 1) * QBLOCK, :], NEG_BIG)

    def far_mask(c):
        off = pl.multiple_of(c * kc, kc)
        sel, kpos = selected(keys_ref[:, pl.ds(off, kc)], off, kc)
        return off, jnp.where(sel & (kpos < near_start), 0.0, NEG_BIG)

    def kt_block(kv, off, width, rows):
        return kt_ref[kv * KT_ROWS:kv * KT_ROWS + rows, pl.ds(off, width)]

    bound = bound_ref[...]
    lane = lax.broadcasted_iota(jnp.int32, (1, LANES), 1)
    pick = lambda idx: jnp.sum(jnp.where(lane == idx, bound, 0.0), axis=-1, keepdims=True)
    col = lax.broadcasted_iota(jnp.int32, (QBLOCK, HEAD_DIM), 1)
    q_aug = []
    for h in range(DSA_HEADS):
        qf = qh[h].astype(f32)
        ub = jnp.sqrt(jnp.sum(qf * qf, axis=-1, keepdims=True)) * pick(h // DSA_GQA) * 1.01 + pick(DSA_KV_HEADS + h)
        q_aug.append(jnp.concatenate([qh[h], jnp.where(col == 0, -ub, 0.0).astype(bf16)], axis=-1))

    def fast_attend(accs, off, width, mask_bias):
        accs = list(accs)
        for kv in range(DSA_KV_HEADS):
            kt = kt_block(kv, off, width, KT_ROWS)
            ps = [jnp.exp(jnp.dot(q_aug[kv * DSA_GQA + g], kt, preferred_element_type=f32)
                          + mask_bias(kv * DSA_GQA + g)).astype(bf16) for g in range(DSA_GQA)]
            accs[kv] = accs[kv] + jnp.dot(jnp.concatenate(ps, axis=0), v_ref[kv, pl.ds(off, width), :],
                                          preferred_element_type=f32)
        return tuple(accs)

    def fast_chunk(c, accs):
        off, mb = far_mask(c)
        return fast_attend(accs, off, kc, lambda h: mb)

    zero_acc = tuple(jnp.zeros((DSA_GQA * QBLOCK, 2 * HEAD_DIM), f32) for _ in range(DSA_KV_HEADS))
    accs = fast_attend(lax.fori_loop(0, n_far, fast_chunk, zero_acc), off_near, DSA_NEAR, near_bias)
    l_min = accs[0][:, HEAD_DIM:]
    for kv in range(1, DSA_KV_HEADS):
        l_min = jnp.minimum(l_min, accs[kv][:, HEAD_DIM:])
    fast_ok = jnp.min(l_min) > 1e-30

    def safe_attend(carry, off, width, mask_bias):
        ms, accs = list(carry[0]), list(carry[1])
        for kv in range(DSA_KV_HEADS):
            kt = kt_block(kv, off, width, HEAD_DIM)
            ps, alphas = [], []
            for g in range(DSA_GQA):
                h = kv * DSA_GQA + g
                logits = jnp.dot(qh[h], kt, preferred_element_type=f32) + mask_bias(h)
                m_new = jnp.maximum(ms[h], jnp.max(logits, axis=-1, keepdims=True))
                alphas.append(jnp.exp(ms[h] - m_new))
                ps.append(jnp.exp(logits - m_new).astype(bf16))
                ms[h] = m_new
            accs[kv] = (jnp.concatenate(alphas, axis=0) * accs[kv]
                        + jnp.dot(jnp.concatenate(ps, axis=0), v_ref[kv, pl.ds(off, width), :], preferred_element_type=f32))
        return tuple(ms), tuple(accs)

    def safe_path():
        def safe_chunk(c, carry):
            off, mb = far_mask(c)
            return safe_attend(carry, off, kc, lambda h: mb)
        init = (tuple(jnp.full((QBLOCK, 1), NEG_BIG, f32) for _ in range(DSA_HEADS)), zero_acc)
        return safe_attend(lax.fori_loop(0, n_far, safe_chunk, init), off_near, DSA_NEAR, near_bias)[1]

    accs = lax.cond(fast_ok, lambda: accs, safe_path)
    outs = []
    for kv in range(DSA_KV_HEADS):
        o = accs[kv][:, :HEAD_DIM] / accs[kv][:, HEAD_DIM:]
        outs += [o[g * QBLOCK:(g + 1) * QBLOCK] for g in range(DSA_GQA)]
    o_ref[...] = jnp.concatenate(outs, axis=-1)


def dsa_prompt_pallas(q, qi, kiwi, k, v, rel_bias):
    bsz, L = q.shape[:2]
    nq = L // QBLOCK
    n_top = min(TOPK_MAX, L // 4)
    f32, bf16 = jnp.float32, jnp.bfloat16
    k_t = k.astype(bf16).reshape(bsz, L, DSA_KV_HEADS, HEAD_DIM).transpose(0, 2, 3, 1)
    k_pad = jnp.zeros((bsz, DSA_KV_HEADS, KT_ROWS - HEAD_DIM, L), bf16).at[:, :, 0, :].set(1.0)
    k_t = jnp.concatenate([k_t, k_pad], axis=2).reshape(bsz, DSA_KV_HEADS * KT_ROWS, L)
    v_h = v.astype(bf16).reshape(bsz, L, DSA_KV_HEADS, HEAD_DIM).swapaxes(1, 2)
    v_h = jnp.concatenate([v_h, jnp.ones_like(v_h)], axis=-1)
    ki_t = kiwi[..., :IDX_DIM].astype(bf16).swapaxes(1, 2)
    lk = max(L, DSA_KEY_CHUNK)
    if lk != L:
        k_t = jnp.pad(k_t, ((0, 0), (0, 0), (0, lk - L)))
        v_h = jnp.pad(v_h, ((0, 0), (0, 0), (0, lk - L), (0, 0)))
        ki_t = jnp.pad(ki_t, ((0, 0), (0, 0), (0, lk - L)))
    tok = lambda n: pl.BlockSpec((None, QBLOCK, n), lambda b, i: (b, i, 0))
    return pl.pallas_call(
        functools.partial(_dsa_prompt_body, n_top),
        grid=(bsz, nq),
        in_specs=[
            pl.BlockSpec(memory_space=pltpu.SMEM),
            tok(SEQ_WIDTH), tok(IDX_HEADS * IDX_DIM), tok(LANES),
            pl.BlockSpec((None, DSA_KV_HEADS * KT_ROWS, lk), lambda b, i: (b, 0, 0)),
            pl.BlockSpec((None, DSA_KV_HEADS, lk, 2 * HEAD_DIM), lambda b, i: (b, 0, 0, 0)),
            pl.BlockSpec((None, IDX_DIM, lk), lambda b, i: (b, 0, 0)),
        ],
        out_specs=tok(SEQ_WIDTH),
        out_shape=jax.ShapeDtypeStruct((bsz, L, SEQ_WIDTH), f32),
        scratch_shapes=[pltpu.VMEM((QBLOCK, lk), jnp.int32),
                        pltpu.VMEM((2, DSA_KV_HEADS, DSA_GQA * QBLOCK, DSA_NEAR), f32),
                        pltpu.VMEM((1, LANES), f32)],
        compiler_params=pltpu.CompilerParams(
            dimension_semantics=("parallel", "arbitrary"), vmem_limit_bytes=VMEM_LIMIT_BYTES),
        name="dsa_prompt",
    )(rel_bias.astype(f32), q, qi, kiwi, k_t, v_h, ki_t)


DEC_PAGES_PER_STEP = 16
KV_WIDTH = DSA_KV_HEADS * HEAD_DIM


def _index_score(qi, wi, ki):
    d = _dot_nt(qi, ki)
    s = jnp.sum(wi * jnp.maximum(d, 0.0), axis=0, keepdims=True)
    return jnp.where(s == 0.0, 0.0, s)


def _dsa_dec_score_body(pt_ref, qi_ref, wi_ref, *refs):
    pages, s_ref = refs[:-1], refs[-1]
    qi = (qi_ref[...].astype(jnp.float32) * IDX_DIM ** -0.5).astype(jnp.bfloat16)
    wi = wi_ref[...]
    s_ref[...] = jnp.concatenate([_index_score(qi, wi, p[...].astype(jnp.bfloat16)) for p in pages], axis=0)


def _dsa_dec_attend_body(n_top, n_pages, pt_ref, rb_ref, q_ref, qi_ref, wi_ref, knew_ref, vnew_ref, kinew_ref, s_ref,
                         *refs):
    pp = DEC_PAGES_PER_STEP if n_pages >= DEC_PAGES_PER_STEP else n_pages
    k_pages, v_pages = refs[:pp], refs[pp:2 * pp]
    o_ref, thr_ref, tie_ref, snew_ref, m_ref, l_ref, acc_ref = refs[2 * pp:]
    f32, bf16 = jnp.float32, jnp.bfloat16
    t = pl.program_id(1)
    nt = pl.num_programs(1)
    past = n_pages * PAGE_SIZE
    page_i = lax.broadcasted_iota(jnp.int32, (n_pages, LANES), 0)
    lane_i = lax.broadcasted_iota(jnp.int32, (n_pages, LANES), 1)
    kpos_all = page_i * PAGE_SIZE + lane_i
    total = lambda x: jnp.sum(jnp.sum(x, axis=1, keepdims=True), axis=0, keepdims=True)

    @pl.when(t == 0)
    def _():
        qi = (qi_ref[...].astype(f32) * IDX_DIM ** -0.5).astype(bf16)
        d_new = jnp.sum(qi.astype(f32) * kinew_ref[...].astype(bf16).astype(f32), axis=-1, keepdims=True)
        s_new = jnp.sum(wi_ref[...] * jnp.maximum(d_new, 0.0), axis=0, keepdims=True)
        s_new = jnp.where(s_new == 0.0, 0.0, s_new)
        key_new = _sortable_key(s_new)
        keys = _sortable_key(s_ref[...])

        def count(pred_past, pred_new):
            return total(jnp.where(pred_past, 1.0, 0.0)) + jnp.where(pred_new, 1.0, 0.0)

        def thr_bit(it, thr):
            cand = thr + lax.shift_left(jnp.int32(1), 31 - it)
            return jnp.where(count(keys >= cand, key_new >= cand) >= n_top, cand, thr)

        thr = lax.fori_loop(0, 32, thr_bit, jnp.full((1, 1), INT32_MIN, jnp.int32))
        need = n_top - count(keys > thr, key_new > thr)

        def idx_bit(it, j):
            cand = j + lax.shift_left(jnp.int32(1), 14 - it)
            cnt = count((keys == thr) & (kpos_all < cand), (key_new == thr) & (past < cand))
            return jnp.where(cnt <= need, cand, j)

        n_eq = count(keys == thr, key_new == thr)
        tie_end = lax.cond(jnp.max(jnp.where(n_eq > need, 1.0, 0.0)) > 0.0,
                           lambda: lax.fori_loop(0, 15, idx_bit, jnp.zeros((1, 1), jnp.int32)),
                           lambda: jnp.full((1, 1), 2 ** 15, jnp.int32))
        thr_ref[...] = thr
        tie_ref[...] = tie_end
        snew_ref[...] = s_new
        m_ref[...] = jnp.full(m_ref.shape, NEG_BIG, f32)
        l_ref[...] = jnp.zeros(l_ref.shape, f32)
        acc_ref[...] = jnp.zeros(acc_ref.shape, f32)

    thr, tie_end = thr_ref[...], tie_ref[...]
    q = (q_ref[...].astype(f32) * HEAD_DIM ** -0.5).astype(bf16)
    lane12 = lax.broadcasted_iota(jnp.int32, (DSA_HEADS, LANES), 1)

    def head_bias(dist):
        bucket = t5_bucket(dist)
        b = jnp.zeros(dist.shape, f32)
        for h in range(DSA_HEADS):
            row = jnp.zeros(dist.shape, f32)
            for bk in range(N_BUCKETS):
                row = jnp.where(bucket == bk, rb_ref[bk, h] - rb_ref[N_BUCKETS - 1, h], row)
            b = jnp.where(lax.broadcasted_iota(jnp.int32, dist.shape, 0) == h, row, b)
        return b

    keys_step = _sortable_key(s_ref[pl.ds(pl.multiple_of(t * pp, pp), pp), :])
    tiles = []
    for j in range(pp):
        kpos = (t * pp + j) * PAGE_SIZE + lax.broadcasted_iota(jnp.int32, (1, LANES), 1)
        keys = keys_step[j:j + 1]
        sel = (keys > thr) | ((keys == thr) & (kpos < tie_end))
        logits = _dot_nt(q, k_pages[j][...].astype(bf16))
        if j == pp - 1:
            near = head_bias(past - (t * pp + j) * PAGE_SIZE - lane12)
            logits = logits + jnp.where(t == nt - 1, near, 0.0)
        tiles.append(jnp.where(sel, logits, NEG_BIG))
    m = m_ref[...]
    tile_max = tiles[0]
    for x in tiles[1:]:
        tile_max = jnp.maximum(tile_max, x)
    m_new = jnp.maximum(m, jnp.max(tile_max, axis=-1, keepdims=True))
    alpha = jnp.exp(m - m_new)
    ps = [jnp.exp(x - m_new) for x in tiles]
    p_sum = ps[0]
    for x in ps[1:]:
        p_sum = p_sum + x
    pv = jnp.dot(ps[0].astype(bf16), v_pages[0][...].astype(bf16), preferred_element_type=f32)
    for j in range(1, pp):
        pv = pv + jnp.dot(ps[j].astype(bf16), v_pages[j][...].astype(bf16), preferred_element_type=f32)
    l_ref[...] = alpha * l_ref[...] + jnp.sum(p_sum, axis=-1, keepdims=True)
    acc_ref[...] = alpha * acc_ref[...] + pv
    m_ref[...] = m_new

    @pl.when(t == nt - 1)
    def _():
        s_new = snew_ref[...]
        key_new = _sortable_key(s_new)
        sel_new = (key_new > thr) | ((key_new == thr) & (past < tie_end))
        k_new = knew_ref[...].astype(bf16).astype(f32)
        logit_new = jnp.sum(q.astype(f32) * k_new, axis=-1, keepdims=True)
        logit_new = logit_new + head_bias(jnp.zeros((DSA_HEADS, LANES), jnp.int32))[:, :1]
        logit_new = jnp.where(sel_new, logit_new, NEG_BIG)
        m = m_ref[...]
        m_new = jnp.maximum(m, logit_new)
        alpha = jnp.exp(m - m_new)
        p_new = jnp.exp(logit_new - m_new)
        l = alpha * l_ref[...] + p_new
        v_new = vnew_ref[...].astype(bf16).astype(f32)
        acc = alpha * acc_ref[...] + p_new.astype(bf16).astype(f32) * v_new
        o = acc / l
        o_ref[...] = jnp.concatenate(
            [o[h:h + 1, (h // DSA_GQA) * HEAD_DIM:(h // DSA_GQA + 1) * HEAD_DIM] for h in range(DSA_HEADS)], axis=-1)


def dsa_decode_pallas(q, qi, kiwi, k_new, v_new, pool_k, pool_v, pool_ki, page_table, rel_bias):
    bsz = q.shape[0]
    n_pages = page_table.shape[1]
    n_phys = pool_k.shape[0]
    pp = DEC_PAGES_PER_STEP if n_pages >= DEC_PAGES_PER_STEP else n_pages
    nt = n_pages // pp
    n_top = min(TOPK_MAX, (n_pages * PAGE_SIZE + 1) // 4)
    f32, bf16 = jnp.float32, jnp.bfloat16
    eye = jnp.eye(DSA_KV_HEADS, dtype=q.dtype)
    q_bd = jnp.einsum('bkgd,kj->bkgjd', q.reshape(bsz, DSA_KV_HEADS, DSA_GQA, HEAD_DIM), eye)
    q_bd = q_bd.reshape(bsz, DSA_HEADS, KV_WIDTH)
    qi3 = qi.reshape(bsz, IDX_HEADS, IDX_DIM)
    wi3 = (kiwi[:, IDX_DIM:IDX_DIM + IDX_HEADS] * IDX_HEADS ** -0.5).reshape(bsz, IDX_HEADS, 1)
    ki_new = kiwi[:, :IDX_DIM].reshape(bsz, 1, IDX_DIM)
    pk = pool_k.reshape(n_phys, PAGE_SIZE, KV_WIDTH)
    pv = pool_v.reshape(n_phys, PAGE_SIZE, KV_WIDTH)
    per_b = lambda *shape: pl.BlockSpec((None,) + shape, lambda b, t, pt: (b,) + (0,) * len(shape))
    page = lambda width, j: pl.BlockSpec((None, PAGE_SIZE, width), lambda b, t, pt: (pt[b, t * pp + j], 0, 0))
    scores = pl.pallas_call(
        _dsa_dec_score_body,
        grid_spec=pltpu.PrefetchScalarGridSpec(
            num_scalar_prefetch=1, grid=(bsz, nt),
            in_specs=[per_b(IDX_HEADS, IDX_DIM), per_b(IDX_HEADS, 1)] + [page(IDX_DIM, j) for j in range(pp)],
            out_specs=pl.BlockSpec((None, pp, LANES), lambda b, t, pt: (b, t, 0))),
        out_shape=jax.ShapeDtypeStruct((bsz, n_pages, LANES), f32),
        compiler_params=pltpu.CompilerParams(
            dimension_semantics=("parallel", "parallel"), vmem_limit_bytes=VMEM_LIMIT_BYTES),
        name="dsa_dec_score",
    )(page_table, qi3, wi3, *([pool_ki] * pp))
    one = lambda: pltpu.VMEM((1, 1), jnp.int32)
    o = pl.pallas_call(
        functools.partial(_dsa_dec_attend_body, n_top, n_pages),
        grid_spec=pltpu.PrefetchScalarGridSpec(
            num_scalar_prefetch=1, grid=(bsz, nt),
            in_specs=[pl.BlockSpec(memory_space=pltpu.SMEM), per_b(DSA_HEADS, KV_WIDTH), per_b(IDX_HEADS, IDX_DIM),
                      per_b(IDX_HEADS, 1), per_b(1, KV_WIDTH), per_b(1, KV_WIDTH), per_b(1, IDX_DIM),
                      per_b(n_pages, LANES)]
            + [page(KV_WIDTH, j) for j in range(pp)] * 2,
            out_specs=per_b(1, SEQ_WIDTH),
            scratch_shapes=[one(), one(), pltpu.VMEM((1, 1), f32), pltpu.VMEM((DSA_HEADS, 1), f32),
                            pltpu.VMEM((DSA_HEADS, 1), f32), pltpu.VMEM((DSA_HEADS, KV_WIDTH), f32)]),
        out_shape=jax.ShapeDtypeStruct((bsz, 1, SEQ_WIDTH), f32),
        compiler_params=pltpu.CompilerParams(
            dimension_semantics=("parallel", "arbitrary"), vmem_limit_bytes=VMEM_LIMIT_BYTES),
        name="dsa_dec_attend",
    )(page_table, rel_bias.astype(f32), q_bd, qi3, wi3, k_new.reshape(bsz, 1, KV_WIDTH),
      v_new.reshape(bsz, 1, KV_WIDTH), ki_new, scores, *([pk] * pp), *([pv] * pp))
    return o


CONV_HALO = 8


def _gdn_prep_body(x_ref, ab_ref, ctx_ref, w_ref, alog_ref, dtb_ref, q_ref, k_ref, v_ref, gb_ref, cs_ref, xp_ref):
    tt = x_ref.shape[0]
    halo = CONV_W - 1

    @pl.when(pl.program_id(1) == 0)
    def _():
        xp_ref[CONV_HALO - halo:CONV_HALO, :] = ctx_ref[...]

    x = x_ref[...]
    xp_ref[CONV_HALO:CONV_HALO + tt, :] = x
    w = w_ref[...]
    y = xp_ref[CONV_HALO - halo:CONV_HALO - halo + tt, :] * w[0:1]
    for j in range(1, CONV_W):
        y = y + xp_ref[CONV_HALO - halo + j:CONV_HALO - halo + j + tt, :] * w[j:j + 1]
    last = x[tt - halo:, :]
    xp_ref[CONV_HALO - halo:CONV_HALO, :] = last
    cs_ref[...] = last
    y = y * _sigmoid(y)

    def l2n(a, scale):
        cols = []
        for h in range(GDN_HEADS):
            s = a[:, h * HEAD_DIM:(h + 1) * HEAD_DIM]
            cols.append(s * (lax.rsqrt(jnp.sum(s * s, axis=-1, keepdims=True) + EPS) * scale))
        return jnp.concatenate(cols, axis=-1)

    q_ref[...] = l2n(y[:, :SEQ_WIDTH], HEAD_DIM ** -0.5)
    k_ref[...] = l2n(y[:, SEQ_WIDTH:2 * SEQ_WIDTH], 1.0)
    v_ref[...] = y[:, 2 * SEQ_WIDTH:]
    ab = ab_ref[...]
    xa = ab + dtb_ref[...]
    softplus = jnp.maximum(xa, 0.0) + jnp.log(1.0 + jnp.exp(-jnp.abs(xa)))
    g = -jnp.exp(alog_ref[...]) * softplus
    lane = lax.broadcasted_iota(jnp.int32, ab.shape, 1)
    gb_ref[...] = jnp.where(lane < GDN_HEADS, g, _sigmoid(ab))


def gdn_prep_pallas(qkv, ab, ctx, conv_w, a_log, dt_bias, *, tt=256):
    bsz, L, ch = qkv.shape
    tt = min(tt, L)
    f32 = jnp.float32
    pad = lambda r: jnp.pad(r.astype(f32).reshape(1, -1), ((0, 0), (0, LANES - r.shape[-1])))
    tok = lambda n: pl.BlockSpec((None, tt, n), lambda b, t: (b, t, 0))
    const2 = lambda b, t: (0, 0)
    return pl.pallas_call(
        _gdn_prep_body,
        grid=(bsz, L // tt),
        in_specs=[tok(ch), tok(LANES), pl.BlockSpec((None, CONV_W - 1, ch), lambda b, t: (b, 0, 0)),
                  pl.BlockSpec((CONV_W, ch), const2), pl.BlockSpec((1, LANES), const2), pl.BlockSpec((1, LANES), const2)],
        out_specs=[tok(SEQ_WIDTH), tok(SEQ_WIDTH), tok(SEQ_WIDTH), tok(LANES),
                   pl.BlockSpec((None, CONV_W - 1, ch), lambda b, t: (b, 0, 0))],
        out_shape=[jax.ShapeDtypeStruct((bsz, L, SEQ_WIDTH), f32)] * 3
        + [jax.ShapeDtypeStruct((bsz, L, LANES), f32), jax.ShapeDtypeStruct((bsz, CONV_W - 1, ch), f32)],
        scratch_shapes=[pltpu.VMEM((CONV_HALO + tt, ch), f32)],
        compiler_params=pltpu.CompilerParams(
            dimension_semantics=("parallel", "arbitrary"), vmem_limit_bytes=VMEM_LIMIT_BYTES),
        name="gdn_prep",
    )(qkv, ab, ctx.astype(f32), conv_w.astype(f32), pad(a_log), pad(dt_bias))


def _split3(a):
    bf16, f32 = jnp.bfloat16, jnp.float32
    h = a.astype(bf16)
    r = a - h.astype(f32)
    m = r.astype(bf16)
    return h, m, (r - m.astype(f32)).astype(bf16)


def _mm_hi(a, b):
    f32 = jnp.float32
    ah, am, _ = _split3(a)
    bh, bm, _ = _split3(b)
    d = lambda x, y: jnp.dot(x, y, preferred_element_type=f32)
    return d(ah, bh) + (d(ah, bm) + d(am, bh))


def _mm_sel(sel, b):
    f32 = jnp.float32
    s = sel.astype(jnp.bfloat16)
    bh, bm, bl = _split3(b)
    d = lambda y: jnp.dot(s, y, preferred_element_type=f32)
    return d(bh) + (d(bm) + d(bl))


def _dot_nt(a, b):
    return lax.dot_general(a, b, (((1,), (1,)), ((), ())), preferred_element_type=jnp.float32)


def _dot_tn(a, b):
    return lax.dot_general(a, b, (((0,), (0,)), ((), ())), preferred_element_type=jnp.float32)


def _gdn_local_body(q_ref, k_ref, v_ref, gb_ref, uv_ref, wk_ref, qh_ref, kt_ref, qk_ref):
    f32, bf16 = jnp.float32, jnp.bfloat16
    c = GDN_CHUNK
    r_i = lax.broadcasted_iota(jnp.int32, (c, c), 0)
    c_i = lax.broadcasted_iota(jnp.int32, (c, c), 1)
    tril = r_i >= c_i
    stril = r_i > c_i
    triu = r_i <= c_i
    eye = jnp.where(r_i == c_i, 1.0, 0.0)
    ones = jnp.ones((c, c), jnp.bool_)
    lane = lax.broadcasted_iota(jnp.int32, (c, LANES), 1)
    gb = gb_ref[...]
    heads = range(GDN_HEADS)
    sl = [slice(h * HEAD_DIM, (h + 1) * HEAD_DIM) for h in heads]
    pick = lambda idx: jnp.sum(jnp.where(lane == idx, gb, 0.0), axis=-1, keepdims=True)
    g_col = [pick(h) for h in heads]
    beta = [pick(GDN_HEADS + h) for h in heads]
    q = [q_ref[:, sl[h]] for h in heads]
    k = [k_ref[:, sl[h]] for h in heads]
    v = [v_ref[:, sl[h]] for h in heads]
    g_mat = [jnp.broadcast_to(g_col[h], (c, c)) for h in heads]
    cum_col = [_mm_sel(tril, g_mat[h]) for h in heads]
    cum_row = [_mm_sel(ones, jnp.where(triu, g_mat[h], 0.0)) for h in heads]
    gam = [jnp.where(tril, jnp.exp(jnp.where(tril, cum_col[h] - cum_row[h], 0.0)), 0.0) for h in heads]
    kb = [k[h].astype(bf16) for h in heads]
    a_mat = [jnp.where(stril, beta[h] * _dot_nt(kb[h], kb[h]) * gam[h], 0.0) for h in heads]
    t_inv = [eye - a_mat[h] for h in heads]
    pw = a_mat
    for _ in range(5):
        pw = [_mm_hi(pw[h], pw[h]) for h in heads]
        t_inv = [t_inv[h] + _mm_hi(t_inv[h], pw[h]) for h in heads]
    gc = [cum_col[h][:, :1] for h in heads]
    egc = [jnp.exp(gc[h]) for h in heads]
    tw = [_mm_hi(t_inv[h], jnp.concatenate([beta[h] * v[h], (beta[h] * egc[h]) * k[h]], axis=-1)) for h in heads]
    qk = [jnp.where(tril, _dot_nt(q[h].astype(bf16), kb[h]) * gam[h], 0.0) for h in heads]
    cat = lambda xs: jnp.concatenate(xs, axis=-1)
    uv_ref[...] = cat([tw[h][:, :HEAD_DIM] for h in heads])
    wk_ref[...] = cat([tw[h][:, HEAD_DIM:] for h in heads]).astype(bf16)
    qh_ref[...] = cat([egc[h] * q[h] for h in heads]).astype(bf16)
    eye_b = eye.astype(bf16)
    kt = [(jnp.exp(gc[h][c - 1:c, :] - gc[h]) * k[h]).astype(bf16) for h in heads]
    kt_ref[...] = cat([_dot_tn(kt[h], eye_b) for h in heads]).astype(bf16)
    qk_ref[...] = cat(qk).astype(bf16)


def gdn_local_pallas(qn, kn, vv, gb):
    bsz, L, w = qn.shape
    c = GDN_CHUNK
    f32, bf16 = jnp.float32, jnp.bfloat16
    tok = lambda n: pl.BlockSpec((None, c, n), lambda b, t: (b, t, 0))
    return pl.pallas_call(
        _gdn_local_body,
        grid=(bsz, L // c),
        in_specs=[tok(w), tok(w), tok(w), tok(LANES)],
        out_specs=[tok(w)] * 5,
        out_shape=[jax.ShapeDtypeStruct((bsz, L, w), f32)] + [jax.ShapeDtypeStruct((bsz, L, w), bf16)] * 4,
        compiler_params=pltpu.CompilerParams(
            dimension_semantics=("parallel", "parallel"), vmem_limit_bytes=VMEM_LIMIT_BYTES),
        name="gdn_local",
    )(qn, kn, vv, gb)


def _gdn_scan_body(uv_ref, wk_ref, qh_ref, ktt_ref, qk_ref, gb_ref, gate_ref, s0_ref, on_ref, o_ref, sf_ref, s_ref):
    f32, bf16 = jnp.float32, jnp.bfloat16
    c = GDN_CHUNK
    tt = uv_ref.shape[0]
    heads = range(GDN_HEADS)
    sl = [slice(h * HEAD_DIM, (h + 1) * HEAD_DIM) for h in heads]

    @pl.when(pl.program_id(1) == 0)
    def _():
        s_ref[...] = s0_ref[...]

    lane = lax.broadcasted_iota(jnp.int32, (1, LANES), 1)
    dot = lambda a, b: jnp.dot(a, b, preferred_element_type=f32)

    def chunk(ci, _):
        rows = pl.ds(pl.multiple_of(ci * c, c), c)
        eg_last = jnp.exp(jnp.sum(gb_ref[rows, :], axis=0, keepdims=True))
        eg = [jnp.sum(jnp.where(lane == h, eg_last, 0.0), axis=-1, keepdims=True) for h in heads]
        st = [s_ref[h] for h in heads]
        sb = [st[h].astype(bf16) for h in heads]
        u = [uv_ref[rows, sl[h]] - dot(wk_ref[rows, sl[h]], sb[h]) for h in heads]
        ub = [u[h].astype(bf16) for h in heads]
        new = [eg[h] * st[h] + dot(ktt_ref[rows, sl[h]], ub[h]) for h in heads]
        for h in heads:
            s_ref[h] = new[h]
        o = [dot(qh_ref[rows, sl[h]], sb[h]) + dot(qk_ref[rows, sl[h]], ub[h]) for h in heads]
        o = [o[h] * lax.rsqrt(jnp.mean(o[h] * o[h], axis=-1, keepdims=True) + EPS) for h in heads]
        gt = gate_ref[rows, :]
        o_ref[rows, :] = (jnp.concatenate(o, axis=-1) * on_ref[...]) * (gt * _sigmoid(gt))
        return 0

    lax.fori_loop(0, tt // c, chunk, 0)
    sf_ref[...] = s_ref[...]


def gdn_scan_pallas(uv, wk, qh, kt, qk, gb, gate, s0, o_norm, *, tt=256):
    bsz, L, w = uv.shape
    tt = min(tt, L)
    f32 = jnp.float32
    tok = lambda n: pl.BlockSpec((None, tt, n), lambda b, t: (b, t, 0))
    st = pl.BlockSpec((None, GDN_HEADS, HEAD_DIM, HEAD_DIM), lambda b, t: (b, 0, 0, 0))
    on = jnp.tile(o_norm.astype(f32), GDN_HEADS).reshape(1, w)
    return pl.pallas_call(
        _gdn_scan_body,
        grid=(bsz, L // tt),
        in_specs=[tok(w)] * 5 + [tok(LANES), tok(w), st, pl.BlockSpec((1, w), lambda b, t: (0, 0))],
        out_specs=[tok(w), st],
        out_shape=[jax.ShapeDtypeStruct((bsz, L, w), f32), jax.ShapeDtypeStruct(s0.shape, f32)],
        scratch_shapes=[pltpu.VMEM((GDN_HEADS, HEAD_DIM, HEAD_DIM), f32)],
        compiler_params=pltpu.CompilerParams(
            dimension_semantics=("parallel", "arbitrary"), vmem_limit_bytes=VMEM_LIMIT_BYTES),
        name="gdn_scan",
    )(uv, wk, qh, kt, qk, gb, gate, s0.astype(f32).swapaxes(2, 3), on)


def gdn_mix_pallas(qkv, ab, gate, conv_ctx, s0, conv_w, a_log, dt_bias, o_norm):
    qn, kn, vv, gb, conv_state = gdn_prep_pallas(qkv, ab, conv_ctx, conv_w, a_log, dt_bias)
    uv, wk, qh, kt, qk = gdn_local_pallas(qn, kn, vv, gb)
    o, s_fin_t = gdn_scan_pallas(uv, wk, qh, kt, qk, gb, gate, s0, o_norm)
    return o, conv_state.astype(qkv.dtype), s_fin_t.swapaxes(2, 3).astype(s0.dtype)


def _pad_cols(w, n):
    return jnp.pad(w, ((0, 0), (0, n - w.shape[1])))


def _in_proj_layout(kind, w_in):
    f32, bf16 = jnp.float32, jnp.bfloat16
    if kind == 0:
        return w_in, ((0, MEM_WIDTH), (MEM_WIDTH, SEQ_WIDTH)), (bf16, f32)
    if kind == 1:
        w = _pad_cols(w_in, MEM_WIDTH + sum(DSA_SPLITS[:4]) + LANES)
        widths = (MEM_WIDTH,) + DSA_SPLITS[:4] + (LANES,)
        starts = np.cumsum((0,) + widths[:-1]).tolist()
        return w, tuple(zip(starts, widths)), (bf16, bf16, f32, f32, bf16, f32)
    c0, c1 = MEM_WIDTH + CONV_CH, MEM_WIDTH + CONV_CH + 2 * GDN_HEADS
    w = jnp.concatenate([w_in[:, :c0], w_in[:, c1:], _pad_cols(w_in[:, c0:c1], LANES)], axis=1)
    widths = (MEM_WIDTH, CONV_CH, SEQ_WIDTH, LANES)
    starts = np.cumsum((0,) + widths[:-1]).tolist()
    return w, tuple(zip(starts, widths)), (bf16, f32, f32, f32)


def kernel(x_prompt, x_sample, cache_mem_k, cache_mem_v, state_ssm_re, state_ssm_im, cache_k, cache_v, cache_kidx, state_conv, state_delta, page_table, mem_prompt, norm_g, final_norm, w_in_a, w_in_b, w_in_c, w_out, w_mem_kv, ffn1_gate, ffn1_up, ffn1_down, ffn2_gate, ffn2_up, ffn2_down, s5_lam_re, s5_lam_im, s5_log_step, s5_b_re, s5_b_im, s5_c_re, s5_c_im, s5_d, s5_w_glu, s5_b_glu, rel_bias, gdn_conv_w, gdn_a_log, gdn_dt_bias, gdn_o_norm):
    depth = norm_g.shape[0]
    bp, lp, d = x_prompt.shape
    bs, ls, _ = x_sample.shape
    assert ls == 1, "the decode-step kernels handle one new token per sample"
    f32, bf16 = jnp.float32, jnp.bfloat16
    hp, hs = x_prompt.reshape(bp * lp, d), x_sample.reshape(bs * ls, d)
    mem_rows = mem_prompt.reshape(bp * N_MEM, d)
    mem_k_p, mem_v_p = [], []
    ssm_re_p, ssm_im_p, ssm_re_s, ssm_im_s = [], [], [], []
    k_p, v_p, ki_p, k_s, v_s, ki_s = [], [], [], [], [], []
    conv_p, delta_p, conv_s, delta_s = [], [], [], []
    w_in_by_kind = (w_in_a, w_in_b, w_in_c)
    for i in range(depth):
        kind, j = i % N_MIXERS, i // N_MIXERS
        last = i == depth - 1
        ffn1 = (ffn1_gate[i].astype(bf16), ffn1_up[i].astype(bf16), ffn1_down[i].astype(bf16))
        ffn2 = (ffn2_gate[i].astype(bf16), ffn2_up[i].astype(bf16), ffn2_down[i].astype(bf16))
        w_in, segments, dtypes = _in_proj_layout(kind, w_in_by_kind[kind][j])
        w_in = w_in.astype(bf16)
        hp = ffn_residual(hp, norm_g[i, 0], *ffn1)
        hs = ffn_residual(hs, norm_g[i, 0], *ffn1)
        zp = [a.reshape(bp, lp, -1) for a in proj_in(hp, norm_g[i, 1], w_in, segments, dtypes)]
        zs = [a.reshape(bs, ls, -1) for a in proj_in(hs, norm_g[i, 1], w_in, segments, dtypes)]
        mk, mv = proj_in(mem_rows, norm_g[i, 1], w_mem_kv[i].astype(bf16), ((0, MEM_WIDTH), (MEM_WIDTH, MEM_WIDTH)),
                         (f32, f32), normalize=False)
        mem_shape = (bp, N_MEM, MEM_HEADS, HEAD_DIM)
        mk, mv = mk.reshape(mem_shape), mv.reshape(mem_shape)
        mem_k_p.append(mk)
        mem_v_p.append(mv)
        if kind == 0:
            disc = s5_discretize(s5_lam_re[j], s5_lam_im[j], s5_log_step[j], s5_b_re[j], s5_b_im[j],
                                 s5_c_re[j], s5_c_im[j])
            gate = (s5_d[j], s5_w_glu[j], s5_b_glu[j])
            h0 = jnp.zeros((bp, S5_GROUPS, S5_STATE), state_ssm_re.dtype)
            op, hr, hi = s5_mix_pallas(zp[1], h0, h0, disc, *gate)
            ssm_re_p.append(hr)
            ssm_im_p.append(hi)
            osm, hr, hi = s5_mix_pallas(zs[1], state_ssm_re[j], state_ssm_im[j], disc, *gate)
            ssm_re_s.append(hr)
            ssm_im_s.append(hi)
        elif kind == 1:
            _, q, k, v, qi, kiwi = zp
            op = dsa_prompt_pallas(q, qi, kiwi, k, v, rel_bias)
            kv_shape = (DSA_KV_HEADS, HEAD_DIM)
            k_p.append(k.reshape((bp, lp) + kv_shape))
            v_p.append(v.reshape((bp, lp) + kv_shape))
            ki_p.append(kiwi[..., :IDX_DIM])
            _, q, k, v, qi, kiwi = zs
            osm = dsa_decode_pallas(q[:, 0], qi[:, 0], kiwi[:, 0], k[:, 0], v[:, 0], cache_k[j], cache_v[j],
                                    cache_kidx[j], page_table, rel_bias)
            k_s.append(k.reshape((bs, ls) + kv_shape))
            v_s.append(v.reshape((bs, ls) + kv_shape))
            ki_s.append(kiwi[..., :IDX_DIM])
        else:
            gdn = (gdn_conv_w[j], gdn_a_log[j], gdn_dt_bias[j], gdn_o_norm[j])
            ctx0 = jnp.zeros((bp, CONV_W - 1, CONV_CH), f32)
            s00 = jnp.zeros((bp, GDN_HEADS, HEAD_DIM, HEAD_DIM), state_delta.dtype)
            _, qkv, gate, ab = zp
            op, cst, sst = gdn_mix_pallas(qkv, ab, gate, ctx0, s00, *gdn)
            conv_p.append(cst)
            delta_p.append(sst)
            _, qkv, gate, ab = zs
            z_s = jnp.concatenate([qkv, ab[..., :2 * GDN_HEADS], gate], axis=-1)
            osm, cst, sst = gdn_mix(z_s, state_conv[j], state_delta[j], *gdn)
            conv_s.append(cst)
            delta_s.append(sst)
        w_o = w_out[i].astype(bf16)
        hp = mix_out(hp.reshape(bp, lp, d), zp[0], op, mk, mv, w_o).reshape(bp * lp, d)
        hs = mix_out(hs.reshape(bs, ls, d), zs[0], osm, cache_mem_k[i], cache_mem_v[i], w_o).reshape(bs * ls, d)
        fg = final_norm if last else None
        hp = ffn_residual(hp, norm_g[i, 2], *ffn2, fg)
        hs = ffn_residual(hs, norm_g[i, 2], *ffn2, fg)
    st = jnp.stack
    return (hp.reshape(bp, lp, d), hs.reshape(bs, ls, d), st(mem_k_p), st(mem_v_p),
            st(ssm_re_p), st(ssm_im_p), st(ssm_re_s), st(ssm_im_s),
            st(k_p), st(v_p), st(ki_p), st(k_s), st(v_s), st(ki_s),
            st(conv_p), st(delta_p), st(conv_s), st(delta_s))
```

```python
import math
import functools
import jax
import jax.numpy as jnp
from jax import lax
import numpy as np
from jax.experimental import pallas as pl
from jax.experimental.pallas import tpu as pltpu

D_MODEL = 1024
N_MIXERS = 3
HEAD_DIM = 64
MIX_WIDTH = D_MODEL
N_MEM = 256
MEM_HEADS = 4
MEM_WIDTH = MEM_HEADS * HEAD_DIM
SEQ_WIDTH = MIX_WIDTH - MEM_WIDTH
S5_GROUP = 16
S5_GROUPS = SEQ_WIDTH // S5_GROUP
S5_STATE = 64
DSA_HEADS = SEQ_WIDTH // HEAD_DIM
DSA_KV_HEADS = 4
DSA_GQA = DSA_HEADS // DSA_KV_HEADS
IDX_HEADS = 8
IDX_DIM = 64
TOPK_MAX = 256
QBLOCK = 128
N_BUCKETS = 32
MAX_DISTANCE = 128
GDN_HEADS = SEQ_WIDTH // HEAD_DIM
CONV_W = 4
CONV_CH = 3 * SEQ_WIDTH
GDN_CHUNK = 64
D_FF = 2816
EPS = 1e-6
PAGE_SIZE = 128
DSA_SPLITS = (DSA_HEADS * HEAD_DIM, DSA_KV_HEADS * HEAD_DIM, DSA_KV_HEADS * HEAD_DIM, IDX_HEADS * IDX_DIM, IDX_DIM, IDX_HEADS)
GDN_SPLITS = (CONV_CH, GDN_HEADS, GDN_HEADS, SEQ_WIDTH)

LANES = 128
VMEM_LIMIT_BYTES = 56 * 1024 * 1024


def _sigmoid(x):
    return 1.0 / (1.0 + jnp.exp(-x))


def t5_bucket(dist):
    max_exact = N_BUCKETS // 2
    n = jnp.maximum(dist, 0)
    nf = jnp.maximum(n, 1).astype(jnp.float32)
    large = max_exact + (jnp.log(nf / max_exact) / math.log(MAX_DISTANCE / max_exact)
                         * (N_BUCKETS - max_exact)).astype(jnp.int32)
    return jnp.where(n < max_exact, n, jnp.minimum(large, N_BUCKETS - 1))


FFN_CHUNK = 256


def _ffn_body(final_norm, x_ref, g_ref, wg_ref, wu_ref, wd_ref, gf_ref, o_ref):
    f32, bf16 = jnp.float32, jnp.bfloat16
    x = x_ref[...]
    n = ((x * lax.rsqrt(jnp.mean(x * x, axis=-1, keepdims=True) + EPS)) * g_ref[...]).astype(bf16)
    acc = jnp.zeros(x.shape, f32)
    for c in range(wg_ref.shape[1] // FFN_CHUNK):
        sl = slice(c * FFN_CHUNK, (c + 1) * FFN_CHUNK)
        a = jnp.dot(n, wg_ref[:, sl], preferred_element_type=f32)
        b = jnp.dot(n, wu_ref[:, sl], preferred_element_type=f32)
        h = (a * _sigmoid(a)) * b
        acc = acc + jnp.dot(h.astype(bf16), wd_ref[sl, :], preferred_element_type=f32)
    y = x + 0.5 * acc
    if final_norm:
        y = (y * lax.rsqrt(jnp.mean(y * y, axis=-1, keepdims=True) + EPS)) * gf_ref[...]
    o_ref[...] = y


def ffn_residual(x, g, wg, wu, wd, final_g=None, *, tm=512):
    t, d = x.shape
    f = wg.shape[1]
    tm = min(tm, t)
    gf = jnp.ones((d,), jnp.float32) if final_g is None else final_g
    resident = lambda shape: pl.BlockSpec(shape, lambda i: (0, 0), pipeline_mode=pl.Buffered(1))
    return pl.pallas_call(
        functools.partial(_ffn_body, final_g is not None),
        grid=(t // tm,),
        in_specs=[pl.BlockSpec((tm, d), lambda i: (i, 0)), resident((1, d)),
                  resident((d, f)), resident((d, f)), resident((f, d)), resident((1, d))],
        out_specs=pl.BlockSpec((tm, d), lambda i: (i, 0)),
        out_shape=jax.ShapeDtypeStruct((t, d), jnp.float32),
        compiler_params=pltpu.CompilerParams(dimension_semantics=("parallel",), vmem_limit_bytes=VMEM_LIMIT_BYTES),
        name="ffn_residual",
    )(x, g.reshape(1, d).astype(jnp.float32), wg, wu, wd, gf.reshape(1, d).astype(jnp.float32))


def _proj_in_body(segments, normalize, x_ref, g_ref, w_ref, *o_refs):
    x = x_ref[...]
    if normalize:
        r = lax.rsqrt(jnp.mean(x * x, axis=-1, keepdims=True) + EPS)
        x = (x * r) * g_ref[...]
    n = x.astype(jnp.bfloat16)
    for (start, width), o_ref in zip(segments, o_refs):
        o_ref[...] = jnp.dot(n, w_ref[:, start:start + width], preferred_element_type=jnp.float32).astype(o_ref.dtype)


def proj_in(x, g, w, segments, dtypes, *, normalize=True, tm=256):
    t, d = x.shape
    tm = min(tm, t)
    return pl.pallas_call(
        functools.partial(_proj_in_body, tuple(segments), normalize),
        grid=(t // tm,),
        in_specs=[pl.BlockSpec((tm, d), lambda i: (i, 0)), pl.BlockSpec((1, d), lambda i: (0, 0)),
                  pl.BlockSpec(w.shape, lambda i: (0, 0))],
        out_specs=[pl.BlockSpec((tm, width), lambda i: (i, 0)) for _, width in segments],
        out_shape=[jax.ShapeDtypeStruct((t, width), dt) for (_, width), dt in zip(segments, dtypes)],
        compiler_params=pltpu.CompilerParams(dimension_semantics=("parallel",), vmem_limit_bytes=VMEM_LIMIT_BYTES),
        name="proj_in",
    )(x, g.reshape(1, d).astype(jnp.float32), w)


def _mix_out_body(x_ref, cq_ref, om_ref, mkt_ref, mv_ref, w_ref, o_ref):
    f32, bf16 = jnp.float32, jnp.bfloat16
    cq = cq_ref[...]
    heads = []
    for h in range(MEM_HEADS):
        sl = slice(h * HEAD_DIM, (h + 1) * HEAD_DIM)
        logits = jnp.dot(cq[:, sl], mkt_ref[sl, :], preferred_element_type=f32) * HEAD_DIM ** -0.5
        p = jnp.exp(logits - jnp.max(logits, axis=-1, keepdims=True))
        p = p / jnp.sum(p, axis=-1, keepdims=True)
        heads.append(jnp.dot(p.astype(bf16), mv_ref[:, sl], preferred_element_type=f32))
    o_mem = jnp.concatenate(heads, axis=-1).astype(bf16)
    y = (jnp.dot(o_mem, w_ref[:MEM_WIDTH, :], preferred_element_type=f32)
         + jnp.dot(om_ref[...].astype(bf16), w_ref[MEM_WIDTH:, :], preferred_element_type=f32))
    o_ref[...] = x_ref[...] + y


def mix_out(x, cq, o_mix, mk, mv, w_out, *, tm=512):
    bsz, L, d = x.shape
    tm = min(tm, L)
    bf16 = jnp.bfloat16
    mkt = mk.astype(bf16).reshape(bsz, N_MEM, MEM_WIDTH).swapaxes(1, 2)
    mvf = mv.astype(bf16).reshape(bsz, N_MEM, MEM_WIDTH)
    tok = lambda n: pl.BlockSpec((None, tm, n), lambda b, t: (b, t, 0))
    per_b = lambda r, c: pl.BlockSpec((None, r, c), lambda b, t: (b, 0, 0))
    return pl.pallas_call(
        _mix_out_body,
        grid=(bsz, L // tm),
        in_specs=[tok(d), tok(MEM_WIDTH), tok(SEQ_WIDTH), per_b(MEM_WIDTH, N_MEM), per_b(N_MEM, MEM_WIDTH),
                  pl.BlockSpec((d, d), lambda b, t: (0, 0))],
        out_specs=tok(d),
        out_shape=jax.ShapeDtypeStruct((bsz, L, d), jnp.float32),
        compiler_params=pltpu.CompilerParams(
            dimension_semantics=("parallel", "parallel"), vmem_limit_bytes=VMEM_LIMIT_BYTES),
        name="mix_out",
    )(x, cq, o_mix, mkt, mvf, w_out)


S5_LANES = S5_GROUPS * S5_STATE
S5_GROUPS_PER_BLOCK = LANES // S5_GROUP
S5_BLOCKS = SEQ_WIDTH // LANES
S5_BLOCK_STATES = S5_GROUPS_PER_BLOCK * S5_STATE


def s5_discretize(lam_re, lam_im, log_step, b_re, b_im, c_re, c_im):
    f32, bf16 = jnp.float32, jnp.bfloat16
    lr, li = lam_re.astype(f32), lam_im.astype(f32)
    step = jnp.exp(log_step.astype(f32))[:, None]
    mag = jnp.exp(lr * step)
    ab_re, ab_im = mag * jnp.cos(li * step), mag * jnp.sin(li * step)
    den = lr * lr + li * li
    nr, ni = ab_re - 1.0, ab_im
    f_re = (nr * lr + ni * li) / den
    f_im = (ni * lr - nr * li) / den
    br, bi = b_re.astype(f32), b_im.astype(f32)
    bb_re = f_re[..., None] * br - f_im[..., None] * bi
    bb_im = f_re[..., None] * bi + f_im[..., None] * br
    eye = jnp.eye(S5_GROUPS_PER_BLOCK, dtype=f32)
    nb, gb = S5_BLOCKS, S5_GROUPS_PER_BLOCK

    def in_blocks(bb):
        w = jnp.einsum('jgpc,gh->jgchp', bb.reshape(nb, gb, S5_STATE, S5_GROUP), eye)
        return w.reshape(nb, LANES, S5_BLOCK_STATES).astype(bf16)

    def out_blocks(c):
        w = jnp.einsum('jgop,gh->jgpho', c.astype(f32).reshape(nb, gb, S5_GROUP, S5_STATE), eye)
        return w.reshape(nb, S5_BLOCK_STATES, LANES).astype(bf16)

    return (ab_re.reshape(1, S5_LANES), ab_im.reshape(1, S5_LANES),
            in_blocks(bb_re), in_blocks(bb_im), out_blocks(c_re), out_blocks(-c_im))


def _s5_project_in(u, wbr_ref, wbi_ref, bur_ref, bui_ref):
    ub = u.astype(jnp.bfloat16)
    for j in range(S5_BLOCKS):
        uj = ub[:, j * LANES:(j + 1) * LANES]
        sl = slice(j * S5_BLOCK_STATES, (j + 1) * S5_BLOCK_STATES)
        bur_ref[:, sl] = jnp.dot(uj, wbr_ref[j], preferred_element_type=jnp.float32)
        bui_ref[:, sl] = jnp.dot(uj, wbi_ref[j], preferred_element_type=jnp.float32)


def _s5_project_out(hr_ref, hi_ref, wcr_ref, wci_ref):
    cols = []
    for j in range(S5_BLOCKS):
        sl = slice(j * S5_BLOCK_STATES, (j + 1) * S5_BLOCK_STATES)
        cols.append(jnp.dot(hr_ref[:, sl].astype(jnp.bfloat16), wcr_ref[j], preferred_element_type=jnp.float32)
                    + jnp.dot(hi_ref[:, sl].astype(jnp.bfloat16), wci_ref[j], preferred_element_type=jnp.float32))
    return jnp.concatenate(cols, axis=-1)


def _s5_gate(y_ssm, u, d_ref, wglu_ref, bglu_ref):
    y = y_ssm + d_ref[...] * u
    y = 0.5 * y * (1.0 + jnp.tanh(math.sqrt(2.0 / math.pi) * (y + 0.044715 * (y * y * y))))
    z = jnp.dot(y.astype(jnp.bfloat16), wglu_ref[...], preferred_element_type=jnp.float32) + bglu_ref[...]
    return y * (1.0 / (1.0 + jnp.exp(-z)))


def _s5_seq_body(u_ref, h0r_ref, h0i_ref, ar_ref, ai_ref, wbr_ref, wbi_ref, wcr_ref, wci_ref, d_ref, wglu_ref, bglu_ref,
                 y_ref, hfr_ref, hfi_ref, bur_ref, bui_ref, cr_ref, ci_ref):
    tt = u_ref.shape[0]

    @pl.when(pl.program_id(1) == 0)
    def _():
        cr_ref[...] = h0r_ref[...]
        ci_ref[...] = h0i_ref[...]

    u = u_ref[...]
    _s5_project_in(u, wbr_ref, wbi_ref, bur_ref, bui_ref)
    ar, ai = ar_ref[...], ai_ref[...]

    def step(t, carry):
        hr, hi = carry
        nhr = ar * hr - ai * hi + bur_ref[pl.ds(t, 1), :]
        nhi = ar * hi + ai * hr + bui_ref[pl.ds(t, 1), :]
        bur_ref[pl.ds(t, 1), :] = nhr
        bui_ref[pl.ds(t, 1), :] = nhi
        return nhr, nhi

    hr, hi = lax.fori_loop(0, tt, step, (cr_ref[...], ci_ref[...]))
    cr_ref[...] = hr
    ci_ref[...] = hi
    hfr_ref[...] = hr
    hfi_ref[...] = hi
    y_ref[...] = _s5_gate(_s5_project_out(bur_ref, bui_ref, wcr_ref, wci_ref), u, d_ref, wglu_ref, bglu_ref)


def _s5_step_body(u_ref, h0r_ref, h0i_ref, ar_ref, ai_ref, wbr_ref, wbi_ref, wcr_ref, wci_ref, d_ref, wglu_ref, bglu_ref,
                  y_ref, hfr_ref, hfi_ref, bur_ref, bui_ref):
    u = u_ref[...]
    _s5_project_in(u, wbr_ref, wbi_ref, bur_ref, bui_ref)
    ar, ai = ar_ref[...], ai_ref[...]
    hr, hi = h0r_ref[...], h0i_ref[...]
    nhr = ar * hr - ai * hi + bur_ref[...]
    nhi = ar * hi + ai * hr + bui_ref[...]
    bur_ref[...] = nhr
    bui_ref[...] = nhi
    hfr_ref[...] = nhr
    hfi_ref[...] = nhi
    y_ref[...] = _s5_gate(_s5_project_out(bur_ref, bui_ref, wcr_ref, wci_ref), u, d_ref, wglu_ref, bglu_ref)


def s5_mix_pallas(u, h0_re, h0_im, disc, d_skip, w_glu, b_glu, *, tt=512):
    bsz, L, w = u.shape
    f32 = jnp.float32
    ar, ai, wbr, wbi, wcr, wci = disc
    d2, bg2, wg = d_skip.reshape(1, w).astype(f32), b_glu.reshape(1, w).astype(f32), w_glu.astype(jnp.bfloat16)
    const2 = lambda *_: (0, 0)
    const3 = lambda *_: (0, 0, 0)
    w_specs = [pl.BlockSpec((1, S5_LANES), const2), pl.BlockSpec((1, S5_LANES), const2),
               pl.BlockSpec(wbr.shape, const3), pl.BlockSpec(wbi.shape, const3),
               pl.BlockSpec(wcr.shape, const3), pl.BlockSpec(wci.shape, const3),
               pl.BlockSpec((1, w), const2), pl.BlockSpec((w, w), const2), pl.BlockSpec((1, w), const2)]
    w_args = (ar, ai, wbr, wbi, wcr, wci, d2, wg, bg2)
    if L == 1:
        rows = bsz
        h0r, h0i = h0_re.reshape(rows, S5_LANES).astype(f32), h0_im.reshape(rows, S5_LANES).astype(f32)
        row_spec = lambda n: pl.BlockSpec((rows, n), const2)
        y, hr, hi = pl.pallas_call(
            _s5_step_body,
            grid=(1,),
            in_specs=[row_spec(w), row_spec(S5_LANES), row_spec(S5_LANES)] + w_specs,
            out_specs=[row_spec(w), row_spec(S5_LANES), row_spec(S5_LANES)],
            out_shape=[jax.ShapeDtypeStruct((rows, w), f32), jax.ShapeDtypeStruct((rows, S5_LANES), f32),
                       jax.ShapeDtypeStruct((rows, S5_LANES), f32)],
            scratch_shapes=[pltpu.VMEM((rows, S5_LANES), f32), pltpu.VMEM((rows, S5_LANES), f32)],
            compiler_params=pltpu.CompilerParams(vmem_limit_bytes=VMEM_LIMIT_BYTES),
            name="s5_step",
        )(u.reshape(rows, w), h0r, h0i, *w_args)
        y = y.reshape(bsz, 1, w)
    else:
        tt = min(tt, L)
        h0r, h0i = h0_re.reshape(bsz, 1, S5_LANES).astype(f32), h0_im.reshape(bsz, 1, S5_LANES).astype(f32)
        st_spec = pl.BlockSpec((None, 1, S5_LANES), lambda b, t: (b, 0, 0))
        y, hr, hi = pl.pallas_call(
            _s5_seq_body,
            grid=(bsz, L // tt),
            in_specs=[pl.BlockSpec((None, tt, w), lambda b, t: (b, t, 0)), st_spec, st_spec] + w_specs,
            out_specs=[pl.BlockSpec((None, tt, w), lambda b, t: (b, t, 0)), st_spec, st_spec],
            out_shape=[jax.ShapeDtypeStruct((bsz, L, w), f32), jax.ShapeDtypeStruct((bsz, 1, S5_LANES), f32),
                       jax.ShapeDtypeStruct((bsz, 1, S5_LANES), f32)],
            scratch_shapes=[pltpu.VMEM((tt, S5_LANES), f32), pltpu.VMEM((tt, S5_LANES), f32),
                            pltpu.VMEM((1, S5_LANES), f32), pltpu.VMEM((1, S5_LANES), f32)],
            compiler_params=pltpu.CompilerParams(
                dimension_semantics=("parallel", "arbitrary"), vmem_limit_bytes=VMEM_LIMIT_BYTES),
            name="s5_seq",
        )(u, h0r, h0i, *w_args)
    shp = (bsz, S5_GROUPS, S5_STATE)
    return y, hr.reshape(shp).astype(h0_re.dtype), hi.reshape(shp).astype(h0_im.dtype)


DSA_KEY_CHUNK = 512
DSA_NEAR = 2 * QBLOCK
INT32_MIN = -2 ** 31
NEG_BIG = -1e30
KT_ROWS = 2 * HEAD_DIM


def _sortable_key(s):
    bits = lax.bitcast_convert_type(s, jnp.int32)
    return jnp.where(bits < 0, bits ^ jnp.int32(0x7FFFFFFF), bits)


def _dsa_prompt_body(n_top, rb_ref, q_ref, qi_ref, kiwi_ref, kt_ref, v_ref, kit_ref, o_ref, keys_ref, bias_ref, bound_ref):
    i = pl.program_id(1)
    f32, bf16 = jnp.float32, jnp.bfloat16
    kc = DSA_KEY_CHUNK
    q_start = i * QBLOCK
    n_all = (q_start + QBLOCK + kc - 1) // kc
    near_start = jnp.maximum(q_start - QBLOCK, 0)
    n_far = (near_start + kc - 1) // kc
    row = lax.broadcasted_iota(jnp.int32, (QBLOCK, 1), 0)
    qpos = q_start + row

    @pl.when(i == 0)
    def _():
        r = lax.broadcasted_iota(jnp.int32, (QBLOCK, DSA_NEAR), 0)
        c = lax.broadcasted_iota(jnp.int32, (QBLOCK, DSA_NEAR), 1)
        for tile in range(2):
            bucket = t5_bucket(r + tile * QBLOCK - c)
            for h in range(DSA_HEADS):
                b = jnp.zeros((QBLOCK, DSA_NEAR), f32)
                for bk in range(N_BUCKETS):
                    b = jnp.where(bucket == bk, rb_ref[bk, h] - rb_ref[N_BUCKETS - 1, h], b)
                kv, g = divmod(h, DSA_GQA)
                bias_ref[tile, kv, g * QBLOCK:(g + 1) * QBLOCK, :] = b
        lane = lax.broadcasted_iota(jnp.int32, (1, LANES), 1)
        bound = jnp.zeros((1, LANES), f32)
        for kv in range(DSA_KV_HEADS):
            kk = kt_ref[kv * KT_ROWS:kv * KT_ROWS + HEAD_DIM, :].astype(f32)
            kmax = jnp.sqrt(jnp.max(jnp.sum(kk * kk, axis=0, keepdims=True), axis=-1, keepdims=True))
            bound = jnp.where(lane == kv, kmax, bound)
        for h in range(DSA_HEADS):
            kv, g = divmod(h, DSA_GQA)
            bmax = jnp.maximum(jnp.max(bias_ref[:, kv, g * QBLOCK:(g + 1) * QBLOCK, :]), 0.0)
            bound = jnp.where(lane == DSA_KV_HEADS + h, bmax, bound)
        bound_ref[...] = bound

    qi_all = (qi_ref[...].astype(f32) * IDX_DIM ** -0.5).astype(bf16)
    qi = jnp.concatenate([qi_all[:, h * IDX_DIM:(h + 1) * IDX_DIM] for h in range(IDX_HEADS)], axis=0)
    wi = kiwi_ref[:, IDX_DIM:IDX_DIM + IDX_HEADS] * IDX_HEADS ** -0.5

    def score_chunk(c, _):
        off = pl.multiple_of(c * kc, kc)
        d = jnp.dot(qi, kit_ref[:, pl.ds(off, kc)], preferred_element_type=f32)
        s = jnp.zeros((QBLOCK, kc), f32)
        for h in range(IDX_HEADS):
            s = s + wi[:, h:h + 1] * jnp.maximum(d[h * QBLOCK:(h + 1) * QBLOCK], 0.0)
        kpos = off + lax.broadcasted_iota(jnp.int32, (QBLOCK, kc), 1)
        s = jnp.where(s == 0.0, 0.0, s)
        s = jnp.where(kpos <= qpos, s, -jnp.inf)
        keys_ref[:, pl.ds(off, kc)] = _sortable_key(s)
        return 0

    lax.fori_loop(0, n_all, score_chunk, 0)

    def count_where(pred_fn):
        def body(c, acc):
            off = pl.multiple_of(c * kc, kc)
            hit = pred_fn(keys_ref[:, pl.ds(off, kc)], off)
            part = jnp.where(hit, 1.0, 0.0)
            for j in range(kc // 128):
                acc = acc + part[:, j * 128:(j + 1) * 128]
            return acc
        acc = lax.fori_loop(0, n_all, body, jnp.zeros((QBLOCK, 128), f32))
        return jnp.sum(acc, axis=-1, keepdims=True)

    def thr_bit(it, t):
        cand = t + lax.shift_left(jnp.int32(1), 31 - it)
        cnt = count_where(lambda k, off: k >= cand)
        return jnp.where(cnt >= n_top, cand, t)

    thr = lax.fori_loop(0, 32, thr_bit, jnp.full((QBLOCK, 1), INT32_MIN, jnp.int32))

    def is_valid(off, width):
        return (off + lax.broadcasted_iota(jnp.int32, (QBLOCK, width), 1)) <= qpos

    n_gt = count_where(lambda k, off: (k > thr) & is_valid(off, kc))
    n_eq = count_where(lambda k, off: (k == thr) & is_valid(off, kc))
    need = n_top - n_gt
    has_extra_ties = jnp.max(jnp.where(n_eq > need, 1.0, 0.0)) > 0.0

    def tie_search():
        def idx_bit(it, j):
            cand = j + lax.shift_left(jnp.int32(1), 13 - it)
            cnt = count_where(lambda k, off: (k == thr) & is_valid(off, kc)
                              & ((off + lax.broadcasted_iota(jnp.int32, (QBLOCK, kc), 1)) < cand))
            return jnp.where(cnt <= need, cand, j)
        return lax.fori_loop(0, 14, idx_bit, jnp.zeros((QBLOCK, 1), jnp.int32))

    tie_end = lax.cond(has_extra_ties, tie_search, lambda: jnp.full((QBLOCK, 1), 2 ** 14, jnp.int32))

    def selected(keys, off, width):
        kpos = off + lax.broadcasted_iota(jnp.int32, (QBLOCK, width), 1)
        return (kpos <= qpos) & ((keys > thr) | ((keys == thr) & (kpos < tie_end))), kpos

    tile = jnp.minimum(i, 1)
    q_all = (q_ref[...].astype(f32) * HEAD_DIM ** -0.5).astype(bf16)
    qh = [q_all[:, h * HEAD_DIM:(h + 1) * HEAD_DIM] for h in range(DSA_HEADS)]
    off_near = pl.multiple_of(near_start, QBLOCK)
    sel_near, _ = selected(keys_ref[:, pl.ds(off_near, DSA_NEAR)], off_near, DSA_NEAR)

    def near_bias(h):
        kv, g = divmod(h, DSA_GQA)
        return jnp.where(sel_near, bias_ref[tile, kv, g * QBLOCK:(g + 1) * QBLOCK, :], NEG_BIG)

    def far_mask(c):
        off = pl.multiple_of(c * kc, kc)
        sel, kpos = selected(keys_ref[:, pl.ds(off, kc)], off, kc)
        return off, jnp.where(sel & (kpos < near_start), 0.0, NEG_BIG)

    def kt_block(kv, off, width, rows):
        return kt_ref[kv * KT_ROWS:kv * KT_ROWS + rows, pl.ds(off, width)]

    bound = bound_ref[...]
    lane = lax.broadcasted_iota(jnp.int32, (1, LANES), 1)
    pick = lambda idx: jnp.sum(jnp.where(lane == idx, bound, 0.0), axis=-1, keepdims=True)
    col = lax.broadcasted_iota(jnp.int32, (QBLOCK, HEAD_DIM), 1)
    q_aug = []
    for h in range(DSA_HEADS):
        qf = qh[h].astype(f32)
        ub = jnp.sqrt(jnp.sum(qf * qf, axis=-1, keepdims=True)) * pick(h // DSA_GQA) * 1.01 + pick(DSA_KV_HEADS + h)
        q_aug.append(jnp.concatenate([qh[h], jnp.where(col == 0, -ub, 0.0).astype(bf16)], axis=-1))

    def fast_attend(accs, off, width, mask_bias):
        accs = list(accs)
        for kv in range(DSA_KV_HEADS):
            kt = kt_block(kv, off, width, KT_ROWS)
            ps = [jnp.exp(jnp.dot(q_aug[kv * DSA_GQA + g], kt, preferred_element_type=f32)
                          + mask_bias(kv * DSA_GQA + g)).astype(bf16) for g in range(DSA_GQA)]
            accs[kv] = accs[kv] + jnp.dot(jnp.concatenate(ps, axis=0), v_ref[kv, pl.ds(off, width), :],
                                          preferred_element_type=f32)
        return tuple(accs)

    def fast_chunk(c, accs):
        off, mb = far_mask(c)
        return fast_attend(accs, off, kc, lambda h: mb)

    zero_acc = tuple(jnp.zeros((DSA_GQA * QBLOCK, 2 * HEAD_DIM), f32) for _ in range(DSA_KV_HEADS))
    accs = fast_attend(lax.fori_loop(0, n_far, fast_chunk, zero_acc), off_near, DSA_NEAR, near_bias)
    l_min = accs[0][:, HEAD_DIM:]
    for kv in range(1, DSA_KV_HEADS):
        l_min = jnp.minimum(l_min, accs[kv][:, HEAD_DIM:])
    fast_ok = jnp.min(l_min) > 1e-30

    def safe_attend(carry, off, width, mask_bias):
        ms, accs = list(carry[0]), list(carry[1])
        for kv in range(DSA_KV_HEADS):
            kt = kt_block(kv, off, width, HEAD_DIM)
            ps, alphas = [], []
            for g in range(DSA_GQA):
                h = kv * DSA_GQA + g
                logits = jnp.dot(qh[h], kt, preferred_element_type=f32) + mask_bias(h)
                m_new = jnp.maximum(ms[h], jnp.max(logits, axis=-1, keepdims=True))
                alphas.append(jnp.exp(ms[h] - m_new))
                ps.append(jnp.exp(logits - m_new).astype(bf16))
                ms[h] = m_new
            accs[kv] = (jnp.concatenate(alphas, axis=0) * accs[kv]
                        + jnp.dot(jnp.concatenate(ps, axis=0), v_ref[kv, pl.ds(off, width), :], preferred_element_type=f32))
        return tuple(ms), tuple(accs)

    def safe_path():
        def safe_chunk(c, carry):
            off, mb = far_mask(c)
            return safe_attend(carry, off, kc, lambda h: mb)
        init = (tuple(jnp.full((QBLOCK, 1), NEG_BIG, f32) for _ in range(DSA_HEADS)), zero_acc)
        return safe_attend(lax.fori_loop(0, n_far, safe_chunk, init), off_near, DSA_NEAR, near_bias)[1]

    accs = lax.cond(fast_ok, lambda: accs, safe_path)
    outs = []
    for kv in range(DSA_KV_HEADS):
        o = accs[kv][:, :HEAD_DIM] / accs[kv][:, HEAD_DIM:]
        outs += [o[g * QBLOCK:(g + 1) * QBLOCK] for g in range(DSA_GQA)]
    o_ref[...] = jnp.concatenate(outs, axis=-1)


def dsa_prompt_pallas(q, qi, kiwi, k, v, rel_bias):
    bsz, L = q.shape[:2]
    nq = L // QBLOCK
    n_top = min(TOPK_MAX, L // 4)
    f32, bf16 = jnp.float32, jnp.bfloat16
    k_t = k.astype(bf16).reshape(bsz, L, DSA_KV_HEADS, HEAD_DIM).transpose(0, 2, 3, 1)
    k_pad = jnp.zeros((bsz, DSA_KV_HEADS, KT_ROWS - HEAD_DIM, L), bf16).at[:, :, 0, :].set(1.0)
    k_t = jnp.concatenate([k_t, k_pad], axis=2).reshape(bsz, DSA_KV_HEADS * KT_ROWS, L)
    v_h = v.astype(bf16).reshape(bsz, L, DSA_KV_HEADS, HEAD_DIM).swapaxes(1, 2)
    v_h = jnp.concatenate([v_h, jnp.ones_like(v_h)], axis=-1)
    ki_t = kiwi[..., :IDX_DIM].astype(bf16).swapaxes(1, 2)
    lk = max(L, DSA_KEY_CHUNK)
    if lk != L:
        k_t = jnp.pad(k_t, ((0, 0), (0, 0), (0, lk - L)))
        v_h = jnp.pad(v_h, ((0, 0), (0, 0), (0, lk - L), (0, 0)))
        ki_t = jnp.pad(ki_t, ((0, 0), (0, 0), (0, lk - L)))
    tok = lambda n: pl.BlockSpec((None, QBLOCK, n), lambda b, i: (b, i, 0))
    return pl.pallas_call(
        functools.partial(_dsa_prompt_body, n_top),
        grid=(bsz, nq),
        in_specs=[
            pl.BlockSpec(memory_space=pltpu.SMEM),
            tok(SEQ_WIDTH), tok(IDX_HEADS * IDX_DIM), tok(LANES),
            pl.BlockSpec((None, DSA_KV_HEADS * KT_ROWS, lk), lambda b, i: (b, 0, 0)),
            pl.BlockSpec((None, DSA_KV_HEADS, lk, 2 * HEAD_DIM), lambda b, i: (b, 0, 0, 0)),
            pl.BlockSpec((None, IDX_DIM, lk), lambda b, i: (b, 0, 0)),
        ],
        out_specs=tok(SEQ_WIDTH),
        out_shape=jax.ShapeDtypeStruct((bsz, L, SEQ_WIDTH), f32),
        scratch_shapes=[pltpu.VMEM((QBLOCK, lk), jnp.int32),
                        pltpu.VMEM((2, DSA_KV_HEADS, DSA_GQA * QBLOCK, DSA_NEAR), f32),
                        pltpu.VMEM((1, LANES), f32)],
        compiler_params=pltpu.CompilerParams(
            dimension_semantics=("parallel", "arbitrary"), vmem_limit_bytes=VMEM_LIMIT_BYTES),
        name="dsa_prompt",
    )(rel_bias.astype(f32), q, qi, kiwi, k_t, v_h, ki_t)


DEC_PAGES_PER_STEP = 16
KV_WIDTH = DSA_KV_HEADS * HEAD_DIM


def _index_score(qi, wi, ki):
    d = _dot_nt(qi, ki)
    s = jnp.sum(wi * jnp.maximum(d, 0.0), axis=0, keepdims=True)
    return jnp.where(s == 0.0, 0.0, s)


def _dsa_dec_score_body(pt_ref, qi_ref, wi_ref, *refs):
    pages, s_ref = refs[:-1], refs[-1]
    qi = (qi_ref[...].astype(jnp.float32) * IDX_DIM ** -0.5).astype(jnp.bfloat16)
    wi = wi_ref[...]
    s_ref[...] = jnp.concatenate([_index_score(qi, wi, p[...].astype(jnp.bfloat16)) for p in pages], axis=0)


def _dsa_dec_attend_body(n_top, n_pages, pt_ref, rb_ref, q_ref, qi_ref, wi_ref, knew_ref, vnew_ref, kinew_ref, s_ref,
                         *refs):
    pp = DEC_PAGES_PER_STEP if n_pages >= DEC_PAGES_PER_STEP else n_pages
    k_pages, v_pages = refs[:pp], refs[pp:2 * pp]
    o_ref, thr_ref, tie_ref, snew_ref, m_ref, l_ref, acc_ref = refs[2 * pp:]
    f32, bf16 = jnp.float32, jnp.bfloat16
    t = pl.program_id(1)
    nt = pl.num_programs(1)
    past = n_pages * PAGE_SIZE
    page_i = lax.broadcasted_iota(jnp.int32, (n_pages, LANES), 0)
    lane_i = lax.broadcasted_iota(jnp.int32, (n_pages, LANES), 1)
    kpos_all = page_i * PAGE_SIZE + lane_i
    total = lambda x: jnp.sum(jnp.sum(x, axis=1, keepdims=True), axis=0, keepdims=True)

    @pl.when(t == 0)
    def _():
        qi = (qi_ref[...].astype(f32) * IDX_DIM ** -0.5).astype(bf16)
        d_new = jnp.sum(qi.astype(f32) * kinew_ref[...].astype(bf16).astype(f32), axis=-1, keepdims=True)
        s_new = jnp.sum(wi_ref[...] * jnp.maximum(d_new, 0.0), axis=0, keepdims=True)
        s_new = jnp.where(s_new == 0.0, 0.0, s_new)
        key_new = _sortable_key(s_new)
        keys = _sortable_key(s_ref[...])

        def count(pred_past, pred_new):
            return total(jnp.where(pred_past, 1.0, 0.0)) + jnp.where(pred_new, 1.0, 0.0)

        def thr_bit(it, thr):
            cand = thr + lax.shift_left(jnp.int32(1), 31 - it)
            return jnp.where(count(keys >= cand, key_new >= cand) >= n_top, cand, thr)

        thr = lax.fori_loop(0, 32, thr_bit, jnp.full((1, 1), INT32_MIN, jnp.int32))
        need = n_top - count(keys > thr, key_new > thr)

        def idx_bit(it, j):
            cand = j + lax.shift_left(jnp.int32(1), 14 - it)
            cnt = count((keys == thr) & (kpos_all < cand), (key_new == thr) & (past < cand))
            return jnp.where(cnt <= need, cand, j)

        n_eq = count(keys == thr, key_new == thr)
        tie_end = lax.cond(jnp.max(jnp.where(n_eq > need, 1.0, 0.0)) > 0.0,
                           lambda: lax.fori_loop(0, 15, idx_bit, jnp.zeros((1, 1), jnp.int32)),
                           lambda: jnp.full((1, 1), 2 ** 15, jnp.int32))
        thr_ref[...] = thr
        tie_ref[...] = tie_end
        snew_ref[...] = s_new
        m_ref[...] = jnp.full(m_ref.shape, NEG_BIG, f32)
        l_ref[...] = jnp.zeros(l_ref.shape, f32)
        acc_ref[...] = jnp.zeros(acc_ref.shape, f32)

    thr, tie_end = thr_ref[...], tie_ref[...]
    q = (q_ref[...].astype(f32) * HEAD_DIM ** -0.5).astype(bf16)
    lane12 = lax.broadcasted_iota(jnp.int32, (DSA_HEADS, LANES), 1)

    def head_bias(dist):
        bucket = t5_bucket(dist)
        b = jnp.zeros(dist.shape, f32)
        for h in range(DSA_HEADS):
            row = jnp.zeros(dist.shape, f32)
            for bk in range(N_BUCKETS):
                row = jnp.where(bucket == bk, rb_ref[bk, h] - rb_ref[N_BUCKETS - 1, h], row)
            b = jnp.where(lax.broadcasted_iota(jnp.int32, dist.shape, 0) == h, row, b)
        return b

    keys_step = _sortable_key(s_ref[pl.ds(pl.multiple_of(t * pp, pp), pp), :])
    tiles = []
    for j in range(pp):
        kpos = (t * pp + j) * PAGE_SIZE + lax.broadcasted_iota(jnp.int32, (1, LANES), 1)
        keys = keys_step[j:j + 1]
        sel = (keys > thr) | ((keys == thr) & (kpos < tie_end))
        logits = _dot_nt(q, k_pages[j][...].astype(bf16))
        if j == pp - 1:
            near = head_bias(past - (t * pp + j) * PAGE_SIZE - lane12)
            logits = logits + jnp.where(t == nt - 1, near, 0.0)
        tiles.append(jnp.where(sel, logits, NEG_BIG))
    m = m_ref[...]
    tile_max = tiles[0]
    for x in tiles[1:]:
        tile_max = jnp.maximum(tile_max, x)
    m_new = jnp.maximum(m, jnp.max(tile_max, axis=-1, keepdims=True))
    alpha = jnp.exp(m - m_new)
    ps = [jnp.exp(x - m_new) for x in tiles]
    p_sum = ps[0]
    for x in ps[1:]:
        p_sum = p_sum + x
    pv = jnp.dot(ps[0].astype(bf16), v_pages[0][...].astype(bf16), preferred_element_type=f32)
    for j in range(1, pp):
        pv = pv + jnp.dot(ps[j].astype(bf16), v_pages[j][...].astype(bf16), preferred_element_type=f32)
    l_ref[...] = alpha * l_ref[...] + jnp.sum(p_sum, axis=-1, keepdims=True)
    acc_ref[...] = alpha * acc_ref[...] + pv
    m_ref[...] = m_new

    @pl.when(t == nt - 1)
    def _():
        s_new = snew_ref[...]
        key_new = _sortable_key(s_new)
        sel_new = (key_new > thr) | ((key_new == thr) & (past < tie_end))
        k_new = knew_ref[...].astype(bf16).astype(f32)
        logit_new = jnp.sum(q.astype(f32) * k_new, axis=-1, keepdims=True)
        logit_new = logit_new + head_bias(jnp.zeros((DSA_HEADS, LANES), jnp.int32))[:, :1]
        logit_new = jnp.where(sel_new, logit_new, NEG_BIG)
        m = m_ref[...]
        m_new = jnp.maximum(m, logit_new)
        alpha = jnp.exp(m - m_new)
        p_new = jnp.exp(logit_new - m_new)
        l = alpha * l_ref[...] + p_new
        v_new = vnew_ref[...].astype(bf16).astype(f32)
        acc = alpha * acc_ref[...] + p_new.astype(bf16).astype(f32) * v_new
        o = acc / l
        o_ref[...] = jnp.concatenate(
            [o[h:h + 1, (h // DSA_GQA) * HEAD_DIM:(h // DSA_GQA + 1) * HEAD_DIM] for h in range(DSA_HEADS)], axis=-1)


def dsa_decode_pallas(q, qi, kiwi, k_new, v_new, pool_k, pool_v, pool_ki, page_table, rel_bias):
    bsz = q.shape[0]
    n_pages = page_table.shape[1]
    n_phys = pool_k.shape[0]
    pp = DEC_PAGES_PER_STEP if n_pages >= DEC_PAGES_PER_STEP else n_pages
    nt = n_pages // pp
    n_top = min(TOPK_MAX, (n_pages * PAGE_SIZE + 1) // 4)
    f32, bf16 = jnp.float32, jnp.bfloat16
    eye = jnp.eye(DSA_KV_HEADS, dtype=q.dtype)
    q_bd = jnp.einsum('bkgd,kj->bkgjd', q.reshape(bsz, DSA_KV_HEADS, DSA_GQA, HEAD_DIM), eye)
    q_bd = q_bd.reshape(bsz, DSA_HEADS, KV_WIDTH)
    qi3 = qi.reshape(bsz, IDX_HEADS, IDX_DIM)
    wi3 = (kiwi[:, IDX_DIM:IDX_DIM + IDX_HEADS] * IDX_HEADS ** -0.5).reshape(bsz, IDX_HEADS, 1)
    ki_new = kiwi[:, :IDX_DIM].reshape(bsz, 1, IDX_DIM)
    pk = pool_k.reshape(n_phys, PAGE_SIZE, KV_WIDTH)
    pv = pool_v.reshape(n_phys, PAGE_SIZE, KV_WIDTH)
    per_b = lambda *shape: pl.BlockSpec((None,) + shape, lambda b, t, pt: (b,) + (0,) * len(shape))
    page = lambda width, j: pl.BlockSpec((None, PAGE_SIZE, width), lambda b, t, pt: (pt[b, t * pp + j], 0, 0))
    scores = pl.pallas_call(
        _dsa_dec_score_body,
        grid_spec=pltpu.PrefetchScalarGridSpec(
            num_scalar_prefetch=1, grid=(bsz, nt),
            in_specs=[per_b(IDX_HEADS, IDX_DIM), per_b(IDX_HEADS, 1)] + [page(IDX_DIM, j) for j in range(pp)],
            out_specs=pl.BlockSpec((None, pp, LANES), lambda b, t, pt: (b, t, 0))),
        out_shape=jax.ShapeDtypeStruct((bsz, n_pages, LANES), f32),
        compiler_params=pltpu.CompilerParams(
            dimension_semantics=("parallel", "parallel"), vmem_limit_bytes=VMEM_LIMIT_BYTES),
        name="dsa_dec_score",
    )(page_table, qi3, wi3, *([pool_ki] * pp))
    one = lambda: pltpu.VMEM((1, 1), jnp.int32)
    o = pl.pallas_call(
        functools.partial(_dsa_dec_attend_body, n_top, n_pages),
        grid_spec=pltpu.PrefetchScalarGridSpec(
            num_scalar_prefetch=1, grid=(bsz, nt),
            in_specs=[pl.BlockSpec(memory_space=pltpu.SMEM), per_b(DSA_HEADS, KV_WIDTH), per_b(IDX_HEADS, IDX_DIM),
                      per_b(IDX_HEADS, 1), per_b(1, KV_WIDTH), per_b(1, KV_WIDTH), per_b(1, IDX_DIM),
                      per_b(n_pages, LANES)]
            + [page(KV_WIDTH, j) for j in range(pp)] * 2,
            out_specs=per_b(1, SEQ_WIDTH),
            scratch_shapes=[one(), one(), pltpu.VMEM((1, 1), f32), pltpu.VMEM((DSA_HEADS, 1), f32),
                            pltpu.VMEM((DSA_HEADS, 1), f32), pltpu.VMEM((DSA_HEADS, KV_WIDTH), f32)]),
        out_shape=jax.ShapeDtypeStruct((bsz, 1, SEQ_WIDTH), f32),
        compiler_params=pltpu.CompilerParams(
            dimension_semantics=("parallel", "arbitrary"), vmem_limit_bytes=VMEM_LIMIT_BYTES),
        name="dsa_dec_attend",
    )(page_table, rel_bias.astype(f32), q_bd, qi3, wi3, k_new.reshape(bsz, 1, KV_WIDTH),
      v_new.reshape(bsz, 1, KV_WIDTH), ki_new, scores, *([pk] * pp), *([pv] * pp))
    return o


CONV_HALO = 8


def _gdn_prep_body(x_ref, ab_ref, ctx_ref, w_ref, alog_ref, dtb_ref, q_ref, k_ref, v_ref, gb_ref, cs_ref, xp_ref):
    tt = x_ref.shape[0]
    halo = CONV_W - 1

    @pl.when(pl.program_id(1) == 0)
    def _():
        xp_ref[CONV_HALO - halo:CONV_HALO, :] = ctx_ref[...]

    x = x_ref[...]
    xp_ref[CONV_HALO:CONV_HALO + tt, :] = x
    w = w_ref[...]
    y = xp_ref[CONV_HALO - halo:CONV_HALO - halo + tt, :] * w[0:1]
    for j in range(1, CONV_W):
        y = y + xp_ref[CONV_HALO - halo + j:CONV_HALO - halo + j + tt, :] * w[j:j + 1]
    last = x[tt - halo:, :]
    xp_ref[CONV_HALO - halo:CONV_HALO, :] = last
    cs_ref[...] = last
    y = y * _sigmoid(y)

    def l2n(a, scale):
        cols = []
        for h in range(GDN_HEADS):
            s = a[:, h * HEAD_DIM:(h + 1) * HEAD_DIM]
            cols.append(s * (lax.rsqrt(jnp.sum(s * s, axis=-1, keepdims=True) + EPS) * scale))
        return jnp.concatenate(cols, axis=-1)

    q_ref[...] = l2n(y[:, :SEQ_WIDTH], HEAD_DIM ** -0.5)
    k_ref[...] = l2n(y[:, SEQ_WIDTH:2 * SEQ_WIDTH], 1.0)
    v_ref[...] = y[:, 2 * SEQ_WIDTH:]
    ab = ab_ref[...]
    xa = ab + dtb_ref[...]
    softplus = jnp.maximum(xa, 0.0) + jnp.log(1.0 + jnp.exp(-jnp.abs(xa)))
    g = -jnp.exp(alog_ref[...]) * softplus
    lane = lax.broadcasted_iota(jnp.int32, ab.shape, 1)
    gb_ref[...] = jnp.where(lane < GDN_HEADS, g, _sigmoid(ab))


def gdn_prep_pallas(qkv, ab, ctx, conv_w, a_log, dt_bias, *, tt=256):
    bsz, L, ch = qkv.shape
    tt = min(tt, L)
    f32 = jnp.float32
    pad = lambda r: jnp.pad(r.astype(f32).reshape(1, -1), ((0, 0), (0, LANES - r.shape[-1])))
    tok = lambda n: pl.BlockSpec((None, tt, n), lambda b, t: (b, t, 0))
    const2 = lambda b, t: (0, 0)
    return pl.pallas_call(
        _gdn_prep_body,
        grid=(bsz, L // tt),
        in_specs=[tok(ch), tok(LANES), pl.BlockSpec((None, CONV_W - 1, ch), lambda b, t: (b, 0, 0)),
                  pl.BlockSpec((CONV_W, ch), const2), pl.BlockSpec((1, LANES), const2), pl.BlockSpec((1, LANES), const2)],
        out_specs=[tok(SEQ_WIDTH), tok(SEQ_WIDTH), tok(SEQ_WIDTH), tok(LANES),
                   pl.BlockSpec((None, CONV_W - 1, ch), lambda b, t: (b, 0, 0))],
        out_shape=[jax.ShapeDtypeStruct((bsz, L, SEQ_WIDTH), f32)] * 3
        + [jax.ShapeDtypeStruct((bsz, L, LANES), f32), jax.ShapeDtypeStruct((bsz, CONV_W - 1, ch), f32)],
        scratch_shapes=[pltpu.VMEM((CONV_HALO + tt, ch), f32)],
        compiler_params=pltpu.CompilerParams(
            dimension_semantics=("parallel", "arbitrary"), vmem_limit_bytes=VMEM_LIMIT_BYTES),
        name="gdn_prep",
    )(qkv, ab, ctx.astype(f32), conv_w.astype(f32), pad(a_log), pad(dt_bias))


def _split3(a):
    bf16, f32 = jnp.bfloat16, jnp.float32
    h = a.astype(bf16)
    r = a - h.astype(f32)
    m = r.astype(bf16)
    return h, m, (r - m.astype(f32)).astype(bf16)


def _mm_hi(a, b):
    f32 = jnp.float32
    ah, am, _ = _split3(a)
    bh, bm, _ = _split3(b)
    d = lambda x, y: jnp.dot(x, y, preferred_element_type=f32)
    return d(ah, bh) + (d(ah, bm) + d(am, bh))


def _mm_sel(sel, b):
    f32 = jnp.float32
    s = sel.astype(jnp.bfloat16)
    bh, bm, bl = _split3(b)
    d = lambda y: jnp.dot(s, y, preferred_element_type=f32)
    return d(bh) + (d(bm) + d(bl))


def _dot_nt(a, b):
    return lax.dot_general(a, b, (((1,), (1,)), ((), ())), preferred_element_type=jnp.float32)


def _dot_tn(a, b):
    return lax.dot_general(a, b, (((0,), (0,)), ((), ())), preferred_element_type=jnp.float32)


def _gdn_local_body(q_ref, k_ref, v_ref, gb_ref, uv_ref, wk_ref, qh_ref, kt_ref, qk_ref):
    f32, bf16 = jnp.float32, jnp.bfloat16
    c = GDN_CHUNK
    r_i = lax.broadcasted_iota(jnp.int32, (c, c), 0)
    c_i = lax.broadcasted_iota(jnp.int32, (c, c), 1)
    tril = r_i >= c_i
    stril = r_i > c_i
    triu = r_i <= c_i
    eye = jnp.where(r_i == c_i, 1.0, 0.0)
    ones = jnp.ones((c, c), jnp.bool_)
    lane = lax.broadcasted_iota(jnp.int32, (c, LANES), 1)
    gb = gb_ref[...]
    heads = range(GDN_HEADS)
    sl = [slice(h * HEAD_DIM, (h + 1) * HEAD_DIM) for h in heads]
    pick = lambda idx: jnp.sum(jnp.where(lane == idx, gb, 0.0), axis=-1, keepdims=True)
    g_col = [pick(h) for h in heads]
    beta = [pick(GDN_HEADS + h) for h in heads]
    q = [q_ref[:, sl[h]] for h in heads]
    k = [k_ref[:, sl[h]] for h in heads]
    v = [v_ref[:, sl[h]] for h in heads]
    g_mat = [jnp.broadcast_to(g_col[h], (c, c)) for h in heads]
    cum_col = [_mm_sel(tril, g_mat[h]) for h in heads]
    cum_row = [_mm_sel(ones, jnp.where(triu, g_mat[h], 0.0)) for h in heads]
    gam = [jnp.where(tril, jnp.exp(jnp.where(tril, cum_col[h] - cum_row[h], 0.0)), 0.0) for h in heads]
    kb = [k[h].astype(bf16) for h in heads]
    a_mat = [jnp.where(stril, beta[h] * _dot_nt(kb[h], kb[h]) * gam[h], 0.0) for h in heads]
    t_inv = [eye - a_mat[h] for h in heads]
    pw = a_mat
    for _ in range(5):
        pw = [_mm_hi(pw[h], pw[h]) for h in heads]
        t_inv = [t_inv[h] + _mm_hi(t_inv[h], pw[h]) for h in heads]
    gc = [cum_col[h][:, :1] for h in heads]
    egc = [jnp.exp(gc[h]) for h in heads]
    tw = [_mm_hi(t_inv[h], jnp.concatenate([beta[h] * v[h], (beta[h] * egc[h]) * k[h]], axis=-1)) for h in heads]
    qk = [jnp.where(tril, _dot_nt(q[h].astype(bf16), kb[h]) * gam[h], 0.0) for h in heads]
    cat = lambda xs: jnp.concatenate(xs, axis=-1)
    uv_ref[...] = cat([tw[h][:, :HEAD_DIM] for h in heads])
    wk_ref[...] = cat([tw[h][:, HEAD_DIM:] for h in heads]).astype(bf16)
    qh_ref[...] = cat([egc[h] * q[h] for h in heads]).astype(bf16)
    eye_b = eye.astype(bf16)
    kt = [(jnp.exp(gc[h][c - 1:c, :] - gc[h]) * k[h]).astype(bf16) for h in heads]
    kt_ref[...] = cat([_dot_tn(kt[h], eye_b) for h in heads]).astype(bf16)
    qk_ref[...] = cat(qk).astype(bf16)


def gdn_local_pallas(qn, kn, vv, gb):
    bsz, L, w = qn.shape
    c = GDN_CHUNK
    f32, bf16 = jnp.float32, jnp.bfloat16
    tok = lambda n: pl.BlockSpec((None, c, n), lambda b, t: (b, t, 0))
    return pl.pallas_call(
        _gdn_local_body,
        grid=(bsz, L // c),
        in_specs=[tok(w), tok(w), tok(w), tok(LANES)],
        out_specs=[tok(w)] * 5,
        out_shape=[jax.ShapeDtypeStruct((bsz, L, w), f32)] + [jax.ShapeDtypeStruct((bsz, L, w), bf16)] * 4,
        compiler_params=pltpu.CompilerParams(
            dimension_semantics=("parallel", "parallel"), vmem_limit_bytes=VMEM_LIMIT_BYTES),
        name="gdn_local",
    )(qn, kn, vv, gb)


def _gdn_scan_body(uv_ref, wk_ref, qh_ref, ktt_ref, qk_ref, gb_ref, gate_ref, s0_ref, on_ref, o_ref, sf_ref, s_ref):
    f32, bf16 = jnp.float32, jnp.bfloat16
    c = GDN_CHUNK
    tt = uv_ref.shape[0]
    heads = range(GDN_HEADS)
    sl = [slice(h * HEAD_DIM, (h + 1) * HEAD_DIM) for h in heads]

    @pl.when(pl.program_id(1) == 0)
    def _():
        s_ref[...] = s0_ref[...]

    lane = lax.broadcasted_iota(jnp.int32, (1, LANES), 1)
    dot = lambda a, b: jnp.dot(a, b, preferred_element_type=f32)

    def chunk(ci, _):
        rows = pl.ds(pl.multiple_of(ci * c, c), c)
        eg_last = jnp.exp(jnp.sum(gb_ref[rows, :], axis=0, keepdims=True))
        eg = [jnp.sum(jnp.where(lane == h, eg_last, 0.0), axis=-1, keepdims=True) for h in heads]
        st = [s_ref[h] for h in heads]
        sb = [st[h].astype(bf16) for h in heads]
        u = [uv_ref[rows, sl[h]] - dot(wk_ref[rows, sl[h]], sb[h]) for h in heads]
        ub = [u[h].astype(bf16) for h in heads]
        new = [eg[h] * st[h] + dot(ktt_ref[rows, sl[h]], ub[h]) for h in heads]
        for h in heads:
            s_ref[h] = new[h]
        o = [dot(qh_ref[rows, sl[h]], sb[h]) + dot(qk_ref[rows, sl[h]], ub[h]) for h in heads]
        o = [o[h] * lax.rsqrt(jnp.mean(o[h] * o[h], axis=-1, keepdims=True) + EPS) for h in heads]
        gt = gate_ref[rows, :]
        o_ref[rows, :] = (jnp.concatenate(o, axis=-1) * on_ref[...]) * (gt * _sigmoid(gt))
        return 0

    lax.fori_loop(0, tt // c, chunk, 0)
    sf_ref[...] = s_ref[...]


def gdn_scan_pallas(uv, wk, qh, kt, qk, gb, gate, s0, o_norm, *, tt=256):
    bsz, L, w = uv.shape
    tt = min(tt, L)
    f32 = jnp.float32
    tok = lambda n: pl.BlockSpec((None, tt, n), lambda b, t: (b, t, 0))
    st = pl.BlockSpec((None, GDN_HEADS, HEAD_DIM, HEAD_DIM), lambda b, t: (b, 0, 0, 0))
    on = jnp.tile(o_norm.astype(f32), GDN_HEADS).reshape(1, w)
    return pl.pallas_call(
        _gdn_scan_body,
        grid=(bsz, L // tt),
        in_specs=[tok(w)] * 5 + [tok(LANES), tok(w), st, pl.BlockSpec((1, w), lambda b, t: (0, 0))],
        out_specs=[tok(w), st],
        out_shape=[jax.ShapeDtypeStruct((bsz, L, w), f32), jax.ShapeDtypeStruct(s0.shape, f32)],
        scratch_shapes=[pltpu.VMEM((GDN_HEADS, HEAD_DIM, HEAD_DIM), f32)],
        compiler_params=pltpu.CompilerParams(
            dimension_semantics=("parallel", "arbitrary"), vmem_limit_bytes=VMEM_LIMIT_BYTES),
        name="gdn_scan",
    )(uv, wk, qh, kt, qk, gb, gate, s0.astype(f32).swapaxes(2, 3), on)


def gdn_mix_pallas(qkv, ab, gate, conv_ctx, s0, conv_w, a_log, dt_bias, o_norm):
    qn, kn, vv, gb, conv_state = gdn_prep_pallas(qkv, ab, conv_ctx, conv_w, a_log, dt_bias)
    uv, wk, qh, kt, qk = gdn_local_pallas(qn, kn, vv, gb)
    o, s_fin_t = gdn_scan_pallas(uv, wk, qh, kt, qk, gb, gate, s0, o_norm)
    return o, conv_state.astype(qkv.dtype), s_fin_t.swapaxes(2, 3).astype(s0.dtype)


def _gdn_dec_body(x_ref, ab_ref, gate_ref, ctx_ref, s_ref, w_ref, alog_ref, dtb_ref, on_ref, o_ref, cs_ref, sn_ref):
    f32, bf16 = jnp.float32, jnp.bfloat16
    rnd = lambda a: a.astype(bf16).astype(f32)
    x, ctx, w = x_ref[...], ctx_ref[...], w_ref[...]
    halo = CONV_W - 1
    y = x * w[halo:halo + 1]
    for j in range(halo):
        y = y + ctx[j:j + 1] * w[j:j + 1]
    y = y * _sigmoid(y)
    cs_ref[...] = jnp.concatenate([ctx[1:], x], axis=0)
    ab = ab_ref[...]
    xa = ab + dtb_ref[...]
    g_all = -jnp.exp(alog_ref[...]) * (jnp.maximum(xa, 0.0) + jnp.log(1.0 + jnp.exp(-jnp.abs(xa))))
    beta_all = _sigmoid(ab)
    lane = lax.broadcasted_iota(jnp.int32, (1, LANES), 1)
    pick = lambda a, idx: jnp.sum(jnp.where(lane == idx, a, 0.0), axis=-1, keepdims=True)
    r_i = lax.broadcasted_iota(jnp.int32, (HEAD_DIM, HEAD_DIM), 0)
    c_i = lax.broadcasted_iota(jnp.int32, (HEAD_DIM, HEAD_DIM), 1)
    eye = r_i == c_i
    to_col = lambda row: jnp.sum(jnp.where(eye, jnp.broadcast_to(row, (HEAD_DIM, HEAD_DIM)), 0.0), axis=-1, keepdims=True)
    to_row = lambda col: jnp.sum(jnp.where(eye, jnp.broadcast_to(col, (HEAD_DIM, HEAD_DIM)), 0.0), axis=0, keepdims=True)
    outs = []
    for h in range(GDN_HEADS):
        sl = slice(h * HEAD_DIM, (h + 1) * HEAD_DIM)
        q, k, v = y[:, sl], y[:, SEQ_WIDTH + h * HEAD_DIM:SEQ_WIDTH + (h + 1) * HEAD_DIM], y[:, 2 * SEQ_WIDTH + h * HEAD_DIM:2 * SEQ_WIDTH + (h + 1) * HEAD_DIM]
        q = q * (lax.rsqrt(jnp.sum(q * q, axis=-1, keepdims=True) + EPS) * HEAD_DIM ** -0.5)
        k = k * lax.rsqrt(jnp.sum(k * k, axis=-1, keepdims=True) + EPS)
        eg = jnp.exp(pick(g_all, h))
        beta = pick(beta_all, GDN_HEADS + h)
        s = s_ref[h]
        sb = rnd(s)
        s_wk = jnp.sum(sb * rnd((beta * eg) * k), axis=-1, keepdims=True)
        u = to_col(beta * v) - s_wk
        ub = rnd(u)
        s_q = jnp.sum(sb * rnd(eg * q), axis=-1, keepdims=True)
        qk = jnp.sum(rnd(q) * rnd(k), axis=-1, keepdims=True)
        o = to_row(s_q + rnd(qk) * ub)
        sn_ref[h] = eg * s + ub * rnd(k)
        outs.append(o * lax.rsqrt(jnp.mean(o * o, axis=-1, keepdims=True) + EPS))
    gt = gate_ref[...]
    o_ref[...] = (jnp.concatenate(outs, axis=-1) * on_ref[...]) * (gt * _sigmoid(gt))


def gdn_decode_pallas(qkv, ab, gate, conv_ctx, s0, conv_w, a_log, dt_bias, o_norm):
    bsz = qkv.shape[0]
    f32 = jnp.float32
    pad = lambda r: jnp.pad(r.astype(f32).reshape(1, -1), ((0, 0), (0, LANES - r.shape[-1])))
    per_b = lambda *shape: pl.BlockSpec((None,) + shape, lambda b: (b,) + (0,) * len(shape))
    const = lambda *shape: pl.BlockSpec(shape, lambda b: (0,) * len(shape))
    st = (GDN_HEADS, HEAD_DIM, HEAD_DIM)
    o, conv_state, s_new = pl.pallas_call(
        _gdn_dec_body,
        grid=(bsz,),
        in_specs=[per_b(1, CONV_CH), per_b(1, LANES), per_b(1, SEQ_WIDTH), per_b(CONV_W - 1, CONV_CH), per_b(*st),
                  const(CONV_W, CONV_CH), const(1, LANES), const(1, LANES), const(1, SEQ_WIDTH)],
        out_specs=[per_b(1, SEQ_WIDTH), per_b(CONV_W - 1, CONV_CH), per_b(*st)],
        out_shape=[jax.ShapeDtypeStruct((bsz, 1, SEQ_WIDTH), f32), jax.ShapeDtypeStruct((bsz, CONV_W - 1, CONV_CH), f32),
                   jax.ShapeDtypeStruct((bsz,) + st, f32)],
        compiler_params=pltpu.CompilerParams(dimension_semantics=("parallel",), vmem_limit_bytes=VMEM_LIMIT_BYTES),
        name="gdn_decode",
    )(qkv, ab, gate, conv_ctx.astype(f32), s0.astype(f32), conv_w.astype(f32), pad(a_log), pad(dt_bias),
      jnp.tile(o_norm.astype(f32), GDN_HEADS).reshape(1, SEQ_WIDTH))
    return o, conv_state.astype(qkv.dtype), s_new.astype(s0.dtype)


def _pad_cols(w, n):
    return jnp.pad(w, ((0, 0), (0, n - w.shape[1])))


def _in_proj_layout(kind, w_in):
    f32, bf16 = jnp.float32, jnp.bfloat16
    if kind == 0:
        return w_in, ((0, MEM_WIDTH), (MEM_WIDTH, SEQ_WIDTH)), (bf16, f32)
    if kind == 1:
        w = _pad_cols(w_in, MEM_WIDTH + sum(DSA_SPLITS[:4]) + LANES)
        widths = (MEM_WIDTH,) + DSA_SPLITS[:4] + (LANES,)
        starts = np.cumsum((0,) + widths[:-1]).tolist()
        return w, tuple(zip(starts, widths)), (bf16, bf16, f32, f32, bf16, f32)
    c0, c1 = MEM_WIDTH + CONV_CH, MEM_WIDTH + CONV_CH + 2 * GDN_HEADS
    w = jnp.concatenate([w_in[:, :c0], w_in[:, c1:], _pad_cols(w_in[:, c0:c1], LANES)], axis=1)
    widths = (MEM_WIDTH, CONV_CH, SEQ_WIDTH, LANES)
    starts = np.cumsum((0,) + widths[:-1]).tolist()
    return w, tuple(zip(starts, widths)), (bf16, f32, f32, f32)


def kernel(x_prompt, x_sample, cache_mem_k, cache_mem_v, state_ssm_re, state_ssm_im, cache_k, cache_v, cache_kidx, state_conv, state_delta, page_table, mem_prompt, norm_g, final_norm, w_in_a, w_in_b, w_in_c, w_out, w_mem_kv, ffn1_gate, ffn1_up, ffn1_down, ffn2_gate, ffn2_up, ffn2_down, s5_lam_re, s5_lam_im, s5_log_step, s5_b_re, s5_b_im, s5_c_re, s5_c_im, s5_d, s5_w_glu, s5_b_glu, rel_bias, gdn_conv_w, gdn_a_log, gdn_dt_bias, gdn_o_norm):
    depth = norm_g.shape[0]
    bp, lp, d = x_prompt.shape
    bs, ls, _ = x_sample.shape
    assert ls == 1, "the decode-step kernels handle one new token per sample"
    f32, bf16 = jnp.float32, jnp.bfloat16
    hp, hs = x_prompt.reshape(bp * lp, d), x_sample.reshape(bs * ls, d)
    mem_rows = mem_prompt.reshape(bp * N_MEM, d)
    mem_k_p, mem_v_p = [], []
    ssm_re_p, ssm_im_p, ssm_re_s, ssm_im_s = [], [], [], []
    k_p, v_p, ki_p, k_s, v_s, ki_s = [], [], [], [], [], []
    conv_p, delta_p, conv_s, delta_s = [], [], [], []
    w_in_by_kind = (w_in_a, w_in_b, w_in_c)
    for i in range(depth):
        kind, j = i % N_MIXERS, i // N_MIXERS
        last = i == depth - 1
        ffn1 = (ffn1_gate[i].astype(bf16), ffn1_up[i].astype(bf16), ffn1_down[i].astype(bf16))
        ffn2 = (ffn2_gate[i].astype(bf16), ffn2_up[i].astype(bf16), ffn2_down[i].astype(bf16))
        w_in, segments, dtypes = _in_proj_layout(kind, w_in_by_kind[kind][j])
        w_in = w_in.astype(bf16)
        hp = ffn_residual(hp, norm_g[i, 0], *ffn1)
        hs = ffn_residual(hs, norm_g[i, 0], *ffn1)
        zp = [a.reshape(bp, lp, -1) for a in proj_in(hp, norm_g[i, 1], w_in, segments, dtypes)]
        zs = [a.reshape(bs, ls, -1) for a in proj_in(hs, norm_g[i, 1], w_in, segments, dtypes)]
        mk, mv = proj_in(mem_rows, norm_g[i, 1], w_mem_kv[i].astype(bf16), ((0, MEM_WIDTH), (MEM_WIDTH, MEM_WIDTH)),
                         (f32, f32), normalize=False)
        mem_shape = (bp, N_MEM, MEM_HEADS, HEAD_DIM)
        mk, mv = mk.reshape(mem_shape), mv.reshape(mem_shape)
        mem_k_p.append(mk)
        mem_v_p.append(mv)
        if kind == 0:
            disc = s5_discretize(s5_lam_re[j], s5_lam_im[j], s5_log_step[j], s5_b_re[j], s5_b_im[j],
                                 s5_c_re[j], s5_c_im[j])
            gate = (s5_d[j], s5_w_glu[j], s5_b_glu[j])
            h0 = jnp.zeros((bp, S5_GROUPS, S5_STATE), state_ssm_re.dtype)
            op, hr, hi = s5_mix_pallas(zp[1], h0, h0, disc, *gate)
            ssm_re_p.append(hr)
            ssm_im_p.append(hi)
            osm, hr, hi = s5_mix_pallas(zs[1], state_ssm_re[j], state_ssm_im[j], disc, *gate)
            ssm_re_s.append(hr)
            ssm_im_s.append(hi)
        elif kind == 1:
            _, q, k, v, qi, kiwi = zp
            op = dsa_prompt_pallas(q, qi, kiwi, k, v, rel_bias)
            kv_shape = (DSA_KV_HEADS, HEAD_DIM)
            k_p.append(k.reshape((bp, lp) + kv_shape))
            v_p.append(v.reshape((bp, lp) + kv_shape))
            ki_p.append(kiwi[..., :IDX_DIM])
            _, q, k, v, qi, kiwi = zs
            osm = dsa_decode_pallas(q[:, 0], qi[:, 0], kiwi[:, 0], k[:, 0], v[:, 0], cache_k[j], cache_v[j],
                                    cache_kidx[j], page_table, rel_bias)
            k_s.append(k.reshape((bs, ls) + kv_shape))
            v_s.append(v.reshape((bs, ls) + kv_shape))
            ki_s.append(kiwi[..., :IDX_DIM])
        else:
            gdn = (gdn_conv_w[j], gdn_a_log[j], gdn_dt_bias[j], gdn_o_norm[j])
            ctx0 = jnp.zeros((bp, CONV_W - 1, CONV_CH), f32)
            s00 = jnp.zeros((bp, GDN_HEADS, HEAD_DIM, HEAD_DIM), state_delta.dtype)
            _, qkv, gate, ab = zp
            op, cst, sst = gdn_mix_pallas(qkv, ab, gate, ctx0, s00, *gdn)
            conv_p.append(cst)
            delta_p.append(sst)
            _, qkv, gate, ab = zs
            osm, cst, sst = gdn_decode_pallas(qkv, ab, gate, state_conv[j], state_delta[j], *gdn)
            conv_s.append(cst)
            delta_s.append(sst)
        w_o = w_out[i].astype(bf16)
        hp = mix_out(hp.reshape(bp, lp, d), zp[0], op, mk, mv, w_o).reshape(bp * lp, d)
        hs = mix_out(hs.reshape(bs, ls, d), zs[0], osm, cache_mem_k[i], cache_mem_v[i], w_o).reshape(bs * ls, d)
        fg = final_norm if last else None
        hp = ffn_residual(hp, norm_g[i, 2], *ffn2, fg)
        hs = ffn_residual(hs, norm_g[i, 2], *ffn2, fg)
    st = jnp.stack
    return (hp.reshape(bp, lp, d), hs.reshape(bs, ls, d), st(mem_k_p), st(mem_v_p),
            st(ssm_re_p), st(ssm_im_p), st(ssm_re_s), st(ssm_im_s),
            st(k_p), st(v_p), st(ki_p), st(k_s), st(v_s), st(ki_s),
            st(conv_p), st(delta_p), st(conv_s), st(delta_s))
```

```python
import math
import functools
import jax
import jax.numpy as jnp
from jax import lax
import numpy as np
from jax.experimental import pallas as pl
from jax.experimental.pallas import tpu as pltpu

D_MODEL = 1024
N_MIXERS = 3
HEAD_DIM = 64
MIX_WIDTH = D_MODEL
N_MEM = 256
MEM_HEADS = 4
MEM_WIDTH = MEM_HEADS * HEAD_DIM
SEQ_WIDTH = MIX_WIDTH - MEM_WIDTH
S5_GROUP = 16
S5_GROUPS = SEQ_WIDTH // S5_GROUP
S5_STATE = 64
DSA_HEADS = SEQ_WIDTH // HEAD_DIM
DSA_KV_HEADS = 4
DSA_GQA = DSA_HEADS // DSA_KV_HEADS
IDX_HEADS = 8
IDX_DIM = 64
TOPK_MAX = 256
QBLOCK = 128
N_BUCKETS = 32
MAX_DISTANCE = 128
GDN_HEADS = SEQ_WIDTH // HEAD_DIM
CONV_W = 4
CONV_CH = 3 * SEQ_WIDTH
GDN_CHUNK = 64
D_FF = 2816
EPS = 1e-6
PAGE_SIZE = 128
DSA_SPLITS = (DSA_HEADS * HEAD_DIM, DSA_KV_HEADS * HEAD_DIM, DSA_KV_HEADS * HEAD_DIM, IDX_HEADS * IDX_DIM, IDX_DIM, IDX_HEADS)
GDN_SPLITS = (CONV_CH, GDN_HEADS, GDN_HEADS, SEQ_WIDTH)

LANES = 128
VMEM_LIMIT_BYTES = 56 * 1024 * 1024


def _sigmoid(x):
    return 1.0 / (1.0 + jnp.exp(-x))


def t5_bucket(dist):
    max_exact = N_BUCKETS // 2
    n = jnp.maximum(dist, 0)
    nf = jnp.maximum(n, 1).astype(jnp.float32)
    large = max_exact + (jnp.log(nf / max_exact) / math.log(MAX_DISTANCE / max_exact)
                         * (N_BUCKETS - max_exact)).astype(jnp.int32)
    return jnp.where(n < max_exact, n, jnp.minimum(large, N_BUCKETS - 1))


FFN_CHUNK = 256


def _ffn_body(final_norm, x_ref, g_ref, wg_ref, wu_ref, wd_ref, gf_ref, o_ref):
    f32, bf16 = jnp.float32, jnp.bfloat16
    x = x_ref[...]
    n = ((x * lax.rsqrt(jnp.mean(x * x, axis=-1, keepdims=True) + EPS)) * g_ref[...]).astype(bf16)
    acc = jnp.zeros(x.shape, f32)
    for c in range(wg_ref.shape[1] // FFN_CHUNK):
        sl = slice(c * FFN_CHUNK, (c + 1) * FFN_CHUNK)
        a = jnp.dot(n, wg_ref[:, sl], preferred_element_type=f32)
        b = jnp.dot(n, wu_ref[:, sl], preferred_element_type=f32)
        h = (a * _sigmoid(a)) * b
        acc = acc + jnp.dot(h.astype(bf16), wd_ref[sl, :], preferred_element_type=f32)
    y = x + 0.5 * acc
    if final_norm:
        y = (y * lax.rsqrt(jnp.mean(y * y, axis=-1, keepdims=True) + EPS)) * gf_ref[...]
    o_ref[...] = y


def ffn_residual(x, g, wg, wu, wd, final_g=None, *, tm=512):
    t, d = x.shape
    f = wg.shape[1]
    tm = min(tm, t)
    gf = jnp.ones((d,), jnp.float32) if final_g is None else final_g
    resident = lambda shape: pl.BlockSpec(shape, lambda i: (0, 0), pipeline_mode=pl.Buffered(1))
    return pl.pallas_call(
        functools.partial(_ffn_body, final_g is not None),
        grid=(t // tm,),
        in_specs=[pl.BlockSpec((tm, d), lambda i: (i, 0)), resident((1, d)),
                  resident((d, f)), resident((d, f)), resident((f, d)), resident((1, d))],
        out_specs=pl.BlockSpec((tm, d), lambda i: (i, 0)),
        out_shape=jax.ShapeDtypeStruct((t, d), jnp.float32),
        compiler_params=pltpu.CompilerParams(dimension_semantics=("parallel",), vmem_limit_bytes=VMEM_LIMIT_BYTES),
        name="ffn_residual",
    )(x, g.reshape(1, d).astype(jnp.float32), wg, wu, wd, gf.reshape(1, d).astype(jnp.float32))


def _proj_in_body(segments, normalize, x_ref, g_ref, w_ref, *o_refs):
    x = x_ref[...]
    if normalize:
        r = lax.rsqrt(jnp.mean(x * x, axis=-1, keepdims=True) + EPS)
        x = (x * r) * g_ref[...]
    n = x.astype(jnp.bfloat16)
    for (start, width), o_ref in zip(segments, o_refs):
        o_ref[...] = jnp.dot(n, w_ref[:, start:start + width], preferred_element_type=jnp.float32).astype(o_ref.dtype)


def proj_in(x, g, w, segments, dtypes, *, normalize=True, tm=256):
    t, d = x.shape
    tm = min(tm, t)
    return pl.pallas_call(
        functools.partial(_proj_in_body, tuple(segments), normalize),
        grid=(t // tm,),
        in_specs=[pl.BlockSpec((tm, d), lambda i: (i, 0)), pl.BlockSpec((1, d), lambda i: (0, 0)),
                  pl.BlockSpec(w.shape, lambda i: (0, 0))],
        out_specs=[pl.BlockSpec((tm, width), lambda i: (i, 0)) for _, width in segments],
        out_shape=[jax.ShapeDtypeStruct((t, width), dt) for (_, width), dt in zip(segments, dtypes)],
        compiler_params=pltpu.CompilerParams(dimension_semantics=("parallel",), vmem_limit_bytes=VMEM_LIMIT_BYTES),
        name="proj_in",
    )(x, g.reshape(1, d).astype(jnp.float32), w)


def _mix_out_body(x_ref, cq_ref, om_ref, mkt_ref, mv_ref, w_ref, o_ref):
    f32, bf16 = jnp.float32, jnp.bfloat16
    cq = cq_ref[...]
    heads = []
    for h in range(MEM_HEADS):
        sl = slice(h * HEAD_DIM, (h + 1) * HEAD_DIM)
        logits = jnp.dot(cq[:, sl], mkt_ref[sl, :], preferred_element_type=f32) * HEAD_DIM ** -0.5
        p = jnp.exp(logits - jnp.max(logits, axis=-1, keepdims=True))
        p = p / jnp.sum(p, axis=-1, keepdims=True)
        heads.append(jnp.dot(p.astype(bf16), mv_ref[:, sl], preferred_element_type=f32))
    o_mem = jnp.concatenate(heads, axis=-1).astype(bf16)
    y = (jnp.dot(o_mem, w_ref[:MEM_WIDTH, :], preferred_element_type=f32)
         + jnp.dot(om_ref[...].astype(bf16), w_ref[MEM_WIDTH:, :], preferred_element_type=f32))
    o_ref[...] = x_ref[...] + y


def mix_out(x, cq, o_mix, mk, mv, w_out, *, tm=512):
    bsz, L, d = x.shape
    tm = min(tm, L)
    bf16 = jnp.bfloat16
    mkt = mk.astype(bf16).reshape(bsz, N_MEM, MEM_WIDTH).swapaxes(1, 2)
    mvf = mv.astype(bf16).reshape(bsz, N_MEM, MEM_WIDTH)
    tok = lambda n: pl.BlockSpec((None, tm, n), lambda b, t: (b, t, 0))
    per_b = lambda r, c: pl.BlockSpec((None, r, c), lambda b, t: (b, 0, 0))
    return pl.pallas_call(
        _mix_out_body,
        grid=(bsz, L // tm),
        in_specs=[tok(d), tok(MEM_WIDTH), tok(SEQ_WIDTH), per_b(MEM_WIDTH, N_MEM), per_b(N_MEM, MEM_WIDTH),
                  pl.BlockSpec((d, d), lambda b, t: (0, 0))],
        out_specs=tok(d),
        out_shape=jax.ShapeDtypeStruct((bsz, L, d), jnp.float32),
        compiler_params=pltpu.CompilerParams(
            dimension_semantics=("parallel", "parallel"), vmem_limit_bytes=VMEM_LIMIT_BYTES),
        name="mix_out",
    )(x, cq, o_mix, mkt, mvf, w_out)


S5_LANES = S5_GROUPS * S5_STATE
S5_GROUPS_PER_BLOCK = LANES // S5_GROUP
S5_BLOCKS = SEQ_WIDTH // LANES
S5_BLOCK_STATES = S5_GROUPS_PER_BLOCK * S5_STATE


def s5_discretize(lam_re, lam_im, log_step, b_re, b_im, c_re, c_im):
    f32, bf16 = jnp.float32, jnp.bfloat16
    lr, li = lam_re.astype(f32), lam_im.astype(f32)
    step = jnp.exp(log_step.astype(f32))[:, None]
    mag = jnp.exp(lr * step)
    ab_re, ab_im = mag * jnp.cos(li * step), mag * jnp.sin(li * step)
    den = lr * lr + li * li
    nr, ni = ab_re - 1.0, ab_im
    f_re = (nr * lr + ni * li) / den
    f_im = (ni * lr - nr * li) / den
    br, bi = b_re.astype(f32), b_im.astype(f32)
    bb_re = f_re[..., None] * br - f_im[..., None] * bi
    bb_im = f_re[..., None] * bi + f_im[..., None] * br
    eye = jnp.eye(S5_GROUPS_PER_BLOCK, dtype=f32)
    nb, gb = S5_BLOCKS, S5_GROUPS_PER_BLOCK

    def in_blocks(bb):
        w = jnp.einsum('jgpc,gh->jgchp', bb.reshape(nb, gb, S5_STATE, S5_GROUP), eye)
        return w.reshape(nb, LANES, S5_BLOCK_STATES).astype(bf16)

    def out_blocks(c):
        w = jnp.einsum('jgop,gh->jgpho', c.astype(f32).reshape(nb, gb, S5_GROUP, S5_STATE), eye)
        return w.reshape(nb, S5_BLOCK_STATES, LANES).astype(bf16)

    return (ab_re.reshape(1, S5_LANES), ab_im.reshape(1, S5_LANES),
            in_blocks(bb_re), in_blocks(bb_im), out_blocks(c_re), out_blocks(-c_im))


def _s5_project_in(u, wbr_ref, wbi_ref, bur_ref, bui_ref):
    ub = u.astype(jnp.bfloat16)
    for j in range(S5_BLOCKS):
        uj = ub[:, j * LANES:(j + 1) * LANES]
        sl = slice(j * S5_BLOCK_STATES, (j + 1) * S5_BLOCK_STATES)
        bur_ref[:, sl] = jnp.dot(uj, wbr_ref[j], preferred_element_type=jnp.float32)
        bui_ref[:, sl] = jnp.dot(uj, wbi_ref[j], preferred_element_type=jnp.float32)


def _s5_project_out(hr_ref, hi_ref, wcr_ref, wci_ref):
    cols = []
    for j in range(S5_BLOCKS):
        sl = slice(j * S5_BLOCK_STATES, (j + 1) * S5_BLOCK_STATES)
        cols.append(jnp.dot(hr_ref[:, sl].astype(jnp.bfloat16), wcr_ref[j], preferred_element_type=jnp.float32)
                    + jnp.dot(hi_ref[:, sl].astype(jnp.bfloat16), wci_ref[j], preferred_element_type=jnp.float32))
    return jnp.concatenate(cols, axis=-1)


def _s5_gate(y_ssm, u, d_ref, wglu_ref, bglu_ref):
    y = y_ssm + d_ref[...] * u
    y = 0.5 * y * (1.0 + jnp.tanh(math.sqrt(2.0 / math.pi) * (y + 0.044715 * (y * y * y))))
    z = jnp.dot(y.astype(jnp.bfloat16), wglu_ref[...], preferred_element_type=jnp.float32) + bglu_ref[...]
    return y * (1.0 / (1.0 + jnp.exp(-z)))


def _s5_seq_body(u_ref, h0r_ref, h0i_ref, ar_ref, ai_ref, wbr_ref, wbi_ref, wcr_ref, wci_ref, d_ref, wglu_ref, bglu_ref,
                 y_ref, hfr_ref, hfi_ref, bur_ref, bui_ref, cr_ref, ci_ref):
    tt = u_ref.shape[0]

    @pl.when(pl.program_id(1) == 0)
    def _():
        cr_ref[...] = h0r_ref[...]
        ci_ref[...] = h0i_ref[...]

    u = u_ref[...]
    _s5_project_in(u, wbr_ref, wbi_ref, bur_ref, bui_ref)
    ar, ai = ar_ref[...], ai_ref[...]

    def step(t, carry):
        hr, hi = carry
        nhr = ar * hr - ai * hi + bur_ref[pl.ds(t, 1), :]
        nhi = ar * hi + ai * hr + bui_ref[pl.ds(t, 1), :]
        bur_ref[pl.ds(t, 1), :] = nhr
        bui_ref[pl.ds(t, 1), :] = nhi
        return nhr, nhi

    hr, hi = lax.fori_loop(0, tt, step, (cr_ref[...], ci_ref[...]))
    cr_ref[...] = hr
    ci_ref[...] = hi
    hfr_ref[...] = hr
    hfi_ref[...] = hi
    y_ref[...] = _s5_gate(_s5_project_out(bur_ref, bui_ref, wcr_ref, wci_ref), u, d_ref, wglu_ref, bglu_ref)


def _s5_step_body(u_ref, h0r_ref, h0i_ref, ar_ref, ai_ref, wbr_ref, wbi_ref, wcr_ref, wci_ref, d_ref, wglu_ref, bglu_ref,
                  y_ref, hfr_ref, hfi_ref, bur_ref, bui_ref):
    u = u_ref[...]
    _s5_project_in(u, wbr_ref, wbi_ref, bur_ref, bui_ref)
    ar, ai = ar_ref[...], ai_ref[...]
    hr, hi = h0r_ref[...], h0i_ref[...]
    nhr = ar * hr - ai * hi + bur_ref[...]
    nhi = ar * hi + ai * hr + bui_ref[...]
    bur_ref[...] = nhr
    bui_ref[...] = nhi
    hfr_ref[...] = nhr
    hfi_ref[...] = nhi
    y_ref[...] = _s5_gate(_s5_project_out(bur_ref, bui_ref, wcr_ref, wci_ref), u, d_ref, wglu_ref, bglu_ref)


def s5_mix_pallas(u, h0_re, h0_im, disc, d_skip, w_glu, b_glu, *, tt=512):
    bsz, L, w = u.shape
    f32 = jnp.float32
    ar, ai, wbr, wbi, wcr, wci = disc
    d2, bg2, wg = d_skip.reshape(1, w).astype(f32), b_glu.reshape(1, w).astype(f32), w_glu.astype(jnp.bfloat16)
    const2 = lambda *_: (0, 0)
    const3 = lambda *_: (0, 0, 0)
    w_specs = [pl.BlockSpec((1, S5_LANES), const2), pl.BlockSpec((1, S5_LANES), const2),
               pl.BlockSpec(wbr.shape, const3), pl.BlockSpec(wbi.shape, const3),
               pl.BlockSpec(wcr.shape, const3), pl.BlockSpec(wci.shape, const3),
               pl.BlockSpec((1, w), const2), pl.BlockSpec((w, w), const2), pl.BlockSpec((1, w), const2)]
    w_args = (ar, ai, wbr, wbi, wcr, wci, d2, wg, bg2)
    if L == 1:
        rows = bsz
        h0r, h0i = h0_re.reshape(rows, S5_LANES).astype(f32), h0_im.reshape(rows, S5_LANES).astype(f32)
        row_spec = lambda n: pl.BlockSpec((rows, n), const2)
        y, hr, hi = pl.pallas_call(
            _s5_step_body,
            grid=(1,),
            in_specs=[row_spec(w), row_spec(S5_LANES), row_spec(S5_LANES)] + w_specs,
            out_specs=[row_spec(w), row_spec(S5_LANES), row_spec(S5_LANES)],
            out_shape=[jax.ShapeDtypeStruct((rows, w), f32), jax.ShapeDtypeStruct((rows, S5_LANES), f32),
                       jax.ShapeDtypeStruct((rows, S5_LANES), f32)],
            scratch_shapes=[pltpu.VMEM((rows, S5_LANES), f32), pltpu.VMEM((rows, S5_LANES), f32)],
            compiler_params=pltpu.CompilerParams(vmem_limit_bytes=VMEM_LIMIT_BYTES),
            name="s5_step",
        )(u.reshape(rows, w), h0r, h0i, *w_args)
        y = y.reshape(bsz, 1, w)
    else:
        tt = min(tt, L)
        h0r, h0i = h0_re.reshape(bsz, 1, S5_LANES).astype(f32), h0_im.reshape(bsz, 1, S5_LANES).astype(f32)
        st_spec = pl.BlockSpec((None, 1, S5_LANES), lambda b, t: (b, 0, 0))
        y, hr, hi = pl.pallas_call(
            _s5_seq_body,
            grid=(bsz, L // tt),
            in_specs=[pl.BlockSpec((None, tt, w), lambda b, t: (b, t, 0)), st_spec, st_spec] + w_specs,
            out_specs=[pl.BlockSpec((None, tt, w), lambda b, t: (b, t, 0)), st_spec, st_spec],
            out_shape=[jax.ShapeDtypeStruct((bsz, L, w), f32), jax.ShapeDtypeStruct((bsz, 1, S5_LANES), f32),
                       jax.ShapeDtypeStruct((bsz, 1, S5_LANES), f32)],
            scratch_shapes=[pltpu.VMEM((tt, S5_LANES), f32), pltpu.VMEM((tt, S5_LANES), f32),
                            pltpu.VMEM((1, S5_LANES), f32), pltpu.VMEM((1, S5_LANES), f32)],
            compiler_params=pltpu.CompilerParams(
                dimension_semantics=("parallel", "arbitrary"), vmem_limit_bytes=VMEM_LIMIT_BYTES),
            name="s5_seq",
        )(u, h0r, h0i, *w_args)
    shp = (bsz, S5_GROUPS, S5_STATE)
    return y, hr.reshape(shp).astype(h0_re.dtype), hi.reshape(shp).astype(h0_im.dtype)


DSA_KEY_CHUNK = 512
DSA_NEAR = 2 * QBLOCK
INT32_MIN = -2 ** 31
NEG_BIG = -1e30
KT_ROWS = HEAD_DIM + 16


def _sortable_key(s):
    bits = lax.bitcast_convert_type(s, jnp.int32)
    return jnp.where(bits < 0, bits ^ jnp.int32(0x7FFFFFFF), bits)


def _dsa_prompt_body(n_top, rb_ref, q_ref, qi_ref, kiwi_ref, kt_ref, v_ref, kit_ref, o_ref, keys_ref, bias_ref, bound_ref, hi_ref, lo_ref):
    i = pl.program_id(1)
    f32, bf16 = jnp.float32, jnp.bfloat16
    kc = DSA_KEY_CHUNK
    q_start = i * QBLOCK
    n_all = (q_start + QBLOCK + kc - 1) // kc
    near_start = jnp.maximum(q_start - QBLOCK, 0)
    n_far = (near_start + kc - 1) // kc
    row = lax.broadcasted_iota(jnp.int32, (QBLOCK, 1), 0)
    qpos = q_start + row

    @pl.when(i == 0)
    def _():
        r = lax.broadcasted_iota(jnp.int32, (QBLOCK, DSA_NEAR), 0)
        c = lax.broadcasted_iota(jnp.int32, (QBLOCK, DSA_NEAR), 1)
        for tile in range(2):
            bucket = t5_bucket(r + tile * QBLOCK - c)
            for h in range(DSA_HEADS):
                b = jnp.zeros((QBLOCK, DSA_NEAR), f32)
                for bk in range(N_BUCKETS):
                    b = jnp.where(bucket == bk, rb_ref[bk, h] - rb_ref[N_BUCKETS - 1, h], b)
                kv, g = divmod(h, DSA_GQA)
                bias_ref[tile, kv, g * QBLOCK:(g + 1) * QBLOCK, :] = b
        lane = lax.broadcasted_iota(jnp.int32, (1, LANES), 1)
        bound = jnp.zeros((1, LANES), f32)
        for kv in range(DSA_KV_HEADS):
            kk = kt_ref[kv * KT_ROWS:kv * KT_ROWS + HEAD_DIM, :].astype(f32)
            kmax = jnp.sqrt(jnp.max(jnp.sum(kk * kk, axis=0, keepdims=True), axis=-1, keepdims=True))
            bound = jnp.where(lane == kv, kmax, bound)
        for h in range(DSA_HEADS):
            kv, g = divmod(h, DSA_GQA)
            bmax = jnp.maximum(jnp.max(bias_ref[:, kv, g * QBLOCK:(g + 1) * QBLOCK, :]), 0.0)
            bound = jnp.where(lane == DSA_KV_HEADS + h, bmax, bound)
        bound_ref[...] = bound

    qi_all = (qi_ref[...].astype(f32) * IDX_DIM ** -0.5).astype(bf16)
    qi = jnp.concatenate([qi_all[:, h * IDX_DIM:(h + 1) * IDX_DIM] for h in range(IDX_HEADS)], axis=0)
    wi = kiwi_ref[:, IDX_DIM:IDX_DIM + IDX_HEADS] * IDX_HEADS ** -0.5

    def score_chunk(c, _):
        off = pl.multiple_of(c * kc, kc)
        d = jnp.dot(qi, kit_ref[:, pl.ds(off, kc)], preferred_element_type=f32)
        s = jnp.zeros((QBLOCK, kc), f32)
        for h in range(IDX_HEADS):
            s = s + wi[:, h:h + 1] * jnp.maximum(d[h * QBLOCK:(h + 1) * QBLOCK], 0.0)
        kpos = off + lax.broadcasted_iota(jnp.int32, (QBLOCK, kc), 1)
        s = jnp.where(s == 0.0, 0.0, s)
        s = jnp.where(kpos <= qpos, s, -jnp.inf)
        key = _sortable_key(s)
        keys_ref[:, pl.ds(off, kc)] = key
        hi_ref[:, pl.ds(off, kc)] = lax.shift_right_arithmetic(key, 16).astype(jnp.int16)
        return 0

    lax.fori_loop(0, n_all, score_chunk, 0)

    def count_where(pred_fn):
        def body(c, acc):
            off = pl.multiple_of(c * kc, kc)
            hit = pred_fn(keys_ref[:, pl.ds(off, kc)], off)
            part = jnp.where(hit, 1.0, 0.0)
            for j in range(kc // 128):
                acc = acc + part[:, j * 128:(j + 1) * 128]
            return acc
        acc = lax.fori_loop(0, n_all, body, jnp.zeros((QBLOCK, 128), f32))
        return jnp.sum(acc, axis=-1, keepdims=True)

    def count16(ref, pred_fn):
        def body(c, acc):
            off = pl.multiple_of(c * kc, kc)
            part = jnp.where(pred_fn(ref[:, pl.ds(off, kc)]), jnp.int16(1), jnp.int16(0))
            return acc + ((part[:, 0:128] + part[:, 128:256]) + (part[:, 256:384] + part[:, 384:512]))
        acc = lax.fori_loop(0, n_all, body, jnp.zeros((QBLOCK, 128), jnp.int16))
        return jnp.sum(acc.astype(f32), axis=-1, keepdims=True)

    def search16(ref, want):
        def bit(it, t):
            cand = t + lax.shift_left(jnp.int32(1), 15 - it)
            c16 = cand.astype(jnp.int16)
            return jnp.where(count16(ref, lambda x: x >= c16) >= want, cand, t)
        return lax.fori_loop(0, 16, bit, jnp.full((QBLOCK, 1), -2 ** 15, jnp.int32))

    t_hi = search16(hi_ref, n_top)
    t_hi16 = t_hi.astype(jnp.int16)
    n_above = count16(hi_ref, lambda x: x > t_hi16)

    def band_chunk(c, _):
        off = pl.multiple_of(c * kc, kc)
        key = keys_ref[:, pl.ds(off, kc)]
        lo = ((key & 0xFFFF) - 2 ** 15).astype(jnp.int16)
        lo_ref[:, pl.ds(off, kc)] = jnp.where(lax.shift_right_arithmetic(key, 16) == t_hi, lo, jnp.int16(-2 ** 15))
        return 0

    lax.fori_loop(0, n_all, band_chunk, 0)
    t_lo = search16(lo_ref, n_top - n_above)
    thr = lax.shift_left(t_hi, 16) + (t_lo + 2 ** 15)

    def is_valid(off, width):
        return (off + lax.broadcasted_iota(jnp.int32, (QBLOCK, width), 1)) <= qpos

    n_gt = count_where(lambda k, off: (k > thr) & is_valid(off, kc))
    n_eq = count_where(lambda k, off: (k == thr) & is_valid(off, kc))
    need = n_top - n_gt
    has_extra_ties = jnp.max(jnp.where(n_eq > need, 1.0, 0.0)) > 0.0

    def tie_search():
        def idx_bit(it, j):
            cand = j + lax.shift_left(jnp.int32(1), 13 - it)
            cnt = count_where(lambda k, off: (k == thr) & is_valid(off, kc)
                              & ((off + lax.broadcasted_iota(jnp.int32, (QBLOCK, kc), 1)) < cand))
            return jnp.where(cnt <= need, cand, j)
        return lax.fori_loop(0, 14, idx_bit, jnp.zeros((QBLOCK, 1), jnp.int32))

    tie_end = lax.cond(has_extra_ties, tie_search, lambda: jnp.full((QBLOCK, 1), 2 ** 14, jnp.int32))

    def selected(keys, off, width):
        kpos = off + lax.broadcasted_iota(jnp.int32, (QBLOCK, width), 1)
        return (kpos <= qpos) & ((keys > thr) | ((keys == thr) & (kpos < tie_end))), kpos

    tile = jnp.minimum(i, 1)
    q_all = (q_ref[...].astype(f32) * HEAD_DIM ** -0.5).astype(bf16)
    qh = [q_all[:, h * HEAD_DIM:(h + 1) * HEAD_DIM] for h in range(DSA_HEADS)]
    off_near = pl.multiple_of(near_start, QBLOCK)
    sel_near, _ = selected(keys_ref[:, pl.ds(off_near, DSA_NEAR)], off_near, DSA_NEAR)

    def near_bias(h):
        kv, g = divmod(h, DSA_GQA)
        return jnp.where(sel_near, bias_ref[tile, kv, g * QBLOCK:(g + 1) * QBLOCK, :], NEG_BIG)

    def far_mask(c):
        off = pl.multiple_of(c * kc, kc)
        sel, kpos = selected(keys_ref[:, pl.ds(off, kc)], off, kc)
        return off, jnp.where(sel & (kpos < near_start), 0.0, NEG_BIG)

    def kt_block(kv, off, width, rows):
        return kt_ref[kv * KT_ROWS:kv * KT_ROWS + rows, pl.ds(off, width)]

    bound = bound_ref[...]
    lane = lax.broadcasted_iota(jnp.int32, (1, LANES), 1)
    pick = lambda idx: jnp.sum(jnp.where(lane == idx, bound, 0.0), axis=-1, keepdims=True)
    col = lax.broadcasted_iota(jnp.int32, (QBLOCK, KT_ROWS - HEAD_DIM), 1)
    q_aug = []
    for h in range(DSA_HEADS):
        qf = qh[h].astype(f32)
        ub = jnp.sqrt(jnp.sum(qf * qf, axis=-1, keepdims=True)) * pick(h // DSA_GQA) * 1.01 + pick(DSA_KV_HEADS + h)
        q_aug.append(jnp.concatenate([qh[h], jnp.where(col == 0, -ub, 0.0).astype(bf16)], axis=-1))

    def fast_attend(accs, off, width, mask_bias):
        accs = list(accs)
        for kv in range(DSA_KV_HEADS):
            kt = kt_block(kv, off, width, KT_ROWS)
            ps = [jnp.exp(jnp.dot(q_aug[kv * DSA_GQA + g], kt, preferred_element_type=f32)
                          + mask_bias(kv * DSA_GQA + g)).astype(bf16) for g in range(DSA_GQA)]
            accs[kv] = accs[kv] + jnp.dot(jnp.concatenate(ps, axis=0), v_ref[kv, pl.ds(off, width), :],
                                          preferred_element_type=f32)
        return tuple(accs)

    def fast_chunk(c, accs):
        off, mb = far_mask(c)
        return fast_attend(accs, off, kc, lambda h: mb)

    zero_acc = tuple(jnp.zeros((DSA_GQA * QBLOCK, 2 * HEAD_DIM), f32) for _ in range(DSA_KV_HEADS))
    accs = fast_attend(lax.fori_loop(0, n_far, fast_chunk, zero_acc), off_near, DSA_NEAR, near_bias)
    l_min = accs[0][:, HEAD_DIM:]
    for kv in range(1, DSA_KV_HEADS):
        l_min = jnp.minimum(l_min, accs[kv][:, HEAD_DIM:])
    fast_ok = jnp.min(l_min) > 1e-30

    def safe_attend(carry, off, width, mask_bias):
        ms, accs = list(carry[0]), list(carry[1])
        for kv in range(DSA_KV_HEADS):
            kt = kt_block(kv, off, width, HEAD_DIM)
            ps, alphas = [], []
            for g in range(DSA_GQA):
                h = kv * DSA_GQA + g
                logits = jnp.dot(qh[h], kt, preferred_element_type=f32) + mask_bias(h)
                m_new = jnp.maximum(ms[h], jnp.max(logits, axis=-1, keepdims=True))
                alphas.append(jnp.exp(ms[h] - m_new))
                ps.append(jnp.exp(logits - m_new).astype(bf16))
                ms[h] = m_new
            accs[kv] = (jnp.concatenate(alphas, axis=0) * accs[kv]
                        + jnp.dot(jnp.concatenate(ps, axis=0), v_ref[kv, pl.ds(off, width), :], preferred_element_type=f32))
        return tuple(ms), tuple(accs)

    def safe_path():
        def safe_chunk(c, carry):
            off, mb = far_mask(c)
            return safe_attend(carry, off, kc, lambda h: mb)
        init = (tuple(jnp.full((QBLOCK, 1), NEG_BIG, f32) for _ in range(DSA_HEADS)), zero_acc)
        return safe_attend(lax.fori_loop(0, n_far, safe_chunk, init), off_near, DSA_NEAR, near_bias)[1]

    accs = lax.cond(fast_ok, lambda: accs, safe_path)
    outs = []
    for kv in range(DSA_KV_HEADS):
        o = accs[kv][:, :HEAD_DIM] / accs[kv][:, HEAD_DIM:]
        outs += [o[g * QBLOCK:(g + 1) * QBLOCK] for g in range(DSA_GQA)]
    o_ref[...] = jnp.concatenate(outs, axis=-1)


def dsa_prompt_pallas(q, qi, kiwi, k, v, rel_bias):
    bsz, L = q.shape[:2]
    nq = L // QBLOCK
    n_top = min(TOPK_MAX, L // 4)
    f32, bf16 = jnp.float32, jnp.bfloat16
    k_t = k.astype(bf16).reshape(bsz, L, DSA_KV_HEADS, HEAD_DIM).transpose(0, 2, 3, 1)
    k_pad = jnp.zeros((bsz, DSA_KV_HEADS, KT_ROWS - HEAD_DIM, L), bf16).at[:, :, 0, :].set(1.0)
    k_t = jnp.concatenate([k_t, k_pad], axis=2).reshape(bsz, DSA_KV_HEADS * KT_ROWS, L)
    v_h = v.astype(bf16).reshape(bsz, L, DSA_KV_HEADS, HEAD_DIM).swapaxes(1, 2)
    v_h = jnp.concatenate([v_h, jnp.ones_like(v_h)], axis=-1)
    ki_t = kiwi[..., :IDX_DIM].astype(bf16).swapaxes(1, 2)
    lk = max(L, DSA_KEY_CHUNK)
    if lk != L:
        k_t = jnp.pad(k_t, ((0, 0), (0, 0), (0, lk - L)))
        v_h = jnp.pad(v_h, ((0, 0), (0, 0), (0, lk - L), (0, 0)))
        ki_t = jnp.pad(ki_t, ((0, 0), (0, 0), (0, lk - L)))
    tok = lambda n: pl.BlockSpec((None, QBLOCK, n), lambda b, i: (b, i, 0))
    return pl.pallas_call(
        functools.partial(_dsa_prompt_body, n_top),
        grid=(bsz, nq),
        in_specs=[
            pl.BlockSpec(memory_space=pltpu.SMEM),
            tok(SEQ_WIDTH), tok(IDX_HEADS * IDX_DIM), tok(LANES),
            pl.BlockSpec((None, DSA_KV_HEADS * KT_ROWS, lk), lambda b, i: (b, 0, 0), pipeline_mode=pl.Buffered(1)),
            pl.BlockSpec((None, DSA_KV_HEADS, lk, 2 * HEAD_DIM), lambda b, i: (b, 0, 0, 0), pipeline_mode=pl.Buffered(1)),
            pl.BlockSpec((None, IDX_DIM, lk), lambda b, i: (b, 0, 0), pipeline_mode=pl.Buffered(1)),
        ],
        out_specs=tok(SEQ_WIDTH),
        out_shape=jax.ShapeDtypeStruct((bsz, L, SEQ_WIDTH), f32),
        scratch_shapes=[pltpu.VMEM((QBLOCK, lk), jnp.int32),
                        pltpu.VMEM((2, DSA_KV_HEADS, DSA_GQA * QBLOCK, DSA_NEAR), f32),
                        pltpu.VMEM((1, LANES), f32),
                        pltpu.VMEM((QBLOCK, lk), jnp.int16), pltpu.VMEM((QBLOCK, lk), jnp.int16)],
        compiler_params=pltpu.CompilerParams(
            dimension_semantics=("parallel", "arbitrary"), vmem_limit_bytes=VMEM_LIMIT_BYTES),
        name="dsa_prompt",
    )(rel_bias.astype(f32), q, qi, kiwi, k_t, v_h, ki_t)


DEC_PAGES_PER_STEP = 16
KV_WIDTH = DSA_KV_HEADS * HEAD_DIM


def _index_score(qi, wi, ki):
    d = _dot_nt(qi, ki)
    s = jnp.sum(wi * jnp.maximum(d, 0.0), axis=0, keepdims=True)
    return jnp.where(s == 0.0, 0.0, s)


def _dsa_dec_score_body(pt_ref, qi_ref, wi_ref, *refs):
    pages, s_ref = refs[:-1], refs[-1]
    qi = (qi_ref[...].astype(jnp.float32) * IDX_DIM ** -0.5).astype(jnp.bfloat16)
    wi = wi_ref[...]
    s_ref[...] = jnp.concatenate([_index_score(qi, wi, p[...].astype(jnp.bfloat16)) for p in pages], axis=0)


def _dsa_dec_attend_body(n_top, n_pages, pt_ref, rb_ref, q_ref, qi_ref, wi_ref, knew_ref, vnew_ref, kinew_ref, s_ref,
                         *refs):
    pp = DEC_PAGES_PER_STEP if n_pages >= DEC_PAGES_PER_STEP else n_pages
    k_pages, v_pages = refs[:pp], refs[pp:2 * pp]
    o_ref, thr_ref, tie_ref, snew_ref, m_ref, l_ref, acc_ref = refs[2 * pp:]
    f32, bf16 = jnp.float32, jnp.bfloat16
    t = pl.program_id(1)
    nt = pl.num_programs(1)
    past = n_pages * PAGE_SIZE
    page_i = lax.broadcasted_iota(jnp.int32, (n_pages, LANES), 0)
    lane_i = lax.broadcasted_iota(jnp.int32, (n_pages, LANES), 1)
    kpos_all = page_i * PAGE_SIZE + lane_i
    total = lambda x: jnp.sum(jnp.sum(x, axis=1, keepdims=True), axis=0, keepdims=True)

    @pl.when(t == 0)
    def _():
        qi = (qi_ref[...].astype(f32) * IDX_DIM ** -0.5).astype(bf16)
        d_new = jnp.sum(qi.astype(f32) * kinew_ref[...].astype(bf16).astype(f32), axis=-1, keepdims=True)
        s_new = jnp.sum(wi_ref[...] * jnp.maximum(d_new, 0.0), axis=0, keepdims=True)
        s_new = jnp.where(s_new == 0.0, 0.0, s_new)
        key_new = _sortable_key(s_new)
        keys = _sortable_key(s_ref[...])

        def count(pred_past, pred_new):
            return total(jnp.where(pred_past, 1.0, 0.0)) + jnp.where(pred_new, 1.0, 0.0)

        def thr_bit(it, thr):
            cand = thr + lax.shift_left(jnp.int32(1), 31 - it)
            return jnp.where(count(keys >= cand, key_new >= cand) >= n_top, cand, thr)

        thr = lax.fori_loop(0, 32, thr_bit, jnp.full((1, 1), INT32_MIN, jnp.int32))
        need = n_top - count(keys > thr, key_new > thr)

        def idx_bit(it, j):
            cand = j + lax.shift_left(jnp.int32(1), 14 - it)
            cnt = count((keys == thr) & (kpos_all < cand), (key_new == thr) & (past < cand))
            return jnp.where(cnt <= need, cand, j)

        n_eq = count(keys == thr, key_new == thr)
        tie_end = lax.cond(jnp.max(jnp.where(n_eq > need, 1.0, 0.0)) > 0.0,
                           lambda: lax.fori_loop(0, 15, idx_bit, jnp.zeros((1, 1), jnp.int32)),
                           lambda: jnp.full((1, 1), 2 ** 15, jnp.int32))
        thr_ref[...] = thr
        tie_ref[...] = tie_end
        snew_ref[...] = s_new
        m_ref[...] = jnp.full(m_ref.shape, NEG_BIG, f32)
        l_ref[...] = jnp.zeros(l_ref.shape, f32)
        acc_ref[...] = jnp.zeros(acc_ref.shape, f32)

    thr, tie_end = thr_ref[...], tie_ref[...]
    q = (q_ref[...].astype(f32) * HEAD_DIM ** -0.5).astype(bf16)
    lane12 = lax.broadcasted_iota(jnp.int32, (DSA_HEADS, LANES), 1)

    def head_bias(dist):
        bucket = t5_bucket(dist)
        b = jnp.zeros(dist.shape, f32)
        for h in range(DSA_HEADS):
            row = jnp.zeros(dist.shape, f32)
            for bk in range(N_BUCKETS):
                row = jnp.where(bucket == bk, rb_ref[bk, h] - rb_ref[N_BUCKETS - 1, h], row)
            b = jnp.where(lax.broadcasted_iota(jnp.int32, dist.shape, 0) == h, row, b)
        return b

    keys_step = _sortable_key(s_ref[pl.ds(pl.multiple_of(t * pp, pp), pp), :])
    tiles = []
    for j in range(pp):
        kpos = (t * pp + j) * PAGE_SIZE + lax.broadcasted_iota(jnp.int32, (1, LANES), 1)
        keys = keys_step[j:j + 1]
        sel = (keys > thr) | ((keys == thr) & (kpos < tie_end))
        logits = _dot_nt(q, k_pages[j][...].astype(bf16))
        if j == pp - 1:
            near = head_bias(past - (t * pp + j) * PAGE_SIZE - lane12)
            logits = logits + jnp.where(t == nt - 1, near, 0.0)
        tiles.append(jnp.where(sel, logits, NEG_BIG))
    m = m_ref[...]
    tile_max = tiles[0]
    for x in tiles[1:]:
        tile_max = jnp.maximum(tile_max, x)
    m_new = jnp.maximum(m, jnp.max(tile_max, axis=-1, keepdims=True))
    alpha = jnp.exp(m - m_new)
    ps = [jnp.exp(x - m_new) for x in tiles]
    p_sum = ps[0]
    for x in ps[1:]:
        p_sum = p_sum + x
    pv = jnp.dot(ps[0].astype(bf16), v_pages[0][...].astype(bf16), preferred_element_type=f32)
    for j in range(1, pp):
        pv = pv + jnp.dot(ps[j].astype(bf16), v_pages[j][...].astype(bf16), preferred_element_type=f32)
    l_ref[...] = alpha * l_ref[...] + jnp.sum(p_sum, axis=-1, keepdims=True)
    acc_ref[...] = alpha * acc_ref[...] + pv
    m_ref[...] = m_new

    @pl.when(t == nt - 1)
    def _():
        s_new = snew_ref[...]
        key_new = _sortable_key(s_new)
        sel_new = (key_new > thr) | ((key_new == thr) & (past < tie_end))
        k_new = knew_ref[...].astype(bf16).astype(f32)
        logit_new = jnp.sum(q.astype(f32) * k_new, axis=-1, keepdims=True)
        logit_new = logit_new + head_bias(jnp.zeros((DSA_HEADS, LANES), jnp.int32))[:, :1]
        logit_new = jnp.where(sel_new, logit_new, NEG_BIG)
        m = m_ref[...]
        m_new = jnp.maximum(m, logit_new)
        alpha = jnp.exp(m - m_new)
        p_new = jnp.exp(logit_new - m_new)
        l = alpha * l_ref[...] + p_new
        v_new = vnew_ref[...].astype(bf16).astype(f32)
        acc = alpha * acc_ref[...] + p_new.astype(bf16).astype(f32) * v_new
        o = acc / l
        o_ref[...] = jnp.concatenate(
            [o[h:h + 1, (h // DSA_GQA) * HEAD_DIM:(h // DSA_GQA + 1) * HEAD_DIM] for h in range(DSA_HEADS)], axis=-1)


def dsa_decode_pallas(q, qi, kiwi, k_new, v_new, pool_k, pool_v, pool_ki, page_table, rel_bias):
    bsz = q.shape[0]
    n_pages = page_table.shape[1]
    n_phys = pool_k.shape[0]
    pp = DEC_PAGES_PER_STEP if n_pages >= DEC_PAGES_PER_STEP else n_pages
    nt = n_pages // pp
    n_top = min(TOPK_MAX, (n_pages * PAGE_SIZE + 1) // 4)
    f32, bf16 = jnp.float32, jnp.bfloat16
    eye = jnp.eye(DSA_KV_HEADS, dtype=q.dtype)
    q_bd = jnp.einsum('bkgd,kj->bkgjd', q.reshape(bsz, DSA_KV_HEADS, DSA_GQA, HEAD_DIM), eye)
    q_bd = q_bd.reshape(bsz, DSA_HEADS, KV_WIDTH)
    qi3 = qi.reshape(bsz, IDX_HEADS, IDX_DIM)
    wi3 = (kiwi[:, IDX_DIM:IDX_DIM + IDX_HEADS] * IDX_HEADS ** -0.5).reshape(bsz, IDX_HEADS, 1)
    ki_new = kiwi[:, :IDX_DIM].reshape(bsz, 1, IDX_DIM)
    pk = pool_k.reshape(n_phys, PAGE_SIZE, KV_WIDTH)
    pv = pool_v.reshape(n_phys, PAGE_SIZE, KV_WIDTH)
    per_b = lambda *shape: pl.BlockSpec((None,) + shape, lambda b, t, pt: (b,) + (0,) * len(shape))
    page = lambda width, j: pl.BlockSpec((None, PAGE_SIZE, width), lambda b, t, pt: (pt[b, t * pp + j], 0, 0))
    scores = pl.pallas_call(
        _dsa_dec_score_body,
        grid_spec=pltpu.PrefetchScalarGridSpec(
            num_scalar_prefetch=1, grid=(bsz, nt),
            in_specs=[per_b(IDX_HEADS, IDX_DIM), per_b(IDX_HEADS, 1)] + [page(IDX_DIM, j) for j in range(pp)],
            out_specs=pl.BlockSpec((None, pp, LANES), lambda b, t, pt: (b, t, 0))),
        out_shape=jax.ShapeDtypeStruct((bsz, n_pages, LANES), f32),
        compiler_params=pltpu.CompilerParams(
            dimension_semantics=("parallel", "parallel"), vmem_limit_bytes=VMEM_LIMIT_BYTES),
        name="dsa_dec_score",
    )(page_table, qi3, wi3, *([pool_ki] * pp))
    one = lambda: pltpu.VMEM((1, 1), jnp.int32)
    o = pl.pallas_call(
        functools.partial(_dsa_dec_attend_body, n_top, n_pages),
        grid_spec=pltpu.PrefetchScalarGridSpec(
            num_scalar_prefetch=1, grid=(bsz, nt),
            in_specs=[pl.BlockSpec(memory_space=pltpu.SMEM), per_b(DSA_HEADS, KV_WIDTH), per_b(IDX_HEADS, IDX_DIM),
                      per_b(IDX_HEADS, 1), per_b(1, KV_WIDTH), per_b(1, KV_WIDTH), per_b(1, IDX_DIM),
                      per_b(n_pages, LANES)]
            + [page(KV_WIDTH, j) for j in range(pp)] * 2,
            out_specs=per_b(1, SEQ_WIDTH),
            scratch_shapes=[one(), one(), pltpu.VMEM((1, 1), f32), pltpu.VMEM((DSA_HEADS, 1), f32),
                            pltpu.VMEM((DSA_HEADS, 1), f32), pltpu.VMEM((DSA_HEADS, KV_WIDTH), f32)]),
        out_shape=jax.ShapeDtypeStruct((bsz, 1, SEQ_WIDTH), f32),
        compiler_params=pltpu.CompilerParams(
            dimension_semantics=("parallel", "arbitrary"), vmem_limit_bytes=VMEM_LIMIT_BYTES),
        name="dsa_dec_attend",
    )(page_table, rel_bias.astype(f32), q_bd, qi3, wi3, k_new.reshape(bsz, 1, KV_WIDTH),
      v_new.reshape(bsz, 1, KV_WIDTH), ki_new, scores, *([pk] * pp), *([pv] * pp))
    return o


CONV_HALO = 8


def _gdn_prep_body(x_ref, ab_ref, ctx_ref, w_ref, alog_ref, dtb_ref, q_ref, k_ref, v_ref, gb_ref, cs_ref, xp_ref):
    tt = x_ref.shape[0]
    halo = CONV_W - 1

    @pl.when(pl.program_id(1) == 0)
    def _():
        xp_ref[CONV_HALO - halo:CONV_HALO, :] = ctx_ref[...]

    x = x_ref[...]
    xp_ref[CONV_HALO:CONV_HALO + tt, :] = x
    w = w_ref[...]
    y = xp_ref[CONV_HALO - halo:CONV_HALO - halo + tt, :] * w[0:1]
    for j in range(1, CONV_W):
        y = y + xp_ref[CONV_HALO - halo + j:CONV_HALO - halo + j + tt, :] * w[j:j + 1]
    last = x[tt - halo:, :]
    xp_ref[CONV_HALO - halo:CONV_HALO, :] = last
    cs_ref[...] = last
    y = y * _sigmoid(y)

    def l2n(a, scale):
        cols = []
        for h in range(GDN_HEADS):
            s = a[:, h * HEAD_DIM:(h + 1) * HEAD_DIM]
            cols.append(s * (lax.rsqrt(jnp.sum(s * s, axis=-1, keepdims=True) + EPS) * scale))
        return jnp.concatenate(cols, axis=-1)

    q_ref[...] = l2n(y[:, :SEQ_WIDTH], HEAD_DIM ** -0.5)
    k_ref[...] = l2n(y[:, SEQ_WIDTH:2 * SEQ_WIDTH], 1.0)
    v_ref[...] = y[:, 2 * SEQ_WIDTH:]
    ab = ab_ref[...]
    xa = ab + dtb_ref[...]
    softplus = jnp.maximum(xa, 0.0) + jnp.log(1.0 + jnp.exp(-jnp.abs(xa)))
    g = -jnp.exp(alog_ref[...]) * softplus
    lane = lax.broadcasted_iota(jnp.int32, ab.shape, 1)
    gb_ref[...] = jnp.where(lane < GDN_HEADS, g, _sigmoid(ab))


def gdn_prep_pallas(qkv, ab, ctx, conv_w, a_log, dt_bias, *, tt=256):
    bsz, L, ch = qkv.shape
    tt = min(tt, L)
    f32 = jnp.float32
    pad = lambda r: jnp.pad(r.astype(f32).reshape(1, -1), ((0, 0), (0, LANES - r.shape[-1])))
    tok = lambda n: pl.BlockSpec((None, tt, n), lambda b, t: (b, t, 0))
    const2 = lambda b, t: (0, 0)
    return pl.pallas_call(
        _gdn_prep_body,
        grid=(bsz, L // tt),
        in_specs=[tok(ch), tok(LANES), pl.BlockSpec((None, CONV_W - 1, ch), lambda b, t: (b, 0, 0)),
                  pl.BlockSpec((CONV_W, ch), const2), pl.BlockSpec((1, LANES), const2), pl.BlockSpec((1, LANES), const2)],
        out_specs=[tok(SEQ_WIDTH), tok(SEQ_WIDTH), tok(SEQ_WIDTH), tok(LANES),
                   pl.BlockSpec((None, CONV_W - 1, ch), lambda b, t: (b, 0, 0))],
        out_shape=[jax.ShapeDtypeStruct((bsz, L, SEQ_WIDTH), f32)] * 3
        + [jax.ShapeDtypeStruct((bsz, L, LANES), f32), jax.ShapeDtypeStruct((bsz, CONV_W - 1, ch), f32)],
        scratch_shapes=[pltpu.VMEM((CONV_HALO + tt, ch), f32)],
        compiler_params=pltpu.CompilerParams(
            dimension_semantics=("parallel", "arbitrary"), vmem_limit_bytes=VMEM_LIMIT_BYTES),
        name="gdn_prep",
    )(qkv, ab, ctx.astype(f32), conv_w.astype(f32), pad(a_log), pad(dt_bias))


def _split3(a):
    bf16, f32 = jnp.bfloat16, jnp.float32
    h = a.astype(bf16)
    r = a - h.astype(f32)
    m = r.astype(bf16)
    return h, m, (r - m.astype(f32)).astype(bf16)


def _mm_hi(a, b):
    f32 = jnp.float32
    ah, am, _ = _split3(a)
    bh, bm, _ = _split3(b)
    d = lambda x, y: jnp.dot(x, y, preferred_element_type=f32)
    return d(ah, bh) + (d(ah, bm) + d(am, bh))


def _mm_sel(sel, b):
    f32 = jnp.float32
    s = sel.astype(jnp.bfloat16)
    bh, bm, bl = _split3(b)
    d = lambda y: jnp.dot(s, y, preferred_element_type=f32)
    return d(bh) + (d(bm) + d(bl))


def _dot_nt(a, b):
    return lax.dot_general(a, b, (((1,), (1,)), ((), ())), preferred_element_type=jnp.float32)


def _dot_tn(a, b):
    return lax.dot_general(a, b, (((0,), (0,)), ((), ())), preferred_element_type=jnp.float32)


def _gdn_local_body(q_ref, k_ref, v_ref, gb_ref, uv_ref, wk_ref, qh_ref, kt_ref, qk_ref):
    f32, bf16 = jnp.float32, jnp.bfloat16
    c = GDN_CHUNK
    r_i = lax.broadcasted_iota(jnp.int32, (c, c), 0)
    c_i = lax.broadcasted_iota(jnp.int32, (c, c), 1)
    tril = r_i >= c_i
    stril = r_i > c_i
    triu = r_i <= c_i
    eye = jnp.where(r_i == c_i, 1.0, 0.0)
    ones = jnp.ones((c, c), jnp.bool_)
    lane = lax.broadcasted_iota(jnp.int32, (c, LANES), 1)
    gb = gb_ref[...]
    heads = range(GDN_HEADS)
    sl = [slice(h * HEAD_DIM, (h + 1) * HEAD_DIM) for h in heads]
    pick = lambda idx: jnp.sum(jnp.where(lane == idx, gb, 0.0), axis=-1, keepdims=True)
    g_col = [pick(h) for h in heads]
    beta = [pick(GDN_HEADS + h) for h in heads]
    q = [q_ref[:, sl[h]] for h in heads]
    k = [k_ref[:, sl[h]] for h in heads]
    v = [v_ref[:, sl[h]] for h in heads]
    g_mat = [jnp.broadcast_to(g_col[h], (c, c)) for h in heads]
    cum_col = [_mm_sel(tril, g_mat[h]) for h in heads]
    cum_row = [_mm_sel(ones, jnp.where(triu, g_mat[h], 0.0)) for h in heads]
    gam = [jnp.where(tril, jnp.exp(jnp.where(tril, cum_col[h] - cum_row[h], 0.0)), 0.0) for h in heads]
    kb = [k[h].astype(bf16) for h in heads]
    a_mat = [jnp.where(stril, beta[h] * _dot_nt(kb[h], kb[h]) * gam[h], 0.0) for h in heads]
    t_inv = [eye - a_mat[h] for h in heads]
    pw = a_mat
    for _ in range(5):
        pw = [_mm_hi(pw[h], pw[h]) for h in heads]
        t_inv = [t_inv[h] + _mm_hi(t_inv[h], pw[h]) for h in heads]
    gc = [cum_col[h][:, :1] for h in heads]
    egc = [jnp.exp(gc[h]) for h in heads]
    tw = [_mm_hi(t_inv[h], jnp.concatenate([beta[h] * v[h], (beta[h] * egc[h]) * k[h]], axis=-1)) for h in heads]
    qk = [jnp.where(tril, _dot_nt(q[h].astype(bf16), kb[h]) * gam[h], 0.0) for h in heads]
    cat = lambda xs: jnp.concatenate(xs, axis=-1)
    uv_ref[...] = cat([tw[h][:, :HEAD_DIM] for h in heads])
    wk_ref[...] = cat([tw[h][:, HEAD_DIM:] for h in heads]).astype(bf16)
    qh_ref[...] = cat([egc[h] * q[h] for h in heads]).astype(bf16)
    eye_b = eye.astype(bf16)
    kt = [(jnp.exp(gc[h][c - 1:c, :] - gc[h]) * k[h]).astype(bf16) for h in heads]
    kt_ref[...] = cat([_dot_tn(kt[h], eye_b) for h in heads]).astype(bf16)
    qk_ref[...] = cat(qk).astype(bf16)


def gdn_local_pallas(qn, kn, vv, gb):
    bsz, L, w = qn.shape
    c = GDN_CHUNK
    f32, bf16 = jnp.float32, jnp.bfloat16
    tok = lambda n: pl.BlockSpec((None, c, n), lambda b, t: (b, t, 0))
    return pl.pallas_call(
        _gdn_local_body,
        grid=(bsz, L // c),
        in_specs=[tok(w), tok(w), tok(w), tok(LANES)],
        out_specs=[tok(w)] * 5,
        out_shape=[jax.ShapeDtypeStruct((bsz, L, w), f32)] + [jax.ShapeDtypeStruct((bsz, L, w), bf16)] * 4,
        compiler_params=pltpu.CompilerParams(
            dimension_semantics=("parallel", "parallel"), vmem_limit_bytes=VMEM_LIMIT_BYTES),
        name="gdn_local",
    )(qn, kn, vv, gb)


def _gdn_scan_body(uv_ref, wk_ref, qh_ref, ktt_ref, qk_ref, gb_ref, gate_ref, s0_ref, on_ref, o_ref, sf_ref, s_ref):
    f32, bf16 = jnp.float32, jnp.bfloat16
    c = GDN_CHUNK
    tt = uv_ref.shape[0]
    heads = range(GDN_HEADS)
    sl = [slice(h * HEAD_DIM, (h + 1) * HEAD_DIM) for h in heads]

    @pl.when(pl.program_id(1) == 0)
    def _():
        s_ref[...] = s0_ref[...]

    lane = lax.broadcasted_iota(jnp.int32, (1, LANES), 1)
    dot = lambda a, b: jnp.dot(a, b, preferred_element_type=f32)

    def chunk(ci, _):
        rows = pl.ds(pl.multiple_of(ci * c, c), c)
        eg_last = jnp.exp(jnp.sum(gb_ref[rows, :], axis=0, keepdims=True))
        eg = [jnp.sum(jnp.where(lane == h, eg_last, 0.0), axis=-1, keepdims=True) for h in heads]
        st = [s_ref[h] for h in heads]
        sb = [st[h].astype(bf16) for h in heads]
        u = [uv_ref[rows, sl[h]] - dot(wk_ref[rows, sl[h]], sb[h]) for h in heads]
        ub = [u[h].astype(bf16) for h in heads]
        new = [eg[h] * st[h] + dot(ktt_ref[rows, sl[h]], ub[h]) for h in heads]
        for h in heads:
            s_ref[h] = new[h]
        o = [dot(qh_ref[rows, sl[h]], sb[h]) + dot(qk_ref[rows, sl[h]], ub[h]) for h in heads]
        o = [o[h] * lax.rsqrt(jnp.mean(o[h] * o[h], axis=-1, keepdims=True) + EPS) for h in heads]
        gt = gate_ref[rows, :]
        o_ref[rows, :] = (jnp.concatenate(o, axis=-1) * on_ref[...]) * (gt * _sigmoid(gt))
        return 0

    lax.fori_loop(0, tt // c, chunk, 0)
    sf_ref[...] = s_ref[...]


def gdn_scan_pallas(uv, wk, qh, kt, qk, gb, gate, s0, o_norm, *, tt=256):
    bsz, L, w = uv.shape
    tt = min(tt, L)
    f32 = jnp.float32
    tok = lambda n: pl.BlockSpec((None, tt, n), lambda b, t: (b, t, 0))
    st = pl.BlockSpec((None, GDN_HEADS, HEAD_DIM, HEAD_DIM), lambda b, t: (b, 0, 0, 0))
    on = jnp.tile(o_norm.astype(f32), GDN_HEADS).reshape(1, w)
    return pl.pallas_call(
        _gdn_scan_body,
        grid=(bsz, L // tt),
        in_specs=[tok(w)] * 5 + [tok(LANES), tok(w), st, pl.BlockSpec((1, w), lambda b, t: (0, 0))],
        out_specs=[tok(w), st],
        out_shape=[jax.ShapeDtypeStruct((bsz, L, w), f32), jax.ShapeDtypeStruct(s0.shape, f32)],
        scratch_shapes=[pltpu.VMEM((GDN_HEADS, HEAD_DIM, HEAD_DIM), f32)],
        compiler_params=pltpu.CompilerParams(
            dimension_semantics=("parallel", "arbitrary"), vmem_limit_bytes=VMEM_LIMIT_BYTES),
        name="gdn_scan",
    )(uv, wk, qh, kt, qk, gb, gate, s0.astype(f32).swapaxes(2, 3), on)


def gdn_mix_pallas(qkv, ab, gate, conv_ctx, s0, conv_w, a_log, dt_bias, o_norm):
    qn, kn, vv, gb, conv_state = gdn_prep_pallas(qkv, ab, conv_ctx, conv_w, a_log, dt_bias)
    uv, wk, qh, kt, qk = gdn_local_pallas(qn, kn, vv, gb)
    o, s_fin_t = gdn_scan_pallas(uv, wk, qh, kt, qk, gb, gate, s0, o_norm)
    return o, conv_state.astype(qkv.dtype), s_fin_t.swapaxes(2, 3).astype(s0.dtype)


def _gdn_dec_body(x_ref, ab_ref, gate_ref, ctx_ref, s_ref, w_ref, alog_ref, dtb_ref, on_ref, o_ref, cs_ref, sn_ref):
    f32, bf16 = jnp.float32, jnp.bfloat16
    rnd = lambda a: a.astype(bf16).astype(f32)
    x, ctx, w = x_ref[...], ctx_ref[...], w_ref[...]
    halo = CONV_W - 1
    y = x * w[halo:halo + 1]
    for j in range(halo):
        y = y + ctx[j:j + 1] * w[j:j + 1]
    y = y * _sigmoid(y)
    cs_ref[...] = jnp.concatenate([ctx[1:], x], axis=0)
    ab = ab_ref[...]
    xa = ab + dtb_ref[...]
    g_all = -jnp.exp(alog_ref[...]) * (jnp.maximum(xa, 0.0) + jnp.log(1.0 + jnp.exp(-jnp.abs(xa))))
    beta_all = _sigmoid(ab)
    lane = lax.broadcasted_iota(jnp.int32, (1, LANES), 1)
    pick = lambda a, idx: jnp.sum(jnp.where(lane == idx, a, 0.0), axis=-1, keepdims=True)
    r_i = lax.broadcasted_iota(jnp.int32, (HEAD_DIM, HEAD_DIM), 0)
    c_i = lax.broadcasted_iota(jnp.int32, (HEAD_DIM, HEAD_DIM), 1)
    eye = r_i == c_i
    to_col = lambda row: jnp.sum(jnp.where(eye, jnp.broadcast_to(row, (HEAD_DIM, HEAD_DIM)), 0.0), axis=-1, keepdims=True)
    to_row = lambda col: jnp.sum(jnp.where(eye, jnp.broadcast_to(col, (HEAD_DIM, HEAD_DIM)), 0.0), axis=0, keepdims=True)
    outs = []
    for h in range(GDN_HEADS):
        sl = slice(h * HEAD_DIM, (h + 1) * HEAD_DIM)
        q, k, v = y[:, sl], y[:, SEQ_WIDTH + h * HEAD_DIM:SEQ_WIDTH + (h + 1) * HEAD_DIM], y[:, 2 * SEQ_WIDTH + h * HEAD_DIM:2 * SEQ_WIDTH + (h + 1) * HEAD_DIM]
        q = q * (lax.rsqrt(jnp.sum(q * q, axis=-1, keepdims=True) + EPS) * HEAD_DIM ** -0.5)
        k = k * lax.rsqrt(jnp.sum(k * k, axis=-1, keepdims=True) + EPS)
        eg = jnp.exp(pick(g_all, h))
        beta = pick(beta_all, GDN_HEADS + h)
        s = s_ref[h]
        sb = rnd(s)
        s_wk = jnp.sum(sb * rnd((beta * eg) * k), axis=-1, keepdims=True)
        u = to_col(beta * v) - s_wk
        ub = rnd(u)
        s_q = jnp.sum(sb * rnd(eg * q), axis=-1, keepdims=True)
        qk = jnp.sum(rnd(q) * rnd(k), axis=-1, keepdims=True)
        o = to_row(s_q + rnd(qk) * ub)
        sn_ref[h] = eg * s + ub * rnd(k)
        outs.append(o * lax.rsqrt(jnp.mean(o * o, axis=-1, keepdims=True) + EPS))
    gt = gate_ref[...]
    o_ref[...] = (jnp.concatenate(outs, axis=-1) * on_ref[...]) * (gt * _sigmoid(gt))


def gdn_decode_pallas(qkv, ab, gate, conv_ctx, s0, conv_w, a_log, dt_bias, o_norm):
    bsz = qkv.shape[0]
    f32 = jnp.float32
    pad = lambda r: jnp.pad(r.astype(f32).reshape(1, -1), ((0, 0), (0, LANES - r.shape[-1])))
    per_b = lambda *shape: pl.BlockSpec((None,) + shape, lambda b: (b,) + (0,) * len(shape))
    const = lambda *shape: pl.BlockSpec(shape, lambda b: (0,) * len(shape))
    st = (GDN_HEADS, HEAD_DIM, HEAD_DIM)
    o, conv_state, s_new = pl.pallas_call(
        _gdn_dec_body,
        grid=(bsz,),
        in_specs=[per_b(1, CONV_CH), per_b(1, LANES), per_b(1, SEQ_WIDTH), per_b(CONV_W - 1, CONV_CH), per_b(*st),
                  const(CONV_W, CONV_CH), const(1, LANES), const(1, LANES), const(1, SEQ_WIDTH)],
        out_specs=[per_b(1, SEQ_WIDTH), per_b(CONV_W - 1, CONV_CH), per_b(*st)],
        out_shape=[jax.ShapeDtypeStruct((bsz, 1, SEQ_WIDTH), f32), jax.ShapeDtypeStruct((bsz, CONV_W - 1, CONV_CH), f32),
                   jax.ShapeDtypeStruct((bsz,) + st, f32)],
        compiler_params=pltpu.CompilerParams(dimension_semantics=("parallel",), vmem_limit_bytes=VMEM_LIMIT_BYTES),
        name="gdn_decode",
    )(qkv, ab, gate, conv_ctx.astype(f32), s0.astype(f32), conv_w.astype(f32), pad(a_log), pad(dt_bias),
      jnp.tile(o_norm.astype(f32), GDN_HEADS).reshape(1, SEQ_WIDTH))
    return o, conv_state.astype(qkv.dtype), s_new.astype(s0.dtype)


def _pad_cols(w, n):
    return jnp.pad(w, ((0, 0), (0, n - w.shape[1])))


def _in_proj_layout(kind, w_in):
    f32, bf16 = jnp.float32, jnp.bfloat16
    if kind == 0:
        return w_in, ((0, MEM_WIDTH), (MEM_WIDTH, SEQ_WIDTH)), (bf16, f32)
    if kind == 1:
        w = _pad_cols(w_in, MEM_WIDTH + sum(DSA_SPLITS[:4]) + LANES)
        widths = (MEM_WIDTH,) + DSA_SPLITS[:4] + (LANES,)
        starts = np.cumsum((0,) + widths[:-1]).tolist()
        return w, tuple(zip(starts, widths)), (bf16, bf16, f32, f32, bf16, f32)
    c0, c1 = MEM_WIDTH + CONV_CH, MEM_WIDTH + CONV_CH + 2 * GDN_HEADS
    w = jnp.concatenate([w_in[:, :c0], w_in[:, c1:], _pad_cols(w_in[:, c0:c1], LANES)], axis=1)
    widths = (MEM_WIDTH, CONV_CH, SEQ_WIDTH, LANES)
    starts = np.cumsum((0,) + widths[:-1]).tolist()
    return w, tuple(zip(starts, widths)), (bf16, f32, f32, f32)


def kernel(x_prompt, x_sample, cache_mem_k, cache_mem_v, state_ssm_re, state_ssm_im, cache_k, cache_v, cache_kidx, state_conv, state_delta, page_table, mem_prompt, norm_g, final_norm, w_in_a, w_in_b, w_in_c, w_out, w_mem_kv, ffn1_gate, ffn1_up, ffn1_down, ffn2_gate, ffn2_up, ffn2_down, s5_lam_re, s5_lam_im, s5_log_step, s5_b_re, s5_b_im, s5_c_re, s5_c_im, s5_d, s5_w_glu, s5_b_glu, rel_bias, gdn_conv_w, gdn_a_log, gdn_dt_bias, gdn_o_norm):
    depth = norm_g.shape[0]
    bp, lp, d = x_prompt.shape
    bs, ls, _ = x_sample.shape
    assert ls == 1, "the decode-step kernels handle one new token per sample"
    f32, bf16 = jnp.float32, jnp.bfloat16
    hp, hs = x_prompt.reshape(bp * lp, d), x_sample.reshape(bs * ls, d)
    mem_rows = mem_prompt.reshape(bp * N_MEM, d)
    mem_k_p, mem_v_p = [], []
    ssm_re_p, ssm_im_p, ssm_re_s, ssm_im_s = [], [], [], []
    k_p, v_p, ki_p, k_s, v_s, ki_s = [], [], [], [], [], []
    conv_p, delta_p, conv_s, delta_s = [], [], [], []
    w_in_by_kind = (w_in_a, w_in_b, w_in_c)
    for i in range(depth):
        kind, j = i % N_MIXERS, i // N_MIXERS
        last = i == depth - 1
        ffn1 = (ffn1_gate[i].astype(bf16), ffn1_up[i].astype(bf16), ffn1_down[i].astype(bf16))
        ffn2 = (ffn2_gate[i].astype(bf16), ffn2_up[i].astype(bf16), ffn2_down[i].astype(bf16))
        w_in, segments, dtypes = _in_proj_layout(kind, w_in_by_kind[kind][j])
        w_in = w_in.astype(bf16)
        hp = ffn_residual(hp, norm_g[i, 0], *ffn1)
        hs = ffn_residual(hs, norm_g[i, 0], *ffn1)
        zp = [a.reshape(bp, lp, -1) for a in proj_in(hp, norm_g[i, 1], w_in, segments, dtypes)]
        zs = [a.reshape(bs, ls, -1) for a in proj_in(hs, norm_g[i, 1], w_in, segments, dtypes)]
        mk, mv = proj_in(mem_rows, norm_g[i, 1], w_mem_kv[i].astype(bf16), ((0, MEM_WIDTH), (MEM_WIDTH, MEM_WIDTH)),
                         (f32, f32), normalize=False)
        mem_shape = (bp, N_MEM, MEM_HEADS, HEAD_DIM)
        mk, mv = mk.reshape(mem_shape), mv.reshape(mem_shape)
        mem_k_p.append(mk)
        mem_v_p.append(mv)
        if kind == 0:
            disc = s5_discretize(s5_lam_re[j], s5_lam_im[j], s5_log_step[j], s5_b_re[j], s5_b_im[j],
                                 s5_c_re[j], s5_c_im[j])
            gate = (s5_d[j], s5_w_glu[j], s5_b_glu[j])
            h0 = jnp.zeros((bp, S5_GROUPS, S5_STATE), state_ssm_re.dtype)
            op, hr, hi = s5_mix_pallas(zp[1], h0, h0, disc, *gate)
            ssm_re_p.append(hr)
            ssm_im_p.append(hi)
            osm, hr, hi = s5_mix_pallas(zs[1], state_ssm_re[j], state_ssm_im[j], disc, *gate)
            ssm_re_s.append(hr)
            ssm_im_s.append(hi)
        elif kind == 1:
            _, q, k, v, qi, kiwi = zp
            op = dsa_prompt_pallas(q, qi, kiwi, k, v, rel_bias)
            kv_shape = (DSA_KV_HEADS, HEAD_DIM)
            k_p.append(k.reshape((bp, lp) + kv_shape))
            v_p.append(v.reshape((bp, lp) + kv_shape))
            ki_p.append(kiwi[..., :IDX_DIM])
            _, q, k, v, qi, kiwi = zs
            osm = dsa_decode_pallas(q[:, 0], qi[:, 0], kiwi[:, 0], k[:, 0], v[:, 0], cache_k[j], cache_v[j],
                                    cache_kidx[j], page_table, rel_bias)
            k_s.append(k.reshape((bs, ls) + kv_shape))
            v_s.append(v.reshape((bs, ls) + kv_shape))
            ki_s.append(kiwi[..., :IDX_DIM])
        else:
            gdn = (gdn_conv_w[j], gdn_a_log[j], gdn_dt_bias[j], gdn_o_norm[j])
            ctx0 = jnp.zeros((bp, CONV_W - 1, CONV_CH), f32)
            s00 = jnp.zeros((bp, GDN_HEADS, HEAD_DIM, HEAD_DIM), state_delta.dtype)
            _, qkv, gate, ab = zp
            op, cst, sst = gdn_mix_pallas(qkv, ab, gate, ctx0, s00, *gdn)
            conv_p.append(cst)
            delta_p.append(sst)
            _, qkv, gate, ab = zs
            osm, cst, sst = gdn_decode_pallas(qkv, ab, gate, state_conv[j], state_delta[j], *gdn)
            conv_s.append(cst)
            delta_s.append(sst)
        w_o = w_out[i].astype(bf16)
        hp = mix_out(hp.reshape(bp, lp, d), zp[0], op, mk, mv, w_o).reshape(bp * lp, d)
        hs = mix_out(hs.reshape(bs, ls, d), zs[0], osm, cache_mem_k[i], cache_mem_v[i], w_o).reshape(bs * ls, d)
        fg = final_norm if last else None
        hp = ffn_residual(hp, norm_g[i, 2], *ffn2, fg)
        hs = ffn_residual(hs, norm_g[i, 2], *ffn2, fg)
    st = jnp.stack
    return (hp.reshape(bp, lp, d), hs.reshape(bs, ls, d), st(mem_k_p), st(mem_v_p),
            st(ssm_re_p), st(ssm_im_p), st(ssm_re_s), st(ssm_im_s),
            st(k_p), st(v_p), st(ki_p), st(k_s), st(v_s), st(ki_s),
            st(conv_p), st(delta_p), st(conv_s), st(delta_s))
```

```python
import math
import functools
import jax
import jax.numpy as jnp
from jax import lax
import numpy as np
from jax.experimental import pallas as pl
from jax.experimental.pallas import tpu as pltpu

D_MODEL = 1024
N_MIXERS = 3
HEAD_DIM = 64
MIX_WIDTH = D_MODEL
N_MEM = 256
MEM_HEADS = 4
MEM_WIDTH = MEM_HEADS * HEAD_DIM
SEQ_WIDTH = MIX_WIDTH - MEM_WIDTH
S5_GROUP = 16
S5_GROUPS = SEQ_WIDTH // S5_GROUP
S5_STATE = 64
DSA_HEADS = SEQ_WIDTH // HEAD_DIM
DSA_KV_HEADS = 4
DSA_GQA = DSA_HEADS // DSA_KV_HEADS
IDX_HEADS = 8
IDX_DIM = 64
TOPK_MAX = 256
QBLOCK = 128
N_BUCKETS = 32
MAX_DISTANCE = 128
GDN_HEADS = SEQ_WIDTH // HEAD_DIM
CONV_W = 4
CONV_CH = 3 * SEQ_WIDTH
GDN_CHUNK = 64
D_FF = 2816
EPS = 1e-6
PAGE_SIZE = 128
DSA_SPLITS = (DSA_HEADS * HEAD_DIM, DSA_KV_HEADS * HEAD_DIM, DSA_KV_HEADS * HEAD_DIM, IDX_HEADS * IDX_DIM, IDX_DIM, IDX_HEADS)
GDN_SPLITS = (CONV_CH, GDN_HEADS, GDN_HEADS, SEQ_WIDTH)

LANES = 128
VMEM_LIMIT_BYTES = 56 * 1024 * 1024


def _sigmoid(x):
    return 1.0 / (1.0 + jnp.exp(-x))


def t5_bucket(dist):
    max_exact = N_BUCKETS // 2
    n = jnp.maximum(dist, 0)
    nf = jnp.maximum(n, 1).astype(jnp.float32)
    large = max_exact + (jnp.log(nf / max_exact) / math.log(MAX_DISTANCE / max_exact)
                         * (N_BUCKETS - max_exact)).astype(jnp.int32)
    return jnp.where(n < max_exact, n, jnp.minimum(large, N_BUCKETS - 1))


FFN_CHUNK = 256


def _ffn_body(final_norm, x_ref, g_ref, wg_ref, wu_ref, wd_ref, gf_ref, o_ref):
    f32, bf16 = jnp.float32, jnp.bfloat16
    x = x_ref[...]
    n = ((x * lax.rsqrt(jnp.mean(x * x, axis=-1, keepdims=True) + EPS)) * g_ref[...]).astype(bf16)
    acc = jnp.zeros(x.shape, f32)
    for c in range(wg_ref.shape[1] // FFN_CHUNK):
        sl = slice(c * FFN_CHUNK, (c + 1) * FFN_CHUNK)
        a = jnp.dot(n, wg_ref[:, sl], preferred_element_type=f32)
        b = jnp.dot(n, wu_ref[:, sl], preferred_element_type=f32)
        h = (a * _sigmoid(a)) * b
        acc = acc + jnp.dot(h.astype(bf16), wd_ref[sl, :], preferred_element_type=f32)
    y = x + 0.5 * acc
    if final_norm:
        y = (y * lax.rsqrt(jnp.mean(y * y, axis=-1, keepdims=True) + EPS)) * gf_ref[...]
    o_ref[...] = y


def ffn_residual(x, g, wg, wu, wd, final_g=None, *, tm=512):
    t, d = x.shape
    f = wg.shape[1]
    tm = min(tm, t)
    gf = jnp.ones((d,), jnp.float32) if final_g is None else final_g
    resident = lambda shape: pl.BlockSpec(shape, lambda i: (0, 0), pipeline_mode=pl.Buffered(1))
    return pl.pallas_call(
        functools.partial(_ffn_body, final_g is not None),
        grid=(t // tm,),
        in_specs=[pl.BlockSpec((tm, d), lambda i: (i, 0)), resident((1, d)),
                  resident((d, f)), resident((d, f)), resident((f, d)), resident((1, d))],
        out_specs=pl.BlockSpec((tm, d), lambda i: (i, 0)),
        out_shape=jax.ShapeDtypeStruct((t, d), jnp.float32),
        compiler_params=pltpu.CompilerParams(dimension_semantics=("parallel",), vmem_limit_bytes=VMEM_LIMIT_BYTES),
        name="ffn_residual",
    )(x, g.reshape(1, d).astype(jnp.float32), wg, wu, wd, gf.reshape(1, d).astype(jnp.float32))


def _proj_in_body(segments, normalize, x_ref, g_ref, w_ref, *o_refs):
    x = x_ref[...]
    if normalize:
        r = lax.rsqrt(jnp.mean(x * x, axis=-1, keepdims=True) + EPS)
        x = (x * r) * g_ref[...]
    n = x.astype(jnp.bfloat16)
    for (start, width), o_ref in zip(segments, o_refs):
        o_ref[...] = jnp.dot(n, w_ref[:, start:start + width], preferred_element_type=jnp.float32).astype(o_ref.dtype)


def proj_in(x, g, w, segments, dtypes, *, normalize=True, tm=256):
    t, d = x.shape
    tm = min(tm, t)
    return pl.pallas_call(
        functools.partial(_proj_in_body, tuple(segments), normalize),
        grid=(t // tm,),
        in_specs=[pl.BlockSpec((tm, d), lambda i: (i, 0)), pl.BlockSpec((1, d), lambda i: (0, 0)),
                  pl.BlockSpec(w.shape, lambda i: (0, 0))],
        out_specs=[pl.BlockSpec((tm, width), lambda i: (i, 0)) for _, width in segments],
        out_shape=[jax.ShapeDtypeStruct((t, width), dt) for (_, width), dt in zip(segments, dtypes)],
        compiler_params=pltpu.CompilerParams(dimension_semantics=("parallel",), vmem_limit_bytes=VMEM_LIMIT_BYTES),
        name="proj_in",
    )(x, g.reshape(1, d).astype(jnp.float32), w)


def _mix_out_body(x_ref, cq_ref, om_ref, mkt_ref, mv_ref, w_ref, o_ref):
    f32, bf16 = jnp.float32, jnp.bfloat16
    cq = cq_ref[...]
    heads = []
    for h in range(MEM_HEADS):
        sl = slice(h * HEAD_DIM, (h + 1) * HEAD_DIM)
        logits = jnp.dot(cq[:, sl], mkt_ref[sl, :], preferred_element_type=f32) * HEAD_DIM ** -0.5
        p = jnp.exp(logits - jnp.max(logits, axis=-1, keepdims=True))
        p = p / jnp.sum(p, axis=-1, keepdims=True)
        heads.append(jnp.dot(p.astype(bf16), mv_ref[:, sl], preferred_element_type=f32))
    o_mem = jnp.concatenate(heads, axis=-1).astype(bf16)
    y = (jnp.dot(o_mem, w_ref[:MEM_WIDTH, :], preferred_element_type=f32)
         + jnp.dot(om_ref[...].astype(bf16), w_ref[MEM_WIDTH:, :], preferred_element_type=f32))
    o_ref[...] = x_ref[...] + y


def mix_out(x, cq, o_mix, mk, mv, w_out, *, tm=512):
    bsz, L, d = x.shape
    tm = min(tm, L)
    bf16 = jnp.bfloat16
    mkt = mk.astype(bf16).reshape(bsz, N_MEM, MEM_WIDTH).swapaxes(1, 2)
    mvf = mv.astype(bf16).reshape(bsz, N_MEM, MEM_WIDTH)
    tok = lambda n: pl.BlockSpec((None, tm, n), lambda b, t: (b, t, 0))
    per_b = lambda r, c: pl.BlockSpec((None, r, c), lambda b, t: (b, 0, 0))
    return pl.pallas_call(
        _mix_out_body,
        grid=(bsz, L // tm),
        in_specs=[tok(d), tok(MEM_WIDTH), tok(SEQ_WIDTH), per_b(MEM_WIDTH, N_MEM), per_b(N_MEM, MEM_WIDTH),
                  pl.BlockSpec((d, d), lambda b, t: (0, 0))],
        out_specs=tok(d),
        out_shape=jax.ShapeDtypeStruct((bsz, L, d), jnp.float32),
        compiler_params=pltpu.CompilerParams(
            dimension_semantics=("parallel", "parallel"), vmem_limit_bytes=VMEM_LIMIT_BYTES),
        name="mix_out",
    )(x, cq, o_mix, mkt, mvf, w_out)


S5_LANES = S5_GROUPS * S5_STATE
S5_GROUPS_PER_BLOCK = LANES // S5_GROUP
S5_BLOCKS = SEQ_WIDTH // LANES
S5_BLOCK_STATES = S5_GROUPS_PER_BLOCK * S5_STATE


def s5_discretize(lam_re, lam_im, log_step, b_re, b_im, c_re, c_im):
    f32, bf16 = jnp.float32, jnp.bfloat16
    lr, li = lam_re.astype(f32), lam_im.astype(f32)
    step = jnp.exp(log_step.astype(f32))[:, None]
    mag = jnp.exp(lr * step)
    ab_re, ab_im = mag * jnp.cos(li * step), mag * jnp.sin(li * step)
    den = lr * lr + li * li
    nr, ni = ab_re - 1.0, ab_im
    f_re = (nr * lr + ni * li) / den
    f_im = (ni * lr - nr * li) / den
    br, bi = b_re.astype(f32), b_im.astype(f32)
    bb_re = f_re[..., None] * br - f_im[..., None] * bi
    bb_im = f_re[..., None] * bi + f_im[..., None] * br
    eye = jnp.eye(S5_GROUPS_PER_BLOCK, dtype=f32)
    nb, gb = S5_BLOCKS, S5_GROUPS_PER_BLOCK

    def in_blocks(bb):
        w = jnp.einsum('jgpc,gh->jgchp', bb.reshape(nb, gb, S5_STATE, S5_GROUP), eye)
        return w.reshape(nb, LANES, S5_BLOCK_STATES).astype(bf16)

    def out_blocks(c):
        w = jnp.einsum('jgop,gh->jgpho', c.astype(f32).reshape(nb, gb, S5_GROUP, S5_STATE), eye)
        return w.reshape(nb, S5_BLOCK_STATES, LANES).astype(bf16)

    return (ab_re.reshape(1, S5_LANES), ab_im.reshape(1, S5_LANES),
            in_blocks(bb_re), in_blocks(bb_im), out_blocks(c_re), out_blocks(-c_im))


def _s5_project_in(u, wbr_ref, wbi_ref, bur_ref, bui_ref):
    ub = u.astype(jnp.bfloat16)
    for j in range(S5_BLOCKS):
        uj = ub[:, j * LANES:(j + 1) * LANES]
        sl = slice(j * S5_BLOCK_STATES, (j + 1) * S5_BLOCK_STATES)
        bur_ref[:, sl] = jnp.dot(uj, wbr_ref[j], preferred_element_type=jnp.float32)
        bui_ref[:, sl] = jnp.dot(uj, wbi_ref[j], preferred_element_type=jnp.float32)


def _s5_project_out(hr_ref, hi_ref, wcr_ref, wci_ref):
    cols = []
    for j in range(S5_BLOCKS):
        sl = slice(j * S5_BLOCK_STATES, (j + 1) * S5_BLOCK_STATES)
        cols.append(jnp.dot(hr_ref[:, sl].astype(jnp.bfloat16), wcr_ref[j], preferred_element_type=jnp.float32)
                    + jnp.dot(hi_ref[:, sl].astype(jnp.bfloat16), wci_ref[j], preferred_element_type=jnp.float32))
    return jnp.concatenate(cols, axis=-1)


def _s5_gate(y_ssm, u, d_ref, wglu_ref, bglu_ref):
    y = y_ssm + d_ref[...] * u
    y = 0.5 * y * (1.0 + jnp.tanh(math.sqrt(2.0 / math.pi) * (y + 0.044715 * (y * y * y))))
    z = jnp.dot(y.astype(jnp.bfloat16), wglu_ref[...], preferred_element_type=jnp.float32) + bglu_ref[...]
    return y * (1.0 / (1.0 + jnp.exp(-z)))


def _s5_seq_body(u_ref, h0r_ref, h0i_ref, ar_ref, ai_ref, wbr_ref, wbi_ref, wcr_ref, wci_ref, d_ref, wglu_ref, bglu_ref,
                 y_ref, hfr_ref, hfi_ref, bur_ref, bui_ref, cr_ref, ci_ref):
    tt = u_ref.shape[0]

    @pl.when(pl.program_id(1) == 0)
    def _():
        cr_ref[...] = h0r_ref[...]
        ci_ref[...] = h0i_ref[...]

    u = u_ref[...]
    _s5_project_in(u, wbr_ref, wbi_ref, bur_ref, bui_ref)
    ar, ai = ar_ref[...], ai_ref[...]

    def step(t, carry):
        hr, hi = carry
        nhr = ar * hr - ai * hi + bur_ref[pl.ds(t, 1), :]
        nhi = ar * hi + ai * hr + bui_ref[pl.ds(t, 1), :]
        bur_ref[pl.ds(t, 1), :] = nhr
        bui_ref[pl.ds(t, 1), :] = nhi
        return nhr, nhi

    hr, hi = lax.fori_loop(0, tt, step, (cr_ref[...], ci_ref[...]))
    cr_ref[...] = hr
    ci_ref[...] = hi
    hfr_ref[...] = hr
    hfi_ref[...] = hi
    y_ref[...] = _s5_gate(_s5_project_out(bur_ref, bui_ref, wcr_ref, wci_ref), u, d_ref, wglu_ref, bglu_ref)


def _s5_step_body(u_ref, h0r_ref, h0i_ref, ar_ref, ai_ref, wbr_ref, wbi_ref, wcr_ref, wci_ref, d_ref, wglu_ref, bglu_ref,
                  y_ref, hfr_ref, hfi_ref, bur_ref, bui_ref):
    u = u_ref[...]
    _s5_project_in(u, wbr_ref, wbi_ref, bur_ref, bui_ref)
    ar, ai = ar_ref[...], ai_ref[...]
    hr, hi = h0r_ref[...], h0i_ref[...]
    nhr = ar * hr - ai * hi + bur_ref[...]
    nhi = ar * hi + ai * hr + bui_ref[...]
    bur_ref[...] = nhr
    bui_ref[...] = nhi
    hfr_ref[...] = nhr
    hfi_ref[...] = nhi
    y_ref[...] = _s5_gate(_s5_project_out(bur_ref, bui_ref, wcr_ref, wci_ref), u, d_ref, wglu_ref, bglu_ref)


def s5_mix_pallas(u, h0_re, h0_im, disc, d_skip, w_glu, b_glu, *, tt=512):
    bsz, L, w = u.shape
    f32 = jnp.float32
    ar, ai, wbr, wbi, wcr, wci = disc
    d2, bg2, wg = d_skip.reshape(1, w).astype(f32), b_glu.reshape(1, w).astype(f32), w_glu.astype(jnp.bfloat16)
    const2 = lambda *_: (0, 0)
    const3 = lambda *_: (0, 0, 0)
    w_specs = [pl.BlockSpec((1, S5_LANES), const2), pl.BlockSpec((1, S5_LANES), const2),
               pl.BlockSpec(wbr.shape, const3), pl.BlockSpec(wbi.shape, const3),
               pl.BlockSpec(wcr.shape, const3), pl.BlockSpec(wci.shape, const3),
               pl.BlockSpec((1, w), const2), pl.BlockSpec((w, w), const2), pl.BlockSpec((1, w), const2)]
    w_args = (ar, ai, wbr, wbi, wcr, wci, d2, wg, bg2)
    if L == 1:
        rows = bsz
        h0r, h0i = h0_re.reshape(rows, S5_LANES).astype(f32), h0_im.reshape(rows, S5_LANES).astype(f32)
        row_spec = lambda n: pl.BlockSpec((rows, n), const2)
        y, hr, hi = pl.pallas_call(
            _s5_step_body,
            grid=(1,),
            in_specs=[row_spec(w), row_spec(S5_LANES), row_spec(S5_LANES)] + w_specs,
            out_specs=[row_spec(w), row_spec(S5_LANES), row_spec(S5_LANES)],
            out_shape=[jax.ShapeDtypeStruct((rows, w), f32), jax.ShapeDtypeStruct((rows, S5_LANES), f32),
                       jax.ShapeDtypeStruct((rows, S5_LANES), f32)],
            scratch_shapes=[pltpu.VMEM((rows, S5_LANES), f32), pltpu.VMEM((rows, S5_LANES), f32)],
            compiler_params=pltpu.CompilerParams(vmem_limit_bytes=VMEM_LIMIT_BYTES),
            name="s5_step",
        )(u.reshape(rows, w), h0r, h0i, *w_args)
        y = y.reshape(bsz, 1, w)
    else:
        tt = min(tt, L)
        h0r, h0i = h0_re.reshape(bsz, 1, S5_LANES).astype(f32), h0_im.reshape(bsz, 1, S5_LANES).astype(f32)
        st_spec = pl.BlockSpec((None, 1, S5_LANES), lambda b, t: (b, 0, 0))
        y, hr, hi = pl.pallas_call(
            _s5_seq_body,
            grid=(bsz, L // tt),
            in_specs=[pl.BlockSpec((None, tt, w), lambda b, t: (b, t, 0)), st_spec, st_spec] + w_specs,
            out_specs=[pl.BlockSpec((None, tt, w), lambda b, t: (b, t, 0)), st_spec, st_spec],
            out_shape=[jax.ShapeDtypeStruct((bsz, L, w), f32), jax.ShapeDtypeStruct((bsz, 1, S5_LANES), f32),
                       jax.ShapeDtypeStruct((bsz, 1, S5_LANES), f32)],
            scratch_shapes=[pltpu.VMEM((tt, S5_LANES), f32), pltpu.VMEM((tt, S5_LANES), f32),
                            pltpu.VMEM((1, S5_LANES), f32), pltpu.VMEM((1, S5_LANES), f32)],
            compiler_params=pltpu.CompilerParams(
                dimension_semantics=("parallel", "arbitrary"), vmem_limit_bytes=VMEM_LIMIT_BYTES),
            name="s5_seq",
        )(u, h0r, h0i, *w_args)
    shp = (bsz, S5_GROUPS, S5_STATE)
    return y, hr.reshape(shp).astype(h0_re.dtype), hi.reshape(shp).astype(h0_im.dtype)


DSA_KEY_CHUNK = 512
DSA_NEAR = 2 * QBLOCK
INT32_MIN = -2 ** 31
NEG_BIG = -1e30
KT_ROWS = HEAD_DIM + 16


def _sortable_key(s):
    bits = lax.bitcast_convert_type(s, jnp.int32)
    return jnp.where(bits < 0, bits ^ jnp.int32(0x7FFFFFFF), bits)


def _dsa_prompt_body(n_top, rb_ref, q_ref, qi_ref, kiwi_ref, kt_ref, v_ref, kit_ref, o_ref, keys_ref, bias_ref, bound_ref, hi_ref, lo_ref):
    i = pl.program_id(1)
    f32, bf16 = jnp.float32, jnp.bfloat16
    kc = DSA_KEY_CHUNK
    q_start = i * QBLOCK
    n_all = (q_start + QBLOCK + kc - 1) // kc
    near_start = jnp.maximum(q_start - QBLOCK, 0)
    n_far = (near_start + kc - 1) // kc
    row = lax.broadcasted_iota(jnp.int32, (QBLOCK, 1), 0)
    qpos = q_start + row

    @pl.when(i == 0)
    def _():
        r = lax.broadcasted_iota(jnp.int32, (QBLOCK, DSA_NEAR), 0)
        c = lax.broadcasted_iota(jnp.int32, (QBLOCK, DSA_NEAR), 1)
        for tile in range(2):
            bucket = t5_bucket(r + tile * QBLOCK - c)
            for h in range(DSA_HEADS):
                b = jnp.zeros((QBLOCK, DSA_NEAR), f32)
                for bk in range(N_BUCKETS):
                    b = jnp.where(bucket == bk, rb_ref[bk, h] - rb_ref[N_BUCKETS - 1, h], b)
                kv, g = divmod(h, DSA_GQA)
                bias_ref[tile, kv, g * QBLOCK:(g + 1) * QBLOCK, :] = b
        lane = lax.broadcasted_iota(jnp.int32, (1, LANES), 1)
        bound = jnp.zeros((1, LANES), f32)
        for kv in range(DSA_KV_HEADS):
            kk = kt_ref[kv * KT_ROWS:kv * KT_ROWS + HEAD_DIM, :].astype(f32)
            kmax = jnp.sqrt(jnp.max(jnp.sum(kk * kk, axis=0, keepdims=True), axis=-1, keepdims=True))
            bound = jnp.where(lane == kv, kmax, bound)
        for h in range(DSA_HEADS):
            kv, g = divmod(h, DSA_GQA)
            bmax = jnp.maximum(jnp.max(bias_ref[:, kv, g * QBLOCK:(g + 1) * QBLOCK, :]), 0.0)
            bound = jnp.where(lane == DSA_KV_HEADS + h, bmax, bound)
        bound_ref[...] = bound

    qi_all = (qi_ref[...].astype(f32) * IDX_DIM ** -0.5).astype(bf16)
    qi = jnp.concatenate([qi_all[:, h * IDX_DIM:(h + 1) * IDX_DIM] for h in range(IDX_HEADS)], axis=0)
    wi = kiwi_ref[:, IDX_DIM:IDX_DIM + IDX_HEADS] * IDX_HEADS ** -0.5

    def score_chunk(c, _):
        off = pl.multiple_of(c * kc, kc)
        d = jnp.dot(qi, kit_ref[:, pl.ds(off, kc)], preferred_element_type=f32)
        s = jnp.zeros((QBLOCK, kc), f32)
        for h in range(IDX_HEADS):
            s = s + wi[:, h:h + 1] * jnp.maximum(d[h * QBLOCK:(h + 1) * QBLOCK], 0.0)
        kpos = off + lax.broadcasted_iota(jnp.int32, (QBLOCK, kc), 1)
        s = jnp.where(s == 0.0, 0.0, s)
        s = jnp.where(kpos <= qpos, s, -jnp.inf)
        key = _sortable_key(s)
        keys_ref[:, pl.ds(off, kc)] = key
        hi_ref[:, pl.ds(off, kc)] = lax.shift_right_arithmetic(key, 16).astype(jnp.int16)
        return 0

    lax.fori_loop(0, n_all, score_chunk, 0)

    def count_where(pred_fn):
        def body(c, acc):
            off = pl.multiple_of(c * kc, kc)
            hit = pred_fn(keys_ref[:, pl.ds(off, kc)], off)
            part = jnp.where(hit, 1.0, 0.0)
            for j in range(kc // 128):
                acc = acc + part[:, j * 128:(j + 1) * 128]
            return acc
        acc = lax.fori_loop(0, n_all, body, jnp.zeros((QBLOCK, 128), f32))
        return jnp.sum(acc, axis=-1, keepdims=True)

    def count16(ref, pred_fn):
        def body(c, acc):
            off = pl.multiple_of(c * kc, kc)
            part = jnp.where(pred_fn(ref[:, pl.ds(off, kc)]), jnp.int16(1), jnp.int16(0))
            return acc + ((part[:, 0:128] + part[:, 128:256]) + (part[:, 256:384] + part[:, 384:512]))
        acc = lax.fori_loop(0, n_all, body, jnp.zeros((QBLOCK, 128), jnp.int16))
        return jnp.sum(acc.astype(f32), axis=-1, keepdims=True)

    def search16(ref, want):
        def bit(it, t):
            cand = t + lax.shift_left(jnp.int32(1), 15 - it)
            c16 = cand.astype(jnp.int16)
            return jnp.where(count16(ref, lambda x: x >= c16) >= want, cand, t)
        return lax.fori_loop(0, 16, bit, jnp.full((QBLOCK, 1), -2 ** 15, jnp.int32))

    t_hi = search16(hi_ref, n_top)
    t_hi16 = t_hi.astype(jnp.int16)
    n_above = count16(hi_ref, lambda x: x > t_hi16)

    def band_chunk(c, _):
        off = pl.multiple_of(c * kc, kc)
        key = keys_ref[:, pl.ds(off, kc)]
        lo = ((key & 0xFFFF) - 2 ** 15).astype(jnp.int16)
        lo_ref[:, pl.ds(off, kc)] = jnp.where(lax.shift_right_arithmetic(key, 16) == t_hi, lo, jnp.int16(-2 ** 15))
        return 0

    lax.fori_loop(0, n_all, band_chunk, 0)
    t_lo = search16(lo_ref, n_top - n_above)
    thr = lax.shift_left(t_hi, 16) + (t_lo + 2 ** 15)

    def is_valid(off, width):
        return (off + lax.broadcasted_iota(jnp.int32, (QBLOCK, width), 1)) <= qpos

    n_gt = count_where(lambda k, off: (k > thr) & is_valid(off, kc))
    n_eq = count_where(lambda k, off: (k == thr) & is_valid(off, kc))
    need = n_top - n_gt
    has_extra_ties = jnp.max(jnp.where(n_eq > need, 1.0, 0.0)) > 0.0

    def tie_search():
        def idx_bit(it, j):
            cand = j + lax.shift_left(jnp.int32(1), 13 - it)
            cnt = count_where(lambda k, off: (k == thr) & is_valid(off, kc)
                              & ((off + lax.broadcasted_iota(jnp.int32, (QBLOCK, kc), 1)) < cand))
            return jnp.where(cnt <= need, cand, j)
        return lax.fori_loop(0, 14, idx_bit, jnp.zeros((QBLOCK, 1), jnp.int32))

    tie_end = lax.cond(has_extra_ties, tie_search, lambda: jnp.full((QBLOCK, 1), 2 ** 14, jnp.int32))

    def selected(keys, off, width):
        kpos = off + lax.broadcasted_iota(jnp.int32, (QBLOCK, width), 1)
        return (kpos <= qpos) & ((keys > thr) | ((keys == thr) & (kpos < tie_end))), kpos

    tile = jnp.minimum(i, 1)
    q_all = (q_ref[...].astype(f32) * HEAD_DIM ** -0.5).astype(bf16)
    qh = [q_all[:, h * HEAD_DIM:(h + 1) * HEAD_DIM] for h in range(DSA_HEADS)]
    off_near = pl.multiple_of(near_start, QBLOCK)
    sel_near, _ = selected(keys_ref[:, pl.ds(off_near, DSA_NEAR)], off_near, DSA_NEAR)

    def near_bias(h):
        kv, g = divmod(h, DSA_GQA)
        return jnp.where(sel_near, bias_ref[tile, kv, g * QBLOCK:(g + 1) * QBLOCK, :], NEG_BIG)

    def far_mask(c):
        off = pl.multiple_of(c * kc, kc)
        sel, kpos = selected(keys_ref[:, pl.ds(off, kc)], off, kc)
        return off, jnp.where(sel & (kpos < near_start), 0.0, NEG_BIG)

    def kt_block(kv, off, width, rows):
        return kt_ref[kv * KT_ROWS:kv * KT_ROWS + rows, pl.ds(off, width)]

    bound = bound_ref[...]
    lane = lax.broadcasted_iota(jnp.int32, (1, LANES), 1)
    pick = lambda idx: jnp.sum(jnp.where(lane == idx, bound, 0.0), axis=-1, keepdims=True)
    col = lax.broadcasted_iota(jnp.int32, (QBLOCK, KT_ROWS - HEAD_DIM), 1)
    q_aug = []
    for h in range(DSA_HEADS):
        qf = qh[h].astype(f32)
        ub = jnp.sqrt(jnp.sum(qf * qf, axis=-1, keepdims=True)) * pick(h // DSA_GQA) * 1.01 + pick(DSA_KV_HEADS + h)
        q_aug.append(jnp.concatenate([qh[h], jnp.where(col == 0, -ub, 0.0).astype(bf16)], axis=-1))

    def fast_attend(accs, off, width, mask_bias):
        accs = list(accs)
        for kv in range(DSA_KV_HEADS):
            kt = kt_block(kv, off, width, KT_ROWS)
            ps = [jnp.exp(jnp.dot(q_aug[kv * DSA_GQA + g], kt, preferred_element_type=f32)
                          + mask_bias(kv * DSA_GQA + g)).astype(bf16) for g in range(DSA_GQA)]
            accs[kv] = accs[kv] + jnp.dot(jnp.concatenate(ps, axis=0), v_ref[kv, pl.ds(off, width), :],
                                          preferred_element_type=f32)
        return tuple(accs)

    def fast_chunk(c, accs):
        off, mb = far_mask(c)
        return fast_attend(accs, off, kc, lambda h: mb)

    zero_acc = tuple(jnp.zeros((DSA_GQA * QBLOCK, 2 * HEAD_DIM), f32) for _ in range(DSA_KV_HEADS))
    accs = fast_attend(lax.fori_loop(0, n_far, fast_chunk, zero_acc), off_near, DSA_NEAR, near_bias)
    l_min = accs[0][:, HEAD_DIM:]
    for kv in range(1, DSA_KV_HEADS):
        l_min = jnp.minimum(l_min, accs[kv][:, HEAD_DIM:])
    fast_ok = jnp.min(l_min) > 1e-30

    def safe_attend(carry, off, width, mask_bias):
        ms, accs = list(carry[0]), list(carry[1])
        for kv in range(DSA_KV_HEADS):
            kt = kt_block(kv, off, width, HEAD_DIM)
            ps, alphas = [], []
            for g in range(DSA_GQA):
                h = kv * DSA_GQA + g
                logits = jnp.dot(qh[h], kt, preferred_element_type=f32) + mask_bias(h)
                m_new = jnp.maximum(ms[h], jnp.max(logits, axis=-1, keepdims=True))
                alphas.append(jnp.exp(ms[h] - m_new))
                ps.append(jnp.exp(logits - m_new).astype(bf16))
                ms[h] = m_new
            accs[kv] = (jnp.concatenate(alphas, axis=0) * accs[kv]
                        + jnp.dot(jnp.concatenate(ps, axis=0), v_ref[kv, pl.ds(off, width), :], preferred_element_type=f32))
        return tuple(ms), tuple(accs)

    def safe_path():
        def safe_chunk(c, carry):
            off, mb = far_mask(c)
            return safe_attend(carry, off, kc, lambda h: mb)
        init = (tuple(jnp.full((QBLOCK, 1), NEG_BIG, f32) for _ in range(DSA_HEADS)), zero_acc)
        return safe_attend(lax.fori_loop(0, n_far, safe_chunk, init), off_near, DSA_NEAR, near_bias)[1]

    accs = lax.cond(fast_ok, lambda: accs, safe_path)
    outs = []
    for kv in range(DSA_KV_HEADS):
        o = accs[kv][:, :HEAD_DIM] / accs[kv][:, HEAD_DIM:]
        outs += [o[g * QBLOCK:(g + 1) * QBLOCK] for g in range(DSA_GQA)]
    o_ref[...] = jnp.concatenate(outs, axis=-1)


def dsa_prompt_pallas(q, qi, kiwi, k, v, rel_bias):
    bsz, L = q.shape[:2]
    nq = L // QBLOCK
    n_top = min(TOPK_MAX, L // 4)
    f32, bf16 = jnp.float32, jnp.bfloat16
    k_t = k.astype(bf16).reshape(bsz, L, DSA_KV_HEADS, HEAD_DIM).transpose(0, 2, 3, 1)
    k_pad = jnp.zeros((bsz, DSA_KV_HEADS, KT_ROWS - HEAD_DIM, L), bf16).at[:, :, 0, :].set(1.0)
    k_t = jnp.concatenate([k_t, k_pad], axis=2).reshape(bsz, DSA_KV_HEADS * KT_ROWS, L)
    v_h = v.astype(bf16).reshape(bsz, L, DSA_KV_HEADS, HEAD_DIM).swapaxes(1, 2)
    v_h = jnp.concatenate([v_h, jnp.ones_like(v_h)], axis=-1)
    ki_t = kiwi[..., :IDX_DIM].astype(bf16).swapaxes(1, 2)
    lk = max(L, DSA_KEY_CHUNK)
    if lk != L:
        k_t = jnp.pad(k_t, ((0, 0), (0, 0), (0, lk - L)))
        v_h = jnp.pad(v_h, ((0, 0), (0, 0), (0, lk - L), (0, 0)))
        ki_t = jnp.pad(ki_t, ((0, 0), (0, 0), (0, lk - L)))
    tok = lambda n: pl.BlockSpec((None, QBLOCK, n), lambda b, i: (b, i, 0))
    return pl.pallas_call(
        functools.partial(_dsa_prompt_body, n_top),
        grid=(bsz, nq),
        in_specs=[
            pl.BlockSpec(memory_space=pltpu.SMEM),
            tok(SEQ_WIDTH), tok(IDX_HEADS * IDX_DIM), tok(LANES),
            pl.BlockSpec((None, DSA_KV_HEADS * KT_ROWS, lk), lambda b, i: (b, 0, 0), pipeline_mode=pl.Buffered(1)),
            pl.BlockSpec((None, DSA_KV_HEADS, lk, 2 * HEAD_DIM), lambda b, i: (b, 0, 0, 0), pipeline_mode=pl.Buffered(1)),
            pl.BlockSpec((None, IDX_DIM, lk), lambda b, i: (b, 0, 0), pipeline_mode=pl.Buffered(1)),
        ],
        out_specs=tok(SEQ_WIDTH),
        out_shape=jax.ShapeDtypeStruct((bsz, L, SEQ_WIDTH), f32),
        scratch_shapes=[pltpu.VMEM((QBLOCK, lk), jnp.int32),
                        pltpu.VMEM((2, DSA_KV_HEADS, DSA_GQA * QBLOCK, DSA_NEAR), f32),
                        pltpu.VMEM((1, LANES), f32),
                        pltpu.VMEM((QBLOCK, lk), jnp.int16), pltpu.VMEM((QBLOCK, lk), jnp.int16)],
        compiler_params=pltpu.CompilerParams(
            dimension_semantics=("parallel", "arbitrary"), vmem_limit_bytes=VMEM_LIMIT_BYTES),
        name="dsa_prompt",
    )(rel_bias.astype(f32), q, qi, kiwi, k_t, v_h, ki_t)


DEC_PAGES_PER_STEP = 16
KV_WIDTH = DSA_KV_HEADS * HEAD_DIM


def _index_score(qi, wi, ki):
    d = _dot_nt(qi, ki)
    s = jnp.sum(wi * jnp.maximum(d, 0.0), axis=0, keepdims=True)
    return jnp.where(s == 0.0, 0.0, s)


def _dsa_dec_score_body(pt_ref, qi_ref, wi_ref, *refs):
    pages, s_ref = refs[:-1], refs[-1]
    qi = (qi_ref[...].astype(jnp.float32) * IDX_DIM ** -0.5).astype(jnp.bfloat16)
    wi = wi_ref[...]
    s_ref[...] = jnp.concatenate([_index_score(qi, wi, p[...].astype(jnp.bfloat16)) for p in pages], axis=0)


def _dsa_dec_select_body(n_top, qi_ref, wi_ref, kinew_ref, s_ref, sel_ref, snew_ref):
    f32, bf16 = jnp.float32, jnp.bfloat16
    bsz, past = s_ref.shape
    qi = (qi_ref[...].astype(f32) * IDX_DIM ** -0.5).astype(bf16).astype(f32)
    d_new = jnp.sum(qi * kinew_ref[...].astype(bf16).astype(f32), axis=-1, keepdims=True)
    s_new = jnp.sum(wi_ref[...] * jnp.maximum(d_new, 0.0), axis=1)
    s_new = jnp.where(s_new == 0.0, 0.0, s_new)
    key_new = _sortable_key(s_new)
    keys = _sortable_key(s_ref[...])
    kpos = lax.broadcasted_iota(jnp.int32, (bsz, past), 1)

    def count(pred_past, pred_new):
        part = jnp.where(pred_past, 1.0, 0.0)
        acc = part[:, 0:LANES]
        for j in range(1, past // LANES):
            acc = acc + part[:, j * LANES:(j + 1) * LANES]
        return jnp.sum(acc, axis=-1, keepdims=True) + jnp.where(pred_new, 1.0, 0.0)

    def thr_bit(it, thr):
        cand = thr + lax.shift_left(jnp.int32(1), 31 - it)
        return jnp.where(count(keys >= cand, key_new >= cand) >= n_top, cand, thr)

    thr = lax.fori_loop(0, 32, thr_bit, jnp.full((bsz, 1), INT32_MIN, jnp.int32))
    need = n_top - count(keys > thr, key_new > thr)

    def idx_bit(it, j):
        cand = j + lax.shift_left(jnp.int32(1), 14 - it)
        cnt = count((keys == thr) & (kpos < cand), (key_new == thr) & (past < cand))
        return jnp.where(cnt <= need, cand, j)

    n_eq = count(keys == thr, key_new == thr)
    tie_end = lax.cond(jnp.max(jnp.where(n_eq > need, 1.0, 0.0)) > 0.0,
                       lambda: lax.fori_loop(0, 15, idx_bit, jnp.zeros((bsz, 1), jnp.int32)),
                       lambda: jnp.full((bsz, 1), 2 ** 15, jnp.int32))
    lane = lax.broadcasted_iota(jnp.int32, (bsz, LANES), 1)
    sel_ref[...] = jnp.where(lane == 0, thr, jnp.where(lane == 1, tie_end, 0))
    snew_ref[...] = jnp.broadcast_to(s_new, (bsz, LANES))


def _dsa_dec_attend_body(n_pages, pt_ref, rb_ref, q_ref, knew_ref, vnew_ref, sel_ref, snew_ref, s_ref, *refs):
    pp = DEC_PAGES_PER_STEP if n_pages >= DEC_PAGES_PER_STEP else n_pages
    k_pages, v_pages = refs[:pp], refs[pp:2 * pp]
    o_ref, m_ref, l_ref, acc_ref = refs[2 * pp:]
    f32, bf16 = jnp.float32, jnp.bfloat16
    t = pl.program_id(1)
    nt = pl.num_programs(1)
    past = n_pages * PAGE_SIZE

    @pl.when(t == 0)
    def _():
        m_ref[...] = jnp.full(m_ref.shape, NEG_BIG, f32)
        l_ref[...] = jnp.zeros(l_ref.shape, f32)
        acc_ref[...] = jnp.zeros(acc_ref.shape, f32)

    thr, tie_end = sel_ref[:, 0:1], sel_ref[:, 1:2]
    q = (q_ref[...].astype(f32) * HEAD_DIM ** -0.5).astype(bf16)
    lane12 = lax.broadcasted_iota(jnp.int32, (DSA_HEADS, LANES), 1)

    def head_bias(dist):
        bucket = t5_bucket(dist)
        b = jnp.zeros(dist.shape, f32)
        for h in range(DSA_HEADS):
            row = jnp.zeros(dist.shape, f32)
            for bk in range(N_BUCKETS):
                row = jnp.where(bucket == bk, rb_ref[bk, h] - rb_ref[N_BUCKETS - 1, h], row)
            b = jnp.where(lax.broadcasted_iota(jnp.int32, dist.shape, 0) == h, row, b)
        return b

    keys_step = _sortable_key(s_ref[pl.ds(pl.multiple_of(t * pp, pp), pp), :])
    tiles = []
    for j in range(pp):
        kpos = (t * pp + j) * PAGE_SIZE + lax.broadcasted_iota(jnp.int32, (1, LANES), 1)
        keys = keys_step[j:j + 1]
        sel = (keys > thr) | ((keys == thr) & (kpos < tie_end))
        logits = _dot_nt(q, k_pages[j][...].astype(bf16))
        if j == pp - 1:
            near = head_bias(past - (t * pp + j) * PAGE_SIZE - lane12)
            logits = logits + jnp.where(t == nt - 1, near, 0.0)
        tiles.append(jnp.where(sel, logits, NEG_BIG))
    m = m_ref[...]
    tile_max = tiles[0]
    for x in tiles[1:]:
        tile_max = jnp.maximum(tile_max, x)
    m_new = jnp.maximum(m, jnp.max(tile_max, axis=-1, keepdims=True))
    alpha = jnp.exp(m - m_new)
    ps = [jnp.exp(x - m_new) for x in tiles]
    p_sum = ps[0]
    for x in ps[1:]:
        p_sum = p_sum + x
    pv = jnp.dot(ps[0].astype(bf16), v_pages[0][...].astype(bf16), preferred_element_type=f32)
    for j in range(1, pp):
        pv = pv + jnp.dot(ps[j].astype(bf16), v_pages[j][...].astype(bf16), preferred_element_type=f32)
    l_ref[...] = alpha * l_ref[...] + jnp.sum(p_sum, axis=-1, keepdims=True)
    acc_ref[...] = alpha * acc_ref[...] + pv
    m_ref[...] = m_new

    @pl.when(t == nt - 1)
    def _():
        key_new = _sortable_key(snew_ref[:, 0:1])
        sel_new = (key_new > thr) | ((key_new == thr) & (past < tie_end))
        k_new = knew_ref[...].astype(bf16).astype(f32)
        logit_new = jnp.sum(q.astype(f32) * k_new, axis=-1, keepdims=True)
        logit_new = logit_new + head_bias(jnp.zeros((DSA_HEADS, LANES), jnp.int32))[:, :1]
        logit_new = jnp.where(sel_new, logit_new, NEG_BIG)
        m = m_ref[...]
        m_new = jnp.maximum(m, logit_new)
        alpha = jnp.exp(m - m_new)
        p_new = jnp.exp(logit_new - m_new)
        l = alpha * l_ref[...] + p_new
        v_new = vnew_ref[...].astype(bf16).astype(f32)
        acc = alpha * acc_ref[...] + p_new.astype(bf16).astype(f32) * v_new
        o = acc / l
        o_ref[...] = jnp.concatenate(
            [o[h:h + 1, (h // DSA_GQA) * HEAD_DIM:(h // DSA_GQA + 1) * HEAD_DIM] for h in range(DSA_HEADS)], axis=-1)


def dsa_decode_pallas(q, qi, kiwi, k_new, v_new, pool_k, pool_v, pool_ki, page_table, rel_bias):
    bsz = q.shape[0]
    n_pages = page_table.shape[1]
    n_phys = pool_k.shape[0]
    pp = DEC_PAGES_PER_STEP if n_pages >= DEC_PAGES_PER_STEP else n_pages
    nt = n_pages // pp
    n_top = min(TOPK_MAX, (n_pages * PAGE_SIZE + 1) // 4)
    f32, bf16 = jnp.float32, jnp.bfloat16
    eye = jnp.eye(DSA_KV_HEADS, dtype=q.dtype)
    q_bd = jnp.einsum('bkgd,kj->bkgjd', q.reshape(bsz, DSA_KV_HEADS, DSA_GQA, HEAD_DIM), eye)
    q_bd = q_bd.reshape(bsz, DSA_HEADS, KV_WIDTH)
    qi3 = qi.reshape(bsz, IDX_HEADS, IDX_DIM)
    wi3 = (kiwi[:, IDX_DIM:IDX_DIM + IDX_HEADS] * IDX_HEADS ** -0.5).reshape(bsz, IDX_HEADS, 1)
    ki_new = kiwi[:, :IDX_DIM].reshape(bsz, 1, IDX_DIM)
    pk = pool_k.reshape(n_phys, PAGE_SIZE, KV_WIDTH)
    pv = pool_v.reshape(n_phys, PAGE_SIZE, KV_WIDTH)
    per_b = lambda *shape: pl.BlockSpec((None,) + shape, lambda b, t, pt: (b,) + (0,) * len(shape))
    page = lambda width, j: pl.BlockSpec((None, PAGE_SIZE, width), lambda b, t, pt: (pt[b, t * pp + j], 0, 0))
    scores = pl.pallas_call(
        _dsa_dec_score_body,
        grid_spec=pltpu.PrefetchScalarGridSpec(
            num_scalar_prefetch=1, grid=(bsz, nt),
            in_specs=[per_b(IDX_HEADS, IDX_DIM), per_b(IDX_HEADS, 1)] + [page(IDX_DIM, j) for j in range(pp)],
            out_specs=pl.BlockSpec((None, pp, LANES), lambda b, t, pt: (b, t, 0))),
        out_shape=jax.ShapeDtypeStruct((bsz, n_pages, LANES), f32),
        compiler_params=pltpu.CompilerParams(
            dimension_semantics=("parallel", "parallel"), vmem_limit_bytes=VMEM_LIMIT_BYTES),
        name="dsa_dec_score",
    )(page_table, qi3, wi3, *([pool_ki] * pp))
    sel, s_new = pl.pallas_call(
        functools.partial(_dsa_dec_select_body, n_top),
        grid=(1,),
        in_specs=[pl.BlockSpec((bsz, IDX_HEADS, IDX_DIM), lambda i: (0, 0, 0)),
                  pl.BlockSpec((bsz, IDX_HEADS, 1), lambda i: (0, 0, 0)),
                  pl.BlockSpec((bsz, 1, IDX_DIM), lambda i: (0, 0, 0)),
                  pl.BlockSpec((bsz, n_pages * PAGE_SIZE), lambda i: (0, 0))],
        out_specs=[pl.BlockSpec((bsz, LANES), lambda i: (0, 0))] * 2,
        out_shape=[jax.ShapeDtypeStruct((bsz, LANES), jnp.int32), jax.ShapeDtypeStruct((bsz, LANES), f32)],
        compiler_params=pltpu.CompilerParams(vmem_limit_bytes=VMEM_LIMIT_BYTES),
        name="dsa_dec_select",
    )(qi3, wi3, ki_new, scores.reshape(bsz, n_pages * PAGE_SIZE))
    o = pl.pallas_call(
        functools.partial(_dsa_dec_attend_body, n_pages),
        grid_spec=pltpu.PrefetchScalarGridSpec(
            num_scalar_prefetch=1, grid=(bsz, nt),
            in_specs=[pl.BlockSpec(memory_space=pltpu.SMEM), per_b(DSA_HEADS, KV_WIDTH), per_b(1, KV_WIDTH),
                      per_b(1, KV_WIDTH), per_b(1, LANES), per_b(1, LANES), per_b(n_pages, LANES)]
            + [page(KV_WIDTH, j) for j in range(pp)] * 2,
            out_specs=per_b(1, SEQ_WIDTH),
            scratch_shapes=[pltpu.VMEM((DSA_HEADS, 1), f32), pltpu.VMEM((DSA_HEADS, 1), f32),
                            pltpu.VMEM((DSA_HEADS, KV_WIDTH), f32)]),
        out_shape=jax.ShapeDtypeStruct((bsz, 1, SEQ_WIDTH), f32),
        compiler_params=pltpu.CompilerParams(
            dimension_semantics=("parallel", "arbitrary"), vmem_limit_bytes=VMEM_LIMIT_BYTES),
        name="dsa_dec_attend",
    )(page_table, rel_bias.astype(f32), q_bd, k_new.reshape(bsz, 1, KV_WIDTH), v_new.reshape(bsz, 1, KV_WIDTH),
      sel.reshape(bsz, 1, LANES), s_new.reshape(bsz, 1, LANES), scores, *([pk] * pp), *([pv] * pp))
    return o


CONV_HALO = 8


def _gdn_prep_body(x_ref, ab_ref, ctx_ref, w_ref, alog_ref, dtb_ref, q_ref, k_ref, v_ref, gb_ref, cs_ref, xp_ref):
    tt = x_ref.shape[0]
    halo = CONV_W - 1

    @pl.when(pl.program_id(1) == 0)
    def _():
        xp_ref[CONV_HALO - halo:CONV_HALO, :] = ctx_ref[...]

    x = x_ref[...]
    xp_ref[CONV_HALO:CONV_HALO + tt, :] = x
    w = w_ref[...]
    y = xp_ref[CONV_HALO - halo:CONV_HALO - halo + tt, :] * w[0:1]
    for j in range(1, CONV_W):
        y = y + xp_ref[CONV_HALO - halo + j:CONV_HALO - halo + j + tt, :] * w[j:j + 1]
    last = x[tt - halo:, :]
    xp_ref[CONV_HALO - halo:CONV_HALO, :] = last
    cs_ref[...] = last
    y = y * _sigmoid(y)

    def l2n(a, scale):
        cols = []
        for h in range(GDN_HEADS):
            s = a[:, h * HEAD_DIM:(h + 1) * HEAD_DIM]
            cols.append(s * (lax.rsqrt(jnp.sum(s * s, axis=-1, keepdims=True) + EPS) * scale))
        return jnp.concatenate(cols, axis=-1)

    q_ref[...] = l2n(y[:, :SEQ_WIDTH], HEAD_DIM ** -0.5)
    k_ref[...] = l2n(y[:, SEQ_WIDTH:2 * SEQ_WIDTH], 1.0)
    v_ref[...] = y[:, 2 * SEQ_WIDTH:]
    ab = ab_ref[...]
    xa = ab + dtb_ref[...]
    softplus = jnp.maximum(xa, 0.0) + jnp.log(1.0 + jnp.exp(-jnp.abs(xa)))
    g = -jnp.exp(alog_ref[...]) * softplus
    lane = lax.broadcasted_iota(jnp.int32, ab.shape, 1)
    gb_ref[...] = jnp.where(lane < GDN_HEADS, g, _sigmoid(ab))


def gdn_prep_pallas(qkv, ab, ctx, conv_w, a_log, dt_bias, *, tt=256):
    bsz, L, ch = qkv.shape
    tt = min(tt, L)
    f32 = jnp.float32
    pad = lambda r: jnp.pad(r.astype(f32).reshape(1, -1), ((0, 0), (0, LANES - r.shape[-1])))
    tok = lambda n: pl.BlockSpec((None, tt, n), lambda b, t: (b, t, 0))
    const2 = lambda b, t: (0, 0)
    return pl.pallas_call(
        _gdn_prep_body,
        grid=(bsz, L // tt),
        in_specs=[tok(ch), tok(LANES), pl.BlockSpec((None, CONV_W - 1, ch), lambda b, t: (b, 0, 0)),
                  pl.BlockSpec((CONV_W, ch), const2), pl.BlockSpec((1, LANES), const2), pl.BlockSpec((1, LANES), const2)],
        out_specs=[tok(SEQ_WIDTH), tok(SEQ_WIDTH), tok(SEQ_WIDTH), tok(LANES),
                   pl.BlockSpec((None, CONV_W - 1, ch), lambda b, t: (b, 0, 0))],
        out_shape=[jax.ShapeDtypeStruct((bsz, L, SEQ_WIDTH), f32)] * 3
        + [jax.ShapeDtypeStruct((bsz, L, LANES), f32), jax.ShapeDtypeStruct((bsz, CONV_W - 1, ch), f32)],
        scratch_shapes=[pltpu.VMEM((CONV_HALO + tt, ch), f32)],
        compiler_params=pltpu.CompilerParams(
            dimension_semantics=("parallel", "arbitrary"), vmem_limit_bytes=VMEM_LIMIT_BYTES),
        name="gdn_prep",
    )(qkv, ab, ctx.astype(f32), conv_w.astype(f32), pad(a_log), pad(dt_bias))


def _split3(a):
    bf16, f32 = jnp.bfloat16, jnp.float32
    h = a.astype(bf16)
    r = a - h.astype(f32)
    m = r.astype(bf16)
    return h, m, (r - m.astype(f32)).astype(bf16)


def _mm_hi(a, b):
    f32 = jnp.float32
    ah, am, _ = _split3(a)
    bh, bm, _ = _split3(b)
    d = lambda x, y: jnp.dot(x, y, preferred_element_type=f32)
    return d(ah, bh) + (d(ah, bm) + d(am, bh))


def _mm_sel(sel, b):
    f32 = jnp.float32
    s = sel.astype(jnp.bfloat16)
    bh, bm, bl = _split3(b)
    d = lambda y: jnp.dot(s, y, preferred_element_type=f32)
    return d(bh) + (d(bm) + d(bl))


def _dot_nt(a, b):
    return lax.dot_general(a, b, (((1,), (1,)), ((), ())), preferred_element_type=jnp.float32)


def _dot_tn(a, b):
    return lax.dot_general(a, b, (((0,), (0,)), ((), ())), preferred_element_type=jnp.float32)


def _gdn_local_body(q_ref, k_ref, v_ref, gb_ref, uv_ref, wk_ref, qh_ref, kt_ref, qk_ref):
    f32, bf16 = jnp.float32, jnp.bfloat16
    c = GDN_CHUNK
    r_i = lax.broadcasted_iota(jnp.int32, (c, c), 0)
    c_i = lax.broadcasted_iota(jnp.int32, (c, c), 1)
    tril = r_i >= c_i
    stril = r_i > c_i
    triu = r_i <= c_i
    eye = jnp.where(r_i == c_i, 1.0, 0.0)
    ones = jnp.ones((c, c), jnp.bool_)
    lane = lax.broadcasted_iota(jnp.int32, (c, LANES), 1)
    gb = gb_ref[...]
    heads = range(GDN_HEADS)
    sl = [slice(h * HEAD_DIM, (h + 1) * HEAD_DIM) for h in heads]
    pick = lambda idx: jnp.sum(jnp.where(lane == idx, gb, 0.0), axis=-1, keepdims=True)
    g_col = [pick(h) for h in heads]
    beta = [pick(GDN_HEADS + h) for h in heads]
    q = [q_ref[:, sl[h]] for h in heads]
    k = [k_ref[:, sl[h]] for h in heads]
    v = [v_ref[:, sl[h]] for h in heads]
    g_mat = [jnp.broadcast_to(g_col[h], (c, c)) for h in heads]
    cum_col = [_mm_sel(tril, g_mat[h]) for h in heads]
    cum_row = [_mm_sel(ones, jnp.where(triu, g_mat[h], 0.0)) for h in heads]
    gam = [jnp.where(tril, jnp.exp(jnp.where(tril, cum_col[h] - cum_row[h], 0.0)), 0.0) for h in heads]
    kb = [k[h].astype(bf16) for h in heads]
    a_mat = [jnp.where(stril, beta[h] * _dot_nt(kb[h], kb[h]) * gam[h], 0.0) for h in heads]
    t_inv = [eye - a_mat[h] for h in heads]
    pw = a_mat
    for _ in range(5):
        pw = [_mm_hi(pw[h], pw[h]) for h in heads]
        t_inv = [t_inv[h] + _mm_hi(t_inv[h], pw[h]) for h in heads]
    gc = [cum_col[h][:, :1] for h in heads]
    egc = [jnp.exp(gc[h]) for h in heads]
    tw = [_mm_hi(t_inv[h], jnp.concatenate([beta[h] * v[h], (beta[h] * egc[h]) * k[h]], axis=-1)) for h in heads]
    qk = [jnp.where(tril, _dot_nt(q[h].astype(bf16), kb[h]) * gam[h], 0.0) for h in heads]
    cat = lambda xs: jnp.concatenate(xs, axis=-1)
    uv_ref[...] = cat([tw[h][:, :HEAD_DIM] for h in heads])
    wk_ref[...] = cat([tw[h][:, HEAD_DIM:] for h in heads]).astype(bf16)
    qh_ref[...] = cat([egc[h] * q[h] for h in heads]).astype(bf16)
    eye_b = eye.astype(bf16)
    kt = [(jnp.exp(gc[h][c - 1:c, :] - gc[h]) * k[h]).astype(bf16) for h in heads]
    kt_ref[...] = cat([_dot_tn(kt[h], eye_b) for h in heads]).astype(bf16)
    qk_ref[...] = cat(qk).astype(bf16)


def gdn_local_pallas(qn, kn, vv, gb):
    bsz, L, w = qn.shape
    c = GDN_CHUNK
    f32, bf16 = jnp.float32, jnp.bfloat16
    tok = lambda n: pl.BlockSpec((None, c, n), lambda b, t: (b, t, 0))
    return pl.pallas_call(
        _gdn_local_body,
        grid=(bsz, L // c),
        in_specs=[tok(w), tok(w), tok(w), tok(LANES)],
        out_specs=[tok(w)] * 5,
        out_shape=[jax.ShapeDtypeStruct((bsz, L, w), f32)] + [jax.ShapeDtypeStruct((bsz, L, w), bf16)] * 4,
        compiler_params=pltpu.CompilerParams(
            dimension_semantics=("parallel", "parallel"), vmem_limit_bytes=VMEM_LIMIT_BYTES),
        name="gdn_local",
    )(qn, kn, vv, gb)


def _gdn_scan_body(uv_ref, wk_ref, qh_ref, ktt_ref, qk_ref, gb_ref, gate_ref, s0_ref, on_ref, o_ref, sf_ref, s_ref):
    f32, bf16 = jnp.float32, jnp.bfloat16
    c = GDN_CHUNK
    tt = uv_ref.shape[0]
    heads = range(GDN_HEADS)
    sl = [slice(h * HEAD_DIM, (h + 1) * HEAD_DIM) for h in heads]

    @pl.when(pl.program_id(1) == 0)
    def _():
        s_ref[...] = s0_ref[...]

    lane = lax.broadcasted_iota(jnp.int32, (1, LANES), 1)
    dot = lambda a, b: jnp.dot(a, b, preferred_element_type=f32)

    def chunk(ci, _):
        rows = pl.ds(pl.multiple_of(ci * c, c), c)
        eg_last = jnp.exp(jnp.sum(gb_ref[rows, :], axis=0, keepdims=True))
        eg = [jnp.sum(jnp.where(lane == h, eg_last, 0.0), axis=-1, keepdims=True) for h in heads]
        st = [s_ref[h] for h in heads]
        sb = [st[h].astype(bf16) for h in heads]
        u = [uv_ref[rows, sl[h]] - dot(wk_ref[rows, sl[h]], sb[h]) for h in heads]
        ub = [u[h].astype(bf16) for h in heads]
        new = [eg[h] * st[h] + dot(ktt_ref[rows, sl[h]], ub[h]) for h in heads]
        for h in heads:
            s_ref[h] = new[h]
        o = [dot(qh_ref[rows, sl[h]], sb[h]) + dot(qk_ref[rows, sl[h]], ub[h]) for h in heads]
        o = [o[h] * lax.rsqrt(jnp.mean(o[h] * o[h], axis=-1, keepdims=True) + EPS) for h in heads]
        gt = gate_ref[rows, :]
        o_ref[rows, :] = (jnp.concatenate(o, axis=-1) * on_ref[...]) * (gt * _sigmoid(gt))
        return 0

    lax.fori_loop(0, tt // c, chunk, 0)
    sf_ref[...] = s_ref[...]


def gdn_scan_pallas(uv, wk, qh, kt, qk, gb, gate, s0, o_norm, *, tt=256):
    bsz, L, w = uv.shape
    tt = min(tt, L)
    f32 = jnp.float32
    tok = lambda n: pl.BlockSpec((None, tt, n), lambda b, t: (b, t, 0))
    st = pl.BlockSpec((None, GDN_HEADS, HEAD_DIM, HEAD_DIM), lambda b, t: (b, 0, 0, 0))
    on = jnp.tile(o_norm.astype(f32), GDN_HEADS).reshape(1, w)
    return pl.pallas_call(
        _gdn_scan_body,
        grid=(bsz, L // tt),
        in_specs=[tok(w)] * 5 + [tok(LANES), tok(w), st, pl.BlockSpec((1, w), lambda b, t: (0, 0))],
        out_specs=[tok(w), st],
        out_shape=[jax.ShapeDtypeStruct((bsz, L, w), f32), jax.ShapeDtypeStruct(s0.shape, f32)],
        scratch_shapes=[pltpu.VMEM((GDN_HEADS, HEAD_DIM, HEAD_DIM), f32)],
        compiler_params=pltpu.CompilerParams(
            dimension_semantics=("parallel", "arbitrary"), vmem_limit_bytes=VMEM_LIMIT_BYTES),
        name="gdn_scan",
    )(uv, wk, qh, kt, qk, gb, gate, s0.astype(f32).swapaxes(2, 3), on)


def gdn_mix_pallas(qkv, ab, gate, conv_ctx, s0, conv_w, a_log, dt_bias, o_norm):
    qn, kn, vv, gb, conv_state = gdn_prep_pallas(qkv, ab, conv_ctx, conv_w, a_log, dt_bias)
    uv, wk, qh, kt, qk = gdn_local_pallas(qn, kn, vv, gb)
    o, s_fin_t = gdn_scan_pallas(uv, wk, qh, kt, qk, gb, gate, s0, o_norm)
    return o, conv_state.astype(qkv.dtype), s_fin_t.swapaxes(2, 3).astype(s0.dtype)


def _gdn_dec_body(x_ref, ab_ref, gate_ref, ctx_ref, s_ref, w_ref, alog_ref, dtb_ref, on_ref, o_ref, cs_ref, sn_ref):
    f32, bf16 = jnp.float32, jnp.bfloat16
    rnd = lambda a: a.astype(bf16).astype(f32)
    x, ctx, w = x_ref[...], ctx_ref[...], w_ref[...]
    halo = CONV_W - 1
    y = x * w[halo:halo + 1]
    for j in range(halo):
        y = y + ctx[j:j + 1] * w[j:j + 1]
    y = y * _sigmoid(y)
    cs_ref[...] = jnp.concatenate([ctx[1:], x], axis=0)
    ab = ab_ref[...]
    xa = ab + dtb_ref[...]
    g_all = -jnp.exp(alog_ref[...]) * (jnp.maximum(xa, 0.0) + jnp.log(1.0 + jnp.exp(-jnp.abs(xa))))
    beta_all = _sigmoid(ab)
    lane = lax.broadcasted_iota(jnp.int32, (1, LANES), 1)
    pick = lambda a, idx: jnp.sum(jnp.where(lane == idx, a, 0.0), axis=-1, keepdims=True)
    r_i = lax.broadcasted_iota(jnp.int32, (HEAD_DIM, HEAD_DIM), 0)
    c_i = lax.broadcasted_iota(jnp.int32, (HEAD_DIM, HEAD_DIM), 1)
    eye = r_i == c_i
    to_col = lambda row: jnp.sum(jnp.where(eye, jnp.broadcast_to(row, (HEAD_DIM, HEAD_DIM)), 0.0), axis=-1, keepdims=True)
    to_row = lambda col: jnp.sum(jnp.where(eye, jnp.broadcast_to(col, (HEAD_DIM, HEAD_DIM)), 0.0), axis=0, keepdims=True)
    outs = []
    for h in range(GDN_HEADS):
        sl = slice(h * HEAD_DIM, (h + 1) * HEAD_DIM)
        q, k, v = y[:, sl], y[:, SEQ_WIDTH + h * HEAD_DIM:SEQ_WIDTH + (h + 1) * HEAD_DIM], y[:, 2 * SEQ_WIDTH + h * HEAD_DIM:2 * SEQ_WIDTH + (h + 1) * HEAD_DIM]
        q = q * (lax.rsqrt(jnp.sum(q * q, axis=-1, keepdims=True) + EPS) * HEAD_DIM ** -0.5)
        k = k * lax.rsqrt(jnp.sum(k * k, axis=-1, keepdims=True) + EPS)
        eg = jnp.exp(pick(g_all, h))
        beta = pick(beta_all, GDN_HEADS + h)
        s = s_ref[h]
        sb = rnd(s)
        s_wk = jnp.sum(sb * rnd((beta * eg) * k), axis=-1, keepdims=True)
        u = to_col(beta * v) - s_wk
        ub = rnd(u)
        s_q = jnp.sum(sb * rnd(eg * q), axis=-1, keepdims=True)
        qk = jnp.sum(rnd(q) * rnd(k), axis=-1, keepdims=True)
        o = to_row(s_q + rnd(qk) * ub)
        sn_ref[h] = eg * s + ub * rnd(k)
        outs.append(o * lax.rsqrt(jnp.mean(o * o, axis=-1, keepdims=True) + EPS))
    gt = gate_ref[...]
    o_ref[...] = (jnp.concatenate(outs, axis=-1) * on_ref[...]) * (gt * _sigmoid(gt))


def gdn_decode_pallas(qkv, ab, gate, conv_ctx, s0, conv_w, a_log, dt_bias, o_norm):
    bsz = qkv.shape[0]
    f32 = jnp.float32
    pad = lambda r: jnp.pad(r.astype(f32).reshape(1, -1), ((0, 0), (0, LANES - r.shape[-1])))
    per_b = lambda *shape: pl.BlockSpec((None,) + shape, lambda b: (b,) + (0,) * len(shape))
    const = lambda *shape: pl.BlockSpec(shape, lambda b: (0,) * len(shape))
    st = (GDN_HEADS, HEAD_DIM, HEAD_DIM)
    o, conv_state, s_new = pl.pallas_call(
        _gdn_dec_body,
        grid=(bsz,),
        in_specs=[per_b(1, CONV_CH), per_b(1, LANES), per_b(1, SEQ_WIDTH), per_b(CONV_W - 1, CONV_CH), per_b(*st),
                  const(CONV_W, CONV_CH), const(1, LANES), const(1, LANES), const(1, SEQ_WIDTH)],
        out_specs=[per_b(1, SEQ_WIDTH), per_b(CONV_W - 1, CONV_CH), per_b(*st)],
        out_shape=[jax.ShapeDtypeStruct((bsz, 1, SEQ_WIDTH), f32), jax.ShapeDtypeStruct((bsz, CONV_W - 1, CONV_CH), f32),
                   jax.ShapeDtypeStruct((bsz,) + st, f32)],
        compiler_params=pltpu.CompilerParams(dimension_semantics=("parallel",), vmem_limit_bytes=VMEM_LIMIT_BYTES),
        name="gdn_decode",
    )(qkv, ab, gate, conv_ctx.astype(f32), s0.astype(f32), conv_w.astype(f32), pad(a_log), pad(dt_bias),
      jnp.tile(o_norm.astype(f32), GDN_HEADS).reshape(1, SEQ_WIDTH))
    return o, conv_state.astype(qkv.dtype), s_new.astype(s0.dtype)


def _pad_cols(w, n):
    return jnp.pad(w, ((0, 0), (0, n - w.shape[1])))


def _in_proj_layout(kind, w_in):
    f32, bf16 = jnp.float32, jnp.bfloat16
    if kind == 0:
        return w_in, ((0, MEM_WIDTH), (MEM_WIDTH, SEQ_WIDTH)), (bf16, f32)
    if kind == 1:
        w = _pad_cols(w_in, MEM_WIDTH + sum(DSA_SPLITS[:4]) + LANES)
        widths = (MEM_WIDTH,) + DSA_SPLITS[:4] + (LANES,)
        starts = np.cumsum((0,) + widths[:-1]).tolist()
        return w, tuple(zip(starts, widths)), (bf16, bf16, f32, f32, bf16, f32)
    c0, c1 = MEM_WIDTH + CONV_CH, MEM_WIDTH + CONV_CH + 2 * GDN_HEADS
    w = jnp.concatenate([w_in[:, :c0], w_in[:, c1:], _pad_cols(w_in[:, c0:c1], LANES)], axis=1)
    widths = (MEM_WIDTH, CONV_CH, SEQ_WIDTH, LANES)
    starts = np.cumsum((0,) + widths[:-1]).tolist()
    return w, tuple(zip(starts, widths)), (bf16, f32, f32, f32)


def kernel(x_prompt, x_sample, cache_mem_k, cache_mem_v, state_ssm_re, state_ssm_im, cache_k, cache_v, cache_kidx, state_conv, state_delta, page_table, mem_prompt, norm_g, final_norm, w_in_a, w_in_b, w_in_c, w_out, w_mem_kv, ffn1_gate, ffn1_up, ffn1_down, ffn2_gate, ffn2_up, ffn2_down, s5_lam_re, s5_lam_im, s5_log_step, s5_b_re, s5_b_im, s5_c_re, s5_c_im, s5_d, s5_w_glu, s5_b_glu, rel_bias, gdn_conv_w, gdn_a_log, gdn_dt_bias, gdn_o_norm):
    depth = norm_g.shape[0]
    bp, lp, d = x_prompt.shape
    bs, ls, _ = x_sample.shape
    assert ls == 1, "the decode-step kernels handle one new token per sample"
    f32, bf16 = jnp.float32, jnp.bfloat16
    hp, hs = x_prompt.reshape(bp * lp, d), x_sample.reshape(bs * ls, d)
    mem_rows = mem_prompt.reshape(bp * N_MEM, d)
    mem_k_p, mem_v_p = [], []
    ssm_re_p, ssm_im_p, ssm_re_s, ssm_im_s = [], [], [], []
    k_p, v_p, ki_p, k_s, v_s, ki_s = [], [], [], [], [], []
    conv_p, delta_p, conv_s, delta_s = [], [], [], []
    w_in_by_kind = (w_in_a, w_in_b, w_in_c)
    for i in range(depth):
        kind, j = i % N_MIXERS, i // N_MIXERS
        last = i == depth - 1
        ffn1 = (ffn1_gate[i].astype(bf16), ffn1_up[i].astype(bf16), ffn1_down[i].astype(bf16))
        ffn2 = (ffn2_gate[i].astype(bf16), ffn2_up[i].astype(bf16), ffn2_down[i].astype(bf16))
        w_in, segments, dtypes = _in_proj_layout(kind, w_in_by_kind[kind][j])
        w_in = w_in.astype(bf16)
        hp = ffn_residual(hp, norm_g[i, 0], *ffn1)
        hs = ffn_residual(hs, norm_g[i, 0], *ffn1)
        zp = [a.reshape(bp, lp, -1) for a in proj_in(hp, norm_g[i, 1], w_in, segments, dtypes)]
        zs = [a.reshape(bs, ls, -1) for a in proj_in(hs, norm_g[i, 1], w_in, segments, dtypes)]
        mk, mv = proj_in(mem_rows, norm_g[i, 1], w_mem_kv[i].astype(bf16), ((0, MEM_WIDTH), (MEM_WIDTH, MEM_WIDTH)),
                         (f32, f32), normalize=False)
        mem_shape = (bp, N_MEM, MEM_HEADS, HEAD_DIM)
        mk, mv = mk.reshape(mem_shape), mv.reshape(mem_shape)
        mem_k_p.append(mk)
        mem_v_p.append(mv)
        if kind == 0:
            disc = s5_discretize(s5_lam_re[j], s5_lam_im[j], s5_log_step[j], s5_b_re[j], s5_b_im[j],
                                 s5_c_re[j], s5_c_im[j])
            gate = (s5_d[j], s5_w_glu[j], s5_b_glu[j])
            h0 = jnp.zeros((bp, S5_GROUPS, S5_STATE), state_ssm_re.dtype)
            op, hr, hi = s5_mix_pallas(zp[1], h0, h0, disc, *gate)
            ssm_re_p.append(hr)
            ssm_im_p.append(hi)
            osm, hr, hi = s5_mix_pallas(zs[1], state_ssm_re[j], state_ssm_im[j], disc, *gate)
            ssm_re_s.append(hr)
            ssm_im_s.append(hi)
        elif kind == 1:
            _, q, k, v, qi, kiwi = zp
            op = dsa_prompt_pallas(q, qi, kiwi, k, v, rel_bias)
            kv_shape = (DSA_KV_HEADS, HEAD_DIM)
            k_p.append(k.reshape((bp, lp) + kv_shape))
            v_p.append(v.reshape((bp, lp) + kv_shape))
            ki_p.append(kiwi[..., :IDX_DIM])
            _, q, k, v, qi, kiwi = zs
            osm = dsa_decode_pallas(q[:, 0], qi[:, 0], kiwi[:, 0], k[:, 0], v[:, 0], cache_k[j], cache_v[j],
                                    cache_kidx[j], page_table, rel_bias)
            k_s.append(k.reshape((bs, ls) + kv_shape))
            v_s.append(v.reshape((bs, ls) + kv_shape))
            ki_s.append(kiwi[..., :IDX_DIM])
        else:
            gdn = (gdn_conv_w[j], gdn_a_log[j], gdn_dt_bias[j], gdn_o_norm[j])
            ctx0 = jnp.zeros((bp, CONV_W - 1, CONV_CH), f32)
            s00 = jnp.zeros((bp, GDN_HEADS, HEAD_DIM, HEAD_DIM), state_delta.dtype)
            _, qkv, gate, ab = zp
            op, cst, sst = gdn_mix_pallas(qkv, ab, gate, ctx0, s00, *gdn)
            conv_p.append(cst)
            delta_p.append(sst)
            _, qkv, gate, ab = zs
            osm, cst, sst = gdn_decode_pallas(qkv, ab, gate, state_conv[j], state_delta[j], *gdn)
            conv_s.append(cst)
            delta_s.append(sst)
        w_o = w_out[i].astype(bf16)
        hp = mix_out(hp.reshape(bp, lp, d), zp[0], op, mk, mv, w_o).reshape(bp * lp, d)
        hs = mix_out(hs.reshape(bs, ls, d), zs[0], osm, cache_mem_k[i], cache_mem_v[i], w_o).reshape(bs * ls, d)
        fg = final_norm if last else None
        hp = ffn_residual(hp, norm_g[i, 2], *ffn2, fg)
        hs = ffn_residual(hs, norm_g[i, 2], *ffn2, fg)
    st = jnp.stack
    return (hp.reshape(bp, lp, d), hs.reshape(bs, ls, d), st(mem_k_p), st(mem_v_p),
            st(ssm_re_p), st(ssm_im_p), st(ssm_re_s), st(ssm_im_s),
            st(k_p), st(v_p), st(ki_p), st(k_s), st(v_s), st(ki_s),
            st(conv_p), st(delta_p), st(conv_s), st(delta_s))
```

```python
import math
import functools
import jax
import jax.numpy as jnp
from jax import lax
import numpy as np
from jax.experimental import pallas as pl
from jax.experimental.pallas import tpu as pltpu

D_MODEL = 1024
N_MIXERS = 3
HEAD_DIM = 64
MIX_WIDTH = D_MODEL
N_MEM = 256
MEM_HEADS = 4
MEM_WIDTH = MEM_HEADS * HEAD_DIM
SEQ_WIDTH = MIX_WIDTH - MEM_WIDTH
S5_GROUP = 16
S5_GROUPS = SEQ_WIDTH // S5_GROUP
S5_STATE = 64
DSA_HEADS = SEQ_WIDTH // HEAD_DIM
DSA_KV_HEADS = 4
DSA_GQA = DSA_HEADS // DSA_KV_HEADS
IDX_HEADS = 8
IDX_DIM = 64
TOPK_MAX = 256
QBLOCK = 128
N_BUCKETS = 32
MAX_DISTANCE = 128
GDN_HEADS = SEQ_WIDTH // HEAD_DIM
CONV_W = 4
CONV_CH = 3 * SEQ_WIDTH
GDN_CHUNK = 64
D_FF = 2816
EPS = 1e-6
PAGE_SIZE = 128
DSA_SPLITS = (DSA_HEADS * HEAD_DIM, DSA_KV_HEADS * HEAD_DIM, DSA_KV_HEADS * HEAD_DIM, IDX_HEADS * IDX_DIM, IDX_DIM, IDX_HEADS)
GDN_SPLITS = (CONV_CH, GDN_HEADS, GDN_HEADS, SEQ_WIDTH)

LANES = 128
VMEM_LIMIT_BYTES = 56 * 1024 * 1024


def _sigmoid(x):
    return 1.0 / (1.0 + jnp.exp(-x))


def t5_bucket(dist):
    max_exact = N_BUCKETS // 2
    n = jnp.maximum(dist, 0)
    nf = jnp.maximum(n, 1).astype(jnp.float32)
    large = max_exact + (jnp.log(nf / max_exact) / math.log(MAX_DISTANCE / max_exact)
                         * (N_BUCKETS - max_exact)).astype(jnp.int32)
    return jnp.where(n < max_exact, n, jnp.minimum(large, N_BUCKETS - 1))


FFN_CHUNK = 256


def _ffn_body(final_norm, x_ref, g_ref, wg_ref, wu_ref, wd_ref, gf_ref, o_ref):
    f32, bf16 = jnp.float32, jnp.bfloat16
    x = x_ref[...]
    n = ((x * lax.rsqrt(jnp.mean(x * x, axis=-1, keepdims=True) + EPS)) * g_ref[...]).astype(bf16)
    acc = jnp.zeros(x.shape, f32)
    for c in range(wg_ref.shape[1] // FFN_CHUNK):
        sl = slice(c * FFN_CHUNK, (c + 1) * FFN_CHUNK)
        a = jnp.dot(n, wg_ref[:, sl], preferred_element_type=f32)
        b = jnp.dot(n, wu_ref[:, sl], preferred_element_type=f32)
        h = (a * _sigmoid(a)) * b
        acc = acc + jnp.dot(h.astype(bf16), wd_ref[sl, :], preferred_element_type=f32)
    y = x + 0.5 * acc
    if final_norm:
        y = (y * lax.rsqrt(jnp.mean(y * y, axis=-1, keepdims=True) + EPS)) * gf_ref[...]
    o_ref[...] = y


def ffn_residual(x, g, wg, wu, wd, final_g=None, *, tm=512):
    t, d = x.shape
    f = wg.shape[1]
    tm = min(tm, t)
    gf = jnp.ones((d,), jnp.float32) if final_g is None else final_g
    resident = lambda shape: pl.BlockSpec(shape, lambda i: (0, 0), pipeline_mode=pl.Buffered(1))
    return pl.pallas_call(
        functools.partial(_ffn_body, final_g is not None),
        grid=(t // tm,),
        in_specs=[pl.BlockSpec((tm, d), lambda i: (i, 0)), resident((1, d)),
                  resident((d, f)), resident((d, f)), resident((f, d)), resident((1, d))],
        out_specs=pl.BlockSpec((tm, d), lambda i: (i, 0)),
        out_shape=jax.ShapeDtypeStruct((t, d), jnp.float32),
        compiler_params=pltpu.CompilerParams(dimension_semantics=("parallel",), vmem_limit_bytes=VMEM_LIMIT_BYTES),
        name="ffn_residual",
    )(x, g.reshape(1, d).astype(jnp.float32), wg, wu, wd, gf.reshape(1, d).astype(jnp.float32))


def _proj_in_body(segments, normalize, x_ref, g_ref, w_ref, *o_refs):
    x = x_ref[...]
    if normalize:
        r = lax.rsqrt(jnp.mean(x * x, axis=-1, keepdims=True) + EPS)
        x = (x * r) * g_ref[...]
    n = x.astype(jnp.bfloat16)
    for (start, width), o_ref in zip(segments, o_refs):
        o_ref[...] = jnp.dot(n, w_ref[:, start:start + width], preferred_element_type=jnp.float32).astype(o_ref.dtype)


def proj_in(x, g, w, segments, dtypes, *, normalize=True, tm=256):
    t, d = x.shape
    tm = min(tm, t)
    return pl.pallas_call(
        functools.partial(_proj_in_body, tuple(segments), normalize),
        grid=(t // tm,),
        in_specs=[pl.BlockSpec((tm, d), lambda i: (i, 0)), pl.BlockSpec((1, d), lambda i: (0, 0)),
                  pl.BlockSpec(w.shape, lambda i: (0, 0))],
        out_specs=[pl.BlockSpec((tm, width), lambda i: (i, 0)) for _, width in segments],
        out_shape=[jax.ShapeDtypeStruct((t, width), dt) for (_, width), dt in zip(segments, dtypes)],
        compiler_params=pltpu.CompilerParams(dimension_semantics=("parallel",), vmem_limit_bytes=VMEM_LIMIT_BYTES),
        name="proj_in",
    )(x, g.reshape(1, d).astype(jnp.float32), w)


def _mix_out_body(x_ref, cq_ref, om_ref, mkt_ref, mv_ref, w_ref, o_ref):
    f32, bf16 = jnp.float32, jnp.bfloat16
    cq = cq_ref[...]
    heads = []
    for h in range(MEM_HEADS):
        sl = slice(h * HEAD_DIM, (h + 1) * HEAD_DIM)
        logits = jnp.dot(cq[:, sl], mkt_ref[sl, :], preferred_element_type=f32) * HEAD_DIM ** -0.5
        p = jnp.exp(logits - jnp.max(logits, axis=-1, keepdims=True))
        p = p / jnp.sum(p, axis=-1, keepdims=True)
        heads.append(jnp.dot(p.astype(bf16), mv_ref[:, sl], preferred_element_type=f32))
    o_mem = jnp.concatenate(heads, axis=-1).astype(bf16)
    y = (jnp.dot(o_mem, w_ref[:MEM_WIDTH, :], preferred_element_type=f32)
         + jnp.dot(om_ref[...].astype(bf16), w_ref[MEM_WIDTH:, :], preferred_element_type=f32))
    o_ref[...] = x_ref[...] + y


def mix_out(x, cq, o_mix, mk, mv, w_out, *, tm=512):
    bsz, L, d = x.shape
    tm = min(tm, L)
    bf16 = jnp.bfloat16
    mkt = mk.astype(bf16).reshape(bsz, N_MEM, MEM_WIDTH).swapaxes(1, 2)
    mvf = mv.astype(bf16).reshape(bsz, N_MEM, MEM_WIDTH)
    tok = lambda n: pl.BlockSpec((None, tm, n), lambda b, t: (b, t, 0))
    per_b = lambda r, c: pl.BlockSpec((None, r, c), lambda b, t: (b, 0, 0))
    return pl.pallas_call(
        _mix_out_body,
        grid=(bsz, L // tm),
        in_specs=[tok(d), tok(MEM_WIDTH), tok(SEQ_WIDTH), per_b(MEM_WIDTH, N_MEM), per_b(N_MEM, MEM_WIDTH),
                  pl.BlockSpec((d, d), lambda b, t: (0, 0))],
        out_specs=tok(d),
        out_shape=jax.ShapeDtypeStruct((bsz, L, d), jnp.float32),
        compiler_params=pltpu.CompilerParams(
            dimension_semantics=("parallel", "parallel"), vmem_limit_bytes=VMEM_LIMIT_BYTES),
        name="mix_out",
    )(x, cq, o_mix, mkt, mvf, w_out)


S5_LANES = S5_GROUPS * S5_STATE
S5_GROUPS_PER_BLOCK = LANES // S5_GROUP
S5_BLOCKS = SEQ_WIDTH // LANES
S5_BLOCK_STATES = S5_GROUPS_PER_BLOCK * S5_STATE


def s5_discretize(lam_re, lam_im, log_step, b_re, b_im, c_re, c_im):
    f32, bf16 = jnp.float32, jnp.bfloat16
    lr, li = lam_re.astype(f32), lam_im.astype(f32)
    step = jnp.exp(log_step.astype(f32))[:, None]
    mag = jnp.exp(lr * step)
    ab_re, ab_im = mag * jnp.cos(li * step), mag * jnp.sin(li * step)
    den = lr * lr + li * li
    nr, ni = ab_re - 1.0, ab_im
    f_re = (nr * lr + ni * li) / den
    f_im = (ni * lr - nr * li) / den
    br, bi = b_re.astype(f32), b_im.astype(f32)
    bb_re = f_re[..., None] * br - f_im[..., None] * bi
    bb_im = f_re[..., None] * bi + f_im[..., None] * br
    eye = jnp.eye(S5_GROUPS_PER_BLOCK, dtype=f32)
    nb, gb = S5_BLOCKS, S5_GROUPS_PER_BLOCK

    def in_blocks(bb):
        w = jnp.einsum('jgpc,gh->jgchp', bb.reshape(nb, gb, S5_STATE, S5_GROUP), eye)
        return w.reshape(nb, LANES, S5_BLOCK_STATES).astype(bf16)

    def out_blocks(c):
        w = jnp.einsum('jgop,gh->jgpho', c.astype(f32).reshape(nb, gb, S5_GROUP, S5_STATE), eye)
        return w.reshape(nb, S5_BLOCK_STATES, LANES).astype(bf16)

    return (ab_re.reshape(1, S5_LANES), ab_im.reshape(1, S5_LANES),
            in_blocks(bb_re), in_blocks(bb_im), out_blocks(c_re), out_blocks(-c_im))


def _s5_project_in(u, wbr_ref, wbi_ref, bur_ref, bui_ref):
    ub = u.astype(jnp.bfloat16)
    for j in range(S5_BLOCKS):
        uj = ub[:, j * LANES:(j + 1) * LANES]
        sl = slice(j * S5_BLOCK_STATES, (j + 1) * S5_BLOCK_STATES)
        bur_ref[:, sl] = jnp.dot(uj, wbr_ref[j], preferred_element_type=jnp.float32)
        bui_ref[:, sl] = jnp.dot(uj, wbi_ref[j], preferred_element_type=jnp.float32)


def _s5_project_out(hr_ref, hi_ref, wcr_ref, wci_ref):
    cols = []
    for j in range(S5_BLOCKS):
        sl = slice(j * S5_BLOCK_STATES, (j + 1) * S5_BLOCK_STATES)
        cols.append(jnp.dot(hr_ref[:, sl].astype(jnp.bfloat16), wcr_ref[j], preferred_element_type=jnp.float32)
                    + jnp.dot(hi_ref[:, sl].astype(jnp.bfloat16), wci_ref[j], preferred_element_type=jnp.float32))
    return jnp.concatenate(cols, axis=-1)


def _s5_gate(y_ssm, u, d_ref, wglu_ref, bglu_ref):
    y = y_ssm + d_ref[...] * u
    y = 0.5 * y * (1.0 + jnp.tanh(math.sqrt(2.0 / math.pi) * (y + 0.044715 * (y * y * y))))
    z = jnp.dot(y.astype(jnp.bfloat16), wglu_ref[...], preferred_element_type=jnp.float32) + bglu_ref[...]
    return y * (1.0 / (1.0 + jnp.exp(-z)))


def _s5_seq_body(u_ref, h0r_ref, h0i_ref, ar_ref, ai_ref, wbr_ref, wbi_ref, wcr_ref, wci_ref, d_ref, wglu_ref, bglu_ref,
                 y_ref, hfr_ref, hfi_ref, bur_ref, bui_ref, cr_ref, ci_ref):
    tt = u_ref.shape[0]

    @pl.when(pl.program_id(1) == 0)
    def _():
        cr_ref[...] = h0r_ref[...]
        ci_ref[...] = h0i_ref[...]

    u = u_ref[...]
    _s5_project_in(u, wbr_ref, wbi_ref, bur_ref, bui_ref)
    ar, ai = ar_ref[...], ai_ref[...]

    def step(t, carry):
        hr, hi = carry
        nhr = ar * hr - ai * hi + bur_ref[pl.ds(t, 1), :]
        nhi = ar * hi + ai * hr + bui_ref[pl.ds(t, 1), :]
        bur_ref[pl.ds(t, 1), :] = nhr
        bui_ref[pl.ds(t, 1), :] = nhi
        return nhr, nhi

    hr, hi = lax.fori_loop(0, tt, step, (cr_ref[...], ci_ref[...]))
    cr_ref[...] = hr
    ci_ref[...] = hi
    hfr_ref[...] = hr
    hfi_ref[...] = hi
    y_ref[...] = _s5_gate(_s5_project_out(bur_ref, bui_ref, wcr_ref, wci_ref), u, d_ref, wglu_ref, bglu_ref)


def _s5_step_body(u_ref, h0r_ref, h0i_ref, ar_ref, ai_ref, wbr_ref, wbi_ref, wcr_ref, wci_ref, d_ref, wglu_ref, bglu_ref,
                  y_ref, hfr_ref, hfi_ref, bur_ref, bui_ref):
    u = u_ref[...]
    _s5_project_in(u, wbr_ref, wbi_ref, bur_ref, bui_ref)
    ar, ai = ar_ref[...], ai_ref[...]
    hr, hi = h0r_ref[...], h0i_ref[...]
    nhr = ar * hr - ai * hi + bur_ref[...]
    nhi = ar * hi + ai * hr + bui_ref[...]
    bur_ref[...] = nhr
    bui_ref[...] = nhi
    hfr_ref[...] = nhr
    hfi_ref[...] = nhi
    y_ref[...] = _s5_gate(_s5_project_out(bur_ref, bui_ref, wcr_ref, wci_ref), u, d_ref, wglu_ref, bglu_ref)


def s5_mix_pallas(u, h0_re, h0_im, disc, d_skip, w_glu, b_glu, *, tt=512):
    bsz, L, w = u.shape
    f32 = jnp.float32
    ar, ai, wbr, wbi, wcr, wci = disc
    d2, bg2, wg = d_skip.reshape(1, w).astype(f32), b_glu.reshape(1, w).astype(f32), w_glu.astype(jnp.bfloat16)
    const2 = lambda *_: (0, 0)
    const3 = lambda *_: (0, 0, 0)
    w_specs = [pl.BlockSpec((1, S5_LANES), const2), pl.BlockSpec((1, S5_LANES), const2),
               pl.BlockSpec(wbr.shape, const3), pl.BlockSpec(wbi.shape, const3),
               pl.BlockSpec(wcr.shape, const3), pl.BlockSpec(wci.shape, const3),
               pl.BlockSpec((1, w), const2), pl.BlockSpec((w, w), const2), pl.BlockSpec((1, w), const2)]
    w_args = (ar, ai, wbr, wbi, wcr, wci, d2, wg, bg2)
    if L == 1:
        rows = bsz
        h0r, h0i = h0_re.reshape(rows, S5_LANES).astype(f32), h0_im.reshape(rows, S5_LANES).astype(f32)
        row_spec = lambda n: pl.BlockSpec((rows, n), const2)
        y, hr, hi = pl.pallas_call(
            _s5_step_body,
            grid=(1,),
            in_specs=[row_spec(w), row_spec(S5_LANES), row_spec(S5_LANES)] + w_specs,
            out_specs=[row_spec(w), row_spec(S5_LANES), row_spec(S5_LANES)],
            out_shape=[jax.ShapeDtypeStruct((rows, w), f32), jax.ShapeDtypeStruct((rows, S5_LANES), f32),
                       jax.ShapeDtypeStruct((rows, S5_LANES), f32)],
            scratch_shapes=[pltpu.VMEM((rows, S5_LANES), f32), pltpu.VMEM((rows, S5_LANES), f32)],
            compiler_params=pltpu.CompilerParams(vmem_limit_bytes=VMEM_LIMIT_BYTES),
            name="s5_step",
        )(u.reshape(rows, w), h0r, h0i, *w_args)
        y = y.reshape(bsz, 1, w)
    else:
        tt = min(tt, L)
        h0r, h0i = h0_re.reshape(bsz, 1, S5_LANES).astype(f32), h0_im.reshape(bsz, 1, S5_LANES).astype(f32)
        st_spec = pl.BlockSpec((None, 1, S5_LANES), lambda b, t: (b, 0, 0))
        y, hr, hi = pl.pallas_call(
            _s5_seq_body,
            grid=(bsz, L // tt),
            in_specs=[pl.BlockSpec((None, tt, w), lambda b, t: (b, t, 0)), st_spec, st_spec] + w_specs,
            out_specs=[pl.BlockSpec((None, tt, w), lambda b, t: (b, t, 0)), st_spec, st_spec],
            out_shape=[jax.ShapeDtypeStruct((bsz, L, w), f32), jax.ShapeDtypeStruct((bsz, 1, S5_LANES), f32),
                       jax.ShapeDtypeStruct((bsz, 1, S5_LANES), f32)],
            scratch_shapes=[pltpu.VMEM((tt, S5_LANES), f32), pltpu.VMEM((tt, S5_LANES), f32),
                            pltpu.VMEM((1, S5_LANES), f32), pltpu.VMEM((1, S5_LANES), f32)],
            compiler_params=pltpu.CompilerParams(
                dimension_semantics=("parallel", "arbitrary"), vmem_limit_bytes=VMEM_LIMIT_BYTES),
            name="s5_seq",
        )(u, h0r, h0i, *w_args)
    shp = (bsz, S5_GROUPS, S5_STATE)
    return y, hr.reshape(shp).astype(h0_re.dtype), hi.reshape(shp).astype(h0_im.dtype)


DSA_KEY_CHUNK = 512
DSA_NEAR = 2 * QBLOCK
INT32_MIN = -2 ** 31
NEG_BIG = -1e30
KT_ROWS = HEAD_DIM + 16


def _sortable_key(s):
    bits = lax.bitcast_convert_type(s, jnp.int32)
    return jnp.where(bits < 0, bits ^ jnp.int32(0x7FFFFFFF), bits)


def _dsa_prompt_body(n_top, rb_ref, q_ref, qi_ref, kiwi_ref, kt_ref, v_ref, kit_ref, o_ref, keys_ref, bias_ref, bound_ref, hi_ref, lo_ref):
    i = pl.program_id(1)
    f32, bf16 = jnp.float32, jnp.bfloat16
    kc = DSA_KEY_CHUNK
    q_start = i * QBLOCK
    n_all = (q_start + QBLOCK + kc - 1) // kc
    near_start = jnp.maximum(q_start - QBLOCK, 0)
    n_far = (near_start + kc - 1) // kc
    row = lax.broadcasted_iota(jnp.int32, (QBLOCK, 1), 0)
    qpos = q_start + row

    @pl.when(i == 0)
    def _():
        r = lax.broadcasted_iota(jnp.int32, (QBLOCK, DSA_NEAR), 0)
        c = lax.broadcasted_iota(jnp.int32, (QBLOCK, DSA_NEAR), 1)
        for tile in range(2):
            bucket = t5_bucket(r + tile * QBLOCK - c)
            for h in range(DSA_HEADS):
                b = jnp.zeros((QBLOCK, DSA_NEAR), f32)
                for bk in range(N_BUCKETS):
                    b = jnp.where(bucket == bk, rb_ref[bk, h] - rb_ref[N_BUCKETS - 1, h], b)
                kv, g = divmod(h, DSA_GQA)
                bias_ref[tile, kv, g * QBLOCK:(g + 1) * QBLOCK, :] = b
        lane = lax.broadcasted_iota(jnp.int32, (1, LANES), 1)
        bound = jnp.zeros((1, LANES), f32)
        for kv in range(DSA_KV_HEADS):
            kk = kt_ref[kv * KT_ROWS:kv * KT_ROWS + HEAD_DIM, :].astype(f32)
            kmax = jnp.sqrt(jnp.max(jnp.sum(kk * kk, axis=0, keepdims=True), axis=-1, keepdims=True))
            bound = jnp.where(lane == kv, kmax, bound)
        for h in range(DSA_HEADS):
            kv, g = divmod(h, DSA_GQA)
            bmax = jnp.maximum(jnp.max(bias_ref[:, kv, g * QBLOCK:(g + 1) * QBLOCK, :]), 0.0)
            bound = jnp.where(lane == DSA_KV_HEADS + h, bmax, bound)
        bound_ref[...] = bound

    qi_all = (qi_ref[...].astype(f32) * IDX_DIM ** -0.5).astype(bf16)
    qi = jnp.concatenate([qi_all[:, h * IDX_DIM:(h + 1) * IDX_DIM] for h in range(IDX_HEADS)], axis=0)
    wi = kiwi_ref[:, IDX_DIM:IDX_DIM + IDX_HEADS] * IDX_HEADS ** -0.5

    def score_chunk(c, _):
        off = pl.multiple_of(c * kc, kc)
        d = jnp.dot(qi, kit_ref[:, pl.ds(off, kc)], preferred_element_type=f32)
        s = jnp.zeros((QBLOCK, kc), f32)
        for h in range(IDX_HEADS):
            s = s + wi[:, h:h + 1] * jnp.maximum(d[h * QBLOCK:(h + 1) * QBLOCK], 0.0)
        kpos = off + lax.broadcasted_iota(jnp.int32, (QBLOCK, kc), 1)
        s = jnp.where(s == 0.0, 0.0, s)
        s = jnp.where(kpos <= qpos, s, -jnp.inf)
        key = _sortable_key(s)
        keys_ref[:, pl.ds(off, kc)] = key
        hi_ref[:, pl.ds(off, kc)] = lax.shift_right_arithmetic(key, 16).astype(jnp.int16)
        return 0

    lax.fori_loop(0, n_all, score_chunk, 0)

    def count_where(pred_fn):
        def body(c, acc):
            off = pl.multiple_of(c * kc, kc)
            hit = pred_fn(keys_ref[:, pl.ds(off, kc)], off)
            part = jnp.where(hit, 1.0, 0.0)
            for j in range(kc // 128):
                acc = acc + part[:, j * 128:(j + 1) * 128]
            return acc
        acc = lax.fori_loop(0, n_all, body, jnp.zeros((QBLOCK, 128), f32))
        return jnp.sum(acc, axis=-1, keepdims=True)

    def count16(ref, pred_fn):
        def body(c, acc):
            off = pl.multiple_of(c * kc, kc)
            part = jnp.where(pred_fn(ref[:, pl.ds(off, kc)]), jnp.int16(1), jnp.int16(0))
            return acc + ((part[:, 0:128] + part[:, 128:256]) + (part[:, 256:384] + part[:, 384:512]))
        acc = lax.fori_loop(0, n_all, body, jnp.zeros((QBLOCK, 128), jnp.int16))
        return jnp.sum(acc.astype(f32), axis=-1, keepdims=True)

    def search16(ref, want):
        def bit(it, t):
            cand = t + lax.shift_left(jnp.int32(1), 15 - it)
            c16 = cand.astype(jnp.int16)
            return jnp.where(count16(ref, lambda x: x >= c16) >= want, cand, t)
        return lax.fori_loop(0, 16, bit, jnp.full((QBLOCK, 1), -2 ** 15, jnp.int32))

    t_hi = search16(hi_ref, n_top)
    t_hi16 = t_hi.astype(jnp.int16)
    n_above = count16(hi_ref, lambda x: x > t_hi16)

    def band_chunk(c, _):
        off = pl.multiple_of(c * kc, kc)
        key = keys_ref[:, pl.ds(off, kc)]
        lo = ((key & 0xFFFF) - 2 ** 15).astype(jnp.int16)
        lo_ref[:, pl.ds(off, kc)] = jnp.where(lax.shift_right_arithmetic(key, 16) == t_hi, lo, jnp.int16(-2 ** 15))
        return 0

    lax.fori_loop(0, n_all, band_chunk, 0)
    t_lo = search16(lo_ref, n_top - n_above)
    thr = lax.shift_left(t_hi, 16) + (t_lo + 2 ** 15)

    def is_valid(off, width):
        return (off + lax.broadcasted_iota(jnp.int32, (QBLOCK, width), 1)) <= qpos

    n_gt = count_where(lambda k, off: (k > thr) & is_valid(off, kc))
    n_eq = count_where(lambda k, off: (k == thr) & is_valid(off, kc))
    need = n_top - n_gt
    has_extra_ties = jnp.max(jnp.where(n_eq > need, 1.0, 0.0)) > 0.0

    def tie_search():
        def idx_bit(it, j):
            cand = j + lax.shift_left(jnp.int32(1), 13 - it)
            cnt = count_where(lambda k, off: (k == thr) & is_valid(off, kc)
                              & ((off + lax.broadcasted_iota(jnp.int32, (QBLOCK, kc), 1)) < cand))
            return jnp.where(cnt <= need, cand, j)
        return lax.fori_loop(0, 14, idx_bit, jnp.zeros((QBLOCK, 1), jnp.int32))

    tie_end = lax.cond(has_extra_ties, tie_search, lambda: jnp.full((QBLOCK, 1), 2 ** 14, jnp.int32))

    def selected(keys, off, width):
        kpos = off + lax.broadcasted_iota(jnp.int32, (QBLOCK, width), 1)
        return (kpos <= qpos) & ((keys > thr) | ((keys == thr) & (kpos < tie_end))), kpos

    tile = jnp.minimum(i, 1)
    q_all = (q_ref[...].astype(f32) * HEAD_DIM ** -0.5).astype(bf16)
    qh = [q_all[:, h * HEAD_DIM:(h + 1) * HEAD_DIM] for h in range(DSA_HEADS)]
    off_near = pl.multiple_of(near_start, QBLOCK)
    sel_near, _ = selected(keys_ref[:, pl.ds(off_near, DSA_NEAR)], off_near, DSA_NEAR)

    def near_bias(h):
        kv, g = divmod(h, DSA_GQA)
        return jnp.where(sel_near, bias_ref[tile, kv, g * QBLOCK:(g + 1) * QBLOCK, :], NEG_BIG)

    def far_mask(c):
        off = pl.multiple_of(c * kc, kc)
        sel, kpos = selected(keys_ref[:, pl.ds(off, kc)], off, kc)
        return off, jnp.where(sel & (kpos < near_start), 0.0, NEG_BIG)

    def kt_block(kv, off, width, rows):
        return kt_ref[kv * KT_ROWS:kv * KT_ROWS + rows, pl.ds(off, width)]

    bound = bound_ref[...]
    lane = lax.broadcasted_iota(jnp.int32, (1, LANES), 1)
    pick = lambda idx: jnp.sum(jnp.where(lane == idx, bound, 0.0), axis=-1, keepdims=True)
    col = lax.broadcasted_iota(jnp.int32, (QBLOCK, KT_ROWS - HEAD_DIM), 1)
    q_aug = []
    for h in range(DSA_HEADS):
        qf = qh[h].astype(f32)
        ub = jnp.sqrt(jnp.sum(qf * qf, axis=-1, keepdims=True)) * pick(h // DSA_GQA) * 1.01 + pick(DSA_KV_HEADS + h)
        q_aug.append(jnp.concatenate([qh[h], jnp.where(col == 0, -ub, 0.0).astype(bf16)], axis=-1))

    def fast_attend(accs, off, width, mask_bias):
        accs = list(accs)
        for kv in range(DSA_KV_HEADS):
            kt = kt_block(kv, off, width, KT_ROWS)
            ps = [jnp.exp(jnp.dot(q_aug[kv * DSA_GQA + g], kt, preferred_element_type=f32)
                          + mask_bias(kv * DSA_GQA + g)).astype(bf16) for g in range(DSA_GQA)]
            accs[kv] = accs[kv] + jnp.dot(jnp.concatenate(ps, axis=0), v_ref[kv, pl.ds(off, width), :],
                                          preferred_element_type=f32)
        return tuple(accs)

    def fast_chunk(c, accs):
        off, mb = far_mask(c)
        return fast_attend(accs, off, kc, lambda h: mb)

    zero_acc = tuple(jnp.zeros((DSA_GQA * QBLOCK, 2 * HEAD_DIM), f32) for _ in range(DSA_KV_HEADS))
    accs = fast_attend(lax.fori_loop(0, n_far, fast_chunk, zero_acc), off_near, DSA_NEAR, near_bias)
    l_min = accs[0][:, HEAD_DIM:]
    for kv in range(1, DSA_KV_HEADS):
        l_min = jnp.minimum(l_min, accs[kv][:, HEAD_DIM:])
    fast_ok = jnp.min(l_min) > 1e-30

    def safe_attend(carry, off, width, mask_bias):
        ms, accs = list(carry[0]), list(carry[1])
        for kv in range(DSA_KV_HEADS):
            kt = kt_block(kv, off, width, HEAD_DIM)
            ps, alphas = [], []
            for g in range(DSA_GQA):
                h = kv * DSA_GQA + g
                logits = jnp.dot(qh[h], kt, preferred_element_type=f32) + mask_bias(h)
                m_new = jnp.maximum(ms[h], jnp.max(logits, axis=-1, keepdims=True))
                alphas.append(jnp.exp(ms[h] - m_new))
                ps.append(jnp.exp(logits - m_new).astype(bf16))
                ms[h] = m_new
            accs[kv] = (jnp.concatenate(alphas, axis=0) * accs[kv]
                        + jnp.dot(jnp.concatenate(ps, axis=0), v_ref[kv, pl.ds(off, width), :], preferred_element_type=f32))
        return tuple(ms), tuple(accs)

    def safe_path():
        def safe_chunk(c, carry):
            off, mb = far_mask(c)
            return safe_attend(carry, off, kc, lambda h: mb)
        init = (tuple(jnp.full((QBLOCK, 1), NEG_BIG, f32) for _ in range(DSA_HEADS)), zero_acc)
        return safe_attend(lax.fori_loop(0, n_far, safe_chunk, init), off_near, DSA_NEAR, near_bias)[1]

    accs = lax.cond(fast_ok, lambda: accs, safe_path)
    outs = []
    for kv in range(DSA_KV_HEADS):
        o = accs[kv][:, :HEAD_DIM] / accs[kv][:, HEAD_DIM:]
        outs += [o[g * QBLOCK:(g + 1) * QBLOCK] for g in range(DSA_GQA)]
    o_ref[...] = jnp.concatenate(outs, axis=-1)


def dsa_prompt_pallas(q, qi, kiwi, k, v, rel_bias):
    bsz, L = q.shape[:2]
    nq = L // QBLOCK
    n_top = min(TOPK_MAX, L // 4)
    f32, bf16 = jnp.float32, jnp.bfloat16
    k_t = k.astype(bf16).reshape(bsz, L, DSA_KV_HEADS, HEAD_DIM).transpose(0, 2, 3, 1)
    k_pad = jnp.zeros((bsz, DSA_KV_HEADS, KT_ROWS - HEAD_DIM, L), bf16).at[:, :, 0, :].set(1.0)
    k_t = jnp.concatenate([k_t, k_pad], axis=2).reshape(bsz, DSA_KV_HEADS * KT_ROWS, L)
    v_h = v.astype(bf16).reshape(bsz, L, DSA_KV_HEADS, HEAD_DIM).swapaxes(1, 2)
    v_h = jnp.concatenate([v_h, jnp.ones_like(v_h)], axis=-1)
    ki_t = kiwi[..., :IDX_DIM].astype(bf16).swapaxes(1, 2)
    lk = max(L, DSA_KEY_CHUNK)
    if lk != L:
        k_t = jnp.pad(k_t, ((0, 0), (0, 0), (0, lk - L)))
        v_h = jnp.pad(v_h, ((0, 0), (0, 0), (0, lk - L), (0, 0)))
        ki_t = jnp.pad(ki_t, ((0, 0), (0, 0), (0, lk - L)))
    tok = lambda n: pl.BlockSpec((None, QBLOCK, n), lambda b, i: (b, i, 0))
    return pl.pallas_call(
        functools.partial(_dsa_prompt_body, n_top),
        grid=(bsz, nq),
        in_specs=[
            pl.BlockSpec(memory_space=pltpu.SMEM),
            tok(SEQ_WIDTH), tok(IDX_HEADS * IDX_DIM), tok(LANES),
            pl.BlockSpec((None, DSA_KV_HEADS * KT_ROWS, lk), lambda b, i: (b, 0, 0), pipeline_mode=pl.Buffered(1)),
            pl.BlockSpec((None, DSA_KV_HEADS, lk, 2 * HEAD_DIM), lambda b, i: (b, 0, 0, 0), pipeline_mode=pl.Buffered(1)),
            pl.BlockSpec((None, IDX_DIM, lk), lambda b, i: (b, 0, 0), pipeline_mode=pl.Buffered(1)),
        ],
        out_specs=tok(SEQ_WIDTH),
        out_shape=jax.ShapeDtypeStruct((bsz, L, SEQ_WIDTH), f32),
        scratch_shapes=[pltpu.VMEM((QBLOCK, lk), jnp.int32),
                        pltpu.VMEM((2, DSA_KV_HEADS, DSA_GQA * QBLOCK, DSA_NEAR), f32),
                        pltpu.VMEM((1, LANES), f32),
                        pltpu.VMEM((QBLOCK, lk), jnp.int16), pltpu.VMEM((QBLOCK, lk), jnp.int16)],
        compiler_params=pltpu.CompilerParams(
            dimension_semantics=("parallel", "arbitrary"), vmem_limit_bytes=VMEM_LIMIT_BYTES),
        name="dsa_prompt",
    )(rel_bias.astype(f32), q, qi, kiwi, k_t, v_h, ki_t)


DEC_PAGES_PER_STEP = 16
KV_WIDTH = DSA_KV_HEADS * HEAD_DIM


def _index_score(qi, wi, ki):
    d = _dot_nt(qi, ki)
    s = jnp.sum(wi * jnp.maximum(d, 0.0), axis=0, keepdims=True)
    return jnp.where(s == 0.0, 0.0, s)


def _dsa_dec_score_body(pt_ref, qi_ref, wi_ref, *refs):
    pages, s_ref = refs[:-1], refs[-1]
    qi = (qi_ref[...].astype(jnp.float32) * IDX_DIM ** -0.5).astype(jnp.bfloat16)
    wi = wi_ref[...]
    s_ref[...] = jnp.concatenate([_index_score(qi, wi, p[...].astype(jnp.bfloat16)) for p in pages], axis=0)


def _dsa_dec_select_body(n_top, qi_ref, wi_ref, kinew_ref, s_ref, sel_ref, snew_ref):
    f32, bf16 = jnp.float32, jnp.bfloat16
    bsz, past = s_ref.shape
    qi = (qi_ref[...].astype(f32) * IDX_DIM ** -0.5).astype(bf16).astype(f32)
    d_new = jnp.sum(qi * kinew_ref[...].astype(bf16).astype(f32), axis=-1, keepdims=True)
    s_new = jnp.sum(wi_ref[...] * jnp.maximum(d_new, 0.0), axis=1)
    s_new = jnp.where(s_new == 0.0, 0.0, s_new)
    key_new = _sortable_key(s_new)
    keys = _sortable_key(s_ref[...])
    kpos = lax.broadcasted_iota(jnp.int32, (bsz, past), 1)

    def count(pred_past, pred_new):
        part = jnp.where(pred_past, 1.0, 0.0)
        acc = part[:, 0:LANES]
        for j in range(1, past // LANES):
            acc = acc + part[:, j * LANES:(j + 1) * LANES]
        return jnp.sum(acc, axis=-1, keepdims=True) + jnp.where(pred_new, 1.0, 0.0)

    def thr_bit(it, thr):
        cand = thr + lax.shift_left(jnp.int32(1), 31 - it)
        return jnp.where(count(keys >= cand, key_new >= cand) >= n_top, cand, thr)

    thr = lax.fori_loop(0, 32, thr_bit, jnp.full((bsz, 1), INT32_MIN, jnp.int32))
    need = n_top - count(keys > thr, key_new > thr)

    def idx_bit(it, j):
        cand = j + lax.shift_left(jnp.int32(1), 14 - it)
        cnt = count((keys == thr) & (kpos < cand), (key_new == thr) & (past < cand))
        return jnp.where(cnt <= need, cand, j)

    n_eq = count(keys == thr, key_new == thr)
    tie_end = lax.cond(jnp.max(jnp.where(n_eq > need, 1.0, 0.0)) > 0.0,
                       lambda: lax.fori_loop(0, 15, idx_bit, jnp.zeros((bsz, 1), jnp.int32)),
                       lambda: jnp.full((bsz, 1), 2 ** 15, jnp.int32))
    lane = lax.broadcasted_iota(jnp.int32, (bsz, LANES), 1)
    sel_ref[...] = jnp.where(lane == 0, thr, jnp.where(lane == 1, tie_end, 0))
    snew_ref[...] = jnp.broadcast_to(s_new, (bsz, LANES))


def _dsa_dec_attend_body(n_pages, pt_ref, rb_ref, q_ref, knew_ref, vnew_ref, sel_ref, snew_ref, s_ref, *refs):
    pp = DEC_PAGES_PER_STEP if n_pages >= DEC_PAGES_PER_STEP else n_pages
    k_pages, v_pages = refs[:pp], refs[pp:2 * pp]
    o_ref, m_ref, l_ref, acc_ref = refs[2 * pp:]
    f32, bf16 = jnp.float32, jnp.bfloat16
    t = pl.program_id(1)
    nt = pl.num_programs(1)
    past = n_pages * PAGE_SIZE

    @pl.when(t == 0)
    def _():
        m_ref[...] = jnp.full(m_ref.shape, NEG_BIG, f32)
        l_ref[...] = jnp.zeros(l_ref.shape, f32)
        acc_ref[...] = jnp.zeros(acc_ref.shape, f32)

    thr, tie_end = sel_ref[:, 0:1], sel_ref[:, 1:2]
    q = (q_ref[...].astype(f32) * HEAD_DIM ** -0.5).astype(bf16)
    lane12 = lax.broadcasted_iota(jnp.int32, (DSA_HEADS, LANES), 1)

    def head_bias(dist):
        bucket = t5_bucket(dist)
        b = jnp.zeros(dist.shape, f32)
        for h in range(DSA_HEADS):
            row = jnp.zeros(dist.shape, f32)
            for bk in range(N_BUCKETS):
                row = jnp.where(bucket == bk, rb_ref[bk, h] - rb_ref[N_BUCKETS - 1, h], row)
            b = jnp.where(lax.broadcasted_iota(jnp.int32, dist.shape, 0) == h, row, b)
        return b

    keys_step = _sortable_key(s_ref[pl.ds(pl.multiple_of(t * pp, pp), pp), :])
    tiles = []
    for j in range(pp):
        kpos = (t * pp + j) * PAGE_SIZE + lax.broadcasted_iota(jnp.int32, (1, LANES), 1)
        keys = keys_step[j:j + 1]
        sel = (keys > thr) | ((keys == thr) & (kpos < tie_end))
        logits = _dot_nt(q, k_pages[j][...].astype(bf16))
        if j == pp - 1:
            near = head_bias(past - (t * pp + j) * PAGE_SIZE - lane12)
            logits = logits + jnp.where(t == nt - 1, near, 0.0)
        tiles.append(jnp.where(sel, logits, NEG_BIG))
    m = m_ref[...]
    tile_max = tiles[0]
    for x in tiles[1:]:
        tile_max = jnp.maximum(tile_max, x)
    m_new = jnp.maximum(m, jnp.max(tile_max, axis=-1, keepdims=True))
    alpha = jnp.exp(m - m_new)
    ps = [jnp.exp(x - m_new) for x in tiles]
    p_sum = ps[0]
    for x in ps[1:]:
        p_sum = p_sum + x
    pv = jnp.dot(ps[0].astype(bf16), v_pages[0][...].astype(bf16), preferred_element_type=f32)
    for j in range(1, pp):
        pv = pv + jnp.dot(ps[j].astype(bf16), v_pages[j][...].astype(bf16), preferred_element_type=f32)
    l_ref[...] = alpha * l_ref[...] + jnp.sum(p_sum, axis=-1, keepdims=True)
    acc_ref[...] = alpha * acc_ref[...] + pv
    m_ref[...] = m_new

    @pl.when(t == nt - 1)
    def _():
        key_new = _sortable_key(snew_ref[:, 0:1])
        sel_new = (key_new > thr) | ((key_new == thr) & (past < tie_end))
        k_new = knew_ref[...].astype(bf16).astype(f32)
        logit_new = jnp.sum(q.astype(f32) * k_new, axis=-1, keepdims=True)
        logit_new = logit_new + head_bias(jnp.zeros((DSA_HEADS, LANES), jnp.int32))[:, :1]
        logit_new = jnp.where(sel_new, logit_new, NEG_BIG)
        m = m_ref[...]
        m_new = jnp.maximum(m, logit_new)
        alpha = jnp.exp(m - m_new)
        p_new = jnp.exp(logit_new - m_new)
        l = alpha * l_ref[...] + p_new
        v_new = vnew_ref[...].astype(bf16).astype(f32)
        acc = alpha * acc_ref[...] + p_new.astype(bf16).astype(f32) * v_new
        o = acc / l
        o_ref[...] = jnp.concatenate(
            [o[h:h + 1, (h // DSA_GQA) * HEAD_DIM:(h // DSA_GQA + 1) * HEAD_DIM] for h in range(DSA_HEADS)], axis=-1)


def dsa_decode_pallas(q, qi, kiwi, k_new, v_new, pool_k, pool_v, pool_ki, page_table, rel_bias):
    bsz = q.shape[0]
    n_pages = page_table.shape[1]
    n_phys = pool_k.shape[0]
    pp = DEC_PAGES_PER_STEP if n_pages >= DEC_PAGES_PER_STEP else n_pages
    nt = n_pages // pp
    n_top = min(TOPK_MAX, (n_pages * PAGE_SIZE + 1) // 4)
    f32, bf16 = jnp.float32, jnp.bfloat16
    eye = jnp.eye(DSA_KV_HEADS, dtype=q.dtype)
    q_bd = jnp.einsum('bkgd,kj->bkgjd', q.reshape(bsz, DSA_KV_HEADS, DSA_GQA, HEAD_DIM), eye)
    q_bd = q_bd.reshape(bsz, DSA_HEADS, KV_WIDTH)
    qi3 = qi.reshape(bsz, IDX_HEADS, IDX_DIM)
    wi3 = (kiwi[:, IDX_DIM:IDX_DIM + IDX_HEADS] * IDX_HEADS ** -0.5).reshape(bsz, IDX_HEADS, 1)
    ki_new = kiwi[:, :IDX_DIM].reshape(bsz, 1, IDX_DIM)
    pk, pv = pool_k, pool_v
    per_b = lambda *shape: pl.BlockSpec((None,) + shape, lambda b, t, pt: (b,) + (0,) * len(shape))
    page = lambda width, j: pl.BlockSpec((None, PAGE_SIZE, width), lambda b, t, pt: (pt[b, t * pp + j], 0, 0))
    scores = pl.pallas_call(
        _dsa_dec_score_body,
        grid_spec=pltpu.PrefetchScalarGridSpec(
            num_scalar_prefetch=1, grid=(bsz, nt),
            in_specs=[per_b(IDX_HEADS, IDX_DIM), per_b(IDX_HEADS, 1)] + [page(IDX_DIM, j) for j in range(pp)],
            out_specs=pl.BlockSpec((None, pp, LANES), lambda b, t, pt: (b, t, 0))),
        out_shape=jax.ShapeDtypeStruct((bsz, n_pages, LANES), f32),
        compiler_params=pltpu.CompilerParams(
            dimension_semantics=("parallel", "parallel"), vmem_limit_bytes=VMEM_LIMIT_BYTES),
        name="dsa_dec_score",
    )(page_table, qi3, wi3, *([pool_ki] * pp))
    sel, s_new = pl.pallas_call(
        functools.partial(_dsa_dec_select_body, n_top),
        grid=(1,),
        in_specs=[pl.BlockSpec((bsz, IDX_HEADS, IDX_DIM), lambda i: (0, 0, 0)),
                  pl.BlockSpec((bsz, IDX_HEADS, 1), lambda i: (0, 0, 0)),
                  pl.BlockSpec((bsz, 1, IDX_DIM), lambda i: (0, 0, 0)),
                  pl.BlockSpec((bsz, n_pages * PAGE_SIZE), lambda i: (0, 0))],
        out_specs=[pl.BlockSpec((bsz, LANES), lambda i: (0, 0))] * 2,
        out_shape=[jax.ShapeDtypeStruct((bsz, LANES), jnp.int32), jax.ShapeDtypeStruct((bsz, LANES), f32)],
        compiler_params=pltpu.CompilerParams(vmem_limit_bytes=VMEM_LIMIT_BYTES),
        name="dsa_dec_select",
    )(qi3, wi3, ki_new, scores.reshape(bsz, n_pages * PAGE_SIZE))
    o = pl.pallas_call(
        functools.partial(_dsa_dec_attend_body, n_pages),
        grid_spec=pltpu.PrefetchScalarGridSpec(
            num_scalar_prefetch=1, grid=(bsz, nt),
            in_specs=[pl.BlockSpec(memory_space=pltpu.SMEM), per_b(DSA_HEADS, KV_WIDTH), per_b(1, KV_WIDTH),
                      per_b(1, KV_WIDTH), per_b(1, LANES), per_b(1, LANES), per_b(n_pages, LANES)]
            + [page(KV_WIDTH, j) for j in range(pp)] * 2,
            out_specs=per_b(1, SEQ_WIDTH),
            scratch_shapes=[pltpu.VMEM((DSA_HEADS, 1), f32), pltpu.VMEM((DSA_HEADS, 1), f32),
                            pltpu.VMEM((DSA_HEADS, KV_WIDTH), f32)]),
        out_shape=jax.ShapeDtypeStruct((bsz, 1, SEQ_WIDTH), f32),
        compiler_params=pltpu.CompilerParams(
            dimension_semantics=("parallel", "arbitrary"), vmem_limit_bytes=VMEM_LIMIT_BYTES),
        name="dsa_dec_attend",
    )(page_table, rel_bias.astype(f32), q_bd, k_new.reshape(bsz, 1, KV_WIDTH), v_new.reshape(bsz, 1, KV_WIDTH),
      sel.reshape(bsz, 1, LANES), s_new.reshape(bsz, 1, LANES), scores, *([pk] * pp), *([pv] * pp))
    return o


CONV_HALO = 8


def _gdn_prep_body(x_ref, ab_ref, ctx_ref, w_ref, alog_ref, dtb_ref, q_ref, k_ref, v_ref, gb_ref, cs_ref, xp_ref):
    tt = x_ref.shape[0]
    halo = CONV_W - 1

    @pl.when(pl.program_id(1) == 0)
    def _():
        xp_ref[CONV_HALO - halo:CONV_HALO, :] = ctx_ref[...]

    x = x_ref[...]
    xp_ref[CONV_HALO:CONV_HALO + tt, :] = x
    w = w_ref[...]
    y = xp_ref[CONV_HALO - halo:CONV_HALO - halo + tt, :] * w[0:1]
    for j in range(1, CONV_W):
        y = y + xp_ref[CONV_HALO - halo + j:CONV_HALO - halo + j + tt, :] * w[j:j + 1]
    last = x[tt - halo:, :]
    xp_ref[CONV_HALO - halo:CONV_HALO, :] = last
    cs_ref[...] = last
    y = y * _sigmoid(y)

    def l2n(a, scale):
        cols = []
        for h in range(GDN_HEADS):
            s = a[:, h * HEAD_DIM:(h + 1) * HEAD_DIM]
            cols.append(s * (lax.rsqrt(jnp.sum(s * s, axis=-1, keepdims=True) + EPS) * scale))
        return jnp.concatenate(cols, axis=-1)

    q_ref[...] = l2n(y[:, :SEQ_WIDTH], HEAD_DIM ** -0.5)
    k_ref[...] = l2n(y[:, SEQ_WIDTH:2 * SEQ_WIDTH], 1.0)
    v_ref[...] = y[:, 2 * SEQ_WIDTH:]
    ab = ab_ref[...]
    xa = ab + dtb_ref[...]
    softplus = jnp.maximum(xa, 0.0) + jnp.log(1.0 + jnp.exp(-jnp.abs(xa)))
    g = -jnp.exp(alog_ref[...]) * softplus
    lane = lax.broadcasted_iota(jnp.int32, ab.shape, 1)
    gb_ref[...] = jnp.where(lane < GDN_HEADS, g, _sigmoid(ab))


def gdn_prep_pallas(qkv, ab, ctx, conv_w, a_log, dt_bias, *, tt=256):
    bsz, L, ch = qkv.shape
    tt = min(tt, L)
    f32 = jnp.float32
    pad = lambda r: jnp.pad(r.astype(f32).reshape(1, -1), ((0, 0), (0, LANES - r.shape[-1])))
    tok = lambda n: pl.BlockSpec((None, tt, n), lambda b, t: (b, t, 0))
    const2 = lambda b, t: (0, 0)
    return pl.pallas_call(
        _gdn_prep_body,
        grid=(bsz, L // tt),
        in_specs=[tok(ch), tok(LANES), pl.BlockSpec((None, CONV_W - 1, ch), lambda b, t: (b, 0, 0)),
                  pl.BlockSpec((CONV_W, ch), const2), pl.BlockSpec((1, LANES), const2), pl.BlockSpec((1, LANES), const2)],
        out_specs=[tok(SEQ_WIDTH), tok(SEQ_WIDTH), tok(SEQ_WIDTH), tok(LANES),
                   pl.BlockSpec((None, CONV_W - 1, ch), lambda b, t: (b, 0, 0))],
        out_shape=[jax.ShapeDtypeStruct((bsz, L, SEQ_WIDTH), f32)] * 3
        + [jax.ShapeDtypeStruct((bsz, L, LANES), f32), jax.ShapeDtypeStruct((bsz, CONV_W - 1, ch), f32)],
        scratch_shapes=[pltpu.VMEM((CONV_HALO + tt, ch), f32)],
        compiler_params=pltpu.CompilerParams(
            dimension_semantics=("parallel", "arbitrary"), vmem_limit_bytes=VMEM_LIMIT_BYTES),
        name="gdn_prep",
    )(qkv, ab, ctx.astype(f32), conv_w.astype(f32), pad(a_log), pad(dt_bias))


def _split3(a):
    bf16, f32 = jnp.bfloat16, jnp.float32
    h = a.astype(bf16)
    r = a - h.astype(f32)
    m = r.astype(bf16)
    return h, m, (r - m.astype(f32)).astype(bf16)


def _mm_hi(a, b):
    f32 = jnp.float32
    ah, am, _ = _split3(a)
    bh, bm, _ = _split3(b)
    d = lambda x, y: jnp.dot(x, y, preferred_element_type=f32)
    return d(ah, bh) + (d(ah, bm) + d(am, bh))


def _mm_sel(sel, b):
    f32 = jnp.float32
    s = sel.astype(jnp.bfloat16)
    bh, bm, bl = _split3(b)
    d = lambda y: jnp.dot(s, y, preferred_element_type=f32)
    return d(bh) + (d(bm) + d(bl))


def _dot_nt(a, b):
    return lax.dot_general(a, b, (((1,), (1,)), ((), ())), preferred_element_type=jnp.float32)


def _dot_tn(a, b):
    return lax.dot_general(a, b, (((0,), (0,)), ((), ())), preferred_element_type=jnp.float32)


def _gdn_local_body(q_ref, k_ref, v_ref, gb_ref, uv_ref, wk_ref, qh_ref, kt_ref, qk_ref):
    f32, bf16 = jnp.float32, jnp.bfloat16
    c = GDN_CHUNK
    r_i = lax.broadcasted_iota(jnp.int32, (c, c), 0)
    c_i = lax.broadcasted_iota(jnp.int32, (c, c), 1)
    tril = r_i >= c_i
    stril = r_i > c_i
    triu = r_i <= c_i
    eye = jnp.where(r_i == c_i, 1.0, 0.0)
    ones = jnp.ones((c, c), jnp.bool_)
    lane = lax.broadcasted_iota(jnp.int32, (c, LANES), 1)
    gb = gb_ref[...]
    heads = range(GDN_HEADS)
    sl = [slice(h * HEAD_DIM, (h + 1) * HEAD_DIM) for h in heads]
    pick = lambda idx: jnp.sum(jnp.where(lane == idx, gb, 0.0), axis=-1, keepdims=True)
    g_col = [pick(h) for h in heads]
    beta = [pick(GDN_HEADS + h) for h in heads]
    q = [q_ref[:, sl[h]] for h in heads]
    k = [k_ref[:, sl[h]] for h in heads]
    v = [v_ref[:, sl[h]] for h in heads]
    g_mat = [jnp.broadcast_to(g_col[h], (c, c)) for h in heads]
    cum_col = [_mm_sel(tril, g_mat[h]) for h in heads]
    cum_row = [_mm_sel(ones, jnp.where(triu, g_mat[h], 0.0)) for h in heads]
    gam = [jnp.where(tril, jnp.exp(jnp.where(tril, cum_col[h] - cum_row[h], 0.0)), 0.0) for h in heads]
    kb = [k[h].astype(bf16) for h in heads]
    a_mat = [jnp.where(stril, beta[h] * _dot_nt(kb[h], kb[h]) * gam[h], 0.0) for h in heads]
    t_inv = [eye - a_mat[h] for h in heads]
    pw = a_mat
    for _ in range(5):
        pw = [_mm_hi(pw[h], pw[h]) for h in heads]
        t_inv = [t_inv[h] + _mm_hi(t_inv[h], pw[h]) for h in heads]
    gc = [cum_col[h][:, :1] for h in heads]
    egc = [jnp.exp(gc[h]) for h in heads]
    tw = [_mm_hi(t_inv[h], jnp.concatenate([beta[h] * v[h], (beta[h] * egc[h]) * k[h]], axis=-1)) for h in heads]
    qk = [jnp.where(tril, _dot_nt(q[h].astype(bf16), kb[h]) * gam[h], 0.0) for h in heads]
    cat = lambda xs: jnp.concatenate(xs, axis=-1)
    uv_ref[...] = cat([tw[h][:, :HEAD_DIM] for h in heads])
    wk_ref[...] = cat([tw[h][:, HEAD_DIM:] for h in heads]).astype(bf16)
    qh_ref[...] = cat([egc[h] * q[h] for h in heads]).astype(bf16)
    eye_b = eye.astype(bf16)
    kt = [(jnp.exp(gc[h][c - 1:c, :] - gc[h]) * k[h]).astype(bf16) for h in heads]
    kt_ref[...] = cat([_dot_tn(kt[h], eye_b) for h in heads]).astype(bf16)
    qk_ref[...] = cat(qk).astype(bf16)


def gdn_local_pallas(qn, kn, vv, gb):
    bsz, L, w = qn.shape
    c = GDN_CHUNK
    f32, bf16 = jnp.float32, jnp.bfloat16
    tok = lambda n: pl.BlockSpec((None, c, n), lambda b, t: (b, t, 0))
    return pl.pallas_call(
        _gdn_local_body,
        grid=(bsz, L // c),
        in_specs=[tok(w), tok(w), tok(w), tok(LANES)],
        out_specs=[tok(w)] * 5,
        out_shape=[jax.ShapeDtypeStruct((bsz, L, w), f32)] + [jax.ShapeDtypeStruct((bsz, L, w), bf16)] * 4,
        compiler_params=pltpu.CompilerParams(
            dimension_semantics=("parallel", "parallel"), vmem_limit_bytes=VMEM_LIMIT_BYTES),
        name="gdn_local",
    )(qn, kn, vv, gb)


def _gdn_scan_body(uv_ref, wk_ref, qh_ref, ktt_ref, qk_ref, gb_ref, gate_ref, s0_ref, on_ref, o_ref, sf_ref, s_ref):
    f32, bf16 = jnp.float32, jnp.bfloat16
    c = GDN_CHUNK
    tt = uv_ref.shape[0]
    heads = range(GDN_HEADS)
    sl = [slice(h * HEAD_DIM, (h + 1) * HEAD_DIM) for h in heads]

    @pl.when(pl.program_id(1) == 0)
    def _():
        s_ref[...] = s0_ref[...]

    lane = lax.broadcasted_iota(jnp.int32, (1, LANES), 1)
    dot = lambda a, b: jnp.dot(a, b, preferred_element_type=f32)

    def chunk(ci, _):
        rows = pl.ds(pl.multiple_of(ci * c, c), c)
        eg_last = jnp.exp(jnp.sum(gb_ref[rows, :], axis=0, keepdims=True))
        eg = [jnp.sum(jnp.where(lane == h, eg_last, 0.0), axis=-1, keepdims=True) for h in heads]
        st = [s_ref[h] for h in heads]
        sb = [st[h].astype(bf16) for h in heads]
        u = [uv_ref[rows, sl[h]] - dot(wk_ref[rows, sl[h]], sb[h]) for h in heads]
        ub = [u[h].astype(bf16) for h in heads]
        new = [eg[h] * st[h] + dot(ktt_ref[rows, sl[h]], ub[h]) for h in heads]
        for h in heads:
            s_ref[h] = new[h]
        o = [dot(qh_ref[rows, sl[h]], sb[h]) + dot(qk_ref[rows, sl[h]], ub[h]) for h in heads]
        o = [o[h] * lax.rsqrt(jnp.mean(o[h] * o[h], axis=-1, keepdims=True) + EPS) for h in heads]
        gt = gate_ref[rows, :]
        o_ref[rows, :] = (jnp.concatenate(o, axis=-1) * on_ref[...]) * (gt * _sigmoid(gt))
        return 0

    lax.fori_loop(0, tt // c, chunk, 0)
    sf_ref[...] = s_ref[...]


def gdn_scan_pallas(uv, wk, qh, kt, qk, gb, gate, s0, o_norm, *, tt=256):
    bsz, L, w = uv.shape
    tt = min(tt, L)
    f32 = jnp.float32
    tok = lambda n: pl.BlockSpec((None, tt, n), lambda b, t: (b, t, 0))
    st = pl.BlockSpec((None, GDN_HEADS, HEAD_DIM, HEAD_DIM), lambda b, t: (b, 0, 0, 0))
    on = jnp.tile(o_norm.astype(f32), GDN_HEADS).reshape(1, w)
    return pl.pallas_call(
        _gdn_scan_body,
        grid=(bsz, L // tt),
        in_specs=[tok(w)] * 5 + [tok(LANES), tok(w), st, pl.BlockSpec((1, w), lambda b, t: (0, 0))],
        out_specs=[tok(w), st],
        out_shape=[jax.ShapeDtypeStruct((bsz, L, w), f32), jax.ShapeDtypeStruct(s0.shape, f32)],
        scratch_shapes=[pltpu.VMEM((GDN_HEADS, HEAD_DIM, HEAD_DIM), f32)],
        compiler_params=pltpu.CompilerParams(
            dimension_semantics=("parallel", "arbitrary"), vmem_limit_bytes=VMEM_LIMIT_BYTES),
        name="gdn_scan",
    )(uv, wk, qh, kt, qk, gb, gate, s0.astype(f32).swapaxes(2, 3), on)


def gdn_mix_pallas(qkv, ab, gate, conv_ctx, s0, conv_w, a_log, dt_bias, o_norm):
    qn, kn, vv, gb, conv_state = gdn_prep_pallas(qkv, ab, conv_ctx, conv_w, a_log, dt_bias)
    uv, wk, qh, kt, qk = gdn_local_pallas(qn, kn, vv, gb)
    o, s_fin_t = gdn_scan_pallas(uv, wk, qh, kt, qk, gb, gate, s0, o_norm)
    return o, conv_state.astype(qkv.dtype), s_fin_t.swapaxes(2, 3).astype(s0.dtype)


def _gdn_dec_body(x_ref, ab_ref, gate_ref, ctx_ref, s_ref, w_ref, alog_ref, dtb_ref, on_ref, o_ref, cs_ref, sn_ref):
    f32, bf16 = jnp.float32, jnp.bfloat16
    rnd = lambda a: a.astype(bf16).astype(f32)
    x, ctx, w = x_ref[...], ctx_ref[...], w_ref[...]
    halo = CONV_W - 1
    y = x * w[halo:halo + 1]
    for j in range(halo):
        y = y + ctx[j:j + 1] * w[j:j + 1]
    y = y * _sigmoid(y)
    cs_ref[...] = jnp.concatenate([ctx[1:], x], axis=0)
    ab = ab_ref[...]
    xa = ab + dtb_ref[...]
    g_all = -jnp.exp(alog_ref[...]) * (jnp.maximum(xa, 0.0) + jnp.log(1.0 + jnp.exp(-jnp.abs(xa))))
    beta_all = _sigmoid(ab)
    lane = lax.broadcasted_iota(jnp.int32, (1, LANES), 1)
    pick = lambda a, idx: jnp.sum(jnp.where(lane == idx, a, 0.0), axis=-1, keepdims=True)
    r_i = lax.broadcasted_iota(jnp.int32, (HEAD_DIM, HEAD_DIM), 0)
    c_i = lax.broadcasted_iota(jnp.int32, (HEAD_DIM, HEAD_DIM), 1)
    eye = r_i == c_i
    to_col = lambda row: jnp.sum(jnp.where(eye, jnp.broadcast_to(row, (HEAD_DIM, HEAD_DIM)), 0.0), axis=-1, keepdims=True)
    to_row = lambda col: jnp.sum(jnp.where(eye, jnp.broadcast_to(col, (HEAD_DIM, HEAD_DIM)), 0.0), axis=0, keepdims=True)
    outs = []
    for h in range(GDN_HEADS):
        sl = slice(h * HEAD_DIM, (h + 1) * HEAD_DIM)
        q, k, v = y[:, sl], y[:, SEQ_WIDTH + h * HEAD_DIM:SEQ_WIDTH + (h + 1) * HEAD_DIM], y[:, 2 * SEQ_WIDTH + h * HEAD_DIM:2 * SEQ_WIDTH + (h + 1) * HEAD_DIM]
        q = q * (lax.rsqrt(jnp.sum(q * q, axis=-1, keepdims=True) + EPS) * HEAD_DIM ** -0.5)
        k = k * lax.rsqrt(jnp.sum(k * k, axis=-1, keepdims=True) + EPS)
        eg = jnp.exp(pick(g_all, h))
        beta = pick(beta_all, GDN_HEADS + h)
        s = s_ref[h]
        sb = rnd(s)
        s_wk = jnp.sum(sb * rnd((beta * eg) * k), axis=-1, keepdims=True)
        u = to_col(beta * v) - s_wk
        ub = rnd(u)
        s_q = jnp.sum(sb * rnd(eg * q), axis=-1, keepdims=True)
        qk = jnp.sum(rnd(q) * rnd(k), axis=-1, keepdims=True)
        o = to_row(s_q + rnd(qk) * ub)
        sn_ref[h] = eg * s + ub * rnd(k)
        outs.append(o * lax.rsqrt(jnp.mean(o * o, axis=-1, keepdims=True) + EPS))
    gt = gate_ref[...]
    o_ref[...] = (jnp.concatenate(outs, axis=-1) * on_ref[...]) * (gt * _sigmoid(gt))


def gdn_decode_pallas(qkv, ab, gate, conv_ctx, s0, conv_w, a_log, dt_bias, o_norm):
    bsz = qkv.shape[0]
    f32 = jnp.float32
    pad = lambda r: jnp.pad(r.astype(f32).reshape(1, -1), ((0, 0), (0, LANES - r.shape[-1])))
    per_b = lambda *shape: pl.BlockSpec((None,) + shape, lambda b: (b,) + (0,) * len(shape))
    const = lambda *shape: pl.BlockSpec(shape, lambda b: (0,) * len(shape))
    st = (GDN_HEADS, HEAD_DIM, HEAD_DIM)
    o, conv_state, s_new = pl.pallas_call(
        _gdn_dec_body,
        grid=(bsz,),
        in_specs=[per_b(1, CONV_CH), per_b(1, LANES), per_b(1, SEQ_WIDTH), per_b(CONV_W - 1, CONV_CH), per_b(*st),
                  const(CONV_W, CONV_CH), const(1, LANES), const(1, LANES), const(1, SEQ_WIDTH)],
        out_specs=[per_b(1, SEQ_WIDTH), per_b(CONV_W - 1, CONV_CH), per_b(*st)],
        out_shape=[jax.ShapeDtypeStruct((bsz, 1, SEQ_WIDTH), f32), jax.ShapeDtypeStruct((bsz, CONV_W - 1, CONV_CH), f32),
                   jax.ShapeDtypeStruct((bsz,) + st, f32)],
        compiler_params=pltpu.CompilerParams(dimension_semantics=("parallel",), vmem_limit_bytes=VMEM_LIMIT_BYTES),
        name="gdn_decode",
    )(qkv, ab, gate, conv_ctx.astype(f32), s0.astype(f32), conv_w.astype(f32), pad(a_log), pad(dt_bias),
      jnp.tile(o_norm.astype(f32), GDN_HEADS).reshape(1, SEQ_WIDTH))
    return o, conv_state.astype(qkv.dtype), s_new.astype(s0.dtype)


def _pad_cols(w, n):
    return jnp.pad(w, ((0, 0), (0, n - w.shape[1])))


def _in_proj_layout(kind, w_in):
    f32, bf16 = jnp.float32, jnp.bfloat16
    if kind == 0:
        return w_in, ((0, MEM_WIDTH), (MEM_WIDTH, SEQ_WIDTH)), (bf16, f32)
    if kind == 1:
        w = _pad_cols(w_in, MEM_WIDTH + sum(DSA_SPLITS[:4]) + LANES)
        widths = (MEM_WIDTH,) + DSA_SPLITS[:4] + (LANES,)
        starts = np.cumsum((0,) + widths[:-1]).tolist()
        return w, tuple(zip(starts, widths)), (bf16, bf16, f32, f32, bf16, f32)
    c0, c1 = MEM_WIDTH + CONV_CH, MEM_WIDTH + CONV_CH + 2 * GDN_HEADS
    w = jnp.concatenate([w_in[:, :c0], w_in[:, c1:], _pad_cols(w_in[:, c0:c1], LANES)], axis=1)
    widths = (MEM_WIDTH, CONV_CH, SEQ_WIDTH, LANES)
    starts = np.cumsum((0,) + widths[:-1]).tolist()
    return w, tuple(zip(starts, widths)), (bf16, f32, f32, f32)


def kernel(x_prompt, x_sample, cache_mem_k, cache_mem_v, state_ssm_re, state_ssm_im, cache_k, cache_v, cache_kidx, state_conv, state_delta, page_table, mem_prompt, norm_g, final_norm, w_in_a, w_in_b, w_in_c, w_out, w_mem_kv, ffn1_gate, ffn1_up, ffn1_down, ffn2_gate, ffn2_up, ffn2_down, s5_lam_re, s5_lam_im, s5_log_step, s5_b_re, s5_b_im, s5_c_re, s5_c_im, s5_d, s5_w_glu, s5_b_glu, rel_bias, gdn_conv_w, gdn_a_log, gdn_dt_bias, gdn_o_norm):
    depth = norm_g.shape[0]
    bp, lp, d = x_prompt.shape
    bs, ls, _ = x_sample.shape
    assert ls == 1, "the decode-step kernels handle one new token per sample"
    f32, bf16 = jnp.float32, jnp.bfloat16
    hp, hs = x_prompt.reshape(bp * lp, d), x_sample.reshape(bs * ls, d)
    mem_rows = mem_prompt.reshape(bp * N_MEM, d)
    page_view = lambda pool: pool.reshape(pool.shape[0], PAGE_SIZE, KV_WIDTH)
    pools = [(page_view(cache_k[j]), page_view(cache_v[j])) for j in range(cache_k.shape[0])]
    mem_k_p, mem_v_p = [], []
    ssm_re_p, ssm_im_p, ssm_re_s, ssm_im_s = [], [], [], []
    k_p, v_p, ki_p, k_s, v_s, ki_s = [], [], [], [], [], []
    conv_p, delta_p, conv_s, delta_s = [], [], [], []
    w_in_by_kind = (w_in_a, w_in_b, w_in_c)
    for i in range(depth):
        kind, j = i % N_MIXERS, i // N_MIXERS
        last = i == depth - 1
        ffn1 = (ffn1_gate[i].astype(bf16), ffn1_up[i].astype(bf16), ffn1_down[i].astype(bf16))
        ffn2 = (ffn2_gate[i].astype(bf16), ffn2_up[i].astype(bf16), ffn2_down[i].astype(bf16))
        w_in, segments, dtypes = _in_proj_layout(kind, w_in_by_kind[kind][j])
        w_in = w_in.astype(bf16)
        hp = ffn_residual(hp, norm_g[i, 0], *ffn1)
        hs = ffn_residual(hs, norm_g[i, 0], *ffn1)
        zp = [a.reshape(bp, lp, -1) for a in proj_in(hp, norm_g[i, 1], w_in, segments, dtypes)]
        zs = [a.reshape(bs, ls, -1) for a in proj_in(hs, norm_g[i, 1], w_in, segments, dtypes)]
        mk, mv = proj_in(mem_rows, norm_g[i, 1], w_mem_kv[i].astype(bf16), ((0, MEM_WIDTH), (MEM_WIDTH, MEM_WIDTH)),
                         (f32, f32), normalize=False)
        mem_shape = (bp, N_MEM, MEM_HEADS, HEAD_DIM)
        mk, mv = mk.reshape(mem_shape), mv.reshape(mem_shape)
        mem_k_p.append(mk)
        mem_v_p.append(mv)
        if kind == 0:
            disc = s5_discretize(s5_lam_re[j], s5_lam_im[j], s5_log_step[j], s5_b_re[j], s5_b_im[j],
                                 s5_c_re[j], s5_c_im[j])
            gate = (s5_d[j], s5_w_glu[j], s5_b_glu[j])
            h0 = jnp.zeros((bp, S5_GROUPS, S5_STATE), state_ssm_re.dtype)
            op, hr, hi = s5_mix_pallas(zp[1], h0, h0, disc, *gate)
            ssm_re_p.append(hr)
            ssm_im_p.append(hi)
            osm, hr, hi = s5_mix_pallas(zs[1], state_ssm_re[j], state_ssm_im[j], disc, *gate)
            ssm_re_s.append(hr)
            ssm_im_s.append(hi)
        elif kind == 1:
            _, q, k, v, qi, kiwi = zp
            op = dsa_prompt_pallas(q, qi, kiwi, k, v, rel_bias)
            kv_shape = (DSA_KV_HEADS, HEAD_DIM)
            k_p.append(k.reshape((bp, lp) + kv_shape))
            v_p.append(v.reshape((bp, lp) + kv_shape))
            ki_p.append(kiwi[..., :IDX_DIM])
            _, q, k, v, qi, kiwi = zs
            osm = dsa_decode_pallas(q[:, 0], qi[:, 0], kiwi[:, 0], k[:, 0], v[:, 0], pools[j][0], pools[j][1],
                                    cache_kidx[j], page_table, rel_bias)
            k_s.append(k.reshape((bs, ls) + kv_shape))
            v_s.append(v.reshape((bs, ls) + kv_shape))
            ki_s.append(kiwi[..., :IDX_DIM])
        else:
            gdn = (gdn_conv_w[j], gdn_a_log[j], gdn_dt_bias[j], gdn_o_norm[j])
            ctx0 = jnp.zeros((bp, CONV_W - 1, CONV_CH), f32)
            s00 = jnp.zeros((bp, GDN_HEADS, HEAD_DIM, HEAD_DIM), state_delta.dtype)
            _, qkv, gate, ab = zp
            op, cst, sst = gdn_mix_pallas(qkv, ab, gate, ctx0, s00, *gdn)
            conv_p.append(cst)
            delta_p.append(sst)
            _, qkv, gate, ab = zs
            osm, cst, sst = gdn_decode_pallas(qkv, ab, gate, state_conv[j], state_delta[j], *gdn)
            conv_s.append(cst)
            delta_s.append(sst)
        w_o = w_out[i].astype(bf16)
        hp = mix_out(hp.reshape(bp, lp, d), zp[0], op, mk, mv, w_o).reshape(bp * lp, d)
        hs = mix_out(hs.reshape(bs, ls, d), zs[0], osm, cache_mem_k[i], cache_mem_v[i], w_o).reshape(bs * ls, d)
        fg = final_norm if last else None
        hp = ffn_residual(hp, norm_g[i, 2], *ffn2, fg)
        hs = ffn_residual(hs, norm_g[i, 2], *ffn2, fg)
    st = jnp.stack
    return (hp.reshape(bp, lp, d), hs.reshape(bs, ls, d), st(mem_k_p), st(mem_v_p),
            st(ssm_re_p), st(ssm_im_p), st(ssm_re_s), st(ssm_im_s),
            st(k_p), st(v_p), st(ki_p), st(k_s), st(v_s), st(ki_s),
            st(conv_p), st(delta_p), st(conv_s), st(delta_s))
```

```python
import math
import functools
import jax
import jax.numpy as jnp
from jax import lax
import numpy as np
from jax.experimental import pallas as pl
from jax.experimental.pallas import tpu as pltpu

D_MODEL = 1024
N_MIXERS = 3
HEAD_DIM = 64
MIX_WIDTH = D_MODEL
N_MEM = 256
MEM_HEADS = 4
MEM_WIDTH = MEM_HEADS * HEAD_DIM
SEQ_WIDTH = MIX_WIDTH - MEM_WIDTH
S5_GROUP = 16
S5_GROUPS = SEQ_WIDTH // S5_GROUP
S5_STATE = 64
DSA_HEADS = SEQ_WIDTH // HEAD_DIM
DSA_KV_HEADS = 4
DSA_GQA = DSA_HEADS // DSA_KV_HEADS
IDX_HEADS = 8
IDX_DIM = 64
TOPK_MAX = 256
QBLOCK = 128
N_BUCKETS = 32
MAX_DISTANCE = 128
GDN_HEADS = SEQ_WIDTH // HEAD_DIM
CONV_W = 4
CONV_CH = 3 * SEQ_WIDTH
GDN_CHUNK = 64
D_FF = 2816
EPS = 1e-6
PAGE_SIZE = 128
DSA_SPLITS = (DSA_HEADS * HEAD_DIM, DSA_KV_HEADS * HEAD_DIM, DSA_KV_HEADS * HEAD_DIM, IDX_HEADS * IDX_DIM, IDX_DIM, IDX_HEADS)
GDN_SPLITS = (CONV_CH, GDN_HEADS, GDN_HEADS, SEQ_WIDTH)

LANES = 128
VMEM_LIMIT_BYTES = 56 * 1024 * 1024


def _sigmoid(x):
    return 1.0 / (1.0 + jnp.exp(-x))


def t5_bucket(dist):
    max_exact = N_BUCKETS // 2
    n = jnp.maximum(dist, 0)
    nf = jnp.maximum(n, 1).astype(jnp.float32)
    large = max_exact + (jnp.log(nf / max_exact) / math.log(MAX_DISTANCE / max_exact)
                         * (N_BUCKETS - max_exact)).astype(jnp.int32)
    return jnp.where(n < max_exact, n, jnp.minimum(large, N_BUCKETS - 1))


FFN_CHUNK = 256


def _ffn_body(final_norm, x_ref, g_ref, wg_ref, wu_ref, wd_ref, gf_ref, o_ref):
    f32, bf16 = jnp.float32, jnp.bfloat16
    x = x_ref[...]
    n = ((x * lax.rsqrt(jnp.mean(x * x, axis=-1, keepdims=True) + EPS)) * g_ref[...]).astype(bf16)
    acc = jnp.zeros(x.shape, f32)
    for c in range(wg_ref.shape[1] // FFN_CHUNK):
        sl = slice(c * FFN_CHUNK, (c + 1) * FFN_CHUNK)
        a = jnp.dot(n, wg_ref[:, sl], preferred_element_type=f32)
        b = jnp.dot(n, wu_ref[:, sl], preferred_element_type=f32)
        h = (a * _sigmoid(a)) * b
        acc = acc + jnp.dot(h.astype(bf16), wd_ref[sl, :], preferred_element_type=f32)
    y = x + 0.5 * acc
    if final_norm:
        y = (y * lax.rsqrt(jnp.mean(y * y, axis=-1, keepdims=True) + EPS)) * gf_ref[...]
    o_ref[...] = y


def ffn_residual(x, g, wg, wu, wd, final_g=None, *, tm=512):
    t, d = x.shape
    f = wg.shape[1]
    tm = min(tm, t)
    gf = jnp.ones((d,), jnp.float32) if final_g is None else final_g
    resident = lambda shape: pl.BlockSpec(shape, lambda i: (0, 0), pipeline_mode=pl.Buffered(1))
    return pl.pallas_call(
        functools.partial(_ffn_body, final_g is not None),
        grid=(t // tm,),
        in_specs=[pl.BlockSpec((tm, d), lambda i: (i, 0)), resident((1, d)),
                  resident((d, f)), resident((d, f)), resident((f, d)), resident((1, d))],
        out_specs=pl.BlockSpec((tm, d), lambda i: (i, 0)),
        out_shape=jax.ShapeDtypeStruct((t, d), jnp.float32),
        compiler_params=pltpu.CompilerParams(dimension_semantics=("parallel",), vmem_limit_bytes=VMEM_LIMIT_BYTES),
        name="ffn_residual",
    )(x, g.reshape(1, d).astype(jnp.float32), wg, wu, wd, gf.reshape(1, d).astype(jnp.float32))


def _proj_in_body(segments, normalize, x_ref, g_ref, w_ref, *o_refs):
    x = x_ref[...]
    if normalize:
        r = lax.rsqrt(jnp.mean(x * x, axis=-1, keepdims=True) + EPS)
        x = (x * r) * g_ref[...]
    n = x.astype(jnp.bfloat16)
    for (start, width), o_ref in zip(segments, o_refs):
        o_ref[...] = jnp.dot(n, w_ref[:, start:start + width], preferred_element_type=jnp.float32).astype(o_ref.dtype)


def proj_in(x, g, w, segments, dtypes, *, normalize=True, tm=512):
    t, d = x.shape
    tm = min(tm, t)
    return pl.pallas_call(
        functools.partial(_proj_in_body, tuple(segments), normalize),
        grid=(t // tm,),
        in_specs=[pl.BlockSpec((tm, d), lambda i: (i, 0)), pl.BlockSpec((1, d), lambda i: (0, 0)),
                  pl.BlockSpec(w.shape, lambda i: (0, 0))],
        out_specs=[pl.BlockSpec((tm, width), lambda i: (i, 0)) for _, width in segments],
        out_shape=[jax.ShapeDtypeStruct((t, width), dt) for (_, width), dt in zip(segments, dtypes)],
        compiler_params=pltpu.CompilerParams(dimension_semantics=("parallel",), vmem_limit_bytes=VMEM_LIMIT_BYTES),
        name="proj_in",
    )(x, g.reshape(1, d).astype(jnp.float32), w)


def _mix_out_body(x_ref, cq_ref, om_ref, mkt_ref, mv_ref, w_ref, o_ref):
    f32, bf16 = jnp.float32, jnp.bfloat16
    cq = cq_ref[...]
    heads = []
    for h in range(MEM_HEADS):
        sl = slice(h * HEAD_DIM, (h + 1) * HEAD_DIM)
        logits = jnp.dot(cq[:, sl], mkt_ref[sl, :], preferred_element_type=f32) * HEAD_DIM ** -0.5
        p = jnp.exp(logits - jnp.max(logits, axis=-1, keepdims=True))
        p = p / jnp.sum(p, axis=-1, keepdims=True)
        heads.append(jnp.dot(p.astype(bf16), mv_ref[:, sl], preferred_element_type=f32))
    o_mem = jnp.concatenate(heads, axis=-1).astype(bf16)
    y = (jnp.dot(o_mem, w_ref[:MEM_WIDTH, :], preferred_element_type=f32)
         + jnp.dot(om_ref[...].astype(bf16), w_ref[MEM_WIDTH:, :], preferred_element_type=f32))
    o_ref[...] = x_ref[...] + y


def mix_out(x, cq, o_mix, mk, mv, w_out, *, tm=512):
    bsz, L, d = x.shape
    tm = min(tm, L)
    bf16 = jnp.bfloat16
    mkt = mk.astype(bf16).reshape(bsz, N_MEM, MEM_WIDTH).swapaxes(1, 2)
    mvf = mv.astype(bf16).reshape(bsz, N_MEM, MEM_WIDTH)
    tok = lambda n: pl.BlockSpec((None, tm, n), lambda b, t: (b, t, 0))
    per_b = lambda r, c: pl.BlockSpec((None, r, c), lambda b, t: (b, 0, 0))
    return pl.pallas_call(
        _mix_out_body,
        grid=(bsz, L // tm),
        in_specs=[tok(d), tok(MEM_WIDTH), tok(SEQ_WIDTH), per_b(MEM_WIDTH, N_MEM), per_b(N_MEM, MEM_WIDTH),
                  pl.BlockSpec((d, d), lambda b, t: (0, 0))],
        out_specs=tok(d),
        out_shape=jax.ShapeDtypeStruct((bsz, L, d), jnp.float32),
        compiler_params=pltpu.CompilerParams(
            dimension_semantics=("parallel", "parallel"), vmem_limit_bytes=VMEM_LIMIT_BYTES),
        name="mix_out",
    )(x, cq, o_mix, mkt, mvf, w_out)


S5_LANES = S5_GROUPS * S5_STATE
S5_GROUPS_PER_BLOCK = LANES // S5_GROUP
S5_BLOCKS = SEQ_WIDTH // LANES
S5_BLOCK_STATES = S5_GROUPS_PER_BLOCK * S5_STATE


def s5_discretize(lam_re, lam_im, log_step, b_re, b_im, c_re, c_im):
    f32, bf16 = jnp.float32, jnp.bfloat16
    lr, li = lam_re.astype(f32), lam_im.astype(f32)
    step = jnp.exp(log_step.astype(f32))[:, None]
    mag = jnp.exp(lr * step)
    ab_re, ab_im = mag * jnp.cos(li * step), mag * jnp.sin(li * step)
    den = lr * lr + li * li
    nr, ni = ab_re - 1.0, ab_im
    f_re = (nr * lr + ni * li) / den
    f_im = (ni * lr - nr * li) / den
    br, bi = b_re.astype(f32), b_im.astype(f32)
    bb_re = f_re[..., None] * br - f_im[..., None] * bi
    bb_im = f_re[..., None] * bi + f_im[..., None] * br
    eye = jnp.eye(S5_GROUPS_PER_BLOCK, dtype=f32)
    nb, gb = S5_BLOCKS, S5_GROUPS_PER_BLOCK

    def in_blocks(bb):
        w = jnp.einsum('jgpc,gh->jgchp', bb.reshape(nb, gb, S5_STATE, S5_GROUP), eye)
        return w.reshape(nb, LANES, S5_BLOCK_STATES).astype(bf16)

    def out_blocks(c):
        w = jnp.einsum('jgop,gh->jgpho', c.astype(f32).reshape(nb, gb, S5_GROUP, S5_STATE), eye)
        return w.reshape(nb, S5_BLOCK_STATES, LANES).astype(bf16)

    return (ab_re.reshape(1, S5_LANES), ab_im.reshape(1, S5_LANES),
            in_blocks(bb_re), in_blocks(bb_im), out_blocks(c_re), out_blocks(-c_im))


def _s5_project_in(u, wbr_ref, wbi_ref, bur_ref, bui_ref):
    ub = u.astype(jnp.bfloat16)
    for j in range(S5_BLOCKS):
        uj = ub[:, j * LANES:(j + 1) * LANES]
        sl = slice(j * S5_BLOCK_STATES, (j + 1) * S5_BLOCK_STATES)
        bur_ref[:, sl] = jnp.dot(uj, wbr_ref[j], preferred_element_type=jnp.float32)
        bui_ref[:, sl] = jnp.dot(uj, wbi_ref[j], preferred_element_type=jnp.float32)


def _s5_project_out(hr_ref, hi_ref, wcr_ref, wci_ref):
    cols = []
    for j in range(S5_BLOCKS):
        sl = slice(j * S5_BLOCK_STATES, (j + 1) * S5_BLOCK_STATES)
        cols.append(jnp.dot(hr_ref[:, sl].astype(jnp.bfloat16), wcr_ref[j], preferred_element_type=jnp.float32)
                    + jnp.dot(hi_ref[:, sl].astype(jnp.bfloat16), wci_ref[j], preferred_element_type=jnp.float32))
    return jnp.concatenate(cols, axis=-1)


def _s5_gate(y_ssm, u, d_ref, wglu_ref, bglu_ref):
    y = y_ssm + d_ref[...] * u
    y = 0.5 * y * (1.0 + jnp.tanh(math.sqrt(2.0 / math.pi) * (y + 0.044715 * (y * y * y))))
    z = jnp.dot(y.astype(jnp.bfloat16), wglu_ref[...], preferred_element_type=jnp.float32) + bglu_ref[...]
    return y * (1.0 / (1.0 + jnp.exp(-z)))


def _s5_seq_body(u_ref, h0r_ref, h0i_ref, ar_ref, ai_ref, wbr_ref, wbi_ref, wcr_ref, wci_ref, d_ref, wglu_ref, bglu_ref,
                 y_ref, hfr_ref, hfi_ref, bur_ref, bui_ref, cr_ref, ci_ref):
    tt = u_ref.shape[0]

    @pl.when(pl.program_id(1) == 0)
    def _():
        cr_ref[...] = h0r_ref[...]
        ci_ref[...] = h0i_ref[...]

    u = u_ref[...]
    _s5_project_in(u, wbr_ref, wbi_ref, bur_ref, bui_ref)
    ar, ai = ar_ref[...], ai_ref[...]

    def step(t, carry):
        hr, hi = carry
        nhr = ar * hr - ai * hi + bur_ref[pl.ds(t, 1), :]
        nhi = ar * hi + ai * hr + bui_ref[pl.ds(t, 1), :]
        bur_ref[pl.ds(t, 1), :] = nhr
        bui_ref[pl.ds(t, 1), :] = nhi
        return nhr, nhi

    hr, hi = lax.fori_loop(0, tt, step, (cr_ref[...], ci_ref[...]))
    cr_ref[...] = hr
    ci_ref[...] = hi
    hfr_ref[...] = hr
    hfi_ref[...] = hi
    y_ref[...] = _s5_gate(_s5_project_out(bur_ref, bui_ref, wcr_ref, wci_ref), u, d_ref, wglu_ref, bglu_ref)


def _s5_step_body(u_ref, h0r_ref, h0i_ref, ar_ref, ai_ref, wbr_ref, wbi_ref, wcr_ref, wci_ref, d_ref, wglu_ref, bglu_ref,
                  y_ref, hfr_ref, hfi_ref, bur_ref, bui_ref):
    u = u_ref[...]
    _s5_project_in(u, wbr_ref, wbi_ref, bur_ref, bui_ref)
    ar, ai = ar_ref[...], ai_ref[...]
    hr, hi = h0r_ref[...], h0i_ref[...]
    nhr = ar * hr - ai * hi + bur_ref[...]
    nhi = ar * hi + ai * hr + bui_ref[...]
    bur_ref[...] = nhr
    bui_ref[...] = nhi
    hfr_ref[...] = nhr
    hfi_ref[...] = nhi
    y_ref[...] = _s5_gate(_s5_project_out(bur_ref, bui_ref, wcr_ref, wci_ref), u, d_ref, wglu_ref, bglu_ref)


def s5_mix_pallas(u, h0_re, h0_im, disc, d_skip, w_glu, b_glu, *, tt=512):
    bsz, L, w = u.shape
    f32 = jnp.float32
    ar, ai, wbr, wbi, wcr, wci = disc
    d2, bg2, wg = d_skip.reshape(1, w).astype(f32), b_glu.reshape(1, w).astype(f32), w_glu.astype(jnp.bfloat16)
    const2 = lambda *_: (0, 0)
    const3 = lambda *_: (0, 0, 0)
    w_specs = [pl.BlockSpec((1, S5_LANES), const2), pl.BlockSpec((1, S5_LANES), const2),
               pl.BlockSpec(wbr.shape, const3), pl.BlockSpec(wbi.shape, const3),
               pl.BlockSpec(wcr.shape, const3), pl.BlockSpec(wci.shape, const3),
               pl.BlockSpec((1, w), const2), pl.BlockSpec((w, w), const2), pl.BlockSpec((1, w), const2)]
    w_args = (ar, ai, wbr, wbi, wcr, wci, d2, wg, bg2)
    if L == 1:
        rows = bsz
        h0r, h0i = h0_re.reshape(rows, S5_LANES).astype(f32), h0_im.reshape(rows, S5_LANES).astype(f32)
        row_spec = lambda n: pl.BlockSpec((rows, n), const2)
        y, hr, hi = pl.pallas_call(
            _s5_step_body,
            grid=(1,),
            in_specs=[row_spec(w), row_spec(S5_LANES), row_spec(S5_LANES)] + w_specs,
            out_specs=[row_spec(w), row_spec(S5_LANES), row_spec(S5_LANES)],
            out_shape=[jax.ShapeDtypeStruct((rows, w), f32), jax.ShapeDtypeStruct((rows, S5_LANES), f32),
                       jax.ShapeDtypeStruct((rows, S5_LANES), f32)],
            scratch_shapes=[pltpu.VMEM((rows, S5_LANES), f32), pltpu.VMEM((rows, S5_LANES), f32)],
            compiler_params=pltpu.CompilerParams(vmem_limit_bytes=VMEM_LIMIT_BYTES),
            name="s5_step",
        )(u.reshape(rows, w), h0r, h0i, *w_args)
        y = y.reshape(bsz, 1, w)
    else:
        tt = min(tt, L)
        h0r, h0i = h0_re.reshape(bsz, 1, S5_LANES).astype(f32), h0_im.reshape(bsz, 1, S5_LANES).astype(f32)
        st_spec = pl.BlockSpec((None, 1, S5_LANES), lambda b, t: (b, 0, 0))
        y, hr, hi = pl.pallas_call(
            _s5_seq_body,
            grid=(bsz, L // tt),
            in_specs=[pl.BlockSpec((None, tt, w), lambda b, t: (b, t, 0)), st_spec, st_spec] + w_specs,
            out_specs=[pl.BlockSpec((None, tt, w), lambda b, t: (b, t, 0)), st_spec, st_spec],
            out_shape=[jax.ShapeDtypeStruct((bsz, L, w), f32), jax.ShapeDtypeStruct((bsz, 1, S5_LANES), f32),
                       jax.ShapeDtypeStruct((bsz, 1, S5_LANES), f32)],
            scratch_shapes=[pltpu.VMEM((tt, S5_LANES), f32), pltpu.VMEM((tt, S5_LANES), f32),
                            pltpu.VMEM((1, S5_LANES), f32), pltpu.VMEM((1, S5_LANES), f32)],
            compiler_params=pltpu.CompilerParams(
                dimension_semantics=("parallel", "arbitrary"), vmem_limit_bytes=VMEM_LIMIT_BYTES),
            name="s5_seq",
        )(u, h0r, h0i, *w_args)
    shp = (bsz, S5_GROUPS, S5_STATE)
    return y, hr.reshape(shp).astype(h0_re.dtype), hi.reshape(shp).astype(h0_im.dtype)


DSA_KEY_CHUNK = 512
DSA_NEAR = 2 * QBLOCK
INT32_MIN = -2 ** 31
NEG_BIG = -1e30
KT_ROWS = HEAD_DIM + 16


def _sortable_key(s):
    bits = lax.bitcast_convert_type(s, jnp.int32)
    return jnp.where(bits < 0, bits ^ jnp.int32(0x7FFFFFFF), bits)


def _dsa_prompt_body(n_top, rb_ref, q_ref, qi_ref, kiwi_ref, kt_ref, v_ref, kit_ref, o_ref, keys_ref, bias_ref, bound_ref, hi_ref, lo_ref):
    i = pl.program_id(1)
    f32, bf16 = jnp.float32, jnp.bfloat16
    kc = DSA_KEY_CHUNK
    q_start = i * QBLOCK
    n_all = (q_start + QBLOCK + kc - 1) // kc
    near_start = jnp.maximum(q_start - QBLOCK, 0)
    n_far = (near_start + kc - 1) // kc
    row = lax.broadcasted_iota(jnp.int32, (QBLOCK, 1), 0)
    qpos = q_start + row

    @pl.when(i == 0)
    def _():
        r = lax.broadcasted_iota(jnp.int32, (QBLOCK, DSA_NEAR), 0)
        c = lax.broadcasted_iota(jnp.int32, (QBLOCK, DSA_NEAR), 1)
        for tile in range(2):
            bucket = t5_bucket(r + tile * QBLOCK - c)
            for h in range(DSA_HEADS):
                b = jnp.zeros((QBLOCK, DSA_NEAR), f32)
                for bk in range(N_BUCKETS):
                    b = jnp.where(bucket == bk, rb_ref[bk, h] - rb_ref[N_BUCKETS - 1, h], b)
                kv, g = divmod(h, DSA_GQA)
                bias_ref[tile, kv, g * QBLOCK:(g + 1) * QBLOCK, :] = b
        lane = lax.broadcasted_iota(jnp.int32, (1, LANES), 1)
        bound = jnp.zeros((1, LANES), f32)
        for kv in range(DSA_KV_HEADS):
            kk = kt_ref[kv * KT_ROWS:kv * KT_ROWS + HEAD_DIM, :].astype(f32)
            kmax = jnp.sqrt(jnp.max(jnp.sum(kk * kk, axis=0, keepdims=True), axis=-1, keepdims=True))
            bound = jnp.where(lane == kv, kmax, bound)
        for h in range(DSA_HEADS):
            kv, g = divmod(h, DSA_GQA)
            bmax = jnp.maximum(jnp.max(bias_ref[:, kv, g * QBLOCK:(g + 1) * QBLOCK, :]), 0.0)
            bound = jnp.where(lane == DSA_KV_HEADS + h, bmax, bound)
        bound_ref[...] = bound

    qi_all = (qi_ref[...].astype(f32) * IDX_DIM ** -0.5).astype(bf16)
    qi = jnp.concatenate([qi_all[:, h * IDX_DIM:(h + 1) * IDX_DIM] for h in range(IDX_HEADS)], axis=0)
    wi = kiwi_ref[:, IDX_DIM:IDX_DIM + IDX_HEADS] * IDX_HEADS ** -0.5

    def score_chunk(c, _):
        off = pl.multiple_of(c * kc, kc)
        d = jnp.dot(qi, kit_ref[:, pl.ds(off, kc)], preferred_element_type=f32)
        s = jnp.zeros((QBLOCK, kc), f32)
        for h in range(IDX_HEADS):
            s = s + wi[:, h:h + 1] * jnp.maximum(d[h * QBLOCK:(h + 1) * QBLOCK], 0.0)
        kpos = off + lax.broadcasted_iota(jnp.int32, (QBLOCK, kc), 1)
        s = jnp.where(s == 0.0, 0.0, s)
        s = jnp.where(kpos <= qpos, s, -jnp.inf)
        key = _sortable_key(s)
        keys_ref[:, pl.ds(off, kc)] = key
        hi_ref[:, pl.ds(off, kc)] = lax.shift_right_arithmetic(key, 16).astype(jnp.int16)
        return 0

    lax.fori_loop(0, n_all, score_chunk, 0)

    def count_where(pred_fn):
        def body(c, acc):
            off = pl.multiple_of(c * kc, kc)
            hit = pred_fn(keys_ref[:, pl.ds(off, kc)], off)
            part = jnp.where(hit, 1.0, 0.0)
            for j in range(kc // 128):
                acc = acc + part[:, j * 128:(j + 1) * 128]
            return acc
        acc = lax.fori_loop(0, n_all, body, jnp.zeros((QBLOCK, 128), f32))
        return jnp.sum(acc, axis=-1, keepdims=True)

    def count16(ref, pred_fn):
        def body(c, acc):
            off = pl.multiple_of(c * kc, kc)
            part = jnp.where(pred_fn(ref[:, pl.ds(off, kc)]), jnp.int16(1), jnp.int16(0))
            return acc + ((part[:, 0:128] + part[:, 128:256]) + (part[:, 256:384] + part[:, 384:512]))
        acc = lax.fori_loop(0, n_all, body, jnp.zeros((QBLOCK, 128), jnp.int16))
        return jnp.sum(acc.astype(f32), axis=-1, keepdims=True)

    def search16(ref, want):
        def bit(it, t):
            cand = t + lax.shift_left(jnp.int32(1), 15 - it)
            c16 = cand.astype(jnp.int16)
            return jnp.where(count16(ref, lambda x: x >= c16) >= want, cand, t)
        return lax.fori_loop(0, 16, bit, jnp.full((QBLOCK, 1), -2 ** 15, jnp.int32))

    t_hi = search16(hi_ref, n_top)
    t_hi16 = t_hi.astype(jnp.int16)
    n_above = count16(hi_ref, lambda x: x > t_hi16)

    def band_chunk(c, _):
        off = pl.multiple_of(c * kc, kc)
        key = keys_ref[:, pl.ds(off, kc)]
        lo = ((key & 0xFFFF) - 2 ** 15).astype(jnp.int16)
        lo_ref[:, pl.ds(off, kc)] = jnp.where(lax.shift_right_arithmetic(key, 16) == t_hi, lo, jnp.int16(-2 ** 15))
        return 0

    lax.fori_loop(0, n_all, band_chunk, 0)
    t_lo = search16(lo_ref, n_top - n_above)
    thr = lax.shift_left(t_hi, 16) + (t_lo + 2 ** 15)

    def is_valid(off, width):
        return (off + lax.broadcasted_iota(jnp.int32, (QBLOCK, width), 1)) <= qpos

    n_gt = count_where(lambda k, off: (k > thr) & is_valid(off, kc))
    n_eq = count_where(lambda k, off: (k == thr) & is_valid(off, kc))
    need = n_top - n_gt
    has_extra_ties = jnp.max(jnp.where(n_eq > need, 1.0, 0.0)) > 0.0

    def tie_search():
        def idx_bit(it, j):
            cand = j + lax.shift_left(jnp.int32(1), 13 - it)
            cnt = count_where(lambda k, off: (k == thr) & is_valid(off, kc)
                              & ((off + lax.broadcasted_iota(jnp.int32, (QBLOCK, kc), 1)) < cand))
            return jnp.where(cnt <= need, cand, j)
        return lax.fori_loop(0, 14, idx_bit, jnp.zeros((QBLOCK, 1), jnp.int32))

    tie_end = lax.cond(has_extra_ties, tie_search, lambda: jnp.full((QBLOCK, 1), 2 ** 14, jnp.int32))

    def selected(keys, off, width):
        kpos = off + lax.broadcasted_iota(jnp.int32, (QBLOCK, width), 1)
        return (kpos <= qpos) & ((keys > thr) | ((keys == thr) & (kpos < tie_end))), kpos

    tile = jnp.minimum(i, 1)
    q_all = (q_ref[...].astype(f32) * HEAD_DIM ** -0.5).astype(bf16)
    qh = [q_all[:, h * HEAD_DIM:(h + 1) * HEAD_DIM] for h in range(DSA_HEADS)]
    off_near = pl.multiple_of(near_start, QBLOCK)
    sel_near, _ = selected(keys_ref[:, pl.ds(off_near, DSA_NEAR)], off_near, DSA_NEAR)

    def near_bias(h):
        kv, g = divmod(h, DSA_GQA)
        return jnp.where(sel_near, bias_ref[tile, kv, g * QBLOCK:(g + 1) * QBLOCK, :], NEG_BIG)

    def far_mask(c):
        off = pl.multiple_of(c * kc, kc)
        sel, kpos = selected(keys_ref[:, pl.ds(off, kc)], off, kc)
        return off, jnp.where(sel & (kpos < near_start), 0.0, NEG_BIG)

    def kt_block(kv, off, width, rows):
        return kt_ref[kv * KT_ROWS:kv * KT_ROWS + rows, pl.ds(off, width)]

    bound = bound_ref[...]
    lane = lax.broadcasted_iota(jnp.int32, (1, LANES), 1)
    pick = lambda idx: jnp.sum(jnp.where(lane == idx, bound, 0.0), axis=-1, keepdims=True)
    col = lax.broadcasted_iota(jnp.int32, (QBLOCK, KT_ROWS - HEAD_DIM), 1)
    q_aug = []
    for h in range(DSA_HEADS):
        qf = qh[h].astype(f32)
        ub = jnp.sqrt(jnp.sum(qf * qf, axis=-1, keepdims=True)) * pick(h // DSA_GQA) * 1.01 + pick(DSA_KV_HEADS + h)
        q_aug.append(jnp.concatenate([qh[h], jnp.where(col == 0, -ub, 0.0).astype(bf16)], axis=-1))

    def fast_attend(accs, off, width, mask_bias):
        accs = list(accs)
        for kv in range(DSA_KV_HEADS):
            kt = kt_block(kv, off, width, KT_ROWS)
            ps = [jnp.exp(jnp.dot(q_aug[kv * DSA_GQA + g], kt, preferred_element_type=f32)
                          + mask_bias(kv * DSA_GQA + g)).astype(bf16) for g in range(DSA_GQA)]
            accs[kv] = accs[kv] + jnp.dot(jnp.concatenate(ps, axis=0), v_ref[kv, pl.ds(off, width), :],
                                          preferred_element_type=f32)
        return tuple(accs)

    def fast_chunk(c, accs):
        off, mb = far_mask(c)
        return fast_attend(accs, off, kc, lambda h: mb)

    zero_acc = tuple(jnp.zeros((DSA_GQA * QBLOCK, 2 * HEAD_DIM), f32) for _ in range(DSA_KV_HEADS))
    accs = fast_attend(lax.fori_loop(0, n_far, fast_chunk, zero_acc), off_near, DSA_NEAR, near_bias)
    l_min = accs[0][:, HEAD_DIM:]
    for kv in range(1, DSA_KV_HEADS):
        l_min = jnp.minimum(l_min, accs[kv][:, HEAD_DIM:])
    fast_ok = jnp.min(l_min) > 1e-30

    def safe_attend(carry, off, width, mask_bias):
        ms, accs = list(carry[0]), list(carry[1])
        for kv in range(DSA_KV_HEADS):
            kt = kt_block(kv, off, width, HEAD_DIM)
            ps, alphas = [], []
            for g in range(DSA_GQA):
                h = kv * DSA_GQA + g
                logits = jnp.dot(qh[h], kt, preferred_element_type=f32) + mask_bias(h)
                m_new = jnp.maximum(ms[h], jnp.max(logits, axis=-1, keepdims=True))
                alphas.append(jnp.exp(ms[h] - m_new))
                ps.append(jnp.exp(logits - m_new).astype(bf16))
                ms[h] = m_new
            accs[kv] = (jnp.concatenate(alphas, axis=0) * accs[kv]
                        + jnp.dot(jnp.concatenate(ps, axis=0), v_ref[kv, pl.ds(off, width), :], preferred_element_type=f32))
        return tuple(ms), tuple(accs)

    def safe_path():
        def safe_chunk(c, carry):
            off, mb = far_mask(c)
            return safe_attend(carry, off, kc, lambda h: mb)
        init = (tuple(jnp.full((QBLOCK, 1), NEG_BIG, f32) for _ in range(DSA_HEADS)), zero_acc)
        return safe_attend(lax.fori_loop(0, n_far, safe_chunk, init), off_near, DSA_NEAR, near_bias)[1]

    accs = lax.cond(fast_ok, lambda: accs, safe_path)
    outs = []
    for kv in range(DSA_KV_HEADS):
        o = accs[kv][:, :HEAD_DIM] / accs[kv][:, HEAD_DIM:]
        outs += [o[g * QBLOCK:(g + 1) * QBLOCK] for g in range(DSA_GQA)]
    o_ref[...] = jnp.concatenate(outs, axis=-1)


def dsa_prompt_pallas(q, qi, kiwi, k, v, rel_bias):
    bsz, L = q.shape[:2]
    nq = L // QBLOCK
    n_top = min(TOPK_MAX, L // 4)
    f32, bf16 = jnp.float32, jnp.bfloat16
    k_t = k.astype(bf16).reshape(bsz, L, DSA_KV_HEADS, HEAD_DIM).transpose(0, 2, 3, 1)
    k_pad = jnp.zeros((bsz, DSA_KV_HEADS, KT_ROWS - HEAD_DIM, L), bf16).at[:, :, 0, :].set(1.0)
    k_t = jnp.concatenate([k_t, k_pad], axis=2).reshape(bsz, DSA_KV_HEADS * KT_ROWS, L)
    v_h = v.astype(bf16).reshape(bsz, L, DSA_KV_HEADS, HEAD_DIM).swapaxes(1, 2)
    v_h = jnp.concatenate([v_h, jnp.ones_like(v_h)], axis=-1)
    ki_t = kiwi[..., :IDX_DIM].astype(bf16).swapaxes(1, 2)
    lk = max(L, DSA_KEY_CHUNK)
    if lk != L:
        k_t = jnp.pad(k_t, ((0, 0), (0, 0), (0, lk - L)))
        v_h = jnp.pad(v_h, ((0, 0), (0, 0), (0, lk - L), (0, 0)))
        ki_t = jnp.pad(ki_t, ((0, 0), (0, 0), (0, lk - L)))
    tok = lambda n: pl.BlockSpec((None, QBLOCK, n), lambda b, i: (b, i, 0))
    return pl.pallas_call(
        functools.partial(_dsa_prompt_body, n_top),
        grid=(bsz, nq),
        in_specs=[
            pl.BlockSpec(memory_space=pltpu.SMEM),
            tok(SEQ_WIDTH), tok(IDX_HEADS * IDX_DIM), tok(LANES),
            pl.BlockSpec((None, DSA_KV_HEADS * KT_ROWS, lk), lambda b, i: (b, 0, 0), pipeline_mode=pl.Buffered(1)),
            pl.BlockSpec((None, DSA_KV_HEADS, lk, 2 * HEAD_DIM), lambda b, i: (b, 0, 0, 0), pipeline_mode=pl.Buffered(1)),
            pl.BlockSpec((None, IDX_DIM, lk), lambda b, i: (b, 0, 0), pipeline_mode=pl.Buffered(1)),
        ],
        out_specs=tok(SEQ_WIDTH),
        out_shape=jax.ShapeDtypeStruct((bsz, L, SEQ_WIDTH), f32),
        scratch_shapes=[pltpu.VMEM((QBLOCK, lk), jnp.int32),
                        pltpu.VMEM((2, DSA_KV_HEADS, DSA_GQA * QBLOCK, DSA_NEAR), f32),
                        pltpu.VMEM((1, LANES), f32),
                        pltpu.VMEM((QBLOCK, lk), jnp.int16), pltpu.VMEM((QBLOCK, lk), jnp.int16)],
        compiler_params=pltpu.CompilerParams(
            dimension_semantics=("parallel", "arbitrary"), vmem_limit_bytes=VMEM_LIMIT_BYTES),
        name="dsa_prompt",
    )(rel_bias.astype(f32), q, qi, kiwi, k_t, v_h, ki_t)


DEC_PAGES_PER_STEP = 16
KV_WIDTH = DSA_KV_HEADS * HEAD_DIM


def _index_score(qi, wi, ki):
    d = _dot_nt(qi, ki)
    s = jnp.sum(wi * jnp.maximum(d, 0.0), axis=0, keepdims=True)
    return jnp.where(s == 0.0, 0.0, s)


def _dsa_dec_score_body(pt_ref, qi_ref, wi_ref, *refs):
    pages, s_ref = refs[:-1], refs[-1]
    qi = (qi_ref[...].astype(jnp.float32) * IDX_DIM ** -0.5).astype(jnp.bfloat16)
    wi = wi_ref[...]
    s_ref[...] = jnp.concatenate([_index_score(qi, wi, p[...].astype(jnp.bfloat16)) for p in pages], axis=0)


def _dsa_dec_select_body(n_top, qi_ref, wi_ref, kinew_ref, s_ref, sel_ref, snew_ref):
    f32, bf16 = jnp.float32, jnp.bfloat16
    bsz, past = s_ref.shape
    qi = (qi_ref[...].astype(f32) * IDX_DIM ** -0.5).astype(bf16).astype(f32)
    d_new = jnp.sum(qi * kinew_ref[...].astype(bf16).astype(f32), axis=-1, keepdims=True)
    s_new = jnp.sum(wi_ref[...] * jnp.maximum(d_new, 0.0), axis=1)
    s_new = jnp.where(s_new == 0.0, 0.0, s_new)
    key_new = _sortable_key(s_new)
    keys = _sortable_key(s_ref[...])
    kpos = lax.broadcasted_iota(jnp.int32, (bsz, past), 1)

    def count(pred_past, pred_new):
        part = jnp.where(pred_past, 1.0, 0.0)
        acc = part[:, 0:LANES]
        for j in range(1, past // LANES):
            acc = acc + part[:, j * LANES:(j + 1) * LANES]
        return jnp.sum(acc, axis=-1, keepdims=True) + jnp.where(pred_new, 1.0, 0.0)

    def thr_bit(it, thr):
        cand = thr + lax.shift_left(jnp.int32(1), 31 - it)
        return jnp.where(count(keys >= cand, key_new >= cand) >= n_top, cand, thr)

    thr = lax.fori_loop(0, 32, thr_bit, jnp.full((bsz, 1), INT32_MIN, jnp.int32))
    need = n_top - count(keys > thr, key_new > thr)

    def idx_bit(it, j):
        cand = j + lax.shift_left(jnp.int32(1), 14 - it)
        cnt = count((keys == thr) & (kpos < cand), (key_new == thr) & (past < cand))
        return jnp.where(cnt <= need, cand, j)

    n_eq = count(keys == thr, key_new == thr)
    tie_end = lax.cond(jnp.max(jnp.where(n_eq > need, 1.0, 0.0)) > 0.0,
                       lambda: lax.fori_loop(0, 15, idx_bit, jnp.zeros((bsz, 1), jnp.int32)),
                       lambda: jnp.full((bsz, 1), 2 ** 15, jnp.int32))
    lane = lax.broadcasted_iota(jnp.int32, (bsz, LANES), 1)
    sel_ref[...] = jnp.where(lane == 0, thr, jnp.where(lane == 1, tie_end, 0))
    snew_ref[...] = jnp.broadcast_to(s_new, (bsz, LANES))


def _dsa_dec_attend_body(n_pages, pt_ref, rb_ref, q_ref, knew_ref, vnew_ref, sel_ref, snew_ref, s_ref, *refs):
    pp = DEC_PAGES_PER_STEP if n_pages >= DEC_PAGES_PER_STEP else n_pages
    k_pages, v_pages = refs[:pp], refs[pp:2 * pp]
    o_ref, m_ref, l_ref, acc_ref = refs[2 * pp:]
    f32, bf16 = jnp.float32, jnp.bfloat16
    t = pl.program_id(1)
    nt = pl.num_programs(1)
    past = n_pages * PAGE_SIZE

    @pl.when(t == 0)
    def _():
        m_ref[...] = jnp.full(m_ref.shape, NEG_BIG, f32)
        l_ref[...] = jnp.zeros(l_ref.shape, f32)
        acc_ref[...] = jnp.zeros(acc_ref.shape, f32)

    thr, tie_end = sel_ref[:, 0:1], sel_ref[:, 1:2]
    q = (q_ref[...].astype(f32) * HEAD_DIM ** -0.5).astype(bf16)
    lane12 = lax.broadcasted_iota(jnp.int32, (DSA_HEADS, LANES), 1)

    def head_bias(dist):
        bucket = t5_bucket(dist)
        b = jnp.zeros(dist.shape, f32)
        for h in range(DSA_HEADS):
            row = jnp.zeros(dist.shape, f32)
            for bk in range(N_BUCKETS):
                row = jnp.where(bucket == bk, rb_ref[bk, h] - rb_ref[N_BUCKETS - 1, h], row)
            b = jnp.where(lax.broadcasted_iota(jnp.int32, dist.shape, 0) == h, row, b)
        return b

    keys_step = _sortable_key(s_ref[pl.ds(pl.multiple_of(t * pp, pp), pp), :])
    tiles = []
    for j in range(pp):
        kpos = (t * pp + j) * PAGE_SIZE + lax.broadcasted_iota(jnp.int32, (1, LANES), 1)
        keys = keys_step[j:j + 1]
        sel = (keys > thr) | ((keys == thr) & (kpos < tie_end))
        logits = _dot_nt(q, k_pages[j][...].astype(bf16))
        if j == pp - 1:
            near = head_bias(past - (t * pp + j) * PAGE_SIZE - lane12)
            logits = logits + jnp.where(t == nt - 1, near, 0.0)
        tiles.append(jnp.where(sel, logits, NEG_BIG))
    m = m_ref[...]
    tile_max = tiles[0]
    for x in tiles[1:]:
        tile_max = jnp.maximum(tile_max, x)
    m_new = jnp.maximum(m, jnp.max(tile_max, axis=-1, keepdims=True))
    alpha = jnp.exp(m - m_new)
    ps = [jnp.exp(x - m_new) for x in tiles]
    p_sum = ps[0]
    for x in ps[1:]:
        p_sum = p_sum + x
    pv = jnp.dot(ps[0].astype(bf16), v_pages[0][...].astype(bf16), preferred_element_type=f32)
    for j in range(1, pp):
        pv = pv + jnp.dot(ps[j].astype(bf16), v_pages[j][...].astype(bf16), preferred_element_type=f32)
    l_ref[...] = alpha * l_ref[...] + jnp.sum(p_sum, axis=-1, keepdims=True)
    acc_ref[...] = alpha * acc_ref[...] + pv
    m_ref[...] = m_new

    @pl.when(t == nt - 1)
    def _():
        key_new = _sortable_key(snew_ref[:, 0:1])
        sel_new = (key_new > thr) | ((key_new == thr) & (past < tie_end))
        k_new = knew_ref[...].astype(bf16).astype(f32)
        logit_new = jnp.sum(q.astype(f32) * k_new, axis=-1, keepdims=True)
        logit_new = logit_new + head_bias(jnp.zeros((DSA_HEADS, LANES), jnp.int32))[:, :1]
        logit_new = jnp.where(sel_new, logit_new, NEG_BIG)
        m = m_ref[...]
        m_new = jnp.maximum(m, logit_new)
        alpha = jnp.exp(m - m_new)
        p_new = jnp.exp(logit_new - m_new)
        l = alpha * l_ref[...] + p_new
        v_new = vnew_ref[...].astype(bf16).astype(f32)
        acc = alpha * acc_ref[...] + p_new.astype(bf16).astype(f32) * v_new
        o = acc / l
        o_ref[...] = jnp.concatenate(
            [o[h:h + 1, (h // DSA_GQA) * HEAD_DIM:(h // DSA_GQA + 1) * HEAD_DIM] for h in range(DSA_HEADS)], axis=-1)


def dsa_decode_pallas(q, qi, kiwi, k_new, v_new, pool_k, pool_v, pool_ki, page_table, rel_bias):
    bsz = q.shape[0]
    n_pages = page_table.shape[1]
    n_phys = pool_k.shape[0]
    pp = DEC_PAGES_PER_STEP if n_pages >= DEC_PAGES_PER_STEP else n_pages
    nt = n_pages // pp
    n_top = min(TOPK_MAX, (n_pages * PAGE_SIZE + 1) // 4)
    f32, bf16 = jnp.float32, jnp.bfloat16
    eye = jnp.eye(DSA_KV_HEADS, dtype=q.dtype)
    q_bd = jnp.einsum('bkgd,kj->bkgjd', q.reshape(bsz, DSA_KV_HEADS, DSA_GQA, HEAD_DIM), eye)
    q_bd = q_bd.reshape(bsz, DSA_HEADS, KV_WIDTH)
    qi3 = qi.reshape(bsz, IDX_HEADS, IDX_DIM)
    wi3 = (kiwi[:, IDX_DIM:IDX_DIM + IDX_HEADS] * IDX_HEADS ** -0.5).reshape(bsz, IDX_HEADS, 1)
    ki_new = kiwi[:, :IDX_DIM].reshape(bsz, 1, IDX_DIM)
    pk, pv = pool_k, pool_v
    per_b = lambda *shape: pl.BlockSpec((None,) + shape, lambda b, t, pt: (b,) + (0,) * len(shape))
    page = lambda width, j: pl.BlockSpec((None, PAGE_SIZE, width), lambda b, t, pt: (pt[b, t * pp + j], 0, 0))
    scores = pl.pallas_call(
        _dsa_dec_score_body,
        grid_spec=pltpu.PrefetchScalarGridSpec(
            num_scalar_prefetch=1, grid=(bsz, nt),
            in_specs=[per_b(IDX_HEADS, IDX_DIM), per_b(IDX_HEADS, 1)] + [page(IDX_DIM, j) for j in range(pp)],
            out_specs=pl.BlockSpec((None, pp, LANES), lambda b, t, pt: (b, t, 0))),
        out_shape=jax.ShapeDtypeStruct((bsz, n_pages, LANES), f32),
        compiler_params=pltpu.CompilerParams(
            dimension_semantics=("parallel", "parallel"), vmem_limit_bytes=VMEM_LIMIT_BYTES),
        name="dsa_dec_score",
    )(page_table, qi3, wi3, *([pool_ki] * pp))
    sel, s_new = pl.pallas_call(
        functools.partial(_dsa_dec_select_body, n_top),
        grid=(1,),
        in_specs=[pl.BlockSpec((bsz, IDX_HEADS, IDX_DIM), lambda i: (0, 0, 0)),
                  pl.BlockSpec((bsz, IDX_HEADS, 1), lambda i: (0, 0, 0)),
                  pl.BlockSpec((bsz, 1, IDX_DIM), lambda i: (0, 0, 0)),
                  pl.BlockSpec((bsz, n_pages * PAGE_SIZE), lambda i: (0, 0))],
        out_specs=[pl.BlockSpec((bsz, LANES), lambda i: (0, 0))] * 2,
        out_shape=[jax.ShapeDtypeStruct((bsz, LANES), jnp.int32), jax.ShapeDtypeStruct((bsz, LANES), f32)],
        compiler_params=pltpu.CompilerParams(vmem_limit_bytes=VMEM_LIMIT_BYTES),
        name="dsa_dec_select",
    )(qi3, wi3, ki_new, scores.reshape(bsz, n_pages * PAGE_SIZE))
    o = pl.pallas_call(
        functools.partial(_dsa_dec_attend_body, n_pages),
        grid_spec=pltpu.PrefetchScalarGridSpec(
            num_scalar_prefetch=1, grid=(bsz, nt),
            in_specs=[pl.BlockSpec(memory_space=pltpu.SMEM), per_b(DSA_HEADS, KV_WIDTH), per_b(1, KV_WIDTH),
                      per_b(1, KV_WIDTH), per_b(1, LANES), per_b(1, LANES), per_b(n_pages, LANES)]
            + [page(KV_WIDTH, j) for j in range(pp)] * 2,
            out_specs=per_b(1, SEQ_WIDTH),
            scratch_shapes=[pltpu.VMEM((DSA_HEADS, 1), f32), pltpu.VMEM((DSA_HEADS, 1), f32),
                            pltpu.VMEM((DSA_HEADS, KV_WIDTH), f32)]),
        out_shape=jax.ShapeDtypeStruct((bsz, 1, SEQ_WIDTH), f32),
        compiler_params=pltpu.CompilerParams(
            dimension_semantics=("parallel", "arbitrary"), vmem_limit_bytes=VMEM_LIMIT_BYTES),
        name="dsa_dec_attend",
    )(page_table, rel_bias.astype(f32), q_bd, k_new.reshape(bsz, 1, KV_WIDTH), v_new.reshape(bsz, 1, KV_WIDTH),
      sel.reshape(bsz, 1, LANES), s_new.reshape(bsz, 1, LANES), scores, *([pk] * pp), *([pv] * pp))
    return o


CONV_HALO = 8


def _gdn_prep_body(x_ref, ab_ref, ctx_ref, w_ref, alog_ref, dtb_ref, q_ref, k_ref, v_ref, gb_ref, cs_ref, xp_ref):
    tt = x_ref.shape[0]
    halo = CONV_W - 1

    @pl.when(pl.program_id(1) == 0)
    def _():
        xp_ref[CONV_HALO - halo:CONV_HALO, :] = ctx_ref[...]

    x = x_ref[...]
    xp_ref[CONV_HALO:CONV_HALO + tt, :] = x
    w = w_ref[...]
    y = xp_ref[CONV_HALO - halo:CONV_HALO - halo + tt, :] * w[0:1]
    for j in range(1, CONV_W):
        y = y + xp_ref[CONV_HALO - halo + j:CONV_HALO - halo + j + tt, :] * w[j:j + 1]
    last = x[tt - halo:, :]
    xp_ref[CONV_HALO - halo:CONV_HALO, :] = last
    cs_ref[...] = last
    y = y * _sigmoid(y)

    def l2n(a, scale):
        cols = []
        for h in range(GDN_HEADS):
            s = a[:, h * HEAD_DIM:(h + 1) * HEAD_DIM]
            cols.append(s * (lax.rsqrt(jnp.sum(s * s, axis=-1, keepdims=True) + EPS) * scale))
        return jnp.concatenate(cols, axis=-1)

    q_ref[...] = l2n(y[:, :SEQ_WIDTH], HEAD_DIM ** -0.5)
    k_ref[...] = l2n(y[:, SEQ_WIDTH:2 * SEQ_WIDTH], 1.0)
    v_ref[...] = y[:, 2 * SEQ_WIDTH:]
    ab = ab_ref[...]
    xa = ab + dtb_ref[...]
    softplus = jnp.maximum(xa, 0.0) + jnp.log(1.0 + jnp.exp(-jnp.abs(xa)))
    g = -jnp.exp(alog_ref[...]) * softplus
    lane = lax.broadcasted_iota(jnp.int32, ab.shape, 1)
    gb_ref[...] = jnp.where(lane < GDN_HEADS, g, _sigmoid(ab))


def gdn_prep_pallas(qkv, ab, ctx, conv_w, a_log, dt_bias, *, tt=256):
    bsz, L, ch = qkv.shape
    tt = min(tt, L)
    f32 = jnp.float32
    pad = lambda r: jnp.pad(r.astype(f32).reshape(1, -1), ((0, 0), (0, LANES - r.shape[-1])))
    tok = lambda n: pl.BlockSpec((None, tt, n), lambda b, t: (b, t, 0))
    const2 = lambda b, t: (0, 0)
    return pl.pallas_call(
        _gdn_prep_body,
        grid=(bsz, L // tt),
        in_specs=[tok(ch), tok(LANES), pl.BlockSpec((None, CONV_W - 1, ch), lambda b, t: (b, 0, 0)),
                  pl.BlockSpec((CONV_W, ch), const2), pl.BlockSpec((1, LANES), const2), pl.BlockSpec((1, LANES), const2)],
        out_specs=[tok(SEQ_WIDTH), tok(SEQ_WIDTH), tok(SEQ_WIDTH), tok(LANES),
                   pl.BlockSpec((None, CONV_W - 1, ch), lambda b, t: (b, 0, 0))],
        out_shape=[jax.ShapeDtypeStruct((bsz, L, SEQ_WIDTH), f32)] * 3
        + [jax.ShapeDtypeStruct((bsz, L, LANES), f32), jax.ShapeDtypeStruct((bsz, CONV_W - 1, ch), f32)],
        scratch_shapes=[pltpu.VMEM((CONV_HALO + tt, ch), f32)],
        compiler_params=pltpu.CompilerParams(
            dimension_semantics=("parallel", "arbitrary"), vmem_limit_bytes=VMEM_LIMIT_BYTES),
        name="gdn_prep",
    )(qkv, ab, ctx.astype(f32), conv_w.astype(f32), pad(a_log), pad(dt_bias))


def _split3(a):
    bf16, f32 = jnp.bfloat16, jnp.float32
    h = a.astype(bf16)
    r = a - h.astype(f32)
    m = r.astype(bf16)
    return h, m, (r - m.astype(f32)).astype(bf16)


def _mm_hi(a, b):
    f32 = jnp.float32
    ah, am, _ = _split3(a)
    bh, bm, _ = _split3(b)
    d = lambda x, y: jnp.dot(x, y, preferred_element_type=f32)
    return d(ah, bh) + (d(ah, bm) + d(am, bh))


def _mm_sel(sel, b):
    f32 = jnp.float32
    s = sel.astype(jnp.bfloat16)
    bh, bm, bl = _split3(b)
    d = lambda y: jnp.dot(s, y, preferred_element_type=f32)
    return d(bh) + (d(bm) + d(bl))


def _dot_nt(a, b):
    return lax.dot_general(a, b, (((1,), (1,)), ((), ())), preferred_element_type=jnp.float32)


def _dot_tn(a, b):
    return lax.dot_general(a, b, (((0,), (0,)), ((), ())), preferred_element_type=jnp.float32)


def _gdn_local_body(q_ref, k_ref, v_ref, gb_ref, uv_ref, wk_ref, qh_ref, kt_ref, qk_ref):
    f32, bf16 = jnp.float32, jnp.bfloat16
    c = GDN_CHUNK
    r_i = lax.broadcasted_iota(jnp.int32, (c, c), 0)
    c_i = lax.broadcasted_iota(jnp.int32, (c, c), 1)
    tril = r_i >= c_i
    stril = r_i > c_i
    triu = r_i <= c_i
    eye = jnp.where(r_i == c_i, 1.0, 0.0)
    ones = jnp.ones((c, c), jnp.bool_)
    lane = lax.broadcasted_iota(jnp.int32, (c, LANES), 1)
    gb = gb_ref[...]
    heads = range(GDN_HEADS)
    sl = [slice(h * HEAD_DIM, (h + 1) * HEAD_DIM) for h in heads]
    pick = lambda idx: jnp.sum(jnp.where(lane == idx, gb, 0.0), axis=-1, keepdims=True)
    g_col = [pick(h) for h in heads]
    beta = [pick(GDN_HEADS + h) for h in heads]
    q = [q_ref[:, sl[h]] for h in heads]
    k = [k_ref[:, sl[h]] for h in heads]
    v = [v_ref[:, sl[h]] for h in heads]
    g_mat = [jnp.broadcast_to(g_col[h], (c, c)) for h in heads]
    cum_col = [_mm_sel(tril, g_mat[h]) for h in heads]
    cum_row = [_mm_sel(ones, jnp.where(triu, g_mat[h], 0.0)) for h in heads]
    gam = [jnp.where(tril, jnp.exp(jnp.where(tril, cum_col[h] - cum_row[h], 0.0)), 0.0) for h in heads]
    kb = [k[h].astype(bf16) for h in heads]
    a_mat = [jnp.where(stril, beta[h] * _dot_nt(kb[h], kb[h]) * gam[h], 0.0) for h in heads]
    t_inv = [eye - a_mat[h] for h in heads]
    pw = a_mat
    for _ in range(5):
        pw = [_mm_hi(pw[h], pw[h]) for h in heads]
        t_inv = [t_inv[h] + _mm_hi(t_inv[h], pw[h]) for h in heads]
    gc = [cum_col[h][:, :1] for h in heads]
    egc = [jnp.exp(gc[h]) for h in heads]
    tw = [_mm_hi(t_inv[h], jnp.concatenate([beta[h] * v[h], (beta[h] * egc[h]) * k[h]], axis=-1)) for h in heads]
    qk = [jnp.where(tril, _dot_nt(q[h].astype(bf16), kb[h]) * gam[h], 0.0) for h in heads]
    cat = lambda xs: jnp.concatenate(xs, axis=-1)
    uv_ref[...] = cat([tw[h][:, :HEAD_DIM] for h in heads])
    wk_ref[...] = cat([tw[h][:, HEAD_DIM:] for h in heads]).astype(bf16)
    qh_ref[...] = cat([egc[h] * q[h] for h in heads]).astype(bf16)
    eye_b = eye.astype(bf16)
    kt = [(jnp.exp(gc[h][c - 1:c, :] - gc[h]) * k[h]).astype(bf16) for h in heads]
    kt_ref[...] = cat([_dot_tn(kt[h], eye_b) for h in heads]).astype(bf16)
    qk_ref[...] = cat(qk).astype(bf16)


def gdn_local_pallas(qn, kn, vv, gb):
    bsz, L, w = qn.shape
    c = GDN_CHUNK
    f32, bf16 = jnp.float32, jnp.bfloat16
    tok = lambda n: pl.BlockSpec((None, c, n), lambda b, t: (b, t, 0))
    return pl.pallas_call(
        _gdn_local_body,
        grid=(bsz, L // c),
        in_specs=[tok(w), tok(w), tok(w), tok(LANES)],
        out_specs=[tok(w)] * 5,
        out_shape=[jax.ShapeDtypeStruct((bsz, L, w), f32)] + [jax.ShapeDtypeStruct((bsz, L, w), bf16)] * 4,
        compiler_params=pltpu.CompilerParams(
            dimension_semantics=("parallel", "parallel"), vmem_limit_bytes=VMEM_LIMIT_BYTES),
        name="gdn_local",
    )(qn, kn, vv, gb)


def _gdn_scan_body(uv_ref, wk_ref, qh_ref, ktt_ref, qk_ref, gb_ref, gate_ref, s0_ref, on_ref, o_ref, sf_ref, s_ref):
    f32, bf16 = jnp.float32, jnp.bfloat16
    c = GDN_CHUNK
    tt = uv_ref.shape[0]
    heads = range(GDN_HEADS)
    sl = [slice(h * HEAD_DIM, (h + 1) * HEAD_DIM) for h in heads]

    @pl.when(pl.program_id(1) == 0)
    def _():
        s_ref[...] = s0_ref[...]

    lane = lax.broadcasted_iota(jnp.int32, (1, LANES), 1)
    dot = lambda a, b: jnp.dot(a, b, preferred_element_type=f32)

    def chunk(ci, _):
        rows = pl.ds(pl.multiple_of(ci * c, c), c)
        eg_last = jnp.exp(jnp.sum(gb_ref[rows, :], axis=0, keepdims=True))
        eg = [jnp.sum(jnp.where(lane == h, eg_last, 0.0), axis=-1, keepdims=True) for h in heads]
        st = [s_ref[h] for h in heads]
        sb = [st[h].astype(bf16) for h in heads]
        u = [uv_ref[rows, sl[h]] - dot(wk_ref[rows, sl[h]], sb[h]) for h in heads]
        ub = [u[h].astype(bf16) for h in heads]
        new = [eg[h] * st[h] + dot(ktt_ref[rows, sl[h]], ub[h]) for h in heads]
        for h in heads:
            s_ref[h] = new[h]
        o = [dot(qh_ref[rows, sl[h]], sb[h]) + dot(qk_ref[rows, sl[h]], ub[h]) for h in heads]
        o = [o[h] * lax.rsqrt(jnp.mean(o[h] * o[h], axis=-1, keepdims=True) + EPS) for h in heads]
        gt = gate_ref[rows, :]
        o_ref[rows, :] = (jnp.concatenate(o, axis=-1) * on_ref[...]) * (gt * _sigmoid(gt))
        return 0

    lax.fori_loop(0, tt // c, chunk, 0)
    sf_ref[...] = s_ref[...]


def gdn_scan_pallas(uv, wk, qh, kt, qk, gb, gate, s0, o_norm, *, tt=512):
    bsz, L, w = uv.shape
    tt = min(tt, L)
    f32 = jnp.float32
    tok = lambda n: pl.BlockSpec((None, tt, n), lambda b, t: (b, t, 0))
    st = pl.BlockSpec((None, GDN_HEADS, HEAD_DIM, HEAD_DIM), lambda b, t: (b, 0, 0, 0))
    on = jnp.tile(o_norm.astype(f32), GDN_HEADS).reshape(1, w)
    return pl.pallas_call(
        _gdn_scan_body,
        grid=(bsz, L // tt),
        in_specs=[tok(w)] * 5 + [tok(LANES), tok(w), st, pl.BlockSpec((1, w), lambda b, t: (0, 0))],
        out_specs=[tok(w), st],
        out_shape=[jax.ShapeDtypeStruct((bsz, L, w), f32), jax.ShapeDtypeStruct(s0.shape, f32)],
        scratch_shapes=[pltpu.VMEM((GDN_HEADS, HEAD_DIM, HEAD_DIM), f32)],
        compiler_params=pltpu.CompilerParams(
            dimension_semantics=("parallel", "arbitrary"), vmem_limit_bytes=VMEM_LIMIT_BYTES),
        name="gdn_scan",
    )(uv, wk, qh, kt, qk, gb, gate, s0.astype(f32).swapaxes(2, 3), on)


def gdn_mix_pallas(qkv, ab, gate, conv_ctx, s0, conv_w, a_log, dt_bias, o_norm):
    qn, kn, vv, gb, conv_state = gdn_prep_pallas(qkv, ab, conv_ctx, conv_w, a_log, dt_bias)
    uv, wk, qh, kt, qk = gdn_local_pallas(qn, kn, vv, gb)
    o, s_fin_t = gdn_scan_pallas(uv, wk, qh, kt, qk, gb, gate, s0, o_norm)
    return o, conv_state.astype(qkv.dtype), s_fin_t.swapaxes(2, 3).astype(s0.dtype)


def _gdn_dec_body(x_ref, ab_ref, gate_ref, ctx_ref, s_ref, w_ref, alog_ref, dtb_ref, on_ref, o_ref, cs_ref, sn_ref):
    f32, bf16 = jnp.float32, jnp.bfloat16
    rnd = lambda a: a.astype(bf16).astype(f32)
    x, ctx, w = x_ref[...], ctx_ref[...], w_ref[...]
    halo = CONV_W - 1
    y = x * w[halo:halo + 1]
    for j in range(halo):
        y = y + ctx[j:j + 1] * w[j:j + 1]
    y = y * _sigmoid(y)
    cs_ref[...] = jnp.concatenate([ctx[1:], x], axis=0)
    ab = ab_ref[...]
    xa = ab + dtb_ref[...]
    g_all = -jnp.exp(alog_ref[...]) * (jnp.maximum(xa, 0.0) + jnp.log(1.0 + jnp.exp(-jnp.abs(xa))))
    beta_all = _sigmoid(ab)
    lane = lax.broadcasted_iota(jnp.int32, (1, LANES), 1)
    pick = lambda a, idx: jnp.sum(jnp.where(lane == idx, a, 0.0), axis=-1, keepdims=True)
    r_i = lax.broadcasted_iota(jnp.int32, (HEAD_DIM, HEAD_DIM), 0)
    c_i = lax.broadcasted_iota(jnp.int32, (HEAD_DIM, HEAD_DIM), 1)
    eye = r_i == c_i
    to_col = lambda row: jnp.sum(jnp.where(eye, jnp.broadcast_to(row, (HEAD_DIM, HEAD_DIM)), 0.0), axis=-1, keepdims=True)
    to_row = lambda col: jnp.sum(jnp.where(eye, jnp.broadcast_to(col, (HEAD_DIM, HEAD_DIM)), 0.0), axis=0, keepdims=True)
    outs = []
    for h in range(GDN_HEADS):
        sl = slice(h * HEAD_DIM, (h + 1) * HEAD_DIM)
        q, k, v = y[:, sl], y[:, SEQ_WIDTH + h * HEAD_DIM:SEQ_WIDTH + (h + 1) * HEAD_DIM], y[:, 2 * SEQ_WIDTH + h * HEAD_DIM:2 * SEQ_WIDTH + (h + 1) * HEAD_DIM]
        q = q * (lax.rsqrt(jnp.sum(q * q, axis=-1, keepdims=True) + EPS) * HEAD_DIM ** -0.5)
        k = k * lax.rsqrt(jnp.sum(k * k, axis=-1, keepdims=True) + EPS)
        eg = jnp.exp(pick(g_all, h))
        beta = pick(beta_all, GDN_HEADS + h)
        s = s_ref[h]
        sb = rnd(s)
        s_wk = jnp.sum(sb * rnd((beta * eg) * k), axis=-1, keepdims=True)
        u = to_col(beta * v) - s_wk
        ub = rnd(u)
        s_q = jnp.sum(sb * rnd(eg * q), axis=-1, keepdims=True)
        qk = jnp.sum(rnd(q) * rnd(k), axis=-1, keepdims=True)
        o = to_row(s_q + rnd(qk) * ub)
        sn_ref[h] = eg * s + ub * rnd(k)
        outs.append(o * lax.rsqrt(jnp.mean(o * o, axis=-1, keepdims=True) + EPS))
    gt = gate_ref[...]
    o_ref[...] = (jnp.concatenate(outs, axis=-1) * on_ref[...]) * (gt * _sigmoid(gt))


def gdn_decode_pallas(qkv, ab, gate, conv_ctx, s0, conv_w, a_log, dt_bias, o_norm):
    bsz = qkv.shape[0]
    f32 = jnp.float32
    pad = lambda r: jnp.pad(r.astype(f32).reshape(1, -1), ((0, 0), (0, LANES - r.shape[-1])))
    per_b = lambda *shape: pl.BlockSpec((None,) + shape, lambda b: (b,) + (0,) * len(shape))
    const = lambda *shape: pl.BlockSpec(shape, lambda b: (0,) * len(shape))
    st = (GDN_HEADS, HEAD_DIM, HEAD_DIM)
    o, conv_state, s_new = pl.pallas_call(
        _gdn_dec_body,
        grid=(bsz,),
        in_specs=[per_b(1, CONV_CH), per_b(1, LANES), per_b(1, SEQ_WIDTH), per_b(CONV_W - 1, CONV_CH), per_b(*st),
                  const(CONV_W, CONV_CH), const(1, LANES), const(1, LANES), const(1, SEQ_WIDTH)],
        out_specs=[per_b(1, SEQ_WIDTH), per_b(CONV_W - 1, CONV_CH), per_b(*st)],
        out_shape=[jax.ShapeDtypeStruct((bsz, 1, SEQ_WIDTH), f32), jax.ShapeDtypeStruct((bsz, CONV_W - 1, CONV_CH), f32),
                   jax.ShapeDtypeStruct((bsz,) + st, f32)],
        compiler_params=pltpu.CompilerParams(dimension_semantics=("parallel",), vmem_limit_bytes=VMEM_LIMIT_BYTES),
        name="gdn_decode",
    )(qkv, ab, gate, conv_ctx.astype(f32), s0.astype(f32), conv_w.astype(f32), pad(a_log), pad(dt_bias),
      jnp.tile(o_norm.astype(f32), GDN_HEADS).reshape(1, SEQ_WIDTH))
    return o, conv_state.astype(qkv.dtype), s_new.astype(s0.dtype)


def _pad_cols(w, n):
    return jnp.pad(w, ((0, 0), (0, n - w.shape[1])))


def _in_proj_layout(kind, w_in):
    f32, bf16 = jnp.float32, jnp.bfloat16
    if kind == 0:
        return w_in, ((0, MEM_WIDTH), (MEM_WIDTH, SEQ_WIDTH)), (bf16, f32)
    if kind == 1:
        w = _pad_cols(w_in, MEM_WIDTH + sum(DSA_SPLITS[:4]) + LANES)
        widths = (MEM_WIDTH,) + DSA_SPLITS[:4] + (LANES,)
        starts = np.cumsum((0,) + widths[:-1]).tolist()
        return w, tuple(zip(starts, widths)), (bf16, bf16, f32, f32, bf16, f32)
    c0, c1 = MEM_WIDTH + CONV_CH, MEM_WIDTH + CONV_CH + 2 * GDN_HEADS
    w = jnp.concatenate([w_in[:, :c0], w_in[:, c1:], _pad_cols(w_in[:, c0:c1], LANES)], axis=1)
    widths = (MEM_WIDTH, CONV_CH, SEQ_WIDTH, LANES)
    starts = np.cumsum((0,) + widths[:-1]).tolist()
    return w, tuple(zip(starts, widths)), (bf16, f32, f32, f32)


def kernel(x_prompt, x_sample, cache_mem_k, cache_mem_v, state_ssm_re, state_ssm_im, cache_k, cache_v, cache_kidx, state_conv, state_delta, page_table, mem_prompt, norm_g, final_norm, w_in_a, w_in_b, w_in_c, w_out, w_mem_kv, ffn1_gate, ffn1_up, ffn1_down, ffn2_gate, ffn2_up, ffn2_down, s5_lam_re, s5_lam_im, s5_log_step, s5_b_re, s5_b_im, s5_c_re, s5_c_im, s5_d, s5_w_glu, s5_b_glu, rel_bias, gdn_conv_w, gdn_a_log, gdn_dt_bias, gdn_o_norm):
    depth = norm_g.shape[0]
    bp, lp, d = x_prompt.shape
    bs, ls, _ = x_sample.shape
    assert ls == 1, "the decode-step kernels handle one new token per sample"
    f32, bf16 = jnp.float32, jnp.bfloat16
    hp, hs = x_prompt.reshape(bp * lp, d), x_sample.reshape(bs * ls, d)
    mem_rows = mem_prompt.reshape(bp * N_MEM, d)
    page_view = lambda pool: pool.reshape(pool.shape[0], PAGE_SIZE, KV_WIDTH)
    pools = [(page_view(cache_k[j]), page_view(cache_v[j])) for j in range(cache_k.shape[0])]
    mem_k_p, mem_v_p = [], []
    ssm_re_p, ssm_im_p, ssm_re_s, ssm_im_s = [], [], [], []
    k_p, v_p, ki_p, k_s, v_s, ki_s = [], [], [], [], [], []
    conv_p, delta_p, conv_s, delta_s = [], [], [], []
    w_in_by_kind = (w_in_a, w_in_b, w_in_c)
    for i in range(depth):
        kind, j = i % N_MIXERS, i // N_MIXERS
        last = i == depth - 1
        ffn1 = (ffn1_gate[i].astype(bf16), ffn1_up[i].astype(bf16), ffn1_down[i].astype(bf16))
        ffn2 = (ffn2_gate[i].astype(bf16), ffn2_up[i].astype(bf16), ffn2_down[i].astype(bf16))
        w_in, segments, dtypes = _in_proj_layout(kind, w_in_by_kind[kind][j])
        w_in = w_in.astype(bf16)
        hp = ffn_residual(hp, norm_g[i, 0], *ffn1)
        hs = ffn_residual(hs, norm_g[i, 0], *ffn1)
        zp = [a.reshape(bp, lp, -1) for a in proj_in(hp, norm_g[i, 1], w_in, segments, dtypes)]
        zs = [a.reshape(bs, ls, -1) for a in proj_in(hs, norm_g[i, 1], w_in, segments, dtypes)]
        mk, mv = proj_in(mem_rows, norm_g[i, 1], w_mem_kv[i].astype(bf16), ((0, MEM_WIDTH), (MEM_WIDTH, MEM_WIDTH)),
                         (f32, f32), normalize=False)
        mem_shape = (bp, N_MEM, MEM_HEADS, HEAD_DIM)
        mk, mv = mk.reshape(mem_shape), mv.reshape(mem_shape)
        mem_k_p.append(mk)
        mem_v_p.append(mv)
        if kind == 0:
            disc = s5_discretize(s5_lam_re[j], s5_lam_im[j], s5_log_step[j], s5_b_re[j], s5_b_im[j],
                                 s5_c_re[j], s5_c_im[j])
            gate = (s5_d[j], s5_w_glu[j], s5_b_glu[j])
            h0 = jnp.zeros((bp, S5_GROUPS, S5_STATE), state_ssm_re.dtype)
            op, hr, hi = s5_mix_pallas(zp[1], h0, h0, disc, *gate)
            ssm_re_p.append(hr)
            ssm_im_p.append(hi)
            osm, hr, hi = s5_mix_pallas(zs[1], state_ssm_re[j], state_ssm_im[j], disc, *gate)
            ssm_re_s.append(hr)
            ssm_im_s.append(hi)
        elif kind == 1:
            _, q, k, v, qi, kiwi = zp
            op = dsa_prompt_pallas(q, qi, kiwi, k, v, rel_bias)
            kv_shape = (DSA_KV_HEADS, HEAD_DIM)
            k_p.append(k.reshape((bp, lp) + kv_shape))
            v_p.append(v.reshape((bp, lp) + kv_shape))
            ki_p.append(kiwi[..., :IDX_DIM])
            _, q, k, v, qi, kiwi = zs
            osm = dsa_decode_pallas(q[:, 0], qi[:, 0], kiwi[:, 0], k[:, 0], v[:, 0], pools[j][0], pools[j][1],
                                    cache_kidx[j], page_table, rel_bias)
            k_s.append(k.reshape((bs, ls) + kv_shape))
            v_s.append(v.reshape((bs, ls) + kv_shape))
            ki_s.append(kiwi[..., :IDX_DIM])
        else:
            gdn = (gdn_conv_w[j], gdn_a_log[j], gdn_dt_bias[j], gdn_o_norm[j])
            ctx0 = jnp.zeros((bp, CONV_W - 1, CONV_CH), f32)
            s00 = jnp.zeros((bp, GDN_HEADS, HEAD_DIM, HEAD_DIM), state_delta.dtype)
            _, qkv, gate, ab = zp
            op, cst, sst = gdn_mix_pallas(qkv, ab, gate, ctx0, s00, *gdn)
            conv_p.append(cst)
            delta_p.append(sst)
            _, qkv, gate, ab = zs
            osm, cst, sst = gdn_decode_pallas(qkv, ab, gate, state_conv[j], state_delta[j], *gdn)
            conv_s.append(cst)
            delta_s.append(sst)
        w_o = w_out[i].astype(bf16)
        hp = mix_out(hp.reshape(bp, lp, d), zp[0], op, mk, mv, w_o).reshape(bp * lp, d)
        hs = mix_out(hs.reshape(bs, ls, d), zs[0], osm, cache_mem_k[i], cache_mem_v[i], w_o).reshape(bs * ls, d)
        fg = final_norm if last else None
        hp = ffn_residual(hp, norm_g[i, 2], *ffn2, fg)
        hs = ffn_residual(hs, norm_g[i, 2], *ffn2, fg)
    st = jnp.stack
    return (hp.reshape(bp, lp, d), hs.reshape(bs, ls, d), st(mem_k_p), st(mem_v_p),
            st(ssm_re_p), st(ssm_im_p), st(ssm_re_s), st(ssm_im_s),
            st(k_p), st(v_p), st(ki_p), st(k_s), st(v_s), st(ki_s),
            st(conv_p), st(delta_p), st(conv_s), st(delta_s))
```

```python
import math
import functools
import jax
import jax.numpy as jnp
from jax import lax
import numpy as np
from jax.experimental import pallas as pl
from jax.experimental.pallas import tpu as pltpu

D_MODEL = 1024
N_MIXERS = 3
HEAD_DIM = 64
MIX_WIDTH = D_MODEL
N_MEM = 256
MEM_HEADS = 4
MEM_WIDTH = MEM_HEADS * HEAD_DIM
SEQ_WIDTH = MIX_WIDTH - MEM_WIDTH
S5_GROUP = 16
S5_GROUPS = SEQ_WIDTH // S5_GROUP
S5_STATE = 64
DSA_HEADS = SEQ_WIDTH // HEAD_DIM
DSA_KV_HEADS = 4
DSA_GQA = DSA_HEADS // DSA_KV_HEADS
IDX_HEADS = 8
IDX_DIM = 64
TOPK_MAX = 256
QBLOCK = 128
N_BUCKETS = 32
MAX_DISTANCE = 128
GDN_HEADS = SEQ_WIDTH // HEAD_DIM
CONV_W = 4
CONV_CH = 3 * SEQ_WIDTH
GDN_CHUNK = 64
D_FF = 2816
EPS = 1e-6
PAGE_SIZE = 128
DSA_SPLITS = (DSA_HEADS * HEAD_DIM, DSA_KV_HEADS * HEAD_DIM, DSA_KV_HEADS * HEAD_DIM, IDX_HEADS * IDX_DIM, IDX_DIM, IDX_HEADS)
GDN_SPLITS = (CONV_CH, GDN_HEADS, GDN_HEADS, SEQ_WIDTH)

LANES = 128
VMEM_LIMIT_BYTES = 56 * 1024 * 1024


def _sigmoid(x):
    return 1.0 / (1.0 + jnp.exp(-x))


def t5_bucket(dist):
    max_exact = N_BUCKETS // 2
    n = jnp.maximum(dist, 0)
    nf = jnp.maximum(n, 1).astype(jnp.float32)
    large = max_exact + (jnp.log(nf / max_exact) / math.log(MAX_DISTANCE / max_exact)
                         * (N_BUCKETS - max_exact)).astype(jnp.int32)
    return jnp.where(n < max_exact, n, jnp.minimum(large, N_BUCKETS - 1))


FFN_CHUNK = 256


def _ffn_body(final_norm, x_ref, g_ref, wg_ref, wu_ref, wd_ref, gf_ref, o_ref):
    f32, bf16 = jnp.float32, jnp.bfloat16
    x = x_ref[...]
    n = ((x * lax.rsqrt(jnp.mean(x * x, axis=-1, keepdims=True) + EPS)) * g_ref[...]).astype(bf16)
    acc = jnp.zeros(x.shape, f32)
    for c in range(wg_ref.shape[1] // FFN_CHUNK):
        sl = slice(c * FFN_CHUNK, (c + 1) * FFN_CHUNK)
        a = jnp.dot(n, wg_ref[:, sl], preferred_element_type=f32)
        b = jnp.dot(n, wu_ref[:, sl], preferred_element_type=f32)
        h = (a * _sigmoid(a)) * b
        acc = acc + jnp.dot(h.astype(bf16), wd_ref[sl, :], preferred_element_type=f32)
    y = x + 0.5 * acc
    if final_norm:
        y = (y * lax.rsqrt(jnp.mean(y * y, axis=-1, keepdims=True) + EPS)) * gf_ref[...]
    o_ref[...] = y


def ffn_residual(x, g, wg, wu, wd, final_g=None, *, tm=1024):
    t, d = x.shape
    f = wg.shape[1]
    tm = min(tm, t)
    gf = jnp.ones((d,), jnp.float32) if final_g is None else final_g
    resident = lambda shape: pl.BlockSpec(shape, lambda i: (0, 0), pipeline_mode=pl.Buffered(1))
    return pl.pallas_call(
        functools.partial(_ffn_body, final_g is not None),
        grid=(t // tm,),
        in_specs=[pl.BlockSpec((tm, d), lambda i: (i, 0)), resident((1, d)),
                  resident((d, f)), resident((d, f)), resident((f, d)), resident((1, d))],
        out_specs=pl.BlockSpec((tm, d), lambda i: (i, 0)),
        out_shape=jax.ShapeDtypeStruct((t, d), jnp.float32),
        compiler_params=pltpu.CompilerParams(dimension_semantics=("parallel",), vmem_limit_bytes=VMEM_LIMIT_BYTES),
        name="ffn_residual",
    )(x, g.reshape(1, d).astype(jnp.float32), wg, wu, wd, gf.reshape(1, d).astype(jnp.float32))


def _proj_in_body(segments, normalize, x_ref, g_ref, w_ref, *o_refs):
    x = x_ref[...]
    if normalize:
        r = lax.rsqrt(jnp.mean(x * x, axis=-1, keepdims=True) + EPS)
        x = (x * r) * g_ref[...]
    n = x.astype(jnp.bfloat16)
    for (start, width), o_ref in zip(segments, o_refs):
        o_ref[...] = jnp.dot(n, w_ref[:, start:start + width], preferred_element_type=jnp.float32).astype(o_ref.dtype)


def proj_in(x, g, w, segments, dtypes, *, normalize=True, tm=512):
    t, d = x.shape
    tm = min(tm, t)
    return pl.pallas_call(
        functools.partial(_proj_in_body, tuple(segments), normalize),
        grid=(t // tm,),
        in_specs=[pl.BlockSpec((tm, d), lambda i: (i, 0)), pl.BlockSpec((1, d), lambda i: (0, 0)),
                  pl.BlockSpec(w.shape, lambda i: (0, 0))],
        out_specs=[pl.BlockSpec((tm, width), lambda i: (i, 0)) for _, width in segments],
        out_shape=[jax.ShapeDtypeStruct((t, width), dt) for (_, width), dt in zip(segments, dtypes)],
        compiler_params=pltpu.CompilerParams(dimension_semantics=("parallel",), vmem_limit_bytes=VMEM_LIMIT_BYTES),
        name="proj_in",
    )(x, g.reshape(1, d).astype(jnp.float32), w)


def _mix_out_body(x_ref, cq_ref, om_ref, mkt_ref, mv_ref, w_ref, o_ref):
    f32, bf16 = jnp.float32, jnp.bfloat16
    cq = cq_ref[...]
    heads = []
    for h in range(MEM_HEADS):
        sl = slice(h * HEAD_DIM, (h + 1) * HEAD_DIM)
        logits = jnp.dot(cq[:, sl], mkt_ref[sl, :], preferred_element_type=f32) * HEAD_DIM ** -0.5
        p = jnp.exp(logits - jnp.max(logits, axis=-1, keepdims=True))
        p = p / jnp.sum(p, axis=-1, keepdims=True)
        heads.append(jnp.dot(p.astype(bf16), mv_ref[:, sl], preferred_element_type=f32))
    o_mem = jnp.concatenate(heads, axis=-1).astype(bf16)
    y = (jnp.dot(o_mem, w_ref[:MEM_WIDTH, :], preferred_element_type=f32)
         + jnp.dot(om_ref[...].astype(bf16), w_ref[MEM_WIDTH:, :], preferred_element_type=f32))
    o_ref[...] = x_ref[...] + y


def mix_out(x, cq, o_mix, mk, mv, w_out, *, tm=512):
    bsz, L, d = x.shape
    tm = min(tm, L)
    bf16 = jnp.bfloat16
    mkt = mk.astype(bf16).reshape(bsz, N_MEM, MEM_WIDTH).swapaxes(1, 2)
    mvf = mv.astype(bf16).reshape(bsz, N_MEM, MEM_WIDTH)
    tok = lambda n: pl.BlockSpec((None, tm, n), lambda b, t: (b, t, 0))
    per_b = lambda r, c: pl.BlockSpec((None, r, c), lambda b, t: (b, 0, 0))
    return pl.pallas_call(
        _mix_out_body,
        grid=(bsz, L // tm),
        in_specs=[tok(d), tok(MEM_WIDTH), tok(SEQ_WIDTH), per_b(MEM_WIDTH, N_MEM), per_b(N_MEM, MEM_WIDTH),
                  pl.BlockSpec((d, d), lambda b, t: (0, 0))],
        out_specs=tok(d),
        out_shape=jax.ShapeDtypeStruct((bsz, L, d), jnp.float32),
        compiler_params=pltpu.CompilerParams(
            dimension_semantics=("parallel", "parallel"), vmem_limit_bytes=VMEM_LIMIT_BYTES),
        name="mix_out",
    )(x, cq, o_mix, mkt, mvf, w_out)


S5_LANES = S5_GROUPS * S5_STATE
S5_GROUPS_PER_BLOCK = LANES // S5_GROUP
S5_BLOCKS = SEQ_WIDTH // LANES
S5_BLOCK_STATES = S5_GROUPS_PER_BLOCK * S5_STATE


def s5_discretize(lam_re, lam_im, log_step, b_re, b_im, c_re, c_im):
    f32, bf16 = jnp.float32, jnp.bfloat16
    lr, li = lam_re.astype(f32), lam_im.astype(f32)
    step = jnp.exp(log_step.astype(f32))[:, None]
    mag = jnp.exp(lr * step)
    ab_re, ab_im = mag * jnp.cos(li * step), mag * jnp.sin(li * step)
    den = lr * lr + li * li
    nr, ni = ab_re - 1.0, ab_im
    f_re = (nr * lr + ni * li) / den
    f_im = (ni * lr - nr * li) / den
    br, bi = b_re.astype(f32), b_im.astype(f32)
    bb_re = f_re[..., None] * br - f_im[..., None] * bi
    bb_im = f_re[..., None] * bi + f_im[..., None] * br
    eye = jnp.eye(S5_GROUPS_PER_BLOCK, dtype=f32)
    nb, gb = S5_BLOCKS, S5_GROUPS_PER_BLOCK

    def in_blocks(bb):
        w = jnp.einsum('jgpc,gh->jgchp', bb.reshape(nb, gb, S5_STATE, S5_GROUP), eye)
        return w.reshape(nb, LANES, S5_BLOCK_STATES).astype(bf16)

    def out_blocks(c):
        w = jnp.einsum('jgop,gh->jgpho', c.astype(f32).reshape(nb, gb, S5_GROUP, S5_STATE), eye)
        return w.reshape(nb, S5_BLOCK_STATES, LANES).astype(bf16)

    return (ab_re.reshape(1, S5_LANES), ab_im.reshape(1, S5_LANES),
            in_blocks(bb_re), in_blocks(bb_im), out_blocks(c_re), out_blocks(-c_im))


def _s5_project_in(u, wbr_ref, wbi_ref, bur_ref, bui_ref):
    ub = u.astype(jnp.bfloat16)
    for j in range(S5_BLOCKS):
        uj = ub[:, j * LANES:(j + 1) * LANES]
        sl = slice(j * S5_BLOCK_STATES, (j + 1) * S5_BLOCK_STATES)
        bur_ref[:, sl] = jnp.dot(uj, wbr_ref[j], preferred_element_type=jnp.float32)
        bui_ref[:, sl] = jnp.dot(uj, wbi_ref[j], preferred_element_type=jnp.float32)


def _s5_project_out(hr_ref, hi_ref, wcr_ref, wci_ref):
    cols = []
    for j in range(S5_BLOCKS):
        sl = slice(j * S5_BLOCK_STATES, (j + 1) * S5_BLOCK_STATES)
        cols.append(jnp.dot(hr_ref[:, sl].astype(jnp.bfloat16), wcr_ref[j], preferred_element_type=jnp.float32)
                    + jnp.dot(hi_ref[:, sl].astype(jnp.bfloat16), wci_ref[j], preferred_element_type=jnp.float32))
    return jnp.concatenate(cols, axis=-1)


def _s5_gate(y_ssm, u, d_ref, wglu_ref, bglu_ref):
    y = y_ssm + d_ref[...] * u
    y = 0.5 * y * (1.0 + jnp.tanh(math.sqrt(2.0 / math.pi) * (y + 0.044715 * (y * y * y))))
    z = jnp.dot(y.astype(jnp.bfloat16), wglu_ref[...], preferred_element_type=jnp.float32) + bglu_ref[...]
    return y * (1.0 / (1.0 + jnp.exp(-z)))


def _s5_seq_body(u_ref, h0r_ref, h0i_ref, ar_ref, ai_ref, wbr_ref, wbi_ref, wcr_ref, wci_ref, d_ref, wglu_ref, bglu_ref,
                 y_ref, hfr_ref, hfi_ref, bur_ref, bui_ref, cr_ref, ci_ref):
    tt = u_ref.shape[0]

    @pl.when(pl.program_id(1) == 0)
    def _():
        cr_ref[...] = h0r_ref[...]
        ci_ref[...] = h0i_ref[...]

    u = u_ref[...]
    _s5_project_in(u, wbr_ref, wbi_ref, bur_ref, bui_ref)
    ar, ai = ar_ref[...], ai_ref[...]

    def step(t, carry):
        hr, hi = carry
        nhr = ar * hr - ai * hi + bur_ref[pl.ds(t, 1), :]
        nhi = ar * hi + ai * hr + bui_ref[pl.ds(t, 1), :]
        bur_ref[pl.ds(t, 1), :] = nhr
        bui_ref[pl.ds(t, 1), :] = nhi
        return nhr, nhi

    hr, hi = lax.fori_loop(0, tt, step, (cr_ref[...], ci_ref[...]))
    cr_ref[...] = hr
    ci_ref[...] = hi
    hfr_ref[...] = hr
    hfi_ref[...] = hi
    y_ref[...] = _s5_gate(_s5_project_out(bur_ref, bui_ref, wcr_ref, wci_ref), u, d_ref, wglu_ref, bglu_ref)


def _s5_step_body(u_ref, h0r_ref, h0i_ref, ar_ref, ai_ref, wbr_ref, wbi_ref, wcr_ref, wci_ref, d_ref, wglu_ref, bglu_ref,
                  y_ref, hfr_ref, hfi_ref, bur_ref, bui_ref):
    u = u_ref[...]
    _s5_project_in(u, wbr_ref, wbi_ref, bur_ref, bui_ref)
    ar, ai = ar_ref[...], ai_ref[...]
    hr, hi = h0r_ref[...], h0i_ref[...]
    nhr = ar * hr - ai * hi + bur_ref[...]
    nhi = ar * hi + ai * hr + bui_ref[...]
    bur_ref[...] = nhr
    bui_ref[...] = nhi
    hfr_ref[...] = nhr
    hfi_ref[...] = nhi
    y_ref[...] = _s5_gate(_s5_project_out(bur_ref, bui_ref, wcr_ref, wci_ref), u, d_ref, wglu_ref, bglu_ref)


def s5_mix_pallas(u, h0_re, h0_im, disc, d_skip, w_glu, b_glu, *, tt=512):
    bsz, L, w = u.shape
    f32 = jnp.float32
    ar, ai, wbr, wbi, wcr, wci = disc
    d2, bg2, wg = d_skip.reshape(1, w).astype(f32), b_glu.reshape(1, w).astype(f32), w_glu.astype(jnp.bfloat16)
    const2 = lambda *_: (0, 0)
    const3 = lambda *_: (0, 0, 0)
    w_specs = [pl.BlockSpec((1, S5_LANES), const2), pl.BlockSpec((1, S5_LANES), const2),
               pl.BlockSpec(wbr.shape, const3), pl.BlockSpec(wbi.shape, const3),
               pl.BlockSpec(wcr.shape, const3), pl.BlockSpec(wci.shape, const3),
               pl.BlockSpec((1, w), const2), pl.BlockSpec((w, w), const2), pl.BlockSpec((1, w), const2)]
    w_args = (ar, ai, wbr, wbi, wcr, wci, d2, wg, bg2)
    if L == 1:
        rows = bsz
        h0r, h0i = h0_re.reshape(rows, S5_LANES).astype(f32), h0_im.reshape(rows, S5_LANES).astype(f32)
        row_spec = lambda n: pl.BlockSpec((rows, n), const2)
        y, hr, hi = pl.pallas_call(
            _s5_step_body,
            grid=(1,),
            in_specs=[row_spec(w), row_spec(S5_LANES), row_spec(S5_LANES)] + w_specs,
            out_specs=[row_spec(w), row_spec(S5_LANES), row_spec(S5_LANES)],
            out_shape=[jax.ShapeDtypeStruct((rows, w), f32), jax.ShapeDtypeStruct((rows, S5_LANES), f32),
                       jax.ShapeDtypeStruct((rows, S5_LANES), f32)],
            scratch_shapes=[pltpu.VMEM((rows, S5_LANES), f32), pltpu.VMEM((rows, S5_LANES), f32)],
            compiler_params=pltpu.CompilerParams(vmem_limit_bytes=VMEM_LIMIT_BYTES),
            name="s5_step",
        )(u.reshape(rows, w), h0r, h0i, *w_args)
        y = y.reshape(bsz, 1, w)
    else:
        tt = min(tt, L)
        h0r, h0i = h0_re.reshape(bsz, 1, S5_LANES).astype(f32), h0_im.reshape(bsz, 1, S5_LANES).astype(f32)
        st_spec = pl.BlockSpec((None, 1, S5_LANES), lambda b, t: (b, 0, 0))
        y, hr, hi = pl.pallas_call(
            _s5_seq_body,
            grid=(bsz, L // tt),
            in_specs=[pl.BlockSpec((None, tt, w), lambda b, t: (b, t, 0)), st_spec, st_spec] + w_specs,
            out_specs=[pl.BlockSpec((None, tt, w), lambda b, t: (b, t, 0)), st_spec, st_spec],
            out_shape=[jax.ShapeDtypeStruct((bsz, L, w), f32), jax.ShapeDtypeStruct((bsz, 1, S5_LANES), f32),
                       jax.ShapeDtypeStruct((bsz, 1, S5_LANES), f32)],
            scratch_shapes=[pltpu.VMEM((tt, S5_LANES), f32), pltpu.VMEM((tt, S5_LANES), f32),
                            pltpu.VMEM((1, S5_LANES), f32), pltpu.VMEM((1, S5_LANES), f32)],
            compiler_params=pltpu.CompilerParams(
                dimension_semantics=("parallel", "arbitrary"), vmem_limit_bytes=VMEM_LIMIT_BYTES),
            name="s5_seq",
        )(u, h0r, h0i, *w_args)
    shp = (bsz, S5_GROUPS, S5_STATE)
    return y, hr.reshape(shp).astype(h0_re.dtype), hi.reshape(shp).astype(h0_im.dtype)


DSA_KEY_CHUNK = 512
DSA_NEAR = 2 * QBLOCK
INT32_MIN = -2 ** 31
NEG_BIG = -1e30
KT_ROWS = HEAD_DIM + 16


def _sortable_key(s):
    bits = lax.bitcast_convert_type(s, jnp.int32)
    return jnp.where(bits < 0, bits ^ jnp.int32(0x7FFFFFFF), bits)


def _dsa_prompt_body(n_top, rb_ref, q_ref, qi_ref, kiwi_ref, kt_ref, v_ref, kit_ref, o_ref, keys_ref, bias_ref, bound_ref, hi_ref, lo_ref):
    i = pl.program_id(1)
    f32, bf16 = jnp.float32, jnp.bfloat16
    kc = DSA_KEY_CHUNK
    q_start = i * QBLOCK
    n_all = (q_start + QBLOCK + kc - 1) // kc
    near_start = jnp.maximum(q_start - QBLOCK, 0)
    n_far = (near_start + kc - 1) // kc
    row = lax.broadcasted_iota(jnp.int32, (QBLOCK, 1), 0)
    qpos = q_start + row

    @pl.when(i == 0)
    def _():
        r = lax.broadcasted_iota(jnp.int32, (QBLOCK, DSA_NEAR), 0)
        c = lax.broadcasted_iota(jnp.int32, (QBLOCK, DSA_NEAR), 1)
        for tile in range(2):
            bucket = t5_bucket(r + tile * QBLOCK - c)
            for h in range(DSA_HEADS):
                b = jnp.zeros((QBLOCK, DSA_NEAR), f32)
                for bk in range(N_BUCKETS):
                    b = jnp.where(bucket == bk, rb_ref[bk, h] - rb_ref[N_BUCKETS - 1, h], b)
                kv, g = divmod(h, DSA_GQA)
                bias_ref[tile, kv, g * QBLOCK:(g + 1) * QBLOCK, :] = b
        lane = lax.broadcasted_iota(jnp.int32, (1, LANES), 1)
        bound = jnp.zeros((1, LANES), f32)
        for kv in range(DSA_KV_HEADS):
            kk = kt_ref[kv * KT_ROWS:kv * KT_ROWS + HEAD_DIM, :].astype(f32)
            kmax = jnp.sqrt(jnp.max(jnp.sum(kk * kk, axis=0, keepdims=True), axis=-1, keepdims=True))
            bound = jnp.where(lane == kv, kmax, bound)
        for h in range(DSA_HEADS):
            kv, g = divmod(h, DSA_GQA)
            bmax = jnp.maximum(jnp.max(bias_ref[:, kv, g * QBLOCK:(g + 1) * QBLOCK, :]), 0.0)
            bound = jnp.where(lane == DSA_KV_HEADS + h, bmax, bound)
        bound_ref[...] = bound

    qi_all = (qi_ref[...].astype(f32) * IDX_DIM ** -0.5).astype(bf16)
    qi = jnp.concatenate([qi_all[:, h * IDX_DIM:(h + 1) * IDX_DIM] for h in range(IDX_HEADS)], axis=0)
    wi = kiwi_ref[:, IDX_DIM:IDX_DIM + IDX_HEADS] * IDX_HEADS ** -0.5

    def score_chunk(c, _):
        off = pl.multiple_of(c * kc, kc)
        d = jnp.dot(qi, kit_ref[:, pl.ds(off, kc)], preferred_element_type=f32)
        s = jnp.zeros((QBLOCK, kc), f32)
        for h in range(IDX_HEADS):
            s = s + wi[:, h:h + 1] * jnp.maximum(d[h * QBLOCK:(h + 1) * QBLOCK], 0.0)
        kpos = off + lax.broadcasted_iota(jnp.int32, (QBLOCK, kc), 1)
        s = jnp.where(s == 0.0, 0.0, s)
        s = jnp.where(kpos <= qpos, s, -jnp.inf)
        key = _sortable_key(s)
        keys_ref[:, pl.ds(off, kc)] = key
        hi_ref[:, pl.ds(off, kc)] = lax.shift_right_arithmetic(key, 16).astype(jnp.int16)
        return 0

    lax.fori_loop(0, n_all, score_chunk, 0)

    def count_where(pred_fn):
        def body(c, acc):
            off = pl.multiple_of(c * kc, kc)
            hit = pred_fn(keys_ref[:, pl.ds(off, kc)], off)
            part = jnp.where(hit, 1.0, 0.0)
            for j in range(kc // 128):
                acc = acc + part[:, j * 128:(j + 1) * 128]
            return acc
        acc = lax.fori_loop(0, n_all, body, jnp.zeros((QBLOCK, 128), f32))
        return jnp.sum(acc, axis=-1, keepdims=True)

    def count16(ref, pred_fn):
        def body(c, acc):
            off = pl.multiple_of(c * kc, kc)
            part = jnp.where(pred_fn(ref[:, pl.ds(off, kc)]), jnp.int16(1), jnp.int16(0))
            return acc + ((part[:, 0:128] + part[:, 128:256]) + (part[:, 256:384] + part[:, 384:512]))
        acc = lax.fori_loop(0, n_all, body, jnp.zeros((QBLOCK, 128), jnp.int16))
        return jnp.sum(acc.astype(f32), axis=-1, keepdims=True)

    def search16(ref, want):
        def bit(it, t):
            cand = t + lax.shift_left(jnp.int32(1), 15 - it)
            c16 = cand.astype(jnp.int16)
            return jnp.where(count16(ref, lambda x: x >= c16) >= want, cand, t)
        return lax.fori_loop(0, 16, bit, jnp.full((QBLOCK, 1), -2 ** 15, jnp.int32))

    t_hi = search16(hi_ref, n_top)
    t_hi16 = t_hi.astype(jnp.int16)
    n_above = count16(hi_ref, lambda x: x > t_hi16)

    def band_chunk(c, _):
        off = pl.multiple_of(c * kc, kc)
        key = keys_ref[:, pl.ds(off, kc)]
        lo = ((key & 0xFFFF) - 2 ** 15).astype(jnp.int16)
        lo_ref[:, pl.ds(off, kc)] = jnp.where(lax.shift_right_arithmetic(key, 16) == t_hi, lo, jnp.int16(-2 ** 15))
        return 0

    lax.fori_loop(0, n_all, band_chunk, 0)
    t_lo = search16(lo_ref, n_top - n_above)
    thr = lax.shift_left(t_hi, 16) + (t_lo + 2 ** 15)

    def is_valid(off, width):
        return (off + lax.broadcasted_iota(jnp.int32, (QBLOCK, width), 1)) <= qpos

    n_gt = count_where(lambda k, off: (k > thr) & is_valid(off, kc))
    n_eq = count_where(lambda k, off: (k == thr) & is_valid(off, kc))
    need = n_top - n_gt
    has_extra_ties = jnp.max(jnp.where(n_eq > need, 1.0, 0.0)) > 0.0

    def tie_search():
        def idx_bit(it, j):
            cand = j + lax.shift_left(jnp.int32(1), 13 - it)
            cnt = count_where(lambda k, off: (k == thr) & is_valid(off, kc)
                              & ((off + lax.broadcasted_iota(jnp.int32, (QBLOCK, kc), 1)) < cand))
            return jnp.where(cnt <= need, cand, j)
        return lax.fori_loop(0, 14, idx_bit, jnp.zeros((QBLOCK, 1), jnp.int32))

    tie_end = lax.cond(has_extra_ties, tie_search, lambda: jnp.full((QBLOCK, 1), 2 ** 14, jnp.int32))

    def selected(keys, off, width):
        kpos = off + lax.broadcasted_iota(jnp.int32, (QBLOCK, width), 1)
        return (kpos <= qpos) & ((keys > thr) | ((keys == thr) & (kpos < tie_end))), kpos

    tile = jnp.minimum(i, 1)
    q_all = (q_ref[...].astype(f32) * HEAD_DIM ** -0.5).astype(bf16)
    qh = [q_all[:, h * HEAD_DIM:(h + 1) * HEAD_DIM] for h in range(DSA_HEADS)]
    off_near = pl.multiple_of(near_start, QBLOCK)
    sel_near, _ = selected(keys_ref[:, pl.ds(off_near, DSA_NEAR)], off_near, DSA_NEAR)

    def near_bias(h):
        kv, g = divmod(h, DSA_GQA)
        return jnp.where(sel_near, bias_ref[tile, kv, g * QBLOCK:(g + 1) * QBLOCK, :], NEG_BIG)

    def far_mask(c):
        off = pl.multiple_of(c * kc, kc)
        sel, kpos = selected(keys_ref[:, pl.ds(off, kc)], off, kc)
        return off, jnp.where(sel & (kpos < near_start), 0.0, NEG_BIG)

    def kt_block(kv, off, width, rows):
        return kt_ref[kv * KT_ROWS:kv * KT_ROWS + rows, pl.ds(off, width)]

    bound = bound_ref[...]
    lane = lax.broadcasted_iota(jnp.int32, (1, LANES), 1)
    pick = lambda idx: jnp.sum(jnp.where(lane == idx, bound, 0.0), axis=-1, keepdims=True)
    col = lax.broadcasted_iota(jnp.int32, (QBLOCK, KT_ROWS - HEAD_DIM), 1)
    q_aug = []
    for h in range(DSA_HEADS):
        qf = qh[h].astype(f32)
        ub = jnp.sqrt(jnp.sum(qf * qf, axis=-1, keepdims=True)) * pick(h // DSA_GQA) * 1.01 + pick(DSA_KV_HEADS + h)
        q_aug.append(jnp.concatenate([qh[h], jnp.where(col == 0, -ub, 0.0).astype(bf16)], axis=-1))

    def fast_attend(accs, off, width, mask_bias):
        accs = list(accs)
        for kv in range(DSA_KV_HEADS):
            kt = kt_block(kv, off, width, KT_ROWS)
            ps = [jnp.exp(jnp.dot(q_aug[kv * DSA_GQA + g], kt, preferred_element_type=f32)
                          + mask_bias(kv * DSA_GQA + g)).astype(bf16) for g in range(DSA_GQA)]
            accs[kv] = accs[kv] + jnp.dot(jnp.concatenate(ps, axis=0), v_ref[kv, pl.ds(off, width), :],
                                          preferred_element_type=f32)
        return tuple(accs)

    def fast_chunk(c, accs):
        off, mb = far_mask(c)
        return fast_attend(accs, off, kc, lambda h: mb)

    zero_acc = tuple(jnp.zeros((DSA_GQA * QBLOCK, 2 * HEAD_DIM), f32) for _ in range(DSA_KV_HEADS))
    accs = fast_attend(lax.fori_loop(0, n_far, fast_chunk, zero_acc), off_near, DSA_NEAR, near_bias)
    l_min = accs[0][:, HEAD_DIM:]
    for kv in range(1, DSA_KV_HEADS):
        l_min = jnp.minimum(l_min, accs[kv][:, HEAD_DIM:])
    fast_ok = jnp.min(l_min) > 1e-30

    def safe_attend(carry, off, width, mask_bias):
        ms, accs = list(carry[0]), list(carry[1])
        for kv in range(DSA_KV_HEADS):
            kt = kt_block(kv, off, width, HEAD_DIM)
            ps, alphas = [], []
            for g in range(DSA_GQA):
                h = kv * DSA_GQA + g
                logits = jnp.dot(qh[h], kt, preferred_element_type=f32) + mask_bias(h)
                m_new = jnp.maximum(ms[h], jnp.max(logits, axis=-1, keepdims=True))
                alphas.append(jnp.exp(ms[h] - m_new))
                ps.append(jnp.exp(logits - m_new).astype(bf16))
                ms[h] = m_new
            accs[kv] = (jnp.concatenate(alphas, axis=0) * accs[kv]
                        + jnp.dot(jnp.concatenate(ps, axis=0), v_ref[kv, pl.ds(off, width), :], preferred_element_type=f32))
        return tuple(ms), tuple(accs)

    def safe_path():
        def safe_chunk(c, carry):
            off, mb = far_mask(c)
            return safe_attend(carry, off, kc, lambda h: mb)
        init = (tuple(jnp.full((QBLOCK, 1), NEG_BIG, f32) for _ in range(DSA_HEADS)), zero_acc)
        return safe_attend(lax.fori_loop(0, n_far, safe_chunk, init), off_near, DSA_NEAR, near_bias)[1]

    accs = lax.cond(fast_ok, lambda: accs, safe_path)
    outs = []
    for kv in range(DSA_KV_HEADS):
        o = accs[kv][:, :HEAD_DIM] / accs[kv][:, HEAD_DIM:]
        outs += [o[g * QBLOCK:(g + 1) * QBLOCK] for g in range(DSA_GQA)]
    o_ref[...] = jnp.concatenate(outs, axis=-1)


def dsa_prompt_pallas(q, qi, kiwi, k, v, rel_bias):
    bsz, L = q.shape[:2]
    nq = L // QBLOCK
    n_top = min(TOPK_MAX, L // 4)
    f32, bf16 = jnp.float32, jnp.bfloat16
    k_t = k.astype(bf16).reshape(bsz, L, DSA_KV_HEADS, HEAD_DIM).transpose(0, 2, 3, 1)
    k_pad = jnp.zeros((bsz, DSA_KV_HEADS, KT_ROWS - HEAD_DIM, L), bf16).at[:, :, 0, :].set(1.0)
    k_t = jnp.concatenate([k_t, k_pad], axis=2).reshape(bsz, DSA_KV_HEADS * KT_ROWS, L)
    v_h = v.astype(bf16).reshape(bsz, L, DSA_KV_HEADS, HEAD_DIM).swapaxes(1, 2)
    v_h = jnp.concatenate([v_h, jnp.ones_like(v_h)], axis=-1)
    ki_t = kiwi[..., :IDX_DIM].astype(bf16).swapaxes(1, 2)
    lk = max(L, DSA_KEY_CHUNK)
    if lk != L:
        k_t = jnp.pad(k_t, ((0, 0), (0, 0), (0, lk - L)))
        v_h = jnp.pad(v_h, ((0, 0), (0, 0), (0, lk - L), (0, 0)))
        ki_t = jnp.pad(ki_t, ((0, 0), (0, 0), (0, lk - L)))
    tok = lambda n: pl.BlockSpec((None, QBLOCK, n), lambda b, i: (b, i, 0))
    return pl.pallas_call(
        functools.partial(_dsa_prompt_body, n_top),
        grid=(bsz, nq),
        in_specs=[
            pl.BlockSpec(memory_space=pltpu.SMEM),
            tok(SEQ_WIDTH), tok(IDX_HEADS * IDX_DIM), tok(LANES),
            pl.BlockSpec((None, DSA_KV_HEADS * KT_ROWS, lk), lambda b, i: (b, 0, 0), pipeline_mode=pl.Buffered(1)),
            pl.BlockSpec((None, DSA_KV_HEADS, lk, 2 * HEAD_DIM), lambda b, i: (b, 0, 0, 0), pipeline_mode=pl.Buffered(1)),
            pl.BlockSpec((None, IDX_DIM, lk), lambda b, i: (b, 0, 0), pipeline_mode=pl.Buffered(1)),
        ],
        out_specs=tok(SEQ_WIDTH),
        out_shape=jax.ShapeDtypeStruct((bsz, L, SEQ_WIDTH), f32),
        scratch_shapes=[pltpu.VMEM((QBLOCK, lk), jnp.int32),
                        pltpu.VMEM((2, DSA_KV_HEADS, DSA_GQA * QBLOCK, DSA_NEAR), f32),
                        pltpu.VMEM((1, LANES), f32),
                        pltpu.VMEM((QBLOCK, lk), jnp.int16), pltpu.VMEM((QBLOCK, lk), jnp.int16)],
        compiler_params=pltpu.CompilerParams(
            dimension_semantics=("parallel", "arbitrary"), vmem_limit_bytes=VMEM_LIMIT_BYTES),
        name="dsa_prompt",
    )(rel_bias.astype(f32), q, qi, kiwi, k_t, v_h, ki_t)


DEC_PAGES_PER_STEP = 16
KV_WIDTH = DSA_KV_HEADS * HEAD_DIM


def _index_score(qi, wi, ki):
    d = _dot_nt(qi, ki)
    s = jnp.sum(wi * jnp.maximum(d, 0.0), axis=0, keepdims=True)
    return jnp.where(s == 0.0, 0.0, s)


def _dsa_dec_score_body(pt_ref, qi_ref, wi_ref, *refs):
    pages, s_ref = refs[:-1], refs[-1]
    qi = (qi_ref[...].astype(jnp.float32) * IDX_DIM ** -0.5).astype(jnp.bfloat16)
    wi = wi_ref[...]
    s_ref[...] = jnp.concatenate([_index_score(qi, wi, p[...].astype(jnp.bfloat16)) for p in pages], axis=0)


def _dsa_dec_select_body(n_top, qi_ref, wi_ref, kinew_ref, s_ref, sel_ref, snew_ref):
    f32, bf16 = jnp.float32, jnp.bfloat16
    bsz, past = s_ref.shape
    qi = (qi_ref[...].astype(f32) * IDX_DIM ** -0.5).astype(bf16).astype(f32)
    d_new = jnp.sum(qi * kinew_ref[...].astype(bf16).astype(f32), axis=-1, keepdims=True)
    s_new = jnp.sum(wi_ref[...] * jnp.maximum(d_new, 0.0), axis=1)
    s_new = jnp.where(s_new == 0.0, 0.0, s_new)
    key_new = _sortable_key(s_new)
    keys = _sortable_key(s_ref[...])
    kpos = lax.broadcasted_iota(jnp.int32, (bsz, past), 1)

    def count(pred_past, pred_new):
        part = jnp.where(pred_past, 1.0, 0.0)
        acc = part[:, 0:LANES]
        for j in range(1, past // LANES):
            acc = acc + part[:, j * LANES:(j + 1) * LANES]
        return jnp.sum(acc, axis=-1, keepdims=True) + jnp.where(pred_new, 1.0, 0.0)

    def thr_bit(it, thr):
        cand = thr + lax.shift_left(jnp.int32(1), 31 - it)
        return jnp.where(count(keys >= cand, key_new >= cand) >= n_top, cand, thr)

    thr = lax.fori_loop(0, 32, thr_bit, jnp.full((bsz, 1), INT32_MIN, jnp.int32))
    need = n_top - count(keys > thr, key_new > thr)

    def idx_bit(it, j):
        cand = j + lax.shift_left(jnp.int32(1), 14 - it)
        cnt = count((keys == thr) & (kpos < cand), (key_new == thr) & (past < cand))
        return jnp.where(cnt <= need, cand, j)

    n_eq = count(keys == thr, key_new == thr)
    tie_end = lax.cond(jnp.max(jnp.where(n_eq > need, 1.0, 0.0)) > 0.0,
                       lambda: lax.fori_loop(0, 15, idx_bit, jnp.zeros((bsz, 1), jnp.int32)),
                       lambda: jnp.full((bsz, 1), 2 ** 15, jnp.int32))
    lane = lax.broadcasted_iota(jnp.int32, (bsz, LANES), 1)
    sel_ref[...] = jnp.where(lane == 0, thr, jnp.where(lane == 1, tie_end, 0))
    snew_ref[...] = jnp.broadcast_to(s_new, (bsz, LANES))


def _dsa_dec_attend_body(n_pages, pt_ref, rb_ref, q_ref, knew_ref, vnew_ref, sel_ref, snew_ref, s_ref, *refs):
    pp = DEC_PAGES_PER_STEP if n_pages >= DEC_PAGES_PER_STEP else n_pages
    k_pages, v_pages = refs[:pp], refs[pp:2 * pp]
    o_ref, m_ref, l_ref, acc_ref = refs[2 * pp:]
    f32, bf16 = jnp.float32, jnp.bfloat16
    t = pl.program_id(1)
    nt = pl.num_programs(1)
    past = n_pages * PAGE_SIZE

    @pl.when(t == 0)
    def _():
        m_ref[...] = jnp.full(m_ref.shape, NEG_BIG, f32)
        l_ref[...] = jnp.zeros(l_ref.shape, f32)
        acc_ref[...] = jnp.zeros(acc_ref.shape, f32)

    thr, tie_end = sel_ref[:, 0:1], sel_ref[:, 1:2]
    q = (q_ref[...].astype(f32) * HEAD_DIM ** -0.5).astype(bf16)
    lane12 = lax.broadcasted_iota(jnp.int32, (DSA_HEADS, LANES), 1)

    def head_bias(dist):
        bucket = t5_bucket(dist)
        b = jnp.zeros(dist.shape, f32)
        for h in range(DSA_HEADS):
            row = jnp.zeros(dist.shape, f32)
            for bk in range(N_BUCKETS):
                row = jnp.where(bucket == bk, rb_ref[bk, h] - rb_ref[N_BUCKETS - 1, h], row)
            b = jnp.where(lax.broadcasted_iota(jnp.int32, dist.shape, 0) == h, row, b)
        return b

    keys_step = _sortable_key(s_ref[pl.ds(pl.multiple_of(t * pp, pp), pp), :])
    tiles = []
    for j in range(pp):
        kpos = (t * pp + j) * PAGE_SIZE + lax.broadcasted_iota(jnp.int32, (1, LANES), 1)
        keys = keys_step[j:j + 1]
        sel = (keys > thr) | ((keys == thr) & (kpos < tie_end))
        logits = _dot_nt(q, k_pages[j][...].astype(bf16))
        if j == pp - 1:
            near = head_bias(past - (t * pp + j) * PAGE_SIZE - lane12)
            logits = logits + jnp.where(t == nt - 1, near, 0.0)
        tiles.append(jnp.where(sel, logits, NEG_BIG))
    m = m_ref[...]
    tile_max = tiles[0]
    for x in tiles[1:]:
        tile_max = jnp.maximum(tile_max, x)
    m_new = jnp.maximum(m, jnp.max(tile_max, axis=-1, keepdims=True))
    alpha = jnp.exp(m - m_new)
    ps = [jnp.exp(x - m_new) for x in tiles]
    p_sum = ps[0]
    for x in ps[1:]:
        p_sum = p_sum + x
    pv = jnp.dot(ps[0].astype(bf16), v_pages[0][...].astype(bf16), preferred_element_type=f32)
    for j in range(1, pp):
        pv = pv + jnp.dot(ps[j].astype(bf16), v_pages[j][...].astype(bf16), preferred_element_type=f32)
    l_ref[...] = alpha * l_ref[...] + jnp.sum(p_sum, axis=-1, keepdims=True)
    acc_ref[...] = alpha * acc_ref[...] + pv
    m_ref[...] = m_new

    @pl.when(t == nt - 1)
    def _():
        key_new = _sortable_key(snew_ref[:, 0:1])
        sel_new = (key_new > thr) | ((key_new == thr) & (past < tie_end))
        k_new = knew_ref[...].astype(bf16).astype(f32)
        logit_new = jnp.sum(q.astype(f32) * k_new, axis=-1, keepdims=True)
        logit_new = logit_new + head_bias(jnp.zeros((DSA_HEADS, LANES), jnp.int32))[:, :1]
        logit_new = jnp.where(sel_new, logit_new, NEG_BIG)
        m = m_ref[...]
        m_new = jnp.maximum(m, logit_new)
        alpha = jnp.exp(m - m_new)
        p_new = jnp.exp(logit_new - m_new)
        l = alpha * l_ref[...] + p_new
        v_new = vnew_ref[...].astype(bf16).astype(f32)
        acc = alpha * acc_ref[...] + p_new.astype(bf16).astype(f32) * v_new
        o = acc / l
        o_ref[...] = jnp.concatenate(
            [o[h:h + 1, (h // DSA_GQA) * HEAD_DIM:(h // DSA_GQA + 1) * HEAD_DIM] for h in range(DSA_HEADS)], axis=-1)


def dsa_decode_pallas(q, qi, kiwi, k_new, v_new, pool_k, pool_v, pool_ki, page_table, rel_bias):
    bsz = q.shape[0]
    n_pages = page_table.shape[1]
    n_phys = pool_k.shape[0]
    pp = DEC_PAGES_PER_STEP if n_pages >= DEC_PAGES_PER_STEP else n_pages
    nt = n_pages // pp
    n_top = min(TOPK_MAX, (n_pages * PAGE_SIZE + 1) // 4)
    f32, bf16 = jnp.float32, jnp.bfloat16
    eye = jnp.eye(DSA_KV_HEADS, dtype=q.dtype)
    q_bd = jnp.einsum('bkgd,kj->bkgjd', q.reshape(bsz, DSA_KV_HEADS, DSA_GQA, HEAD_DIM), eye)
    q_bd = q_bd.reshape(bsz, DSA_HEADS, KV_WIDTH)
    qi3 = qi.reshape(bsz, IDX_HEADS, IDX_DIM)
    wi3 = (kiwi[:, IDX_DIM:IDX_DIM + IDX_HEADS] * IDX_HEADS ** -0.5).reshape(bsz, IDX_HEADS, 1)
    ki_new = kiwi[:, :IDX_DIM].reshape(bsz, 1, IDX_DIM)
    pk, pv = pool_k, pool_v
    per_b = lambda *shape: pl.BlockSpec((None,) + shape, lambda b, t, pt: (b,) + (0,) * len(shape))
    page = lambda width, j: pl.BlockSpec((None, PAGE_SIZE, width), lambda b, t, pt: (pt[b, t * pp + j], 0, 0))
    scores = pl.pallas_call(
        _dsa_dec_score_body,
        grid_spec=pltpu.PrefetchScalarGridSpec(
            num_scalar_prefetch=1, grid=(bsz, nt),
            in_specs=[per_b(IDX_HEADS, IDX_DIM), per_b(IDX_HEADS, 1)] + [page(IDX_DIM, j) for j in range(pp)],
            out_specs=pl.BlockSpec((None, pp, LANES), lambda b, t, pt: (b, t, 0))),
        out_shape=jax.ShapeDtypeStruct((bsz, n_pages, LANES), f32),
        compiler_params=pltpu.CompilerParams(
            dimension_semantics=("parallel", "parallel"), vmem_limit_bytes=VMEM_LIMIT_BYTES),
        name="dsa_dec_score",
    )(page_table, qi3, wi3, *([pool_ki] * pp))
    sel, s_new = pl.pallas_call(
        functools.partial(_dsa_dec_select_body, n_top),
        grid=(1,),
        in_specs=[pl.BlockSpec((bsz, IDX_HEADS, IDX_DIM), lambda i: (0, 0, 0)),
                  pl.BlockSpec((bsz, IDX_HEADS, 1), lambda i: (0, 0, 0)),
                  pl.BlockSpec((bsz, 1, IDX_DIM), lambda i: (0, 0, 0)),
                  pl.BlockSpec((bsz, n_pages * PAGE_SIZE), lambda i: (0, 0))],
        out_specs=[pl.BlockSpec((bsz, LANES), lambda i: (0, 0))] * 2,
        out_shape=[jax.ShapeDtypeStruct((bsz, LANES), jnp.int32), jax.ShapeDtypeStruct((bsz, LANES), f32)],
        compiler_params=pltpu.CompilerParams(vmem_limit_bytes=VMEM_LIMIT_BYTES),
        name="dsa_dec_select",
    )(qi3, wi3, ki_new, scores.reshape(bsz, n_pages * PAGE_SIZE))
    o = pl.pallas_call(
        functools.partial(_dsa_dec_attend_body, n_pages),
        grid_spec=pltpu.PrefetchScalarGridSpec(
            num_scalar_prefetch=1, grid=(bsz, nt),
            in_specs=[pl.BlockSpec(memory_space=pltpu.SMEM), per_b(DSA_HEADS, KV_WIDTH), per_b(1, KV_WIDTH),
                      per_b(1, KV_WIDTH), per_b(1, LANES), per_b(1, LANES), per_b(n_pages, LANES)]
            + [page(KV_WIDTH, j) for j in range(pp)] * 2,
            out_specs=per_b(1, SEQ_WIDTH),
            scratch_shapes=[pltpu.VMEM((DSA_HEADS, 1), f32), pltpu.VMEM((DSA_HEADS, 1), f32),
                            pltpu.VMEM((DSA_HEADS, KV_WIDTH), f32)]),
        out_shape=jax.ShapeDtypeStruct((bsz, 1, SEQ_WIDTH), f32),
        compiler_params=pltpu.CompilerParams(
            dimension_semantics=("parallel", "arbitrary"), vmem_limit_bytes=VMEM_LIMIT_BYTES),
        name="dsa_dec_attend",
    )(page_table, rel_bias.astype(f32), q_bd, k_new.reshape(bsz, 1, KV_WIDTH), v_new.reshape(bsz, 1, KV_WIDTH),
      sel.reshape(bsz, 1, LANES), s_new.reshape(bsz, 1, LANES), scores, *([pk] * pp), *([pv] * pp))
    return o


CONV_HALO = 8


def _gdn_prep_body(x_ref, ab_ref, ctx_ref, w_ref, alog_ref, dtb_ref, q_ref, k_ref, v_ref, gb_ref, cs_ref, xp_ref):
    tt = x_ref.shape[0]
    halo = CONV_W - 1

    @pl.when(pl.program_id(1) == 0)
    def _():
        xp_ref[CONV_HALO - halo:CONV_HALO, :] = ctx_ref[...]

    x = x_ref[...]
    xp_ref[CONV_HALO:CONV_HALO + tt, :] = x
    w = w_ref[...]
    y = xp_ref[CONV_HALO - halo:CONV_HALO - halo + tt, :] * w[0:1]
    for j in range(1, CONV_W):
        y = y + xp_ref[CONV_HALO - halo + j:CONV_HALO - halo + j + tt, :] * w[j:j + 1]
    last = x[tt - halo:, :]
    xp_ref[CONV_HALO - halo:CONV_HALO, :] = last
    cs_ref[...] = last
    y = y * _sigmoid(y)

    def l2n(a, scale):
        cols = []
        for h in range(GDN_HEADS):
            s = a[:, h * HEAD_DIM:(h + 1) * HEAD_DIM]
            cols.append(s * (lax.rsqrt(jnp.sum(s * s, axis=-1, keepdims=True) + EPS) * scale))
        return jnp.concatenate(cols, axis=-1)

    q_ref[...] = l2n(y[:, :SEQ_WIDTH], HEAD_DIM ** -0.5)
    k_ref[...] = l2n(y[:, SEQ_WIDTH:2 * SEQ_WIDTH], 1.0)
    v_ref[...] = y[:, 2 * SEQ_WIDTH:]
    ab = ab_ref[...]
    xa = ab + dtb_ref[...]
    softplus = jnp.maximum(xa, 0.0) + jnp.log(1.0 + jnp.exp(-jnp.abs(xa)))
    g = -jnp.exp(alog_ref[...]) * softplus
    lane = lax.broadcasted_iota(jnp.int32, ab.shape, 1)
    gb_ref[...] = jnp.where(lane < GDN_HEADS, g, _sigmoid(ab))


def gdn_prep_pallas(qkv, ab, ctx, conv_w, a_log, dt_bias, *, tt=256):
    bsz, L, ch = qkv.shape
    tt = min(tt, L)
    f32 = jnp.float32
    pad = lambda r: jnp.pad(r.astype(f32).reshape(1, -1), ((0, 0), (0, LANES - r.shape[-1])))
    tok = lambda n: pl.BlockSpec((None, tt, n), lambda b, t: (b, t, 0))
    const2 = lambda b, t: (0, 0)
    return pl.pallas_call(
        _gdn_prep_body,
        grid=(bsz, L // tt),
        in_specs=[tok(ch), tok(LANES), pl.BlockSpec((None, CONV_W - 1, ch), lambda b, t: (b, 0, 0)),
                  pl.BlockSpec((CONV_W, ch), const2), pl.BlockSpec((1, LANES), const2), pl.BlockSpec((1, LANES), const2)],
        out_specs=[tok(SEQ_WIDTH), tok(SEQ_WIDTH), tok(SEQ_WIDTH), tok(LANES),
                   pl.BlockSpec((None, CONV_W - 1, ch), lambda b, t: (b, 0, 0))],
        out_shape=[jax.ShapeDtypeStruct((bsz, L, SEQ_WIDTH), f32)] * 3
        + [jax.ShapeDtypeStruct((bsz, L, LANES), f32), jax.ShapeDtypeStruct((bsz, CONV_W - 1, ch), f32)],
        scratch_shapes=[pltpu.VMEM((CONV_HALO + tt, ch), f32)],
        compiler_params=pltpu.CompilerParams(
            dimension_semantics=("parallel", "arbitrary"), vmem_limit_bytes=VMEM_LIMIT_BYTES),
        name="gdn_prep",
    )(qkv, ab, ctx.astype(f32), conv_w.astype(f32), pad(a_log), pad(dt_bias))


def _split3(a):
    bf16, f32 = jnp.bfloat16, jnp.float32
    h = a.astype(bf16)
    r = a - h.astype(f32)
    m = r.astype(bf16)
    return h, m, (r - m.astype(f32)).astype(bf16)


def _mm_hi(a, b):
    f32 = jnp.float32
    ah, am, _ = _split3(a)
    bh, bm, _ = _split3(b)
    d = lambda x, y: jnp.dot(x, y, preferred_element_type=f32)
    return d(ah, bh) + (d(ah, bm) + d(am, bh))


def _mm_sel(sel, b):
    f32 = jnp.float32
    s = sel.astype(jnp.bfloat16)
    bh, bm, bl = _split3(b)
    d = lambda y: jnp.dot(s, y, preferred_element_type=f32)
    return d(bh) + (d(bm) + d(bl))


def _dot_nt(a, b):
    return lax.dot_general(a, b, (((1,), (1,)), ((), ())), preferred_element_type=jnp.float32)


def _dot_tn(a, b):
    return lax.dot_general(a, b, (((0,), (0,)), ((), ())), preferred_element_type=jnp.float32)


def _gdn_local_body(q_ref, k_ref, v_ref, gb_ref, uv_ref, wk_ref, qh_ref, kt_ref, qk_ref):
    f32, bf16 = jnp.float32, jnp.bfloat16
    c = GDN_CHUNK
    r_i = lax.broadcasted_iota(jnp.int32, (c, c), 0)
    c_i = lax.broadcasted_iota(jnp.int32, (c, c), 1)
    tril = r_i >= c_i
    stril = r_i > c_i
    triu = r_i <= c_i
    eye = jnp.where(r_i == c_i, 1.0, 0.0)
    ones = jnp.ones((c, c), jnp.bool_)
    lane = lax.broadcasted_iota(jnp.int32, (c, LANES), 1)
    gb = gb_ref[...]
    heads = range(GDN_HEADS)
    sl = [slice(h * HEAD_DIM, (h + 1) * HEAD_DIM) for h in heads]
    pick = lambda idx: jnp.sum(jnp.where(lane == idx, gb, 0.0), axis=-1, keepdims=True)
    g_col = [pick(h) for h in heads]
    beta = [pick(GDN_HEADS + h) for h in heads]
    q = [q_ref[:, sl[h]] for h in heads]
    k = [k_ref[:, sl[h]] for h in heads]
    v = [v_ref[:, sl[h]] for h in heads]
    g_mat = [jnp.broadcast_to(g_col[h], (c, c)) for h in heads]
    cum_col = [_mm_sel(tril, g_mat[h]) for h in heads]
    cum_row = [_mm_sel(ones, jnp.where(triu, g_mat[h], 0.0)) for h in heads]
    gam = [jnp.where(tril, jnp.exp(jnp.where(tril, cum_col[h] - cum_row[h], 0.0)), 0.0) for h in heads]
    kb = [k[h].astype(bf16) for h in heads]
    a_mat = [jnp.where(stril, beta[h] * _dot_nt(kb[h], kb[h]) * gam[h], 0.0) for h in heads]
    t_inv = [eye - a_mat[h] for h in heads]
    pw = a_mat
    for _ in range(5):
        pw = [_mm_hi(pw[h], pw[h]) for h in heads]
        t_inv = [t_inv[h] + _mm_hi(t_inv[h], pw[h]) for h in heads]
    gc = [cum_col[h][:, :1] for h in heads]
    egc = [jnp.exp(gc[h]) for h in heads]
    tw = [_mm_hi(t_inv[h], jnp.concatenate([beta[h] * v[h], (beta[h] * egc[h]) * k[h]], axis=-1)) for h in heads]
    qk = [jnp.where(tril, _dot_nt(q[h].astype(bf16), kb[h]) * gam[h], 0.0) for h in heads]
    cat = lambda xs: jnp.concatenate(xs, axis=-1)
    uv_ref[...] = cat([tw[h][:, :HEAD_DIM] for h in heads])
    wk_ref[...] = cat([tw[h][:, HEAD_DIM:] for h in heads]).astype(bf16)
    qh_ref[...] = cat([egc[h] * q[h] for h in heads]).astype(bf16)
    eye_b = eye.astype(bf16)
    kt = [(jnp.exp(gc[h][c - 1:c, :] - gc[h]) * k[h]).astype(bf16) for h in heads]
    kt_ref[...] = cat([_dot_tn(kt[h], eye_b) for h in heads]).astype(bf16)
    qk_ref[...] = cat(qk).astype(bf16)


def gdn_local_pallas(qn, kn, vv, gb):
    bsz, L, w = qn.shape
    c = GDN_CHUNK
    f32, bf16 = jnp.float32, jnp.bfloat16
    tok = lambda n: pl.BlockSpec((None, c, n), lambda b, t: (b, t, 0))
    return pl.pallas_call(
        _gdn_local_body,
        grid=(bsz, L // c),
        in_specs=[tok(w), tok(w), tok(w), tok(LANES)],
        out_specs=[tok(w)] * 5,
        out_shape=[jax.ShapeDtypeStruct((bsz, L, w), f32)] + [jax.ShapeDtypeStruct((bsz, L, w), bf16)] * 4,
        compiler_params=pltpu.CompilerParams(
            dimension_semantics=("parallel", "parallel"), vmem_limit_bytes=VMEM_LIMIT_BYTES),
        name="gdn_local",
    )(qn, kn, vv, gb)


def _gdn_scan_body(uv_ref, wk_ref, qh_ref, ktt_ref, qk_ref, gb_ref, gate_ref, s0_ref, on_ref, o_ref, sf_ref, s_ref):
    f32, bf16 = jnp.float32, jnp.bfloat16
    c = GDN_CHUNK
    tt = uv_ref.shape[0]
    heads = range(GDN_HEADS)
    sl = [slice(h * HEAD_DIM, (h + 1) * HEAD_DIM) for h in heads]

    @pl.when(pl.program_id(1) == 0)
    def _():
        s_ref[...] = s0_ref[...]

    lane = lax.broadcasted_iota(jnp.int32, (1, LANES), 1)
    dot = lambda a, b: jnp.dot(a, b, preferred_element_type=f32)

    def chunk(ci, _):
        rows = pl.ds(pl.multiple_of(ci * c, c), c)
        eg_last = jnp.exp(jnp.sum(gb_ref[rows, :], axis=0, keepdims=True))
        eg = [jnp.sum(jnp.where(lane == h, eg_last, 0.0), axis=-1, keepdims=True) for h in heads]
        st = [s_ref[h] for h in heads]
        sb = [st[h].astype(bf16) for h in heads]
        u = [uv_ref[rows, sl[h]] - dot(wk_ref[rows, sl[h]], sb[h]) for h in heads]
        ub = [u[h].astype(bf16) for h in heads]
        new = [eg[h] * st[h] + dot(ktt_ref[rows, sl[h]], ub[h]) for h in heads]
        for h in heads:
            s_ref[h] = new[h]
        o = [dot(qh_ref[rows, sl[h]], sb[h]) + dot(qk_ref[rows, sl[h]], ub[h]) for h in heads]
        o = [o[h] * lax.rsqrt(jnp.mean(o[h] * o[h], axis=-1, keepdims=True) + EPS) for h in heads]
        gt = gate_ref[rows, :]
        o_ref[rows, :] = (jnp.concatenate(o, axis=-1) * on_ref[...]) * (gt * _sigmoid(gt))
        return 0

    lax.fori_loop(0, tt // c, chunk, 0)
    sf_ref[...] = s_ref[...]


def gdn_scan_pallas(uv, wk, qh, kt, qk, gb, gate, s0, o_norm, *, tt=512):
    bsz, L, w = uv.shape
    tt = min(tt, L)
    f32 = jnp.float32
    tok = lambda n: pl.BlockSpec((None, tt, n), lambda b, t: (b, t, 0))
    st = pl.BlockSpec((None, GDN_HEADS, HEAD_DIM, HEAD_DIM), lambda b, t: (b, 0, 0, 0))
    on = jnp.tile(o_norm.astype(f32), GDN_HEADS).reshape(1, w)
    return pl.pallas_call(
        _gdn_scan_body,
        grid=(bsz, L // tt),
        in_specs=[tok(w)] * 5 + [tok(LANES), tok(w), st, pl.BlockSpec((1, w), lambda b, t: (0, 0))],
        out_specs=[tok(w), st],
        out_shape=[jax.ShapeDtypeStruct((bsz, L, w), f32), jax.ShapeDtypeStruct(s0.shape, f32)],
        scratch_shapes=[pltpu.VMEM((GDN_HEADS, HEAD_DIM, HEAD_DIM), f32)],
        compiler_params=pltpu.CompilerParams(
            dimension_semantics=("parallel", "arbitrary"), vmem_limit_bytes=VMEM_LIMIT_BYTES),
        name="gdn_scan",
    )(uv, wk, qh, kt, qk, gb, gate, s0.astype(f32).swapaxes(2, 3), on)


def gdn_mix_pallas(qkv, ab, gate, conv_ctx, s0, conv_w, a_log, dt_bias, o_norm):
    qn, kn, vv, gb, conv_state = gdn_prep_pallas(qkv, ab, conv_ctx, conv_w, a_log, dt_bias)
    uv, wk, qh, kt, qk = gdn_local_pallas(qn, kn, vv, gb)
    o, s_fin_t = gdn_scan_pallas(uv, wk, qh, kt, qk, gb, gate, s0, o_norm)
    return o, conv_state.astype(qkv.dtype), s_fin_t.swapaxes(2, 3).astype(s0.dtype)


def _gdn_dec_body(x_ref, ab_ref, gate_ref, ctx_ref, s_ref, w_ref, alog_ref, dtb_ref, on_ref, o_ref, cs_ref, sn_ref):
    f32, bf16 = jnp.float32, jnp.bfloat16
    rnd = lambda a: a.astype(bf16).astype(f32)
    x, ctx, w = x_ref[...], ctx_ref[...], w_ref[...]
    halo = CONV_W - 1
    y = x * w[halo:halo + 1]
    for j in range(halo):
        y = y + ctx[j:j + 1] * w[j:j + 1]
    y = y * _sigmoid(y)
    cs_ref[...] = jnp.concatenate([ctx[1:], x], axis=0)
    ab = ab_ref[...]
    xa = ab + dtb_ref[...]
    g_all = -jnp.exp(alog_ref[...]) * (jnp.maximum(xa, 0.0) + jnp.log(1.0 + jnp.exp(-jnp.abs(xa))))
    beta_all = _sigmoid(ab)
    lane = lax.broadcasted_iota(jnp.int32, (1, LANES), 1)
    pick = lambda a, idx: jnp.sum(jnp.where(lane == idx, a, 0.0), axis=-1, keepdims=True)
    r_i = lax.broadcasted_iota(jnp.int32, (HEAD_DIM, HEAD_DIM), 0)
    c_i = lax.broadcasted_iota(jnp.int32, (HEAD_DIM, HEAD_DIM), 1)
    eye = r_i == c_i
    to_col = lambda row: jnp.sum(jnp.where(eye, jnp.broadcast_to(row, (HEAD_DIM, HEAD_DIM)), 0.0), axis=-1, keepdims=True)
    to_row = lambda col: jnp.sum(jnp.where(eye, jnp.broadcast_to(col, (HEAD_DIM, HEAD_DIM)), 0.0), axis=0, keepdims=True)
    outs = []
    for h in range(GDN_HEADS):
        sl = slice(h * HEAD_DIM, (h + 1) * HEAD_DIM)
        q, k, v = y[:, sl], y[:, SEQ_WIDTH + h * HEAD_DIM:SEQ_WIDTH + (h + 1) * HEAD_DIM], y[:, 2 * SEQ_WIDTH + h * HEAD_DIM:2 * SEQ_WIDTH + (h + 1) * HEAD_DIM]
        q = q * (lax.rsqrt(jnp.sum(q * q, axis=-1, keepdims=True) + EPS) * HEAD_DIM ** -0.5)
        k = k * lax.rsqrt(jnp.sum(k * k, axis=-1, keepdims=True) + EPS)
        eg = jnp.exp(pick(g_all, h))
        beta = pick(beta_all, GDN_HEADS + h)
        s = s_ref[h]
        sb = rnd(s)
        s_wk = jnp.sum(sb * rnd((beta * eg) * k), axis=-1, keepdims=True)
        u = to_col(beta * v) - s_wk
        ub = rnd(u)
        s_q = jnp.sum(sb * rnd(eg * q), axis=-1, keepdims=True)
        qk = jnp.sum(rnd(q) * rnd(k), axis=-1, keepdims=True)
        o = to_row(s_q + rnd(qk) * ub)
        sn_ref[h] = eg * s + ub * rnd(k)
        outs.append(o * lax.rsqrt(jnp.mean(o * o, axis=-1, keepdims=True) + EPS))
    gt = gate_ref[...]
    o_ref[...] = (jnp.concatenate(outs, axis=-1) * on_ref[...]) * (gt * _sigmoid(gt))


def gdn_decode_pallas(qkv, ab, gate, conv_ctx, s0, conv_w, a_log, dt_bias, o_norm):
    bsz = qkv.shape[0]
    f32 = jnp.float32
    pad = lambda r: jnp.pad(r.astype(f32).reshape(1, -1), ((0, 0), (0, LANES - r.shape[-1])))
    per_b = lambda *shape: pl.BlockSpec((None,) + shape, lambda b: (b,) + (0,) * len(shape))
    const = lambda *shape: pl.BlockSpec(shape, lambda b: (0,) * len(shape))
    st = (GDN_HEADS, HEAD_DIM, HEAD_DIM)
    o, conv_state, s_new = pl.pallas_call(
        _gdn_dec_body,
        grid=(bsz,),
        in_specs=[per_b(1, CONV_CH), per_b(1, LANES), per_b(1, SEQ_WIDTH), per_b(CONV_W - 1, CONV_CH), per_b(*st),
                  const(CONV_W, CONV_CH), const(1, LANES), const(1, LANES), const(1, SEQ_WIDTH)],
        out_specs=[per_b(1, SEQ_WIDTH), per_b(CONV_W - 1, CONV_CH), per_b(*st)],
        out_shape=[jax.ShapeDtypeStruct((bsz, 1, SEQ_WIDTH), f32), jax.ShapeDtypeStruct((bsz, CONV_W - 1, CONV_CH), f32),
                   jax.ShapeDtypeStruct((bsz,) + st, f32)],
        compiler_params=pltpu.CompilerParams(dimension_semantics=("parallel",), vmem_limit_bytes=VMEM_LIMIT_BYTES),
        name="gdn_decode",
    )(qkv, ab, gate, conv_ctx.astype(f32), s0.astype(f32), conv_w.astype(f32), pad(a_log), pad(dt_bias),
      jnp.tile(o_norm.astype(f32), GDN_HEADS).reshape(1, SEQ_WIDTH))
    return o, conv_state.astype(qkv.dtype), s_new.astype(s0.dtype)


def _pad_cols(w, n):
    return jnp.pad(w, ((0, 0), (0, n - w.shape[1])))


def _in_proj_layout(kind, w_in):
    f32, bf16 = jnp.float32, jnp.bfloat16
    if kind == 0:
        return w_in, ((0, MEM_WIDTH), (MEM_WIDTH, SEQ_WIDTH)), (bf16, f32)
    if kind == 1:
        w = _pad_cols(w_in, MEM_WIDTH + sum(DSA_SPLITS[:4]) + LANES)
        widths = (MEM_WIDTH,) + DSA_SPLITS[:4] + (LANES,)
        starts = np.cumsum((0,) + widths[:-1]).tolist()
        return w, tuple(zip(starts, widths)), (bf16, bf16, f32, f32, bf16, f32)
    c0, c1 = MEM_WIDTH + CONV_CH, MEM_WIDTH + CONV_CH + 2 * GDN_HEADS
    w = jnp.concatenate([w_in[:, :c0], w_in[:, c1:], _pad_cols(w_in[:, c0:c1], LANES)], axis=1)
    widths = (MEM_WIDTH, CONV_CH, SEQ_WIDTH, LANES)
    starts = np.cumsum((0,) + widths[:-1]).tolist()
    return w, tuple(zip(starts, widths)), (bf16, f32, f32, f32)


def kernel(x_prompt, x_sample, cache_mem_k, cache_mem_v, state_ssm_re, state_ssm_im, cache_k, cache_v, cache_kidx, state_conv, state_delta, page_table, mem_prompt, norm_g, final_norm, w_in_a, w_in_b, w_in_c, w_out, w_mem_kv, ffn1_gate, ffn1_up, ffn1_down, ffn2_gate, ffn2_up, ffn2_down, s5_lam_re, s5_lam_im, s5_log_step, s5_b_re, s5_b_im, s5_c_re, s5_c_im, s5_d, s5_w_glu, s5_b_glu, rel_bias, gdn_conv_w, gdn_a_log, gdn_dt_bias, gdn_o_norm):
    depth = norm_g.shape[0]
    bp, lp, d = x_prompt.shape
    bs, ls, _ = x_sample.shape
    assert ls == 1, "the decode-step kernels handle one new token per sample"
    f32, bf16 = jnp.float32, jnp.bfloat16
    hp, hs = x_prompt.reshape(bp * lp, d), x_sample.reshape(bs * ls, d)
    mem_rows = mem_prompt.reshape(bp * N_MEM, d)
    page_view = lambda pool: pool.reshape(pool.shape[0], PAGE_SIZE, KV_WIDTH)
    pools = [(page_view(cache_k[j]), page_view(cache_v[j])) for j in range(cache_k.shape[0])]
    mem_k_p, mem_v_p = [], []
    ssm_re_p, ssm_im_p, ssm_re_s, ssm_im_s = [], [], [], []
    k_p, v_p, ki_p, k_s, v_s, ki_s = [], [], [], [], [], []
    conv_p, delta_p, conv_s, delta_s = [], [], [], []
    w_in_by_kind = (w_in_a, w_in_b, w_in_c)
    for i in range(depth):
        kind, j = i % N_MIXERS, i // N_MIXERS
        last = i == depth - 1
        ffn1 = (ffn1_gate[i].astype(bf16), ffn1_up[i].astype(bf16), ffn1_down[i].astype(bf16))
        ffn2 = (ffn2_gate[i].astype(bf16), ffn2_up[i].astype(bf16), ffn2_down[i].astype(bf16))
        w_in, segments, dtypes = _in_proj_layout(kind, w_in_by_kind[kind][j])
        w_in = w_in.astype(bf16)
        hp = ffn_residual(hp, norm_g[i, 0], *ffn1)
        hs = ffn_residual(hs, norm_g[i, 0], *ffn1)
        zp = [a.reshape(bp, lp, -1) for a in proj_in(hp, norm_g[i, 1], w_in, segments, dtypes)]
        zs = [a.reshape(bs, ls, -1) for a in proj_in(hs, norm_g[i, 1], w_in, segments, dtypes)]
        mk, mv = proj_in(mem_rows, norm_g[i, 1], w_mem_kv[i].astype(bf16), ((0, MEM_WIDTH), (MEM_WIDTH, MEM_WIDTH)),
                         (f32, f32), normalize=False)
        mem_shape = (bp, N_MEM, MEM_HEADS, HEAD_DIM)
        mk, mv = mk.reshape(mem_shape), mv.reshape(mem_shape)
        mem_k_p.append(mk)
        mem_v_p.append(mv)
        if kind == 0:
            disc = s5_discretize(s5_lam_re[j], s5_lam_im[j], s5_log_step[j], s5_b_re[j], s5_b_im[j],
                                 s5_c_re[j], s5_c_im[j])
            gate = (s5_d[j], s5_w_glu[j], s5_b_glu[j])
            h0 = jnp.zeros((bp, S5_GROUPS, S5_STATE), state_ssm_re.dtype)
            op, hr, hi = s5_mix_pallas(zp[1], h0, h0, disc, *gate)
            ssm_re_p.append(hr)
            ssm_im_p.append(hi)
            osm, hr, hi = s5_mix_pallas(zs[1], state_ssm_re[j], state_ssm_im[j], disc, *gate)
            ssm_re_s.append(hr)
            ssm_im_s.append(hi)
        elif kind == 1:
            _, q, k, v, qi, kiwi = zp
            op = dsa_prompt_pallas(q, qi, kiwi, k, v, rel_bias)
            kv_shape = (DSA_KV_HEADS, HEAD_DIM)
            k_p.append(k.reshape((bp, lp) + kv_shape))
            v_p.append(v.reshape((bp, lp) + kv_shape))
            ki_p.append(kiwi[..., :IDX_DIM])
            _, q, k, v, qi, kiwi = zs
            osm = dsa_decode_pallas(q[:, 0], qi[:, 0], kiwi[:, 0], k[:, 0], v[:, 0], pools[j][0], pools[j][1],
                                    cache_kidx[j], page_table, rel_bias)
            k_s.append(k.reshape((bs, ls) + kv_shape))
            v_s.append(v.reshape((bs, ls) + kv_shape))
            ki_s.append(kiwi[..., :IDX_DIM])
        else:
            gdn = (gdn_conv_w[j], gdn_a_log[j], gdn_dt_bias[j], gdn_o_norm[j])
            ctx0 = jnp.zeros((bp, CONV_W - 1, CONV_CH), f32)
            s00 = jnp.zeros((bp, GDN_HEADS, HEAD_DIM, HEAD_DIM), state_delta.dtype)
            _, qkv, gate, ab = zp
            op, cst, sst = gdn_mix_pallas(qkv, ab, gate, ctx0, s00, *gdn)
            conv_p.append(cst)
            delta_p.append(sst)
            _, qkv, gate, ab = zs
            osm, cst, sst = gdn_decode_pallas(qkv, ab, gate, state_conv[j], state_delta[j], *gdn)
            conv_s.append(cst)
            delta_s.append(sst)
        w_o = w_out[i].astype(bf16)
        hp = mix_out(hp.reshape(bp, lp, d), zp[0], op, mk, mv, w_o).reshape(bp * lp, d)
        hs = mix_out(hs.reshape(bs, ls, d), zs[0], osm, cache_mem_k[i], cache_mem_v[i], w_o).reshape(bs * ls, d)
        fg = final_norm if last else None
        hp = ffn_residual(hp, norm_g[i, 2], *ffn2, fg)
        hs = ffn_residual(hs, norm_g[i, 2], *ffn2, fg)
    st = jnp.stack
    return (hp.reshape(bp, lp, d), hs.reshape(bs, ls, d), st(mem_k_p), st(mem_v_p),
            st(ssm_re_p), st(ssm_im_p), st(ssm_re_s), st(ssm_im_s),
            st(k_p), st(v_p), st(ki_p), st(k_s), st(v_s), st(ki_s),
            st(conv_p), st(delta_p), st(conv_s), st(delta_s))
```
